```python
import jax, jax.numpy as jnp
from jax import lax
import numpy as np

D_MODEL = 1024
BATCH = 4
SEQ = 4096
DEPTH = 4

CONV_CH = 512
CONV_WIDTH = 31
RET_HEADS = 4
RET_DK = 128
RET_DV = 256
RET_CHUNK = 128
RET_ROPE_BASE = 10000.0
NSA_HEADS = 8
NSA_GROUPS = 2
NSA_HPG = NSA_HEADS // NSA_GROUPS
NSA_DH = 64
NSA_N_BRANCH = 3
CMP_BLOCK = 32
CMP_STRIDE = 16
CMP_HIDDEN = 256
SEL_BLOCK = 64
SEL_TOPK = 16
SEL_QBLOCK = 64
WINDOW = 512
WIN_QBLOCK = 128
ROPE_THETA = 500000.0
ROT_DIM = NSA_DH // 4
N_EXPERTS = 64
N_EXPERT_GROUPS = 8
TOPK_GROUPS = 4
TOPK = 8
D_EXPERT = 256
D_SHARED = 256
ROUTED_SCALE = 2.5
MOE_TOKEN_BLOCK = 128
N_BRANCHES = 3
DN_ALPHA = (2.0 * DEPTH) ** 0.25
DN_BETA = (8.0 * DEPTH) ** -0.25
LN_EPS = 1e-5
NEG = -1e30

IN_WIDTHS = (CONV_CH, CONV_CH,
             RET_HEADS * RET_DK, RET_HEADS * RET_DK, RET_HEADS * RET_DV, RET_HEADS * RET_DV,
             NSA_HEADS * NSA_DH,
             NSA_GROUPS * NSA_DH, NSA_GROUPS * NSA_DH, NSA_GROUPS * NSA_DH,
             NSA_GROUPS * NSA_DH, NSA_GROUPS * NSA_DH, NSA_GROUPS * NSA_DH,
             NSA_HEADS * NSA_N_BRANCH,
             N_BRANCHES * D_MODEL)
D_IN = sum(IN_WIDTHS)

kernel_name = "hybrid_conv_retnet_nsa_moe_deepnorm"


def layer_norm(x, g, b):
    xf = x.astype(jnp.float32)
    mu = jnp.mean(xf, -1, keepdims=True)
    var = jnp.mean(jnp.square(xf - mu), -1, keepdims=True)
    y = (xf - mu) * lax.rsqrt(var + LN_EPS)
    return (y * g.astype(jnp.float32) + b.astype(jnp.float32)).astype(x.dtype)


def rope(x, pos, inv_freq):
    half = inv_freq.shape[0]
    rd = 2 * half
    ang = pos.astype(jnp.float32)[:, :, None, None] * inv_freq
    cos, sin = jnp.cos(ang), jnp.sin(ang)
    x1 = x[..., :half].astype(jnp.float32)
    x2 = x[..., half:rd].astype(jnp.float32)
    rot = jnp.concatenate([x1 * cos - x2 * sin, x2 * cos + x1 * sin], -1).astype(x.dtype)
    return jnp.concatenate([rot, x[..., rd:]], -1)


def nsa_inv_freq():
    return jnp.power(jnp.float32(ROPE_THETA), -jnp.arange(0, ROT_DIM, 2, dtype=jnp.float32) / ROT_DIM)


def conformer_conv(a, b, dw, db, ln_g, ln_b, w_pw):
    u = a * jax.nn.sigmoid(b)
    u = lax.conv_general_dilated(u, dw[:, None, :], window_strides=(1,),
                                 padding=[(CONV_WIDTH - 1, 0)],
                                 dimension_numbers=("NWC", "WIO", "NWC"),
                                 feature_group_count=CONV_CH) + db
    u = jax.nn.silu(layer_norm(u, ln_g, ln_b))
    return u @ w_pw


def retention(q, k, v, g, pos, w_o):
    B, T, _ = q.shape
    H, C = RET_HEADS, RET_CHUNK
    nc = T // C
    inv = 1.0 / jnp.power(jnp.float32(RET_ROPE_BASE), jnp.linspace(0.0, 1.0, RET_DK // 2, dtype=jnp.float32))
    q = rope(q.reshape(B, T, H, RET_DK), pos, inv)
    k = rope(k.reshape(B, T, H, RET_DK), pos, inv) * (RET_DK ** -0.5)
    v = v.reshape(B, T, H, RET_DV)
    log_gamma = jnp.log1p(-jnp.exp2(-5.0 - jnp.arange(H, dtype=jnp.float32)))
    idx = jnp.arange(C, dtype=jnp.float32)
    diff = idx[:, None] - idx[None, :]
    dmat = jnp.where(diff >= 0, jnp.exp(log_gamma[:, None, None] * jnp.maximum(diff, 0.0)), 0.0).astype(q.dtype)
    xi = jnp.exp(log_gamma[:, None] * (idx + 1.0)).astype(q.dtype)
    zeta = jnp.exp(log_gamma[:, None] * (C - 1.0 - idx)).astype(q.dtype)
    chunk_decay = jnp.exp(log_gamma * C).astype(q.dtype)

    def chunks(t, d):
        return t.reshape(B, nc, C, H, d).transpose(1, 0, 3, 2, 4)

    def step(R, inp):
        qi, ki, vi = inp
        inner = jnp.einsum("bhcd,bhsd->bhcs", qi, ki) * dmat
        o = (jnp.einsum("bhcs,bhsv->bhcv", inner, vi)
             + jnp.einsum("bhcd,bhdv->bhcv", qi, R) * xi[None, :, :, None])
        R = R * chunk_decay[None, :, None, None] + jnp.einsum(
            "bhsd,bhsv->bhdv", ki, vi * zeta[None, :, :, None])
        return R, o

    R0 = jnp.zeros((B, H, RET_DK, RET_DV), q.dtype)
    _, o = lax.scan(step, R0, (chunks(q, RET_DK), chunks(k, RET_DK), chunks(v, RET_DV)))
    o = o.transpose(1, 0, 3, 2, 4).reshape(B, T, H, RET_DV)
    of = o.astype(jnp.float32)
    mu = jnp.mean(of, -1, keepdims=True)
    var = jnp.mean(jnp.square(of - mu), -1, keepdims=True)
    o = ((of - mu) * lax.rsqrt(var + LN_EPS)).astype(v.dtype).reshape(B, T, H * RET_DV)
    return (jax.nn.silu(g) * o) @ w_o


def nsa(q, kc, vc, ks, vs, kw, vw, gl, pos, pe_k, w1_k, w2_k, pe_v, w1_v, w2_v, w_o):
    B, T, _ = q.shape
    G, HPG, DH = NSA_GROUPS, NSA_HPG, NSA_DH
    inv = nsa_inv_freq()
    q = (rope(q.reshape(B, T, NSA_HEADS, DH), pos, inv) * (DH ** -0.5)).reshape(B, T, G, HPG, DH)
    t_idx = np.arange(T)

    n_cmp = (T - CMP_BLOCK) // CMP_STRIDE + 1
    blk_idx = np.arange(n_cmp)[:, None] * CMP_STRIDE + np.arange(CMP_BLOCK)[None, :]
    end = blk_idx[:, -1]

    def compress(t, pe, w1, w2):
        tb = t.reshape(B, T, G, DH)[:, blk_idx] + pe[None, None, :, None, :]
        tb = tb.transpose(0, 1, 3, 2, 4).reshape(B, n_cmp, G, CMP_BLOCK * DH)
        return jax.nn.silu(tb @ w1) @ w2

    k_cmp = rope(compress(kc, pe_k, w1_k, w2_k), pos[:, end], inv)
    v_cmp = compress(vc, pe_v, w1_v, w2_v)
    cmp_ok = end[None, :] <= t_idx[:, None]
    s = jnp.einsum("btghd,bngd->bghtn", q, k_cmp).astype(jnp.float32)
    p = jnp.where(cmp_ok, jax.nn.softmax(jnp.where(cmp_ok, s, NEG), axis=-1), 0.0)
    o_cmp = jnp.einsum("bghtn,bngd->btghd", p.astype(v_cmp.dtype), v_cmp)

    n_sel = T // SEL_BLOCK
    topk = min(SEL_TOPK, n_sel)
    ii = np.arange(n_cmp)[:, None]
    jj = np.arange(n_sel)[None, :]
    lo = np.maximum(ii * CMP_STRIDE, jj * SEL_BLOCK)
    hi = np.minimum(ii * CMP_STRIDE + CMP_BLOCK, (jj + 1) * SEL_BLOCK)
    overlap = jnp.asarray(np.maximum(hi - lo, 0).astype(np.float32) / CMP_BLOCK)
    imp = jnp.einsum("bghtn,nj->bgtj", p, overlap)
    cur = t_idx // SEL_BLOCK
    jsel = np.arange(n_sel)[None, :]
    eligible = jsel <= cur[:, None]
    forced = (jsel == 0) | (jsel == cur[:, None]) | (jsel == cur[:, None] - 1)
    imp = jnp.where(forced, jnp.inf, jnp.where(eligible, imp, -jnp.inf))
    _, sel_idx = lax.top_k(imp, topk)

    ks = rope(ks.reshape(B, T, G, DH), pos, inv)
    ks_blocks = ks.reshape(B, n_sel, SEL_BLOCK, G, DH).transpose(0, 3, 1, 2, 4)
    vs_blocks = vs.reshape(B, n_sel, SEL_BLOCK, G, DH).transpose(0, 3, 1, 2, 4)
    nqb = T // SEL_QBLOCK
    q_sb = q.reshape(B, nqb, SEL_QBLOCK, G, HPG, DH).transpose(1, 0, 2, 3, 4, 5)
    idx_sb = sel_idx.reshape(B, G, nqb, SEL_QBLOCK, topk).transpose(2, 0, 1, 3, 4)
    t0_sb = jnp.arange(nqb, dtype=jnp.int32) * SEL_QBLOCK
    bi = jnp.arange(B)[:, None, None, None]
    gi = jnp.arange(G)[None, :, None, None]

    def sel_block(args):
        qb, ib, t0 = args
        kg = ks_blocks[bi, gi, ib].reshape(B, G, SEL_QBLOCK, topk * SEL_BLOCK, DH)
        vg = vs_blocks[bi, gi, ib].reshape(B, G, SEL_QBLOCK, topk * SEL_BLOCK, DH)
        tok = (ib[..., None] * SEL_BLOCK + jnp.arange(SEL_BLOCK)).reshape(B, G, SEL_QBLOCK, topk * SEL_BLOCK)
        tq = t0 + jnp.arange(SEL_QBLOCK)
        ok = (tok <= tq[None, None, :, None])[:, :, None]
        sc = jnp.einsum("bqghd,bgqsd->bghqs", qb, kg).astype(jnp.float32)
        pr = jax.nn.softmax(jnp.where(ok, sc, NEG), axis=-1)
        return jnp.einsum("bghqs,bgqsd->bqghd", pr.astype(vg.dtype), vg)

    o_slc = lax.map(sel_block, (q_sb, idx_sb, t0_sb))
    o_slc = o_slc.transpose(1, 0, 2, 3, 4, 5).reshape(B, T, G, HPG, DH)

    kw = rope(kw.reshape(B, T, G, DH), pos, inv)
    kw_pad = jnp.pad(kw, ((0, 0), (WINDOW, 0), (0, 0), (0, 0)))
    vw_pad = jnp.pad(vw.reshape(B, T, G, DH), ((0, 0), (WINDOW, 0), (0, 0), (0, 0)))
    nwb = T // WIN_QBLOCK
    span = WINDOW + WIN_QBLOCK
    q_wb = q.reshape(B, nwb, WIN_QBLOCK, G, HPG, DH).transpose(1, 0, 2, 3, 4, 5)
    t0_wb = jnp.arange(nwb, dtype=jnp.int32) * WIN_QBLOCK

    def win_block(args):
        qb, t0 = args
        kb = lax.dynamic_slice_in_dim(kw_pad, t0, span, axis=1)
        vb = lax.dynamic_slice_in_dim(vw_pad, t0, span, axis=1)
        tq = t0 + jnp.arange(WIN_QBLOCK)
        tk = t0 - WINDOW + jnp.arange(span)
        d = tq[:, None] - tk[None, :]
        ok = (d >= 0) & (d < WINDOW) & (tk[None, :] >= 0)
        sc = jnp.einsum("bqghd,bsgd->bghqs", qb, kb).astype(jnp.float32)
        pr = jax.nn.softmax(jnp.where(ok, sc, NEG), axis=-1)
        return jnp.einsum("bghqs,bsgd->bqghd", pr.astype(vb.dtype), vb)

    o_win = lax.map(win_block, (q_wb, t0_wb))
    o_win = o_win.transpose(1, 0, 2, 3, 4, 5).reshape(B, T, G, HPG, DH)

    gates = jax.nn.sigmoid(gl).reshape(B, T, G, HPG, NSA_N_BRANCH)
    o = gates[..., 0, None] * o_cmp + gates[..., 1, None] * o_slc + gates[..., 2, None] * o_win
    return o.reshape(B, T, NSA_HEADS * DH) @ w_o


def hybrid_mixer(x, pos, w_in, conv_dw, conv_db, conv_ln_g, conv_ln_b, conv_w_pw, ret_w_o,
                 pe_k, w1_k, w2_k, pe_v, w1_v, w2_v, nsa_w_o, w_out):
    B, T, D = x.shape
    z = x @ w_in
    splits = [int(c) for c in np.cumsum(IN_WIDTHS)[:-1]]
    (c_a, c_b, r_q, r_k, r_v, r_g, n_q, n_kc, n_vc, n_ks, n_vs, n_kw, n_vw, n_g, m_g) = jnp.split(z, splits, axis=-1)
    y_conv = conformer_conv(c_a, c_b, conv_dw, conv_db, conv_ln_g, conv_ln_b, conv_w_pw)
    y_ret = retention(r_q, r_k, r_v, r_g, pos, ret_w_o)
    y_nsa = nsa(n_q, n_kc, n_vc, n_ks, n_vs, n_kw, n_vw, n_g, pos, pe_k, w1_k, w2_k, pe_v, w1_v, w2_v, nsa_w_o)
    gm = jax.nn.sigmoid(m_g).reshape(B, T, N_BRANCHES, D)
    m = gm[:, :, 0] * y_conv + gm[:, :, 1] * y_ret + gm[:, :, 2] * y_nsa
    return m @ w_out


def moe(x, router_w, router_b, w1, w3, w2, ws1, ws3, ws2):
    B, T, D = x.shape
    N = B * T
    xt = x.reshape(N, D)
    s = jax.nn.sigmoid((xt @ router_w).astype(jnp.float32))
    sb = s + router_b.astype(jnp.float32)
    per_group = N_EXPERTS // N_EXPERT_GROUPS
    grp_score = lax.top_k(sb.reshape(N, N_EXPERT_GROUPS, per_group), 2)[0].sum(-1)
    _, gidx = lax.top_k(grp_score, TOPK_GROUPS)
    gmask = jnp.sum(jax.nn.one_hot(gidx, N_EXPERT_GROUPS, dtype=jnp.float32), -2) > 0
    emask = jnp.repeat(gmask, per_group, axis=-1)
    _, eidx = lax.top_k(jnp.where(emask, sb, -jnp.inf), TOPK)
    w = jnp.take_along_axis(s, eidx, -1)
    w = w / jnp.sum(w, -1, keepdims=True) * ROUTED_SCALE
    gates = jnp.sum(jax.nn.one_hot(eidx, N_EXPERTS, dtype=jnp.float32) * w[..., None], -2).astype(x.dtype)
    nb = N // MOE_TOKEN_BLOCK

    def block(args):
        xb, gb = args
        h = jax.nn.silu(jnp.einsum("nd,edf->nef", xb, w1)) * jnp.einsum("nd,edf->nef", xb, w3)
        return jnp.einsum("nef,efd->nd", h * gb[..., None], w2)

    y = lax.map(block, (xt.reshape(nb, MOE_TOKEN_BLOCK, D), gates.reshape(nb, MOE_TOKEN_BLOCK, N_EXPERTS)))
    shared = (jax.nn.silu(xt @ ws1) * (xt @ ws3)) @ ws2
    return (y.reshape(N, D) + shared).reshape(B, T, D)


def setup_inputs(seed: int = 0) -> dict:
    key = jax.random.key(seed)
    ks = jax.random.split(key, 32)
    L, D = DEPTH, D_MODEL

    def nrm(k, shape, scale):
        return jax.random.normal(k, shape, jnp.float32) * scale

    return {
        "x": nrm(ks[0], (BATCH, SEQ, D), 1.0),
        "positions": jnp.arange(SEQ, dtype=jnp.int32)[None, :]
                     + jax.random.randint(ks[1], (BATCH, 1), 0, 1024, dtype=jnp.int32),
        "w_in": nrm(ks[2], (L, D, D_IN), D ** -0.5),
        "conv_dw": nrm(ks[3], (L, CONV_WIDTH, CONV_CH), CONV_WIDTH ** -0.5),
        "conv_db": nrm(ks[4], (L, CONV_CH), 0.01),
        "conv_ln_g": 1.0 + nrm(ks[5], (L, CONV_CH), 0.01),
        "conv_ln_b": nrm(ks[6], (L, CONV_CH), 0.01),
        "conv_w_pw": nrm(ks[7], (L, CONV_CH, D), DN_BETA * CONV_CH ** -0.5),
        "ret_w_o": nrm(ks[8], (L, RET_HEADS * RET_DV, D), DN_BETA * (RET_HEADS * RET_DV) ** -0.5),
        "nsa_pe_k": nrm(ks[9], (L, CMP_BLOCK, NSA_DH), 0.1),
        "nsa_w1_k": nrm(ks[10], (L, CMP_BLOCK * NSA_DH, CMP_HIDDEN), (CMP_BLOCK * NSA_DH) ** -0.5),
        "nsa_w2_k": nrm(ks[11], (L, CMP_HIDDEN, NSA_DH), CMP_HIDDEN ** -0.5),
        "nsa_pe_v": nrm(ks[12], (L, CMP_BLOCK, NSA_DH), 0.1),
        "nsa_w1_v": nrm(ks[13], (L, CMP_BLOCK * NSA_DH, CMP_HIDDEN), (CMP_BLOCK * NSA_DH) ** -0.5),
        "nsa_w2_v": nrm(ks[14], (L, CMP_HIDDEN, NSA_DH), CMP_HIDDEN ** -0.5),
        "nsa_w_o": nrm(ks[15], (L, NSA_HEADS * NSA_DH, D), DN_BETA * (NSA_HEADS * NSA_DH) ** -0.5),
        "w_out": nrm(ks[16], (L, D, D), DN_BETA * D ** -0.5),
        "ln1_g": 1.0 + nrm(ks[17], (L, D), 0.01),
        "ln1_b": nrm(ks[18], (L, D), 0.01),
        "router_w": nrm(ks[19], (L, D, N_EXPERTS), D ** -0.5),
        "router_b": nrm(ks[20], (L, N_EXPERTS), 0.01),
        "moe_w1": nrm(ks[21], (L, N_EXPERTS, D, D_EXPERT), DN_BETA * D ** -0.5),
        "moe_w3": nrm(ks[22], (L, N_EXPERTS, D, D_EXPERT), DN_BETA * D ** -0.5),
        "moe_w2": nrm(ks[23], (L, N_EXPERTS, D_EXPERT, D), DN_BETA * D_EXPERT ** -0.5),
        "shared_w1": nrm(ks[24], (L, D, D_SHARED), DN_BETA * D ** -0.5),
        "shared_w3": nrm(ks[25], (L, D, D_SHARED), DN_BETA * D ** -0.5),
        "shared_w2": nrm(ks[26], (L, D_SHARED, D), DN_BETA * D_SHARED ** -0.5),
        "ln2_g": 1.0 + nrm(ks[27], (L, D), 0.01),
        "ln2_b": nrm(ks[28], (L, D), 0.01),
    }


def reference(x, positions, w_in, conv_dw, conv_db, conv_ln_g, conv_ln_b, conv_w_pw, ret_w_o,
              nsa_pe_k, nsa_w1_k, nsa_w2_k, nsa_pe_v, nsa_w1_v, nsa_w2_v, nsa_w_o, w_out,
              ln1_g, ln1_b, router_w, router_b, moe_w1, moe_w3, moe_w2,
              shared_w1, shared_w3, shared_w2, ln2_g, ln2_b):
    for l in range(DEPTH):
        h = hybrid_mixer(x, positions, w_in[l], conv_dw[l], conv_db[l], conv_ln_g[l], conv_ln_b[l],
                         conv_w_pw[l], ret_w_o[l], nsa_pe_k[l], nsa_w1_k[l], nsa_w2_k[l],
                         nsa_pe_v[l], nsa_w1_v[l], nsa_w2_v[l], nsa_w_o[l], w_out[l])
        x = layer_norm(DN_ALPHA * x + h, ln1_g[l], ln1_b[l])
        h = moe(x, router_w[l], router_b[l], moe_w1[l], moe_w3[l], moe_w2[l],
                shared_w1[l], shared_w3[l], shared_w2[l])
        x = layer_norm(DN_ALPHA * x + h, ln2_g[l], ln2_b[l])
    return x
```

```python
import functools
import math

import jax
import jax.numpy as jnp
import numpy as np
from jax import lax
from jax.experimental import pallas as pl
from jax.experimental.pallas import tpu as pltpu

F32 = jnp.float32
BF16 = jnp.bfloat16
I32 = jnp.int32

D_MODEL = 1024
DEPTH = 4
CONV_CH = 512
CONV_WIDTH = 31
RET_HEADS = 4
RET_DK = 128
RET_DV = 256
RET_CHUNK = 128
RET_ROPE_BASE = 10000.0
NSA_HEADS = 8
NSA_GROUPS = 2
NSA_HPG = NSA_HEADS // NSA_GROUPS
NSA_DH = 64
NSA_N_BRANCH = 3
CMP_BLOCK = 32
CMP_STRIDE = 16
CMP_HIDDEN = 256
SEL_BLOCK = 64
SEL_TOPK = 16
WINDOW = 512
ROPE_THETA = 500000.0
ROT_DIM = NSA_DH // 4
N_EXPERTS = 64
N_EXPERT_GROUPS = 8
TOPK_GROUPS = 4
TOPK = 8
D_EXPERT = 256
D_SHARED = 256
ROUTED_SCALE = 2.5
DN_ALPHA = (2.0 * DEPTH) ** 0.25
LN_EPS = 1e-5
NEG = -1e30

LANES = 128

Z_MG = 0
Z_CA = 3072
Z_CB = 3584
Z_RQ = 4096
Z_RK = 4608
Z_RV = 5120
Z_RG = 6144
Z_NQ = 7168
Z_NKV = 7680
Z_NG = 8448
Z_W = 8704

_IN_WIDTHS = (512, 512, 512, 512, 1024, 1024, 512, 128, 128, 128, 128, 128, 128, 24, 3072)
_IN_OFFS = tuple(int(v) for v in np.concatenate([[0], np.cumsum(_IN_WIDTHS)[:-1]]))

MOE_TILE = 512
MOE_ROWS_PER_TOKEN = TOPK


def _cparams(sem, vmem_mb=None):
    kw = dict(dimension_semantics=sem)
    if vmem_mb is not None:
        kw["vmem_limit_bytes"] = vmem_mb * 1024 * 1024
    return pltpu.CompilerParams(**kw)


def _mm_kernel(x_ref, w_ref, o_ref):
    o_ref[...] = jnp.dot(x_ref[...].astype(BF16), w_ref[...], preferred_element_type=F32).astype(o_ref.dtype)


def matmul(x, w, *, tm, tn, out_dtype=F32, x_col_block=0, k=None, name="mm"):
    m = x.shape[0]
    kk, n = w.shape
    if k is None:
        k = kk
    return pl.pallas_call(
        _mm_kernel,
        out_shape=jax.ShapeDtypeStruct((m, n), out_dtype),
        grid=(m // tm, n // tn),
        in_specs=[pl.BlockSpec((tm, k), lambda i, j: (i, x_col_block)),
                  pl.BlockSpec((k, tn), lambda i, j: (0, j))],
        out_specs=pl.BlockSpec((tm, tn), lambda i, j: (i, j)),
        compiler_params=_cparams(("parallel", "arbitrary"), 48),
        name=name,
    )(x, w)


def _rope_table_kernel(pos_ref, inv_ref, sgn_ref, cos_ref, sin_ref):
    ang = pos_ref[...] * inv_ref[...]
    cos_ref[...] = jnp.cos(ang)
    sin_ref[...] = jnp.sin(ang) * sgn_ref[...]


def rope_tables(pos_col, inv, sgn, *, tm):
    n = pos_col.shape[0]
    w = inv.shape[1]
    return pl.pallas_call(
        _rope_table_kernel,
        out_shape=(jax.ShapeDtypeStruct((n, w), F32), jax.ShapeDtypeStruct((n, w), F32)),
        grid=(n // tm,),
        in_specs=[pl.BlockSpec((tm, 1), lambda i: (i, 0)),
                  pl.BlockSpec((1, w), lambda i: (0, 0)),
                  pl.BlockSpec((1, w), lambda i: (0, 0))],
        out_specs=(pl.BlockSpec((tm, w), lambda i: (i, 0)), pl.BlockSpec((tm, w), lambda i: (i, 0))),
        compiler_params=_cparams(("parallel",)),
        name="rope_tables",
    )(pos_col, inv, sgn)


def _ret_inv_freq():
    inv = 1.0 / jnp.power(jnp.float32(RET_ROPE_BASE), jnp.linspace(0.0, 1.0, RET_DK // 2, dtype=F32))
    inv = jnp.concatenate([inv, inv])[None, :]
    sgn = jnp.concatenate([-jnp.ones((RET_DK // 2,), F32), jnp.ones((RET_DK // 2,), F32)])[None, :]
    return inv, sgn


def _nsa_inv_freq():
    half = ROT_DIM // 2
    inv = jnp.power(jnp.float32(ROPE_THETA), -jnp.arange(0, ROT_DIM, 2, dtype=F32) / ROT_DIM)
    z = jnp.zeros((NSA_DH - ROT_DIM,), F32)
    inv64 = jnp.concatenate([inv, inv, z])
    sgn64 = jnp.concatenate([-jnp.ones((half,), F32), jnp.ones((half,), F32), z])
    return jnp.concatenate([inv64, inv64])[None, :], jnp.concatenate([sgn64, sgn64])[None, :]


def _nsa_rope(x, cos, sin):
    w = x.shape[1]
    half = ROT_DIM // 2
    lane = lax.broadcasted_iota(I32, x.shape, 1) % NSA_DH
    partner = jnp.where(lane < half, pltpu.roll(x, w - half, 1), pltpu.roll(x, half, 1))
    return x * cos + partner * sin


CONV_TT = 256
CONV_HALO = 32


def _conv_kernel(a_ref, b_ref, ah_ref, bh_ref, dw_ref, db_ref, g_ref, be_ref, wpw_ref, o_ref, ubuf):
    i = pl.program_id(1)
    tt = a_ref.shape[0]
    u = a_ref[...] * jax.nn.sigmoid(b_ref[...])
    uh = ah_ref[...] * jax.nn.sigmoid(bh_ref[...])
    ubuf[0:CONV_HALO, :] = jnp.where(i > 0, uh, 0.0)
    ubuf[CONV_HALO:CONV_HALO + tt, :] = u
    acc = jnp.zeros((tt, CONV_CH), F32)
    base = CONV_HALO - (CONV_WIDTH - 1)
    for k in range(CONV_WIDTH):
        acc = acc + dw_ref[k:k + 1, :] * ubuf[base + k:base + k + tt, :]
    acc = acc + db_ref[...]
    mu = jnp.mean(acc, axis=-1, keepdims=True)
    var = jnp.mean(jnp.square(acc - mu), axis=-1, keepdims=True)
    y = (acc - mu) * lax.rsqrt(var + LN_EPS) * g_ref[...] + be_ref[...]
    y = y * jax.nn.sigmoid(y)
    o_ref[...] = jnp.dot(y.astype(BF16), wpw_ref[...], preferred_element_type=F32)


def conv_branch(z, dw, db, ln_g, ln_b, w_pw_bf, *, batch, seq):
    tt = CONV_TT
    nt = seq // tt
    r = tt // CONV_HALO
    ca, cb = Z_CA // CONV_CH, Z_CB // CONV_CH

    def halo_map(col):
        return lambda b, i: (jnp.maximum((b * nt + i) * r - 1, 0), col)

    return pl.pallas_call(
        _conv_kernel,
        out_shape=jax.ShapeDtypeStruct((batch * seq, D_MODEL), F32),
        grid=(batch, nt),
        in_specs=[pl.BlockSpec((tt, CONV_CH), lambda b, i: (b * nt + i, ca)),
                  pl.BlockSpec((tt, CONV_CH), lambda b, i: (b * nt + i, cb)),
                  pl.BlockSpec((CONV_HALO, CONV_CH), halo_map(ca)),
                  pl.BlockSpec((CONV_HALO, CONV_CH), halo_map(cb)),
                  pl.BlockSpec((CONV_WIDTH, CONV_CH), lambda b, i: (0, 0)),
                  pl.BlockSpec((1, CONV_CH), lambda b, i: (0, 0)),
                  pl.BlockSpec((1, CONV_CH), lambda b, i: (0, 0)),
                  pl.BlockSpec((1, CONV_CH), lambda b, i: (0, 0)),
                  pl.BlockSpec((CONV_CH, D_MODEL), lambda b, i: (0, 0))],
        out_specs=pl.BlockSpec((tt, D_MODEL), lambda b, i: (b * nt + i, 0)),
        scratch_shapes=[pltpu.VMEM((CONV_HALO + tt, CONV_CH), F32)],
        compiler_params=_cparams(("parallel", "parallel")),
        name="conv_branch",
    )(z, z, z, z, dw, db, ln_g, ln_b, w_pw_bf)


RET_TQ = 512


def _ret_tables():
    h, c = RET_HEADS, RET_CHUNK
    log_gamma = jnp.log1p(-jnp.exp2(-5.0 - jnp.arange(h, dtype=F32)))
    idx = jnp.arange(c, dtype=F32)
    diff = idx[:, None] - idx[None, :]
    dmat = jnp.where(diff >= 0, jnp.exp(log_gamma[:, None, None] * jnp.maximum(diff, 0.0)), 0.0).astype(F32)
    xi = jnp.exp(log_gamma[:, None] * (idx + 1.0)).astype(F32)
    zeta = jnp.exp(log_gamma[:, None] * (c - 1.0 - idx)).astype(F32)
    decay = jnp.exp(log_gamma * c).astype(F32)
    xi_b = jnp.broadcast_to(xi[:, :, None], (h, c, RET_DV))
    zeta_b = jnp.broadcast_to(zeta[:, :, None], (h, c, RET_DV))
    decay_b = jnp.broadcast_to(decay[:, None, None], (h, RET_DK, RET_DV))
    return dmat, xi_b, zeta_b, decay_b


def _ret_kernel(q_ref, k_ref, v_ref, g_ref, cos_ref, sin_ref, dmat_ref, xi_ref, zeta_ref, dec_ref, o_ref, r_ref):
    @pl.when(pl.program_id(1) == 0)
    def _():
        r_ref[...] = jnp.zeros_like(r_ref)

    c = RET_CHUNK
    n_chunks = q_ref.shape[0] // c
    for ci in range(n_chunks):
        rows = slice(ci * c, (ci + 1) * c)
        cos = cos_ref[rows, :]
        sin = sin_ref[rows, :]
        for h in range(RET_HEADS):
            qk_cols = slice(h * RET_DK, (h + 1) * RET_DK)
            v_cols = slice(h * RET_DV, (h + 1) * RET_DV)
            q = q_ref[rows, qk_cols]
            k = k_ref[rows, qk_cols]
            q = q * cos + pltpu.roll(q, RET_DK // 2, 1) * sin
            k = (k * cos + pltpu.roll(k, RET_DK // 2, 1) * sin) * (RET_DK ** -0.5)
            v = v_ref[rows, v_cols]
            qb = q.astype(BF16)
            kb = k.astype(BF16)
            inner = lax.dot_general(qb, kb, (((1,), (1,)), ((), ())), preferred_element_type=F32) * dmat_ref[h]
            r_old = r_ref[h]
            o = (jnp.dot(inner.astype(BF16), v.astype(BF16), preferred_element_type=F32)
                 + jnp.dot(qb, r_old.astype(BF16), preferred_element_type=F32) * xi_ref[h])
            vz = (v * zeta_ref[h]).astype(BF16)
            r_ref[h] = r_old * dec_ref[h] + jnp.dot(k.T.astype(BF16), vz, preferred_element_type=F32)
            mu = jnp.mean(o, axis=-1, keepdims=True)
            var = jnp.mean(jnp.square(o - mu), axis=-1, keepdims=True)
            on = (o - mu) * lax.rsqrt(var + LN_EPS)
            g = g_ref[rows, v_cols]
            o_ref[rows, v_cols] = g * jax.nn.sigmoid(g) * on


def retention_branch(z, ret_cos, ret_sin, tables, *, batch, seq):
    tq = RET_TQ
    nt = seq // tq
    dmat, xi_b, zeta_b, decay_b = tables
    qw = RET_HEADS * RET_DK
    vw = RET_HEADS * RET_DV
    row = lambda b, i: b * nt + i
    full3 = lambda b, i: (0, 0, 0)
    return pl.pallas_call(
        _ret_kernel,
        out_shape=jax.ShapeDtypeStruct((batch * seq, vw), F32),
        grid=(batch, nt),
        in_specs=[pl.BlockSpec((tq, qw), lambda b, i: (row(b, i), Z_RQ // qw)),
                  pl.BlockSpec((tq, qw), lambda b, i: (row(b, i), Z_RK // qw)),
                  pl.BlockSpec((tq, vw), lambda b, i: (row(b, i), Z_RV // vw)),
                  pl.BlockSpec((tq, vw), lambda b, i: (row(b, i), Z_RG // vw)),
                  pl.BlockSpec((tq, RET_DK), lambda b, i: (row(b, i), 0)),
                  pl.BlockSpec((tq, RET_DK), lambda b, i: (row(b, i), 0)),
                  pl.BlockSpec(dmat.shape, full3),
                  pl.BlockSpec(xi_b.shape, full3),
                  pl.BlockSpec(zeta_b.shape, full3),
                  pl.BlockSpec(decay_b.shape, full3)],
        out_specs=pl.BlockSpec((tq, vw), lambda b, i: (row(b, i), 0)),
        scratch_shapes=[pltpu.VMEM((RET_HEADS, RET_DK, RET_DV), F32)],
        compiler_params=_cparams(("parallel", "arbitrary"), 48),
        name="retention",
    )(z, z, z, z, ret_cos, ret_sin, dmat, xi_b, zeta_b, decay_b)


def _layout_w_in(w):
    seg = lambda i: w[:, _IN_OFFS[i]:_IN_OFFS[i] + _IN_WIDTHS[i]]
    order = (14, 0, 1, 2, 3, 4, 5, 6, 7, 8, 9, 10, 11, 12, 13)
    parts = [seg(i) for i in order]
    used = sum(_IN_WIDTHS)
    parts.append(jnp.zeros((w.shape[0], Z_W - used), w.dtype))
    return jnp.concatenate(parts, axis=1).astype(BF16)


CMP_HALF = CMP_BLOCK // 2
CMP_ROW = CMP_HALF * NSA_GROUPS * NSA_DH


def _n_cmp_pad(seq):
    return seq // CMP_STRIDE


def _cmp_weights(pe, w1, w2):
    g = NSA_GROUPS
    eye = jnp.eye(g, dtype=F32)
    w = w1.reshape(2, CMP_HALF, NSA_DH, CMP_HIDDEN)
    w1ab = jnp.einsum("hldf,gk->hlgdkf", w, eye).reshape(2, CMP_ROW, g * CMP_HIDDEN).astype(BF16)
    peab = jnp.broadcast_to(pe.reshape(2, CMP_HALF, 1, NSA_DH), (2, CMP_HALF, g, NSA_DH)).reshape(2, 1, CMP_ROW)
    w2bd = jnp.einsum("fd,gk->gfkd", w2, eye).reshape(g * CMP_HIDDEN, g * NSA_DH).astype(BF16)
    return peab, w1ab, w2bd


def _cmp_mlp(x, pe_ref, w1_ref, w2_ref):
    a = jnp.dot((x + pe_ref[0]).astype(BF16), w1_ref[0], preferred_element_type=F32)
    b = jnp.dot((x + pe_ref[1]).astype(BF16), w1_ref[1], preferred_element_type=F32)
    hid = a + pltpu.roll(b, b.shape[0] - 1, 0)
    hid = hid * jax.nn.sigmoid(hid)
    return jnp.dot(hid.astype(BF16), w2_ref[...], preferred_element_type=F32)


def _compress_kernel(xk_ref, xv_ref, pek_ref, w1k_ref, w2k_ref, pev_ref, w1v_ref, w2v_ref, cos_ref, sin_ref,
                     kt_ref, v_ref):
    k = _cmp_mlp(xk_ref[...], pek_ref, w1k_ref, w2k_ref)
    k = _nsa_rope(k, cos_ref[...], sin_ref[...])
    kt_ref[0] = k.T.astype(BF16)
    v_ref[0] = _cmp_mlp(xv_ref[...], pev_ref, w1v_ref, w2v_ref).astype(BF16)


def nsa_compress(xk, xv, wk, wv, cmp_cos, cmp_sin, *, batch):
    pek, w1k, w2k = wk
    pev, w1v, w2v = wv
    npad = xk.shape[0] // batch
    gw = NSA_GROUPS * NSA_DH
    c3 = lambda b: (0, 0, 0)
    c2 = lambda b: (0, 0)
    return pl.pallas_call(
        _compress_kernel,
        out_shape=(jax.ShapeDtypeStruct((batch, gw, npad), BF16), jax.ShapeDtypeStruct((batch, npad, gw), BF16)),
        grid=(batch,),
        in_specs=[pl.BlockSpec((npad, CMP_ROW), lambda b: (b, 0)),
                  pl.BlockSpec((npad, CMP_ROW), lambda b: (b, 0)),
                  pl.BlockSpec(pek.shape, c3), pl.BlockSpec(w1k.shape, c3), pl.BlockSpec(w2k.shape, c2),
                  pl.BlockSpec(pev.shape, c3), pl.BlockSpec(w1v.shape, c3), pl.BlockSpec(w2v.shape, c2),
                  pl.BlockSpec((npad, gw), lambda b: (b, 0)),
                  pl.BlockSpec((npad, gw), lambda b: (b, 0))],
        out_specs=(pl.BlockSpec((1, gw, npad), lambda b: (b, 0, 0)), pl.BlockSpec((1, npad, gw), lambda b: (b, 0, 0))),
        compiler_params=_cparams(("parallel",), 48),
        name="nsa_compress",
    )(xk, xv, pek, w1k, w2k, pev, w1v, w2v, cmp_cos, cmp_sin)


KV_TT = 512
ATT_TK = 256


def _block_onehot(seq):
    return (jnp.arange(seq)[None, :] // SEL_BLOCK == jnp.arange(seq // SEL_BLOCK)[:, None]).astype(BF16)


def _kv_prep_kernel(z_ref, cos_ref, sin_ref, e_ref, ksa_ref, kwt_ref, vsw_ref):
    cos = cos_ref[...]
    sin = sin_ref[...]
    w = NSA_GROUPS * NSA_DH
    ks = _nsa_rope(z_ref[:, 2 * w:3 * w], cos, sin)
    vs = z_ref[:, 3 * w:4 * w]
    kw = _nsa_rope(z_ref[:, 4 * w:5 * w], cos, sin)
    vw = z_ref[:, 5 * w:6 * w]
    kst = ks.T.astype(BF16)
    kwt = kw.T.astype(BF16)
    e = e_ref[...]
    for g in range(NSA_GROUPS):
        rows = slice(g * NSA_DH, (g + 1) * NSA_DH)
        ksa_ref[0, g] = jnp.concatenate([kst[rows], e], axis=0)
        kwt_ref[0, g] = kwt[rows]
        vsw_ref[0, g] = jnp.concatenate([vs[:, rows], vw[:, rows]], axis=1).astype(BF16)


def nsa_kv_prep(z, nsa_cos, nsa_sin, onehot, *, batch, seq):
    tt = KV_TT
    nt = seq // tt
    g = NSA_GROUPS
    w = g * NSA_DH
    nsel = onehot.shape[0]
    return pl.pallas_call(
        _kv_prep_kernel,
        out_shape=(jax.ShapeDtypeStruct((batch, g, NSA_DH + nsel, seq), BF16),
                   jax.ShapeDtypeStruct((batch, g, NSA_DH, seq), BF16),
                   jax.ShapeDtypeStruct((batch, g, seq, w), BF16)),
        grid=(batch, nt),
        in_specs=[pl.BlockSpec((tt, 6 * w), lambda b, i: (b * nt + i, Z_NKV // (6 * w))),
                  pl.BlockSpec((tt, w), lambda b, i: (b * nt + i, 0)),
                  pl.BlockSpec((tt, w), lambda b, i: (b * nt + i, 0)),
                  pl.BlockSpec((nsel, tt), lambda b, i: (0, i))],
        out_specs=(pl.BlockSpec((1, g, NSA_DH + nsel, tt), lambda b, i: (b, 0, 0, i)),
                   pl.BlockSpec((1, g, NSA_DH, tt), lambda b, i: (b, 0, 0, i)),
                   pl.BlockSpec((1, g, tt, w), lambda b, i: (b, 0, i, 0))),
        compiler_params=_cparams(("parallel", "parallel")),
        name="nsa_kv_prep",
    )(z, nsa_cos, nsa_sin, onehot)


ATT_TQ = 256


def _overlap_matrix(seq):
    n_cmp = (seq - CMP_BLOCK) // CMP_STRIDE + 1
    n_sel = seq // SEL_BLOCK
    ii = np.arange(_n_cmp_pad(seq))[:, None]
    jj = np.arange(LANES)[None, :]
    lo = np.maximum(ii * CMP_STRIDE, jj * SEL_BLOCK)
    hi = np.minimum(ii * CMP_STRIDE + CMP_BLOCK, (jj + 1) * SEL_BLOCK)
    ov = np.maximum(hi - lo, 0).astype(np.float32) / CMP_BLOCK
    ov = np.where((ii < n_cmp) & (jj < n_sel), ov, 0.0)
    return jnp.asarray(np.tile(ov, (NSA_HPG, 1)), BF16)


def _softmax_step(s, ok, v, m_ref, l_ref, a_ref):
    m_old = m_ref[...]
    m_new = jnp.maximum(m_old, jnp.max(s, axis=1, keepdims=True))
    alpha = jnp.exp(m_old - m_new)
    p = jnp.exp(s - m_new)
    if ok is not None:
        p = jnp.where(ok, p, 0.0)
    l_ref[...] = alpha * l_ref[...] + jnp.sum(p, axis=1, keepdims=True)
    a_ref[...] = alpha * a_ref[...] + jnp.dot(p.astype(BF16), v, preferred_element_type=F32)
    m_ref[...] = m_new


def _att_kernel(q_ref, cos_ref, sin_ref, gl_ref, kct_ref, vc_ref, ksa_ref, kwt_ref, vsw_ref, ovl_ref, o_ref,
                m_sel, l_sel, a_sel, m_win, l_win, a_win):
    g = pl.program_id(1)
    qi = pl.program_id(2)
    tq = ATT_TQ
    tk = ATT_TK
    hpg = NSA_HPG
    m_rows = hpg * tq
    t0 = qi * tq

    cos = jnp.concatenate([cos_ref[...]] * (hpg // 2), axis=1)
    sin = jnp.concatenate([sin_ref[...]] * (hpg // 2), axis=1)
    q = _nsa_rope(q_ref[...], cos, sin) * (NSA_DH ** -0.5)
    qs = jnp.concatenate([q[:, h * NSA_DH:(h + 1) * NSA_DH] for h in range(hpg)], axis=0)
    qsb = qs.astype(BF16)
    t_idx = t0 + lax.broadcasted_iota(I32, (m_rows, tk), 0) % tq
    col = lax.broadcasted_iota(I32, (m_rows, tk), 1)

    s = jnp.dot(qsb, kct_ref[0], preferred_element_type=F32)
    n_cmp = (vsw_ref.shape[2] - CMP_BLOCK) // CMP_STRIDE + 1
    ncol = lax.broadcasted_iota(I32, s.shape, 1)
    tc_idx = t0 + lax.broadcasted_iota(I32, s.shape, 0) % tq
    ok = (ncol * CMP_STRIDE + (CMP_BLOCK - 1) <= tc_idx) & (ncol < n_cmp)
    sm = jnp.where(ok, s, NEG)
    e = jnp.where(ok, jnp.exp(sm - jnp.max(sm, axis=1, keepdims=True)), 0.0)
    den = jnp.sum(e, axis=1, keepdims=True)
    pb = (e / jnp.where(den > 0.0, den, 1.0)).astype(BF16)
    oc = jnp.dot(pb, vc_ref[0], preferred_element_type=F32)
    o_cmp = jnp.where(g == 0, oc[:, :NSA_DH], oc[:, NSA_DH:])
    pcat = jnp.concatenate([pb[h * tq:(h + 1) * tq] for h in range(hpg)], axis=1)
    imp = jnp.dot(pcat, ovl_ref[...], preferred_element_type=F32)

    jl = lax.broadcasted_iota(I32, (tq, LANES), 1)
    cur = (t0 + lax.broadcasted_iota(I32, (tq, LANES), 0)) // SEL_BLOCK
    forced = (jl == 0) | (jl == cur) | (jl == cur - 1)
    impm = jnp.where(forced, jnp.inf, jnp.where(jl <= cur, imp, -jnp.inf))
    n_sel = ksa_ref.shape[2] - NSA_DH
    imp_t = impm.T[0:n_sel]
    jrow = lax.broadcasted_iota(I32, (n_sel, tq), 0)
    rank = jnp.zeros((n_sel, tq), F32)
    for i in range(n_sel):
        ri = imp_t[i:i + 1, :]
        rank = rank + jnp.where(jrow > i, jnp.where(ri >= imp_t, 1.0, 0.0), jnp.where(ri > imp_t, 1.0, 0.0))
    bias_t = jnp.where(rank < float(SEL_TOPK), 0.0, NEG)
    if n_sel < LANES:
        bias_t = jnp.concatenate([bias_t, jnp.zeros((LANES - n_sel, tq), F32)], axis=0)
    bias = bias_t.T[:, 0:n_sel]
    qa = jnp.concatenate([qs, jnp.concatenate([bias] * hpg, axis=0)], axis=1).astype(BF16)

    for m_ref, l_ref, a_ref in ((m_sel, l_sel, a_sel), (m_win, l_win, a_win)):
        m_ref[...] = jnp.full(m_ref.shape, NEG, F32)
        l_ref[...] = jnp.zeros(l_ref.shape, F32)
        a_ref[...] = jnp.zeros(a_ref.shape, F32)

    def sel_body(j, carry):
        k0 = pl.multiple_of(j * tk, tk)
        sc = jnp.dot(qa, ksa_ref[0, 0, :, pl.ds(k0, tk)], preferred_element_type=F32)
        _softmax_step(sc, None, vsw_ref[0, 0, pl.ds(k0, tk), :], m_sel, l_sel, a_sel)
        return carry

    lax.fori_loop(0, qi, sel_body, 0)
    k0 = pl.multiple_of(qi * tk, tk)
    sc = jnp.dot(qa, ksa_ref[0, 0, :, pl.ds(k0, tk)], preferred_element_type=F32)
    ok = (t0 + col) <= t_idx
    _softmax_step(jnp.where(ok, sc, NEG), ok, vsw_ref[0, 0, pl.ds(k0, tk), :], m_sel, l_sel, a_sel)

    for dj in range(WINDOW // tk, -1, -1):
        jt = qi - dj
        k0 = pl.multiple_of(jnp.maximum(jt, 0) * tk, tk)
        d = t_idx - (k0 + col)
        ok = (d >= 0) & (d < WINDOW) & (jt >= 0)
        sc = jnp.dot(qsb, kwt_ref[0, 0, :, pl.ds(k0, tk)], preferred_element_type=F32)
        _softmax_step(jnp.where(ok, sc, NEG), ok, vsw_ref[0, 0, pl.ds(k0, tk), :], m_win, l_win, a_win)

    o_slc = a_sel[:, 0:NSA_DH] / l_sel[...]
    o_win = a_win[:, NSA_DH:2 * NSA_DH] / l_win[...]
    gl = jax.nn.sigmoid(gl_ref[...])
    nb = NSA_N_BRANCH
    gs = jnp.where(g == 0, gl[:, 0:hpg * nb], gl[:, hpg * nb:2 * hpg * nb])
    outs = []
    for h in range(hpg):
        rows = slice(h * tq, (h + 1) * tq)
        outs.append(gs[:, nb * h:nb * h + 1] * o_cmp[rows] + gs[:, nb * h + 1:nb * h + 2] * o_slc[rows]
                    + gs[:, nb * h + 2:nb * h + 3] * o_win[rows])
    o_ref[...] = jnp.concatenate(outs, axis=1)


def nsa_attention(z, nsa_cos, nsa_sin, kct, vc, ksa, kwt, vsw, ovl, *, batch, seq):
    tq = ATT_TQ
    nt = seq // tq
    g = NSA_GROUPS
    gw = g * NSA_DH
    qw = NSA_HPG * NSA_DH
    m_rows = NSA_HPG * tq
    row = lambda b, gg, i: b * nt + i
    return pl.pallas_call(
        _att_kernel,
        out_shape=jax.ShapeDtypeStruct((batch * seq, g * qw), F32),
        grid=(batch, g, nt),
        in_specs=[pl.BlockSpec((tq, qw), lambda b, gg, i: (row(b, gg, i), Z_NQ // qw + gg)),
                  pl.BlockSpec((tq, gw), lambda b, gg, i: (row(b, gg, i), 0)),
                  pl.BlockSpec((tq, gw), lambda b, gg, i: (row(b, gg, i), 0)),
                  pl.BlockSpec((tq, LANES), lambda b, gg, i: (row(b, gg, i), Z_NG // LANES)),
                  pl.BlockSpec((1, NSA_DH, kct.shape[2]), lambda b, gg, i: (b, gg, 0)),
                  pl.BlockSpec((1, vc.shape[1], gw), lambda b, gg, i: (b, 0, 0)),
                  pl.BlockSpec((1, 1) + ksa.shape[2:], lambda b, gg, i: (b, gg, 0, 0)),
                  pl.BlockSpec((1, 1) + kwt.shape[2:], lambda b, gg, i: (b, gg, 0, 0)),
                  pl.BlockSpec((1, 1) + vsw.shape[2:], lambda b, gg, i: (b, gg, 0, 0)),
                  pl.BlockSpec(ovl.shape, lambda b, gg, i: (0, 0))],
        out_specs=pl.BlockSpec((tq, qw), lambda b, gg, i: (row(b, gg, i), gg)),
        scratch_shapes=[pltpu.VMEM((m_rows, 1), F32), pltpu.VMEM((m_rows, 1), F32), pltpu.VMEM((m_rows, gw), F32),
                        pltpu.VMEM((m_rows, 1), F32), pltpu.VMEM((m_rows, 1), F32), pltpu.VMEM((m_rows, gw), F32)],
        compiler_params=_cparams(("parallel", "parallel", "parallel"), 48),
        name="nsa_attention",
    )(z, nsa_cos, nsa_sin, z, kct, vc, ksa, kwt, vsw, ovl)


MERGE_TM = 256


def _layer_norm(y, g, b):
    mu = jnp.mean(y, axis=-1, keepdims=True)
    var = jnp.mean(jnp.square(y - mu), axis=-1, keepdims=True)
    return (y - mu) * lax.rsqrt(var + LN_EPS) * g + b


def _merge_kernel(x_ref, yc_ref, or_ref, on_ref, g0_ref, g1_ref, g2_ref, wr_ref, wn_ref, wo_ref, lg_ref, lb_ref, o_ref):
    y_ret = jnp.dot(or_ref[...].astype(BF16), wr_ref[...], preferred_element_type=F32)
    y_nsa = jnp.dot(on_ref[...].astype(BF16), wn_ref[...], preferred_element_type=F32)
    m = (jax.nn.sigmoid(g0_ref[...]) * yc_ref[...] + jax.nn.sigmoid(g1_ref[...]) * y_ret
         + jax.nn.sigmoid(g2_ref[...]) * y_nsa)
    h = jnp.dot(m.astype(BF16), wo_ref[...], preferred_element_type=F32)
    o_ref[...] = _layer_norm(DN_ALPHA * x_ref[...] + h, lg_ref[...], lb_ref[...])


def merge_block(x, y_conv, o_ret, o_nsa, z, ret_w_o, nsa_w_o, w_out, ln_g, ln_b):
    n = x.shape[0]
    tm = MERGE_TM
    d = D_MODEL
    rowd = lambda i: (i, 0)
    const = lambda i: (0, 0)
    return pl.pallas_call(
        _merge_kernel,
        out_shape=jax.ShapeDtypeStruct((n, d), F32),
        grid=(n // tm,),
        in_specs=[pl.BlockSpec((tm, d), rowd), pl.BlockSpec((tm, d), rowd),
                  pl.BlockSpec((tm, o_ret.shape[1]), rowd), pl.BlockSpec((tm, o_nsa.shape[1]), rowd),
                  pl.BlockSpec((tm, d), lambda i: (i, Z_MG // d)),
                  pl.BlockSpec((tm, d), lambda i: (i, Z_MG // d + 1)),
                  pl.BlockSpec((tm, d), lambda i: (i, Z_MG // d + 2)),
                  pl.BlockSpec(ret_w_o.shape, const), pl.BlockSpec(nsa_w_o.shape, const), pl.BlockSpec(w_out.shape, const),
                  pl.BlockSpec((1, d), const), pl.BlockSpec((1, d), const)],
        out_specs=pl.BlockSpec((tm, d), rowd),
        compiler_params=_cparams(("parallel",), 48),
        name="merge_ln1",
    )(x, y_conv, o_ret, o_nsa, z, z, z, ret_w_o, nsa_w_o, w_out, ln_g, ln_b)


RT_TM = 256


def _stable_rank(v):
    n = v.shape[0]
    row = lax.broadcasted_iota(I32, v.shape, 0)
    rank = jnp.zeros(v.shape, F32)
    for i in range(n):
        r = v[i:i + 1, :]
        rank = rank + jnp.where(row > i, jnp.where(r >= v, 1.0, 0.0), jnp.where(r > v, 1.0, 0.0))
    return rank


def _router_kernel(x_ref, wr_ref, b_ref, ltri_ref, utri_ref, eidx_ref, rnk_ref, wts_ref, cnt_ref, carry):
    @pl.when(pl.program_id(0) == 0)
    def _():
        carry[...] = jnp.zeros_like(carry)

    tm = x_ref.shape[0]
    ne = N_EXPERTS
    per = ne // N_EXPERT_GROUPS
    logits = jnp.dot(x_ref[...].astype(BF16), wr_ref[...], preferred_element_type=F32)
    s = jax.nn.sigmoid(logits.T[0:ne])
    sb = s + b_ref[...]
    sub = lax.broadcasted_iota(I32, (per, tm), 0)
    gscore = []
    for gi in range(N_EXPERT_GROUPS):
        v = sb[gi * per:(gi + 1) * per]
        m1 = jnp.max(v, axis=0, keepdims=True)
        first = jnp.min(jnp.where(v == m1, sub, per), axis=0, keepdims=True)
        m2 = jnp.max(jnp.where(sub == first, -jnp.inf, v), axis=0, keepdims=True)
        gscore.append(m1 + m2)
    gscore = jnp.concatenate(gscore, axis=0)
    gkeep = jnp.where(_stable_rank(gscore) < float(TOPK_GROUPS), 1.0, 0.0)
    ekeep = jnp.concatenate([jnp.broadcast_to(gkeep[gi:gi + 1], (per, tm)) for gi in range(N_EXPERT_GROUPS)], axis=0)
    sel = jnp.where(_stable_rank(jnp.where(ekeep > 0.0, sb, -jnp.inf)) < float(TOPK), 1.0, 0.0)
    ssel = s * sel
    gate = ssel / jnp.sum(ssel, axis=0, keepdims=True) * ROUTED_SCALE

    selb = sel.astype(BF16)
    slot = jnp.dot(ltri_ref[...], selb, preferred_element_type=F32)
    incl = jnp.dot(selb, utri_ref[...], preferred_element_type=F32)
    rnk = carry[...] + incl - 1.0
    carry[...] = carry[...] + incl[:, tm - 1:tm]
    erow = lax.broadcasted_iota(I32, (ne, tm), 0).astype(F32)
    es, rs, ws = [], [], []
    for k in range(TOPK):
        pick = jnp.where(slot == float(k), sel, 0.0)
        es.append(jnp.sum(pick * erow, axis=0, keepdims=True))
        rs.append(jnp.sum(pick * rnk, axis=0, keepdims=True))
        ws.append(jnp.sum(pick * gate, axis=0, keepdims=True))
    eidx_ref[...] = jnp.concatenate(es, axis=0).astype(I32)
    rnk_ref[...] = jnp.concatenate(rs, axis=0).astype(I32)
    wts_ref[...] = jnp.concatenate(ws + [jnp.zeros((LANES - TOPK, tm), F32)], axis=0).T
    cnt_ref[...] = jnp.broadcast_to(carry[...], cnt_ref.shape).astype(I32)


def moe_route(x, router_w_pad, router_b_col):
    n = x.shape[0]
    tm = RT_TM
    ne = N_EXPERTS
    ltri = jnp.asarray(np.tril(np.ones((ne, ne), np.float32), -1), BF16)
    utri = jnp.asarray(np.triu(np.ones((tm, tm), np.float32)), BF16)
    const = lambda i: (0, 0)
    return pl.pallas_call(
        _router_kernel,
        out_shape=(jax.ShapeDtypeStruct((TOPK, n), I32), jax.ShapeDtypeStruct((TOPK, n), I32),
                   jax.ShapeDtypeStruct((n, LANES), F32), jax.ShapeDtypeStruct((ne, LANES), I32)),
        grid=(n // tm,),
        in_specs=[pl.BlockSpec((tm, D_MODEL), lambda i: (i, 0)),
                  pl.BlockSpec(router_w_pad.shape, const), pl.BlockSpec((ne, 1), const),
                  pl.BlockSpec((ne, ne), const), pl.BlockSpec((tm, tm), const)],
        out_specs=(pl.BlockSpec((TOPK, tm), lambda i: (0, i)), pl.BlockSpec((TOPK, tm), lambda i: (0, i)),
                   pl.BlockSpec((tm, LANES), lambda i: (i, 0)), pl.BlockSpec((ne, LANES), const)),
        scratch_shapes=[pltpu.VMEM((ne, 1), F32)],
        compiler_params=_cparams(("arbitrary",)),
        name="moe_route",
    )(x, router_w_pad, router_b_col, ltri, utri)


def _moe_rows(n_tokens):
    return n_tokens * TOPK + N_EXPERTS * MOE_TILE


def _plan_kernel(cnt_ref, off_ref, texp_ref, nused_ref):
    shift = MOE_TILE.bit_length() - 1

    def per_expert(e, carry):
        off, ti = carry
        off_ref[e] = off
        ntile = lax.shift_right_logical(cnt_ref[e] + (MOE_TILE - 1), shift)

        def mark(j, c):
            texp_ref[ti + j] = e
            return c

        lax.fori_loop(0, ntile, mark, 0)
        return off + ntile * MOE_TILE, ti + ntile

    _, used = lax.fori_loop(0, N_EXPERTS, per_expert, (jnp.int32(0), jnp.int32(0)))
    nused_ref[0] = used

    def fill(j, c):
        texp_ref[j] = N_EXPERTS - 1
        return c

    lax.fori_loop(used, texp_ref.shape[0], fill, 0)


def moe_plan(counts, n_tokens):
    nt = _moe_rows(n_tokens) // MOE_TILE
    smem = pl.BlockSpec(memory_space=pltpu.SMEM)
    return pl.pallas_call(
        _plan_kernel,
        out_shape=(jax.ShapeDtypeStruct((N_EXPERTS,), I32), jax.ShapeDtypeStruct((nt,), I32),
                   jax.ShapeDtypeStruct((1,), I32)),
        in_specs=[smem],
        out_specs=(smem, smem, smem),
        name="moe_plan",
    )(counts)


DSP_TB = 256


def _row_copy(src_ref, src_row, dst_ref, dst_row, sem):
    return pltpu.make_async_copy(src_ref.at[pl.ds(src_row, 1)], dst_ref.at[pl.ds(dst_row, 1)], sem)


def _dispatch_kernel(eidx_ref, rnk_ref, off_ref, cnt_ref, x_ref, xs_ref, zbuf, sem, zsem):
    tb = x_ref.shape[0]

    def pad_copy(e):
        cnt = cnt_ref[e]
        rem = jnp.bitwise_and(cnt, MOE_TILE - 1)
        start = pl.multiple_of(off_ref[e] + cnt - rem, MOE_TILE)
        return rem != 0, pltpu.make_async_copy(zbuf, xs_ref.at[pl.ds(start, MOE_TILE)], zsem)

    @pl.when(pl.program_id(0) == 0)
    def _():
        zbuf[...] = jnp.zeros_like(zbuf)

        def start(e, c):
            has_pad, cp = pad_copy(e)

            @pl.when(has_pad)
            def _():
                cp.start()
            return c

        def wait(e, c):
            has_pad, cp = pad_copy(e)

            @pl.when(has_pad)
            def _():
                cp.wait()
            return c

        lax.fori_loop(0, N_EXPERTS, start, 0)
        lax.fori_loop(0, N_EXPERTS, wait, 0)

    def issue(t, c):
        for k in range(TOPK):
            dst = off_ref[eidx_ref[k, t]] + rnk_ref[k, t]
            _row_copy(x_ref, t, xs_ref, dst, sem).start()
        return c

    def drain(t, c):
        for k in range(TOPK):
            _row_copy(x_ref, 0, xs_ref, 0, sem).wait()
        return c

    lax.fori_loop(0, tb, issue, 0)
    lax.fori_loop(0, tb, drain, 0)


def moe_dispatch(x, eidx, rnk, off, counts):
    n, d = x.shape
    tb = DSP_TB
    smem_all = pl.BlockSpec(memory_space=pltpu.SMEM)
    smem_blk = pl.BlockSpec((TOPK, tb), lambda i: (0, i), memory_space=pltpu.SMEM)
    return pl.pallas_call(
        _dispatch_kernel,
        out_shape=jax.ShapeDtypeStruct((_moe_rows(n), d), x.dtype),
        grid=(n // tb,),
        in_specs=[smem_blk, smem_blk, smem_all, smem_all, pl.BlockSpec((tb, d), lambda i: (i, 0))],
        out_specs=pl.BlockSpec(memory_space=pl.ANY),
        scratch_shapes=[pltpu.VMEM((MOE_TILE, d), x.dtype), pltpu.SemaphoreType.DMA(()), pltpu.SemaphoreType.DMA(())],
        compiler_params=_cparams(("arbitrary",)),
        name="moe_dispatch",
    )(eidx, rnk, off, counts, x)


def _expert_kernel(texp_ref, nused_ref, xs_ref, w1_ref, w3_ref, w2_ref, ys_ref, w1b, w3b, w2b, last):
    i = pl.program_id(0)
    e = texp_ref[i]

    @pl.when(i == 0)
    def _():
        last[0] = -1

    @pl.when(e != last[0])
    def _():
        w1b[...] = w1_ref[0, 0].astype(BF16)
        w3b[...] = w3_ref[0, 0].astype(BF16)
        w2b[...] = w2_ref[0, 0].astype(BF16)
        last[0] = e

    @pl.when(i < nused_ref[0])
    def _():
        xb = xs_ref[...].astype(BF16)
        h1 = jnp.dot(xb, w1b[...], preferred_element_type=F32)
        h3 = jnp.dot(xb, w3b[...], preferred_element_type=F32)
        h = h1 * jax.nn.sigmoid(h1) * h3
        ys_ref[...] = jnp.dot(h.astype(BF16), w2b[...], preferred_element_type=F32)

    @pl.when(i >= nused_ref[0])
    def _():
        ys_ref[...] = jnp.zeros_like(ys_ref)


def moe_experts(xs, texp, nused, w1, w3, w2, layer):
    rows, d = xs.shape
    nt = rows // MOE_TILE
    f = w1.shape[3]
    grid_spec = pltpu.PrefetchScalarGridSpec(
        num_scalar_prefetch=2,
        grid=(nt,),
        in_specs=[pl.BlockSpec((MOE_TILE, d), lambda i, te, nu: (jnp.where(i < nu[0], i, 0), 0)),
                  pl.BlockSpec((1, 1, d, f), lambda i, te, nu: (layer, te[i], 0, 0)),
                  pl.BlockSpec((1, 1, d, f), lambda i, te, nu: (layer, te[i], 0, 0)),
                  pl.BlockSpec((1, 1, f, d), lambda i, te, nu: (layer, te[i], 0, 0))],
        out_specs=pl.BlockSpec((MOE_TILE, d), lambda i, te, nu: (i, 0)),
        scratch_shapes=[pltpu.VMEM((d, f), BF16), pltpu.VMEM((d, f), BF16), pltpu.VMEM((f, d), BF16),
                        pltpu.SMEM((1,), I32)],
    )
    return pl.pallas_call(
        _expert_kernel,
        out_shape=jax.ShapeDtypeStruct((rows, d), F32),
        grid_spec=grid_spec,
        compiler_params=_cparams(("arbitrary",), 48),
        name="moe_experts",
    )(texp, nused, xs, w1, w3, w2)


CMB_TB = 128


def _combine_kernel(eidx_ref, rnk_ref, off_ref, x_ref, wts_ref, ws1_ref, ws3_ref, ws2_ref, lg_ref, lb_ref, ys_ref,
                    o_ref, buf, sem):
    tb = x_ref.shape[0]

    def issue(t, c):
        for k in range(TOPK):
            src = off_ref[eidx_ref[k, t]] + rnk_ref[k, t]
            pltpu.make_async_copy(ys_ref.at[pl.ds(src, 1)], buf.at[k, pl.ds(t, 1)], sem).start()
        return c

    def drain(t, c):
        for k in range(TOPK):
            pltpu.make_async_copy(ys_ref.at[pl.ds(0, 1)], buf.at[0, pl.ds(0, 1)], sem).wait()
        return c

    lax.fori_loop(0, tb, issue, 0)
    x = x_ref[...]
    xb = x.astype(BF16)
    h1 = jnp.dot(xb, ws1_ref[...], preferred_element_type=F32)
    h3 = jnp.dot(xb, ws3_ref[...], preferred_element_type=F32)
    y = jnp.dot((h1 * jax.nn.sigmoid(h1) * h3).astype(BF16), ws2_ref[...], preferred_element_type=F32)
    lax.fori_loop(0, tb, drain, 0)
    w = wts_ref[...]
    routed = w[:, 0:1] * buf[0]
    for k in range(1, TOPK):
        routed = routed + w[:, k:k + 1] * buf[k]
    o_ref[...] = _layer_norm(DN_ALPHA * x + (routed + y), lg_ref[...], lb_ref[...])


def moe_combine(x, ys, eidx, rnk, off, wts, ws1, ws3, ws2, ln_g, ln_b):
    n, d = x.shape
    tb = CMB_TB
    smem_all = pl.BlockSpec(memory_space=pltpu.SMEM)
    smem_blk = pl.BlockSpec((TOPK, tb), lambda i: (0, i), memory_space=pltpu.SMEM)
    const = lambda i: (0, 0)
    return pl.pallas_call(
        _combine_kernel,
        out_shape=jax.ShapeDtypeStruct((n, d), F32),
        grid=(n // tb,),
        in_specs=[smem_blk, smem_blk, smem_all,
                  pl.BlockSpec((tb, d), lambda i: (i, 0)), pl.BlockSpec((tb, LANES), lambda i: (i, 0)),
                  pl.BlockSpec(ws1.shape, const), pl.BlockSpec(ws3.shape, const), pl.BlockSpec(ws2.shape, const),
                  pl.BlockSpec((1, d), const), pl.BlockSpec((1, d), const),
                  pl.BlockSpec(memory_space=pl.ANY)],
        out_specs=pl.BlockSpec((tb, d), lambda i: (i, 0)),
        scratch_shapes=[pltpu.VMEM((TOPK, tb, d), F32), pltpu.SemaphoreType.DMA(())],
        compiler_params=_cparams(("parallel",), 48),
        name="moe_combine_ln2",
    )(eidx, rnk, off, x, wts, ws1, ws3, ws2, ln_g, ln_b, ys)


def moe_block(x, router_w, router_b, w1, w3, w2, layer, ws1, ws3, ws2, ln_g, ln_b):
    n = x.shape[0]
    rw = jnp.pad(router_w, ((0, 0), (0, LANES - N_EXPERTS))).astype(BF16)
    eidx, rnk, wts, cnt = moe_route(x, rw, router_b.reshape(N_EXPERTS, 1))
    counts = cnt[:, 0]
    off, texp, nused = moe_plan(counts, n)
    xs = moe_dispatch(x, eidx, rnk, off, counts)
    ys = moe_experts(xs, texp, nused, w1, w3, w2, layer)
    return moe_combine(x, ys, eidx, rnk, off, wts, ws1.astype(BF16), ws3.astype(BF16), ws2.astype(BF16),
                       ln_g.reshape(1, -1), ln_b.reshape(1, -1))


def nsa_rope_tables(positions):
    batch, seq = positions.shape
    inv, sgn = _nsa_inv_freq()
    posf = positions.astype(F32)
    tok = rope_tables(posf.reshape(batch * seq, 1), inv, sgn, tm=512)
    end = posf[:, CMP_BLOCK - 1::CMP_STRIDE]
    npad = _n_cmp_pad(seq)
    end = jnp.pad(end, ((0, 0), (0, npad - end.shape[1])))
    cmp = rope_tables(end.reshape(batch * npad, 1), inv, sgn, tm=npad)
    return tok, cmp


def nsa_branch(z, tok_tab, cmp_tab, wk, wv, onehot, ovl, *, batch, seq):
    n = batch * seq
    w = NSA_GROUPS * NSA_DH
    xk = z[:, Z_NKV:Z_NKV + w].reshape(n // CMP_HALF, CMP_ROW)
    xv = z[:, Z_NKV + w:Z_NKV + 2 * w].reshape(n // CMP_HALF, CMP_ROW)
    kct, vc = nsa_compress(xk, xv, wk, wv, cmp_tab[0], cmp_tab[1], batch=batch)
    ksa, kwt, vsw = nsa_kv_prep(z, tok_tab[0], tok_tab[1], onehot, batch=batch, seq=seq)
    return nsa_attention(z, tok_tab[0], tok_tab[1], kct, vc, ksa, kwt, vsw, ovl, batch=batch, seq=seq)


def kernel(x, positions, w_in, conv_dw, conv_db, conv_ln_g, conv_ln_b, conv_w_pw, ret_w_o, nsa_pe_k, nsa_w1_k, nsa_w2_k,
           nsa_pe_v, nsa_w1_v, nsa_w2_v, nsa_w_o, w_out, ln1_g, ln1_b, router_w, router_b, moe_w1, moe_w3, moe_w2,
           shared_w1, shared_w3, shared_w2, ln2_g, ln2_b):
    batch, seq, d = x.shape
    n = batch * seq
    xf = x.reshape(n, d)
    row = lambda v: v.reshape(1, -1)

    ret_inv, ret_sgn = _ret_inv_freq()
    ret_tab = rope_tables(positions.astype(F32).reshape(n, 1), ret_inv, ret_sgn, tm=512)
    tok_tab, cmp_tab = nsa_rope_tables(positions)
    ret_consts = _ret_tables()
    onehot = _block_onehot(seq)
    ovl = _overlap_matrix(seq)

    for l in range(w_in.shape[0]):
        z = matmul(xf, _layout_w_in(w_in[l]), tm=1024, tn=512, name="in_proj")
        y_conv = conv_branch(z, conv_dw[l], row(conv_db[l]), row(conv_ln_g[l]), row(conv_ln_b[l]),
                             conv_w_pw[l].astype(BF16), batch=batch, seq=seq)
        o_ret = retention_branch(z, ret_tab[0], ret_tab[1], ret_consts, batch=batch, seq=seq)
        o_nsa = nsa_branch(z, tok_tab, cmp_tab, _cmp_weights(nsa_pe_k[l], nsa_w1_k[l], nsa_w2_k[l]),
                           _cmp_weights(nsa_pe_v[l], nsa_w1_v[l], nsa_w2_v[l]), onehot, ovl, batch=batch, seq=seq)
        x1 = merge_block(xf, y_conv, o_ret, o_nsa, z, ret_w_o[l].astype(BF16), nsa_w_o[l].astype(BF16),
                         w_out[l].astype(BF16), row(ln1_g[l]), row(ln1_b[l]))
        xf = moe_block(x1, router_w[l], router_b[l], moe_w1, moe_w3, moe_w2, l,
                       shared_w1[l], shared_w3[l], shared_w2[l], ln2_g[l], ln2_b[l])
    return xf.reshape(batch, seq, d)
```

```python
import functools
import math

import jax
import jax.numpy as jnp
import numpy as np
from jax import lax
from jax.experimental import pallas as pl
from jax.experimental.pallas import tpu as pltpu

F32 = jnp.float32
BF16 = jnp.bfloat16
I32 = jnp.int32

D_MODEL = 1024
DEPTH = 4
CONV_CH = 512
CONV_WIDTH = 31
RET_HEADS = 4
RET_DK = 128
RET_DV = 256
RET_CHUNK = 128
RET_ROPE_BASE = 10000.0
NSA_HEADS = 8
NSA_GROUPS = 2
NSA_HPG = NSA_HEADS // NSA_GROUPS
NSA_DH = 64
NSA_N_BRANCH = 3
CMP_BLOCK = 32
CMP_STRIDE = 16
CMP_HIDDEN = 256
SEL_BLOCK = 64
SEL_TOPK = 16
WINDOW = 512
ROPE_THETA = 500000.0
ROT_DIM = NSA_DH // 4
N_EXPERTS = 64
N_EXPERT_GROUPS = 8
TOPK_GROUPS = 4
TOPK = 8
D_EXPERT = 256
D_SHARED = 256
ROUTED_SCALE = 2.5
DN_ALPHA = (2.0 * DEPTH) ** 0.25
LN_EPS = 1e-5
NEG = -1e30

LANES = 128

Z_MG = 0
Z_CA = 3072
Z_CB = 3584
Z_RQ = 4096
Z_RK = 4608
Z_RV = 5120
Z_RG = 6144
Z_NQ = 7168
Z_NKV = 7680
Z_NG = 8448
Z_W = 8704

_IN_WIDTHS = (512, 512, 512, 512, 1024, 1024, 512, 128, 128, 128, 128, 128, 128, 24, 3072)
_IN_OFFS = tuple(int(v) for v in np.concatenate([[0], np.cumsum(_IN_WIDTHS)[:-1]]))

MOE_TILE = 512
MOE_ROWS_PER_TOKEN = TOPK


def _cparams(sem, vmem_mb=None):
    kw = dict(dimension_semantics=sem)
    if vmem_mb is not None:
        kw["vmem_limit_bytes"] = vmem_mb * 1024 * 1024
    return pltpu.CompilerParams(**kw)


def _mm_kernel(x_ref, w_ref, o_ref):
    o_ref[...] = jnp.dot(x_ref[...].astype(BF16), w_ref[...], preferred_element_type=F32).astype(o_ref.dtype)


def matmul(x, w, *, tm, tn, out_dtype=F32, x_col_block=0, k=None, name="mm"):
    m = x.shape[0]
    kk, n = w.shape
    if k is None:
        k = kk
    return pl.pallas_call(
        _mm_kernel,
        out_shape=jax.ShapeDtypeStruct((m, n), out_dtype),
        grid=(m // tm, n // tn),
        in_specs=[pl.BlockSpec((tm, k), lambda i, j: (i, x_col_block)),
                  pl.BlockSpec((k, tn), lambda i, j: (0, j))],
        out_specs=pl.BlockSpec((tm, tn), lambda i, j: (i, j)),
        compiler_params=_cparams(("parallel", "arbitrary"), 48),
        name=name,
    )(x, w)


def _rope_table_kernel(pos_ref, inv_ref, sgn_ref, cos_ref, sin_ref):
    ang = pos_ref[...] * inv_ref[...]
    cos_ref[...] = jnp.cos(ang)
    sin_ref[...] = jnp.sin(ang) * sgn_ref[...]


def rope_tables(pos_col, inv, sgn, *, tm):
    n = pos_col.shape[0]
    w = inv.shape[1]
    return pl.pallas_call(
        _rope_table_kernel,
        out_shape=(jax.ShapeDtypeStruct((n, w), F32), jax.ShapeDtypeStruct((n, w), F32)),
        grid=(n // tm,),
        in_specs=[pl.BlockSpec((tm, 1), lambda i: (i, 0)),
                  pl.BlockSpec((1, w), lambda i: (0, 0)),
                  pl.BlockSpec((1, w), lambda i: (0, 0))],
        out_specs=(pl.BlockSpec((tm, w), lambda i: (i, 0)), pl.BlockSpec((tm, w), lambda i: (i, 0))),
        compiler_params=_cparams(("parallel",)),
        name="rope_tables",
    )(pos_col, inv, sgn)


def _ret_inv_freq():
    inv = 1.0 / jnp.power(jnp.float32(RET_ROPE_BASE), jnp.linspace(0.0, 1.0, RET_DK // 2, dtype=F32))
    inv = jnp.concatenate([inv, inv])[None, :]
    sgn = jnp.concatenate([-jnp.ones((RET_DK // 2,), F32), jnp.ones((RET_DK // 2,), F32)])[None, :]
    return inv, sgn


def _nsa_inv_freq():
    half = ROT_DIM // 2
    inv = jnp.power(jnp.float32(ROPE_THETA), -jnp.arange(0, ROT_DIM, 2, dtype=F32) / ROT_DIM)
    z = jnp.zeros((NSA_DH - ROT_DIM,), F32)
    inv64 = jnp.concatenate([inv, inv, z])
    sgn64 = jnp.concatenate([-jnp.ones((half,), F32), jnp.ones((half,), F32), z])
    return jnp.concatenate([inv64, inv64])[None, :], jnp.concatenate([sgn64, sgn64])[None, :]


def _nsa_rope(x, cos, sin):
    w = x.shape[1]
    half = ROT_DIM // 2
    lane = lax.broadcasted_iota(I32, x.shape, 1) % NSA_DH
    partner = jnp.where(lane < half, pltpu.roll(x, w - half, 1), pltpu.roll(x, half, 1))
    return x * cos + partner * sin


CONV_TT = 256
CONV_HALO = 32


def _conv_kernel(a_ref, b_ref, ah_ref, bh_ref, dw_ref, db_ref, g_ref, be_ref, wpw_ref, o_ref, ubuf):
    i = pl.program_id(1)
    tt = a_ref.shape[0]
    u = a_ref[...] * jax.nn.sigmoid(b_ref[...])
    uh = ah_ref[...] * jax.nn.sigmoid(bh_ref[...])
    ubuf[0:CONV_HALO, :] = jnp.where(i > 0, uh, 0.0)
    ubuf[CONV_HALO:CONV_HALO + tt, :] = u
    acc = jnp.zeros((tt, CONV_CH), F32)
    base = CONV_HALO - (CONV_WIDTH - 1)
    for k in range(CONV_WIDTH):
        acc = acc + dw_ref[k:k + 1, :] * ubuf[base + k:base + k + tt, :]
    acc = acc + db_ref[...]
    mu = jnp.mean(acc, axis=-1, keepdims=True)
    var = jnp.mean(jnp.square(acc - mu), axis=-1, keepdims=True)
    y = (acc - mu) * lax.rsqrt(var + LN_EPS) * g_ref[...] + be_ref[...]
    y = y * jax.nn.sigmoid(y)
    o_ref[...] = jnp.dot(y.astype(BF16), wpw_ref[...], preferred_element_type=F32)


def conv_branch(z, dw, db, ln_g, ln_b, w_pw_bf, *, batch, seq):
    tt = CONV_TT
    nt = seq // tt
    r = tt // CONV_HALO
    ca, cb = Z_CA // CONV_CH, Z_CB // CONV_CH

    def halo_map(col):
        return lambda b, i: (jnp.maximum((b * nt + i) * r - 1, 0), col)

    return pl.pallas_call(
        _conv_kernel,
        out_shape=jax.ShapeDtypeStruct((batch * seq, D_MODEL), F32),
        grid=(batch, nt),
        in_specs=[pl.BlockSpec((tt, CONV_CH), lambda b, i: (b * nt + i, ca)),
                  pl.BlockSpec((tt, CONV_CH), lambda b, i: (b * nt + i, cb)),
                  pl.BlockSpec((CONV_HALO, CONV_CH), halo_map(ca)),
                  pl.BlockSpec((CONV_HALO, CONV_CH), halo_map(cb)),
                  pl.BlockSpec((CONV_WIDTH, CONV_CH), lambda b, i: (0, 0)),
                  pl.BlockSpec((1, CONV_CH), lambda b, i: (0, 0)),
                  pl.BlockSpec((1, CONV_CH), lambda b, i: (0, 0)),
                  pl.BlockSpec((1, CONV_CH), lambda b, i: (0, 0)),
                  pl.BlockSpec((CONV_CH, D_MODEL), lambda b, i: (0, 0))],
        out_specs=pl.BlockSpec((tt, D_MODEL), lambda b, i: (b * nt + i, 0)),
        scratch_shapes=[pltpu.VMEM((CONV_HALO + tt, CONV_CH), F32)],
        compiler_params=_cparams(("parallel", "parallel")),
        name="conv_branch",
    )(z, z, z, z, dw, db, ln_g, ln_b, w_pw_bf)


RET_TQ = 512


def _ret_tables():
    h, c = RET_HEADS, RET_CHUNK
    log_gamma = jnp.log1p(-jnp.exp2(-5.0 - jnp.arange(h, dtype=F32)))
    idx = jnp.arange(c, dtype=F32)
    diff = idx[:, None] - idx[None, :]
    dmat = jnp.where(diff >= 0, jnp.exp(log_gamma[:, None, None] * jnp.maximum(diff, 0.0)), 0.0).astype(F32)
    xi = jnp.exp(log_gamma[:, None] * (idx + 1.0)).astype(F32)
    zeta = jnp.exp(log_gamma[:, None] * (c - 1.0 - idx)).astype(F32)
    decay = jnp.exp(log_gamma * c).astype(F32)
    xi_b = jnp.broadcast_to(xi[:, :, None], (h, c, RET_DV))
    zeta_b = jnp.broadcast_to(zeta[:, :, None], (h, c, RET_DV))
    decay_b = jnp.broadcast_to(decay[:, None, None], (h, RET_DK, RET_DV))
    return dmat, xi_b, zeta_b, decay_b


def _ret_kernel(q_ref, k_ref, v_ref, g_ref, cos_ref, sin_ref, dmat_ref, xi_ref, zeta_ref, dec_ref, o_ref, r_ref):
    @pl.when(pl.program_id(1) == 0)
    def _():
        r_ref[...] = jnp.zeros_like(r_ref)

    c = RET_CHUNK
    n_chunks = q_ref.shape[0] // c
    for ci in range(n_chunks):
        rows = slice(ci * c, (ci + 1) * c)
        cos = cos_ref[rows, :]
        sin = sin_ref[rows, :]
        for h in range(RET_HEADS):
            qk_cols = slice(h * RET_DK, (h + 1) * RET_DK)
            v_cols = slice(h * RET_DV, (h + 1) * RET_DV)
            q = q_ref[rows, qk_cols]
            k = k_ref[rows, qk_cols]
            q = q * cos + pltpu.roll(q, RET_DK // 2, 1) * sin
            k = (k * cos + pltpu.roll(k, RET_DK // 2, 1) * sin) * (RET_DK ** -0.5)
            v = v_ref[rows, v_cols]
            qb = q.astype(BF16)
            kb = k.astype(BF16)
            inner = lax.dot_general(qb, kb, (((1,), (1,)), ((), ())), preferred_element_type=F32) * dmat_ref[h]
            r_old = r_ref[h]
            o = (jnp.dot(inner.astype(BF16), v.astype(BF16), preferred_element_type=F32)
                 + jnp.dot(qb, r_old.astype(BF16), preferred_element_type=F32) * xi_ref[h])
            vz = (v * zeta_ref[h]).astype(BF16)
            r_ref[h] = r_old * dec_ref[h] + jnp.dot(k.T.astype(BF16), vz, preferred_element_type=F32)
            mu = jnp.mean(o, axis=-1, keepdims=True)
            var = jnp.mean(jnp.square(o - mu), axis=-1, keepdims=True)
            on = (o - mu) * lax.rsqrt(var + LN_EPS)
            g = g_ref[rows, v_cols]
            o_ref[rows, v_cols] = g * jax.nn.sigmoid(g) * on


def retention_branch(z, ret_cos, ret_sin, tables, *, batch, seq):
    tq = RET_TQ
    nt = seq // tq
    dmat, xi_b, zeta_b, decay_b = tables
    qw = RET_HEADS * RET_DK
    vw = RET_HEADS * RET_DV
    row = lambda b, i: b * nt + i
    full3 = lambda b, i: (0, 0, 0)
    return pl.pallas_call(
        _ret_kernel,
        out_shape=jax.ShapeDtypeStruct((batch * seq, vw), F32),
        grid=(batch, nt),
        in_specs=[pl.BlockSpec((tq, qw), lambda b, i: (row(b, i), Z_RQ // qw)),
                  pl.BlockSpec((tq, qw), lambda b, i: (row(b, i), Z_RK // qw)),
                  pl.BlockSpec((tq, vw), lambda b, i: (row(b, i), Z_RV // vw)),
                  pl.BlockSpec((tq, vw), lambda b, i: (row(b, i), Z_RG // vw)),
                  pl.BlockSpec((tq, RET_DK), lambda b, i: (row(b, i), 0)),
                  pl.BlockSpec((tq, RET_DK), lambda b, i: (row(b, i), 0)),
                  pl.BlockSpec(dmat.shape, full3),
                  pl.BlockSpec(xi_b.shape, full3),
                  pl.BlockSpec(zeta_b.shape, full3),
                  pl.BlockSpec(decay_b.shape, full3)],
        out_specs=pl.BlockSpec((tq, vw), lambda b, i: (row(b, i), 0)),
        scratch_shapes=[pltpu.VMEM((RET_HEADS, RET_DK, RET_DV), F32)],
        compiler_params=_cparams(("parallel", "arbitrary"), 48),
        name="retention",
    )(z, z, z, z, ret_cos, ret_sin, dmat, xi_b, zeta_b, decay_b)


def _layout_w_in(w):
    seg = lambda i: w[:, _IN_OFFS[i]:_IN_OFFS[i] + _IN_WIDTHS[i]]
    order = (14, 0, 1, 2, 3, 4, 5, 6, 7, 8, 9, 10, 11, 12, 13)
    parts = [seg(i) for i in order]
    used = sum(_IN_WIDTHS)
    parts.append(jnp.zeros((w.shape[0], Z_W - used), w.dtype))
    return jnp.concatenate(parts, axis=1).astype(BF16)


CMP_HALF = CMP_BLOCK // 2
CMP_ROW = CMP_HALF * NSA_GROUPS * NSA_DH


def _n_cmp_pad(seq):
    return seq // CMP_STRIDE


def _cmp_weights(pe, w1, w2):
    g = NSA_GROUPS
    eye = jnp.eye(g, dtype=F32)
    w = w1.reshape(2, CMP_HALF, NSA_DH, CMP_HIDDEN)
    w1ab = jnp.einsum("hldf,gk->hlgdkf", w, eye).reshape(2, CMP_ROW, g * CMP_HIDDEN).astype(BF16)
    peab = jnp.broadcast_to(pe.reshape(2, CMP_HALF, 1, NSA_DH), (2, CMP_HALF, g, NSA_DH)).reshape(2, 1, CMP_ROW)
    w2bd = jnp.einsum("fd,gk->gfkd", w2, eye).reshape(g * CMP_HIDDEN, g * NSA_DH).astype(BF16)
    return peab, w1ab, w2bd


def _cmp_mlp(x, pe_ref, w1_ref, w2_ref):
    a = jnp.dot((x + pe_ref[0]).astype(BF16), w1_ref[0], preferred_element_type=F32)
    b = jnp.dot((x + pe_ref[1]).astype(BF16), w1_ref[1], preferred_element_type=F32)
    hid = a + pltpu.roll(b, b.shape[0] - 1, 0)
    hid = hid * jax.nn.sigmoid(hid)
    return jnp.dot(hid.astype(BF16), w2_ref[...], preferred_element_type=F32)


def _compress_kernel(xk_ref, xv_ref, pek_ref, w1k_ref, w2k_ref, pev_ref, w1v_ref, w2v_ref, cos_ref, sin_ref,
                     k_ref, vt_ref):
    k = _cmp_mlp(xk_ref[...], pek_ref, w1k_ref, w2k_ref)
    k_ref[0] = _nsa_rope(k, cos_ref[...], sin_ref[...]).astype(BF16)
    vt_ref[0] = _cmp_mlp(xv_ref[...], pev_ref, w1v_ref, w2v_ref).T.astype(BF16)


def nsa_compress(xk, xv, wk, wv, cmp_cos, cmp_sin, *, batch):
    pek, w1k, w2k = wk
    pev, w1v, w2v = wv
    npad = xk.shape[0] // batch
    gw = NSA_GROUPS * NSA_DH
    c3 = lambda b: (0, 0, 0)
    c2 = lambda b: (0, 0)
    return pl.pallas_call(
        _compress_kernel,
        out_shape=(jax.ShapeDtypeStruct((batch, npad, gw), BF16), jax.ShapeDtypeStruct((batch, gw, npad), BF16)),
        grid=(batch,),
        in_specs=[pl.BlockSpec((npad, CMP_ROW), lambda b: (b, 0)),
                  pl.BlockSpec((npad, CMP_ROW), lambda b: (b, 0)),
                  pl.BlockSpec(pek.shape, c3), pl.BlockSpec(w1k.shape, c3), pl.BlockSpec(w2k.shape, c2),
                  pl.BlockSpec(pev.shape, c3), pl.BlockSpec(w1v.shape, c3), pl.BlockSpec(w2v.shape, c2),
                  pl.BlockSpec((npad, gw), lambda b: (b, 0)),
                  pl.BlockSpec((npad, gw), lambda b: (b, 0))],
        out_specs=(pl.BlockSpec((1, npad, gw), lambda b: (b, 0, 0)), pl.BlockSpec((1, gw, npad), lambda b: (b, 0, 0))),
        compiler_params=_cparams(("parallel",), 48),
        name="nsa_compress",
    )(xk, xv, pek, w1k, w2k, pev, w1v, w2v, cmp_cos, cmp_sin)


KV_TT = 512
ATT_TK = 512


def _kv_prep_kernel(z_ref, cos_ref, sin_ref, ksa_ref, kw_ref, vswt_ref):
    tt = z_ref.shape[0]
    cos = cos_ref[...]
    sin = sin_ref[...]
    w = NSA_GROUPS * NSA_DH
    n_sel = ksa_ref.shape[3] - NSA_DH
    ks = _nsa_rope(z_ref[:, 2 * w:3 * w], cos, sin)
    vs = z_ref[:, 3 * w:4 * w]
    kw = _nsa_rope(z_ref[:, 4 * w:5 * w], cos, sin)
    vw = z_ref[:, 5 * w:6 * w]
    blk = (pl.program_id(1) * tt + lax.broadcasted_iota(I32, (tt, n_sel), 0)) // SEL_BLOCK
    onehot = jnp.where(blk == lax.broadcasted_iota(I32, (tt, n_sel), 1), 1.0, 0.0)
    for g in range(NSA_GROUPS):
        cols = slice(g * NSA_DH, (g + 1) * NSA_DH)
        ksa_ref[0, g] = jnp.concatenate([ks[:, cols], onehot], axis=1).astype(BF16)
        kw_ref[0, g] = kw[:, cols].astype(BF16)
        vswt_ref[0, g] = jnp.concatenate([vs[:, cols], vw[:, cols]], axis=1).T.astype(BF16)


def nsa_kv_prep(z, nsa_cos, nsa_sin, *, batch, seq):
    tt = KV_TT
    nt = seq // tt
    g = NSA_GROUPS
    w = g * NSA_DH
    n_sel = seq // SEL_BLOCK
    return pl.pallas_call(
        _kv_prep_kernel,
        out_shape=(jax.ShapeDtypeStruct((batch, g, seq, NSA_DH + n_sel), BF16),
                   jax.ShapeDtypeStruct((batch, g, seq, NSA_DH), BF16),
                   jax.ShapeDtypeStruct((batch, g, w, seq), BF16)),
        grid=(batch, nt),
        in_specs=[pl.BlockSpec((tt, 6 * w), lambda b, i: (b * nt + i, Z_NKV // (6 * w))),
                  pl.BlockSpec((tt, w), lambda b, i: (b * nt + i, 0)),
                  pl.BlockSpec((tt, w), lambda b, i: (b * nt + i, 0))],
        out_specs=(pl.BlockSpec((1, g, tt, NSA_DH + n_sel), lambda b, i: (b, 0, i, 0)),
                   pl.BlockSpec((1, g, tt, NSA_DH), lambda b, i: (b, 0, i, 0)),
                   pl.BlockSpec((1, g, w, tt), lambda b, i: (b, 0, 0, i))),
        compiler_params=_cparams(("parallel", "parallel")),
        name="nsa_kv_prep",
    )(z, nsa_cos, nsa_sin)


ATT_TQ = 256


def _overlap_matrix(seq):
    n_cmp = (seq - CMP_BLOCK) // CMP_STRIDE + 1
    n_sel = seq // SEL_BLOCK
    ii = np.arange(_n_cmp_pad(seq))[None, :]
    jj = np.arange(LANES)[:, None]
    lo = np.maximum(ii * CMP_STRIDE, jj * SEL_BLOCK)
    hi = np.minimum(ii * CMP_STRIDE + CMP_BLOCK, (jj + 1) * SEL_BLOCK)
    ov = np.maximum(hi - lo, 0).astype(np.float32) / CMP_BLOCK
    ov = np.where((ii < n_cmp) & (jj < n_sel), ov, 0.0)
    return jnp.asarray(np.tile(ov, (1, NSA_HPG)), BF16)


def _softmax_step(s, ok, vt, cols, m_ref, l_ref, a_ref):
    m_old = m_ref[:, cols]
    m_new = jnp.maximum(m_old, jnp.max(s, axis=0, keepdims=True))
    alpha = jnp.exp(m_old - m_new)
    p = jnp.exp(s - m_new)
    if ok is not None:
        p = jnp.where(ok, p, 0.0)
    l_ref[:, cols] = alpha * l_ref[:, cols] + jnp.sum(p, axis=0, keepdims=True)
    a_ref[:, cols] = alpha * a_ref[:, cols] + jnp.dot(vt, p.astype(BF16), preferred_element_type=F32)
    m_ref[:, cols] = m_new


def _att_kernel(q_ref, cos_ref, sin_ref, gl_ref, kc_ref, vct_ref, ksa_ref, kw_ref, vswt_ref, ovl_ref, o_ref,
                qa_s, m_sel, l_sel, a_sel):
    g = pl.program_id(1)
    qi = pl.program_id(2)
    tq = ATT_TQ
    tk = ATT_TK
    hpg = NSA_HPG
    dh = NSA_DH
    t0 = qi * tq
    seq = ksa_ref.shape[2]
    n_sel = ksa_ref.shape[3] - dh
    heads = [slice(h * tq, (h + 1) * tq) for h in range(hpg)]

    cos = jnp.concatenate([cos_ref[...]] * (hpg // 2), axis=1)
    sin = jnp.concatenate([sin_ref[...]] * (hpg // 2), axis=1)
    qt = (_nsa_rope(q_ref[...], cos, sin) * (dh ** -0.5)).T
    q_t = jnp.concatenate([qt[h * dh:(h + 1) * dh] for h in range(hpg)], axis=1)
    q_tb = q_t.astype(BF16)
    qa_s[0:dh, :] = q_tb

    zero = jnp.zeros_like(q_tb)
    q2 = jnp.where(g == 0, jnp.concatenate([q_tb, zero], axis=0), jnp.concatenate([zero, q_tb], axis=0))
    s = jnp.dot(kc_ref[0], q2, preferred_element_type=F32)
    n_cmp = (seq - CMP_BLOCK) // CMP_STRIDE + 1
    nrow = lax.broadcasted_iota(I32, s.shape, 0)
    tcol = t0 + lax.broadcasted_iota(I32, s.shape, 1) % tq
    ok = (nrow * CMP_STRIDE + (CMP_BLOCK - 1) <= tcol) & (nrow < n_cmp)
    sm = jnp.where(ok, s, NEG)
    e = jnp.where(ok, jnp.exp(sm - jnp.max(sm, axis=0, keepdims=True)), 0.0)
    den = jnp.sum(e, axis=0, keepdims=True)
    pb = (e / jnp.where(den > 0.0, den, 1.0)).astype(BF16)
    oc = jnp.dot(vct_ref[0], pb, preferred_element_type=F32)
    o_cmp = jnp.where(g == 0, oc[0:dh], oc[dh:2 * dh])
    pcat = jnp.concatenate([pb[:, hs] for hs in heads], axis=0)
    imp_t = jnp.dot(ovl_ref[...], pcat, preferred_element_type=F32)[0:n_sel]

    jrow = lax.broadcasted_iota(I32, (n_sel, tq), 0)
    cur = (t0 + lax.broadcasted_iota(I32, (n_sel, tq), 1)) // SEL_BLOCK
    forced = (jrow == 0) | (jrow == cur) | (jrow == cur - 1)
    imp_t = jnp.where(forced, jnp.inf, jnp.where(jrow <= cur, imp_t, -jnp.inf))
    rank = jnp.zeros((n_sel, tq), F32)
    for i in range(n_sel):
        ri = imp_t[i:i + 1, :]
        rank = rank + jnp.where(jrow > i, jnp.where(ri >= imp_t, 1.0, 0.0), jnp.where(ri > imp_t, 1.0, 0.0))
    bias_t = jnp.where(rank < float(SEL_TOPK), 0.0, NEG).astype(BF16)
    qa_s[dh:dh + n_sel, :] = jnp.concatenate([bias_t] * hpg, axis=1)

    m_sel[...] = jnp.full(m_sel.shape, NEG, F32)
    l_sel[...] = jnp.zeros(l_sel.shape, F32)
    a_sel[...] = jnp.zeros(a_sel.shape, F32)

    def sel_tile(k0, ok):
        kt = ksa_ref[0, 0, pl.ds(k0, tk), :]
        vt = vswt_ref[0, 0, 0:dh, pl.ds(k0, tk)]

        def scores(hs):
            sc = jnp.dot(kt, qa_s[:, hs], preferred_element_type=F32)
            return sc if ok is None else jnp.where(ok, sc, NEG)

        ahead = 2
        sc = [scores(hs) for hs in heads[:ahead]]
        for h, hs in enumerate(heads):
            if h + ahead < hpg:
                sc.append(scores(heads[h + ahead]))
            _softmax_step(sc[h], ok, vt, hs, m_sel, l_sel, a_sel)

    def sel_body(j, carry):
        sel_tile(pl.multiple_of(j * tk, tk), None)
        return carry

    n_full = t0 // tk
    lax.fori_loop(0, n_full, sel_body, 0)
    k0 = pl.multiple_of(n_full * tk, tk)
    krow = lax.broadcasted_iota(I32, (tk, tq), 0)
    qcol = lax.broadcasted_iota(I32, (tk, tq), 1)
    sel_tile(k0, (k0 + krow) <= (t0 + qcol))

    wk = WINDOW + tq
    k0 = pl.multiple_of(jnp.maximum(t0 - WINDOW, 0), tq)
    d = (t0 + lax.broadcasted_iota(I32, (wk, tq), 1)) - (k0 + lax.broadcasted_iota(I32, (wk, tq), 0))
    ok = (d >= 0) & (d < WINDOW)
    kt = kw_ref[0, 0, pl.ds(k0, wk), :]
    vt = vswt_ref[0, 0, dh:2 * dh, pl.ds(k0, wk)]
    win_scores = lambda hs: jnp.where(ok, jnp.dot(kt, qa_s[0:dh, hs], preferred_element_type=F32), NEG)
    o_win = []
    sc = win_scores(heads[0])
    for h, hs in enumerate(heads):
        nxt = win_scores(heads[h + 1]) if h + 1 < hpg else None
        p = jnp.where(ok, jnp.exp(sc - jnp.max(sc, axis=0, keepdims=True)), 0.0)
        o_win.append(jnp.dot(vt, p.astype(BF16), preferred_element_type=F32) / jnp.sum(p, axis=0, keepdims=True))
        sc = nxt
    o_win = jnp.concatenate(o_win, axis=1)

    o_slc = a_sel[...] / l_sel[...]
    gl_t = jax.nn.sigmoid(gl_ref[...]).T
    nb = NSA_N_BRANCH
    outs = []
    for h, hs in enumerate(heads):
        gate = lambda br: jnp.where(g == 0, gl_t[nb * h + br:nb * h + br + 1],
                                    gl_t[nb * (hpg + h) + br:nb * (hpg + h) + br + 1])
        outs.append(gate(0) * o_cmp[:, hs] + gate(1) * o_slc[:, hs] + gate(2) * o_win[:, hs])
    o_ref[...] = jnp.concatenate(outs, axis=0).T


def nsa_attention(z, nsa_cos, nsa_sin, kc, vct, ksa, kw, vswt, ovl, *, batch, seq):
    tq = ATT_TQ
    nt = seq // tq
    g = NSA_GROUPS
    gw = g * NSA_DH
    qw = NSA_HPG * NSA_DH
    m_cols = NSA_HPG * tq
    row = lambda b, gg, i: b * nt + i
    per_bg = lambda a: pl.BlockSpec((1, 1) + a.shape[2:], lambda b, gg, i: (b, gg, 0, 0))
    stat = pltpu.VMEM((1, m_cols), F32)
    vals = pltpu.VMEM((NSA_DH, m_cols), F32)
    return pl.pallas_call(
        _att_kernel,
        out_shape=jax.ShapeDtypeStruct((batch * seq, g * qw), F32),
        grid=(batch, g, nt),
        in_specs=[pl.BlockSpec((tq, qw), lambda b, gg, i: (row(b, gg, i), Z_NQ // qw + gg)),
                  pl.BlockSpec((tq, gw), lambda b, gg, i: (row(b, gg, i), 0)),
                  pl.BlockSpec((tq, gw), lambda b, gg, i: (row(b, gg, i), 0)),
                  pl.BlockSpec((tq, LANES), lambda b, gg, i: (row(b, gg, i), Z_NG // LANES)),
                  pl.BlockSpec((1,) + kc.shape[1:], lambda b, gg, i: (b, 0, 0)),
                  pl.BlockSpec((1,) + vct.shape[1:], lambda b, gg, i: (b, 0, 0)),
                  per_bg(ksa), per_bg(kw), per_bg(vswt),
                  pl.BlockSpec(ovl.shape, lambda b, gg, i: (0, 0))],
        out_specs=pl.BlockSpec((tq, qw), lambda b, gg, i: (row(b, gg, i), gg)),
        scratch_shapes=[pltpu.VMEM((ksa.shape[3], m_cols), BF16), stat, stat, vals],
        compiler_params=_cparams(("parallel", "parallel", "parallel"), 48),
        name="nsa_attention",
    )(z, nsa_cos, nsa_sin, z, kc, vct, ksa, kw, vswt, ovl)


MERGE_TM = 256


def _layer_norm(y, g, b):
    mu = jnp.mean(y, axis=-1, keepdims=True)
    var = jnp.mean(jnp.square(y - mu), axis=-1, keepdims=True)
    return (y - mu) * lax.rsqrt(var + LN_EPS) * g + b


def _merge_kernel(x_ref, yc_ref, or_ref, on_ref, g0_ref, g1_ref, g2_ref, wr_ref, wn_ref, wo_ref, lg_ref, lb_ref, o_ref):
    y_ret = jnp.dot(or_ref[...].astype(BF16), wr_ref[...], preferred_element_type=F32)
    y_nsa = jnp.dot(on_ref[...].astype(BF16), wn_ref[...], preferred_element_type=F32)
    m = (jax.nn.sigmoid(g0_ref[...]) * yc_ref[...] + jax.nn.sigmoid(g1_ref[...]) * y_ret
         + jax.nn.sigmoid(g2_ref[...]) * y_nsa)
    h = jnp.dot(m.astype(BF16), wo_ref[...], preferred_element_type=F32)
    o_ref[...] = _layer_norm(DN_ALPHA * x_ref[...] + h, lg_ref[...], lb_ref[...])


def merge_block(x, y_conv, o_ret, o_nsa, z, ret_w_o, nsa_w_o, w_out, ln_g, ln_b):
    n = x.shape[0]
    tm = MERGE_TM
    d = D_MODEL
    rowd = lambda i: (i, 0)
    const = lambda i: (0, 0)
    return pl.pallas_call(
        _merge_kernel,
        out_shape=jax.ShapeDtypeStruct((n, d), F32),
        grid=(n // tm,),
        in_specs=[pl.BlockSpec((tm, d), rowd), pl.BlockSpec((tm, d), rowd),
                  pl.BlockSpec((tm, o_ret.shape[1]), rowd), pl.BlockSpec((tm, o_nsa.shape[1]), rowd),
                  pl.BlockSpec((tm, d), lambda i: (i, Z_MG // d)),
                  pl.BlockSpec((tm, d), lambda i: (i, Z_MG // d + 1)),
                  pl.BlockSpec((tm, d), lambda i: (i, Z_MG // d + 2)),
                  pl.BlockSpec(ret_w_o.shape, const), pl.BlockSpec(nsa_w_o.shape, const), pl.BlockSpec(w_out.shape, const),
                  pl.BlockSpec((1, d), const), pl.BlockSpec((1, d), const)],
        out_specs=pl.BlockSpec((tm, d), rowd),
        compiler_params=_cparams(("parallel",), 48),
        name="merge_ln1",
    )(x, y_conv, o_ret, o_nsa, z, z, z, ret_w_o, nsa_w_o, w_out, ln_g, ln_b)


RT_TM = 256


def _stable_rank(v):
    n = v.shape[0]
    row = lax.broadcasted_iota(I32, v.shape, 0)
    rank = jnp.zeros(v.shape, F32)
    for i in range(n):
        r = v[i:i + 1, :]
        rank = rank + jnp.where(row > i, jnp.where(r >= v, 1.0, 0.0), jnp.where(r > v, 1.0, 0.0))
    return rank


def _router_kernel(x_ref, wr_ref, b_ref, ltri_ref, utri_ref, eidx_ref, rnk_ref, wts_ref, cnt_ref, carry):
    @pl.when(pl.program_id(0) == 0)
    def _():
        carry[...] = jnp.zeros_like(carry)

    tm = x_ref.shape[0]
    ne = N_EXPERTS
    per = ne // N_EXPERT_GROUPS
    logits = jnp.dot(x_ref[...].astype(BF16), wr_ref[...], preferred_element_type=F32)
    s = jax.nn.sigmoid(logits.T[0:ne])
    sb = s + b_ref[...]
    sub = lax.broadcasted_iota(I32, (per, tm), 0)
    gscore = []
    for gi in range(N_EXPERT_GROUPS):
        v = sb[gi * per:(gi + 1) * per]
        m1 = jnp.max(v, axis=0, keepdims=True)
        first = jnp.min(jnp.where(v == m1, sub, per), axis=0, keepdims=True)
        m2 = jnp.max(jnp.where(sub == first, -jnp.inf, v), axis=0, keepdims=True)
        gscore.append(m1 + m2)
    gscore = jnp.concatenate(gscore, axis=0)
    gkeep = jnp.where(_stable_rank(gscore) < float(TOPK_GROUPS), 1.0, 0.0)
    ekeep = jnp.concatenate([jnp.broadcast_to(gkeep[gi:gi + 1], (per, tm)) for gi in range(N_EXPERT_GROUPS)], axis=0)
    sel = jnp.where(_stable_rank(jnp.where(ekeep > 0.0, sb, -jnp.inf)) < float(TOPK), 1.0, 0.0)
    ssel = s * sel
    gate = ssel / jnp.sum(ssel, axis=0, keepdims=True) * ROUTED_SCALE

    selb = sel.astype(BF16)
    slot = jnp.dot(ltri_ref[...], selb, preferred_element_type=F32)
    incl = jnp.dot(selb, utri_ref[...], preferred_element_type=F32)
    rnk = carry[...] + incl - 1.0
    carry[...] = carry[...] + incl[:, tm - 1:tm]
    erow = lax.broadcasted_iota(I32, (ne, tm), 0).astype(F32)
    es, rs, ws = [], [], []
    for k in range(TOPK):
        pick = jnp.where(slot == float(k), sel, 0.0)
        es.append(jnp.sum(pick * erow, axis=0, keepdims=True))
        rs.append(jnp.sum(pick * rnk, axis=0, keepdims=True))
        ws.append(jnp.sum(pick * gate, axis=0, keepdims=True))
    eidx_ref[...] = jnp.concatenate(es, axis=0).astype(I32)
    rnk_ref[...] = jnp.concatenate(rs, axis=0).astype(I32)
    wts_ref[...] = jnp.concatenate(ws + [jnp.zeros((LANES - TOPK, tm), F32)], axis=0).T
    cnt_ref[...] = jnp.broadcast_to(carry[...], cnt_ref.shape).astype(I32)


def moe_route(x, router_w_pad, router_b_col):
    n = x.shape[0]
    tm = RT_TM
    ne = N_EXPERTS
    ltri = jnp.asarray(np.tril(np.ones((ne, ne), np.float32), -1), BF16)
    utri = jnp.asarray(np.triu(np.ones((tm, tm), np.float32)), BF16)
    const = lambda i: (0, 0)
    return pl.pallas_call(
        _router_kernel,
        out_shape=(jax.ShapeDtypeStruct((TOPK, n), I32), jax.ShapeDtypeStruct((TOPK, n), I32),
                   jax.ShapeDtypeStruct((n, LANES), F32), jax.ShapeDtypeStruct((ne, LANES), I32)),
        grid=(n // tm,),
        in_specs=[pl.BlockSpec((tm, D_MODEL), lambda i: (i, 0)),
                  pl.BlockSpec(router_w_pad.shape, const), pl.BlockSpec((ne, 1), const),
                  pl.BlockSpec((ne, ne), const), pl.BlockSpec((tm, tm), const)],
        out_specs=(pl.BlockSpec((TOPK, tm), lambda i: (0, i)), pl.BlockSpec((TOPK, tm), lambda i: (0, i)),
                   pl.BlockSpec((tm, LANES), lambda i: (i, 0)), pl.BlockSpec((ne, LANES), const)),
        scratch_shapes=[pltpu.VMEM((ne, 1), F32)],
        compiler_params=_cparams(("arbitrary",)),
        name="moe_route",
    )(x, router_w_pad, router_b_col, ltri, utri)


def _moe_rows(n_tokens):
    return n_tokens * TOPK + N_EXPERTS * MOE_TILE


def _plan_kernel(cnt_ref, off_ref, texp_ref, nused_ref):
    shift = MOE_TILE.bit_length() - 1

    def per_expert(e, carry):
        off, ti = carry
        off_ref[e] = off
        ntile = lax.shift_right_logical(cnt_ref[e] + (MOE_TILE - 1), shift)

        def mark(j, c):
            texp_ref[ti + j] = e
            return c

        lax.fori_loop(0, ntile, mark, 0)
        return off + ntile * MOE_TILE, ti + ntile

    _, used = lax.fori_loop(0, N_EXPERTS, per_expert, (jnp.int32(0), jnp.int32(0)))
    nused_ref[0] = used

    def fill(j, c):
        texp_ref[j] = N_EXPERTS - 1
        return c

    lax.fori_loop(used, texp_ref.shape[0], fill, 0)


def moe_plan(counts, n_tokens):
    nt = _moe_rows(n_tokens) // MOE_TILE
    smem = pl.BlockSpec(memory_space=pltpu.SMEM)
    return pl.pallas_call(
        _plan_kernel,
        out_shape=(jax.ShapeDtypeStruct((N_EXPERTS,), I32), jax.ShapeDtypeStruct((nt,), I32),
                   jax.ShapeDtypeStruct((1,), I32)),
        in_specs=[smem],
        out_specs=(smem, smem, smem),
        name="moe_plan",
    )(counts)


DSP_TB = 256


def _row_copy(src_ref, src_row, dst_ref, dst_row, sem):
    return pltpu.make_async_copy(src_ref.at[pl.ds(src_row, 1)], dst_ref.at[pl.ds(dst_row, 1)], sem)


def _dispatch_kernel(eidx_ref, rnk_ref, off_ref, cnt_ref, x_ref, xs_ref, zbuf, sem, zsem):
    tb = x_ref.shape[0]

    def pad_copy(e):
        cnt = cnt_ref[e]
        rem = jnp.bitwise_and(cnt, MOE_TILE - 1)
        start = pl.multiple_of(off_ref[e] + cnt - rem, MOE_TILE)
        return rem != 0, pltpu.make_async_copy(zbuf, xs_ref.at[pl.ds(start, MOE_TILE)], zsem)

    @pl.when(pl.program_id(0) == 0)
    def _():
        zbuf[...] = jnp.zeros_like(zbuf)

        def start(e, c):
            has_pad, cp = pad_copy(e)

            @pl.when(has_pad)
            def _():
                cp.start()
            return c

        def wait(e, c):
            has_pad, cp = pad_copy(e)

            @pl.when(has_pad)
            def _():
                cp.wait()
            return c

        lax.fori_loop(0, N_EXPERTS, start, 0)
        lax.fori_loop(0, N_EXPERTS, wait, 0)

    def issue(t, c):
        for k in range(TOPK):
            dst = off_ref[eidx_ref[k, t]] + rnk_ref[k, t]
            _row_copy(x_ref, t, xs_ref, dst, sem).start()
        return c

    def drain(t, c):
        for k in range(TOPK):
            _row_copy(x_ref, 0, xs_ref, 0, sem).wait()
        return c

    lax.fori_loop(0, tb, issue, 0)
    lax.fori_loop(0, tb, drain, 0)


def moe_dispatch(x, eidx, rnk, off, counts):
    n, d = x.shape
    tb = DSP_TB
    smem_all = pl.BlockSpec(memory_space=pltpu.SMEM)
    smem_blk = pl.BlockSpec((TOPK, tb), lambda i: (0, i), memory_space=pltpu.SMEM)
    return pl.pallas_call(
        _dispatch_kernel,
        out_shape=jax.ShapeDtypeStruct((_moe_rows(n), d), x.dtype),
        grid=(n // tb,),
        in_specs=[smem_blk, smem_blk, smem_all, smem_all, pl.BlockSpec((tb, d), lambda i: (i, 0))],
        out_specs=pl.BlockSpec(memory_space=pl.ANY),
        scratch_shapes=[pltpu.VMEM((MOE_TILE, d), x.dtype), pltpu.SemaphoreType.DMA(()), pltpu.SemaphoreType.DMA(())],
        compiler_params=_cparams(("arbitrary",)),
        name="moe_dispatch",
    )(eidx, rnk, off, counts, x)


def _expert_kernel(texp_ref, nused_ref, xs_ref, w1_ref, w3_ref, w2_ref, ys_ref, w1b, w3b, w2b, last):
    i = pl.program_id(0)
    e = texp_ref[i]

    @pl.when(i == 0)
    def _():
        last[0] = -1

    @pl.when(e != last[0])
    def _():
        w1b[...] = w1_ref[0, 0].astype(BF16)
        w3b[...] = w3_ref[0, 0].astype(BF16)
        w2b[...] = w2_ref[0, 0].astype(BF16)
        last[0] = e

    @pl.when(i < nused_ref[0])
    def _():
        xb = xs_ref[...].astype(BF16)
        h1 = jnp.dot(xb, w1b[...], preferred_element_type=F32)
        h3 = jnp.dot(xb, w3b[...], preferred_element_type=F32)
        h = h1 * jax.nn.sigmoid(h1) * h3
        ys_ref[...] = jnp.dot(h.astype(BF16), w2b[...], preferred_element_type=F32)

    @pl.when(i >= nused_ref[0])
    def _():
        ys_ref[...] = jnp.zeros_like(ys_ref)


def moe_experts(xs, texp, nused, w1, w3, w2, layer):
    rows, d = xs.shape
    nt = rows // MOE_TILE
    f = w1.shape[3]
    grid_spec = pltpu.PrefetchScalarGridSpec(
        num_scalar_prefetch=2,
        grid=(nt,),
        in_specs=[pl.BlockSpec((MOE_TILE, d), lambda i, te, nu: (jnp.where(i < nu[0], i, 0), 0)),
                  pl.BlockSpec((1, 1, d, f), lambda i, te, nu: (layer, te[i], 0, 0)),
                  pl.BlockSpec((1, 1, d, f), lambda i, te, nu: (layer, te[i], 0, 0)),
                  pl.BlockSpec((1, 1, f, d), lambda i, te, nu: (layer, te[i], 0, 0))],
        out_specs=pl.BlockSpec((MOE_TILE, d), lambda i, te, nu: (i, 0)),
        scratch_shapes=[pltpu.VMEM((d, f), BF16), pltpu.VMEM((d, f), BF16), pltpu.VMEM((f, d), BF16),
                        pltpu.SMEM((1,), I32)],
    )
    return pl.pallas_call(
        _expert_kernel,
        out_shape=jax.ShapeDtypeStruct((rows, d), F32),
        grid_spec=grid_spec,
        compiler_params=_cparams(("arbitrary",), 48),
        name="moe_experts",
    )(texp, nused, xs, w1, w3, w2)


CMB_TB = 128


def _combine_kernel(eidx_ref, rnk_ref, off_ref, x_ref, wts_ref, ws1_ref, ws3_ref, ws2_ref, lg_ref, lb_ref, ys_ref,
                    o_ref, buf, sem):
    tb = x_ref.shape[0]

    def issue(t, c):
        for k in range(TOPK):
            src = off_ref[eidx_ref[k, t]] + rnk_ref[k, t]
            pltpu.make_async_copy(ys_ref.at[pl.ds(src, 1)], buf.at[k, pl.ds(t, 1)], sem).start()
        return c

    def drain(t, c):
        for k in range(TOPK):
            pltpu.make_async_copy(ys_ref.at[pl.ds(0, 1)], buf.at[0, pl.ds(0, 1)], sem).wait()
        return c

    lax.fori_loop(0, tb, issue, 0)
    x = x_ref[...]
    xb = x.astype(BF16)
    h1 = jnp.dot(xb, ws1_ref[...], preferred_element_type=F32)
    h3 = jnp.dot(xb, ws3_ref[...], preferred_element_type=F32)
    y = jnp.dot((h1 * jax.nn.sigmoid(h1) * h3).astype(BF16), ws2_ref[...], preferred_element_type=F32)
    lax.fori_loop(0, tb, drain, 0)
    w = wts_ref[...]
    routed = w[:, 0:1] * buf[0]
    for k in range(1, TOPK):
        routed = routed + w[:, k:k + 1] * buf[k]
    o_ref[...] = _layer_norm(DN_ALPHA * x + (routed + y), lg_ref[...], lb_ref[...])


def moe_combine(x, ys, eidx, rnk, off, wts, ws1, ws3, ws2, ln_g, ln_b):
    n, d = x.shape
    tb = CMB_TB
    smem_all = pl.BlockSpec(memory_space=pltpu.SMEM)
    smem_blk = pl.BlockSpec((TOPK, tb), lambda i: (0, i), memory_space=pltpu.SMEM)
    const = lambda i: (0, 0)
    return pl.pallas_call(
        _combine_kernel,
        out_shape=jax.ShapeDtypeStruct((n, d), F32),
        grid=(n // tb,),
        in_specs=[smem_blk, smem_blk, smem_all,
                  pl.BlockSpec((tb, d), lambda i: (i, 0)), pl.BlockSpec((tb, LANES), lambda i: (i, 0)),
                  pl.BlockSpec(ws1.shape, const), pl.BlockSpec(ws3.shape, const), pl.BlockSpec(ws2.shape, const),
                  pl.BlockSpec((1, d), const), pl.BlockSpec((1, d), const),
                  pl.BlockSpec(memory_space=pl.ANY)],
        out_specs=pl.BlockSpec((tb, d), lambda i: (i, 0)),
        scratch_shapes=[pltpu.VMEM((TOPK, tb, d), F32), pltpu.SemaphoreType.DMA(())],
        compiler_params=_cparams(("parallel",), 48),
        name="moe_combine_ln2",
    )(eidx, rnk, off, x, wts, ws1, ws3, ws2, ln_g, ln_b, ys)


def moe_block(x, router_w, router_b, w1, w3, w2, layer, ws1, ws3, ws2, ln_g, ln_b):
    n = x.shape[0]
    rw = jnp.pad(router_w, ((0, 0), (0, LANES - N_EXPERTS))).astype(BF16)
    eidx, rnk, wts, cnt = moe_route(x, rw, router_b.reshape(N_EXPERTS, 1))
    counts = cnt[:, 0]
    off, texp, nused = moe_plan(counts, n)
    xs = moe_dispatch(x, eidx, rnk, off, counts)
    ys = moe_experts(xs, texp, nused, w1, w3, w2, layer)
    return moe_combine(x, ys, eidx, rnk, off, wts, ws1.astype(BF16), ws3.astype(BF16), ws2.astype(BF16),
                       ln_g.reshape(1, -1), ln_b.reshape(1, -1))


def nsa_rope_tables(positions):
    batch, seq = positions.shape
    inv, sgn = _nsa_inv_freq()
    posf = positions.astype(F32)
    tok = rope_tables(posf.reshape(batch * seq, 1), inv, sgn, tm=512)
    end = posf[:, CMP_BLOCK - 1::CMP_STRIDE]
    npad = _n_cmp_pad(seq)
    end = jnp.pad(end, ((0, 0), (0, npad - end.shape[1])))
    cmp = rope_tables(end.reshape(batch * npad, 1), inv, sgn, tm=npad)
    return tok, cmp


def nsa_branch(z, tok_tab, cmp_tab, wk, wv, ovl, *, batch, seq):
    n = batch * seq
    w = NSA_GROUPS * NSA_DH
    xk = z[:, Z_NKV:Z_NKV + w].reshape(n // CMP_HALF, CMP_ROW)
    xv = z[:, Z_NKV + w:Z_NKV + 2 * w].reshape(n // CMP_HALF, CMP_ROW)
    kc, vct = nsa_compress(xk, xv, wk, wv, cmp_tab[0], cmp_tab[1], batch=batch)
    ksa, kw, vswt = nsa_kv_prep(z, tok_tab[0], tok_tab[1], batch=batch, seq=seq)
    return nsa_attention(z, tok_tab[0], tok_tab[1], kc, vct, ksa, kw, vswt, ovl, batch=batch, seq=seq)


def kernel(x, positions, w_in, conv_dw, conv_db, conv_ln_g, conv_ln_b, conv_w_pw, ret_w_o, nsa_pe_k, nsa_w1_k, nsa_w2_k,
           nsa_pe_v, nsa_w1_v, nsa_w2_v, nsa_w_o, w_out, ln1_g, ln1_b, router_w, router_b, moe_w1, moe_w3, moe_w2,
           shared_w1, shared_w3, shared_w2, ln2_g, ln2_b):
    batch, seq, d = x.shape
    n = batch * seq
    xf = x.reshape(n, d)
    row = lambda v: v.reshape(1, -1)

    ret_inv, ret_sgn = _ret_inv_freq()
    ret_tab = rope_tables(positions.astype(F32).reshape(n, 1), ret_inv, ret_sgn, tm=512)
    tok_tab, cmp_tab = nsa_rope_tables(positions)
    ret_consts = _ret_tables()
    ovl = _overlap_matrix(seq)

    for l in range(w_in.shape[0]):
        z = matmul(xf, _layout_w_in(w_in[l]), tm=1024, tn=512, name="in_proj")
        y_conv = conv_branch(z, conv_dw[l], row(conv_db[l]), row(conv_ln_g[l]), row(conv_ln_b[l]),
                             conv_w_pw[l].astype(BF16), batch=batch, seq=seq)
        o_ret = retention_branch(z, ret_tab[0], ret_tab[1], ret_consts, batch=batch, seq=seq)
        o_nsa = nsa_branch(z, tok_tab, cmp_tab, _cmp_weights(nsa_pe_k[l], nsa_w1_k[l], nsa_w2_k[l]),
                           _cmp_weights(nsa_pe_v[l], nsa_w1_v[l], nsa_w2_v[l]), ovl, batch=batch, seq=seq)
        x1 = merge_block(xf, y_conv, o_ret, o_nsa, z, ret_w_o[l].astype(BF16), nsa_w_o[l].astype(BF16),
                         w_out[l].astype(BF16), row(ln1_g[l]), row(ln1_b[l]))
        xf = moe_block(x1, router_w[l], router_b[l], moe_w1, moe_w3, moe_w2, l,
                       shared_w1[l], shared_w3[l], shared_w2[l], ln2_g[l], ln2_b[l])
    return xf.reshape(batch, seq, d)
```

```python
import functools
import math

import jax
import jax.numpy as jnp
import numpy as np
from jax import lax
from jax.experimental import pallas as pl
from jax.experimental.pallas import tpu as pltpu

F32 = jnp.float32
BF16 = jnp.bfloat16
I32 = jnp.int32

D_MODEL = 1024
DEPTH = 4
CONV_CH = 512
CONV_WIDTH = 31
RET_HEADS = 4
RET_DK = 128
RET_DV = 256
RET_CHUNK = 128
RET_ROPE_BASE = 10000.0
NSA_HEADS = 8
NSA_GROUPS = 2
NSA_HPG = NSA_HEADS // NSA_GROUPS
NSA_DH = 64
NSA_N_BRANCH = 3
CMP_BLOCK = 32
CMP_STRIDE = 16
CMP_HIDDEN = 256
SEL_BLOCK = 64
SEL_TOPK = 16
WINDOW = 512
ROPE_THETA = 500000.0
ROT_DIM = NSA_DH // 4
N_EXPERTS = 64
N_EXPERT_GROUPS = 8
TOPK_GROUPS = 4
TOPK = 8
D_EXPERT = 256
D_SHARED = 256
ROUTED_SCALE = 2.5
DN_ALPHA = (2.0 * DEPTH) ** 0.25
LN_EPS = 1e-5
NEG = -1e30

LANES = 128

Z_MG = 0
Z_CA = 3072
Z_CB = 3584
Z_RQ = 4096
Z_RK = 4608
Z_RV = 5120
Z_RG = 6144
Z_NQ = 7168
Z_NKV = 7680
Z_NG = 8448
Z_W = 8704

_IN_WIDTHS = (512, 512, 512, 512, 1024, 1024, 512, 128, 128, 128, 128, 128, 128, 24, 3072)
_IN_OFFS = tuple(int(v) for v in np.concatenate([[0], np.cumsum(_IN_WIDTHS)[:-1]]))

MOE_TILE = 512
MOE_ROWS_PER_TOKEN = TOPK


def _cparams(sem, vmem_mb=None):
    kw = dict(dimension_semantics=sem)
    if vmem_mb is not None:
        kw["vmem_limit_bytes"] = vmem_mb * 1024 * 1024
    return pltpu.CompilerParams(**kw)


def _mm_kernel(x_ref, w_ref, o_ref):
    o_ref[...] = jnp.dot(x_ref[...].astype(BF16), w_ref[...], preferred_element_type=F32).astype(o_ref.dtype)


def matmul(x, w, *, tm, tn, out_dtype=F32, x_col_block=0, k=None, name="mm"):
    m = x.shape[0]
    kk, n = w.shape
    if k is None:
        k = kk
    return pl.pallas_call(
        _mm_kernel,
        out_shape=jax.ShapeDtypeStruct((m, n), out_dtype),
        grid=(m // tm, n // tn),
        in_specs=[pl.BlockSpec((tm, k), lambda i, j: (i, x_col_block)),
                  pl.BlockSpec((k, tn), lambda i, j: (0, j))],
        out_specs=pl.BlockSpec((tm, tn), lambda i, j: (i, j)),
        compiler_params=_cparams(("parallel", "arbitrary"), 48),
        name=name,
    )(x, w)


def _rope_table_kernel(pos_ref, inv_ref, sgn_ref, cos_ref, sin_ref):
    ang = pos_ref[...] * inv_ref[...]
    cos_ref[...] = jnp.cos(ang)
    sin_ref[...] = jnp.sin(ang) * sgn_ref[...]


def rope_tables(pos_col, inv, sgn, *, tm):
    n = pos_col.shape[0]
    w = inv.shape[1]
    return pl.pallas_call(
        _rope_table_kernel,
        out_shape=(jax.ShapeDtypeStruct((n, w), F32), jax.ShapeDtypeStruct((n, w), F32)),
        grid=(n // tm,),
        in_specs=[pl.BlockSpec((tm, 1), lambda i: (i, 0)),
                  pl.BlockSpec((1, w), lambda i: (0, 0)),
                  pl.BlockSpec((1, w), lambda i: (0, 0))],
        out_specs=(pl.BlockSpec((tm, w), lambda i: (i, 0)), pl.BlockSpec((tm, w), lambda i: (i, 0))),
        compiler_params=_cparams(("parallel",)),
        name="rope_tables",
    )(pos_col, inv, sgn)


def _ret_inv_freq():
    inv = 1.0 / jnp.power(jnp.float32(RET_ROPE_BASE), jnp.linspace(0.0, 1.0, RET_DK // 2, dtype=F32))
    inv = jnp.concatenate([inv, inv])[None, :]
    sgn = jnp.concatenate([-jnp.ones((RET_DK // 2,), F32), jnp.ones((RET_DK // 2,), F32)])[None, :]
    return inv, sgn


def _nsa_inv_freq():
    half = ROT_DIM // 2
    inv = jnp.power(jnp.float32(ROPE_THETA), -jnp.arange(0, ROT_DIM, 2, dtype=F32) / ROT_DIM)
    z = jnp.zeros((NSA_DH - ROT_DIM,), F32)
    inv64 = jnp.concatenate([inv, inv, z])
    sgn64 = jnp.concatenate([-jnp.ones((half,), F32), jnp.ones((half,), F32), z])
    return jnp.concatenate([inv64, inv64])[None, :], jnp.concatenate([sgn64, sgn64])[None, :]


def _nsa_rope(x, cos, sin):
    w = x.shape[1]
    half = ROT_DIM // 2
    lane = lax.broadcasted_iota(I32, x.shape, 1) % NSA_DH
    partner = jnp.where(lane < half, pltpu.roll(x, w - half, 1), pltpu.roll(x, half, 1))
    return x * cos + partner * sin


CONV_TT = 256
CONV_HALO = 32


def _conv_kernel(a_ref, b_ref, ah_ref, bh_ref, dw_ref, db_ref, g_ref, be_ref, wpw_ref, o_ref, ubuf):
    i = pl.program_id(1)
    tt = a_ref.shape[0]
    u = a_ref[...] * jax.nn.sigmoid(b_ref[...])
    uh = ah_ref[...] * jax.nn.sigmoid(bh_ref[...])
    ubuf[0:CONV_HALO, :] = jnp.where(i > 0, uh, 0.0)
    ubuf[CONV_HALO:CONV_HALO + tt, :] = u
    acc = jnp.zeros((tt, CONV_CH), F32)
    base = CONV_HALO - (CONV_WIDTH - 1)
    for k in range(CONV_WIDTH):
        acc = acc + dw_ref[k:k + 1, :] * ubuf[base + k:base + k + tt, :]
    acc = acc + db_ref[...]
    mu = jnp.mean(acc, axis=-1, keepdims=True)
    var = jnp.mean(jnp.square(acc - mu), axis=-1, keepdims=True)
    y = (acc - mu) * lax.rsqrt(var + LN_EPS) * g_ref[...] + be_ref[...]
    y = y * jax.nn.sigmoid(y)
    o_ref[...] = jnp.dot(y.astype(BF16), wpw_ref[...], preferred_element_type=F32)


def conv_branch(z, dw, db, ln_g, ln_b, w_pw_bf, *, batch, seq):
    tt = CONV_TT
    nt = seq // tt
    r = tt // CONV_HALO
    ca, cb = Z_CA // CONV_CH, Z_CB // CONV_CH

    def halo_map(col):
        return lambda b, i: (jnp.maximum((b * nt + i) * r - 1, 0), col)

    return pl.pallas_call(
        _conv_kernel,
        out_shape=jax.ShapeDtypeStruct((batch * seq, D_MODEL), F32),
        grid=(batch, nt),
        in_specs=[pl.BlockSpec((tt, CONV_CH), lambda b, i: (b * nt + i, ca)),
                  pl.BlockSpec((tt, CONV_CH), lambda b, i: (b * nt + i, cb)),
                  pl.BlockSpec((CONV_HALO, CONV_CH), halo_map(ca)),
                  pl.BlockSpec((CONV_HALO, CONV_CH), halo_map(cb)),
                  pl.BlockSpec((CONV_WIDTH, CONV_CH), lambda b, i: (0, 0)),
                  pl.BlockSpec((1, CONV_CH), lambda b, i: (0, 0)),
                  pl.BlockSpec((1, CONV_CH), lambda b, i: (0, 0)),
                  pl.BlockSpec((1, CONV_CH), lambda b, i: (0, 0)),
                  pl.BlockSpec((CONV_CH, D_MODEL), lambda b, i: (0, 0))],
        out_specs=pl.BlockSpec((tt, D_MODEL), lambda b, i: (b * nt + i, 0)),
        scratch_shapes=[pltpu.VMEM((CONV_HALO + tt, CONV_CH), F32)],
        compiler_params=_cparams(("parallel", "parallel")),
        name="conv_branch",
    )(z, z, z, z, dw, db, ln_g, ln_b, w_pw_bf)


RET_TQ = 512


def _ret_tables():
    h, c = RET_HEADS, RET_CHUNK
    log_gamma = jnp.log1p(-jnp.exp2(-5.0 - jnp.arange(h, dtype=F32)))
    idx = jnp.arange(c, dtype=F32)
    diff = idx[:, None] - idx[None, :]
    dmat = jnp.where(diff >= 0, jnp.exp(log_gamma[:, None, None] * jnp.maximum(diff, 0.0)), 0.0).astype(F32)
    xi = jnp.exp(log_gamma[:, None] * (idx + 1.0)).astype(F32)
    zeta = jnp.exp(log_gamma[:, None] * (c - 1.0 - idx)).astype(F32)
    decay = jnp.exp(log_gamma * c).astype(F32)
    xi_b = jnp.broadcast_to(xi[:, :, None], (h, c, RET_DV))
    zeta_b = jnp.broadcast_to(zeta[:, :, None], (h, c, RET_DV))
    decay_b = jnp.broadcast_to(decay[:, None, None], (h, RET_DK, RET_DV))
    return dmat, xi_b, zeta_b, decay_b


def _ret_kernel(q_ref, k_ref, v_ref, g_ref, cos_ref, sin_ref, dmat_ref, xi_ref, zeta_ref, dec_ref, o_ref, r_ref):
    @pl.when(pl.program_id(1) == 0)
    def _():
        r_ref[...] = jnp.zeros_like(r_ref)

    c = RET_CHUNK
    n_chunks = q_ref.shape[0] // c
    for ci in range(n_chunks):
        rows = slice(ci * c, (ci + 1) * c)
        cos = cos_ref[rows, :]
        sin = sin_ref[rows, :]
        for h in range(RET_HEADS):
            qk_cols = slice(h * RET_DK, (h + 1) * RET_DK)
            v_cols = slice(h * RET_DV, (h + 1) * RET_DV)
            q = q_ref[rows, qk_cols]
            k = k_ref[rows, qk_cols]
            q = q * cos + pltpu.roll(q, RET_DK // 2, 1) * sin
            k = (k * cos + pltpu.roll(k, RET_DK // 2, 1) * sin) * (RET_DK ** -0.5)
            v = v_ref[rows, v_cols]
            qb = q.astype(BF16)
            kb = k.astype(BF16)
            inner = lax.dot_general(qb, kb, (((1,), (1,)), ((), ())), preferred_element_type=F32) * dmat_ref[h]
            r_old = r_ref[h]
            o = (jnp.dot(inner.astype(BF16), v.astype(BF16), preferred_element_type=F32)
                 + jnp.dot(qb, r_old.astype(BF16), preferred_element_type=F32) * xi_ref[h])
            vz = (v * zeta_ref[h]).astype(BF16)
            r_ref[h] = r_old * dec_ref[h] + jnp.dot(k.T.astype(BF16), vz, preferred_element_type=F32)
            mu = jnp.mean(o, axis=-1, keepdims=True)
            var = jnp.mean(jnp.square(o - mu), axis=-1, keepdims=True)
            on = (o - mu) * lax.rsqrt(var + LN_EPS)
            g = g_ref[rows, v_cols]
            o_ref[rows, v_cols] = g * jax.nn.sigmoid(g) * on


def retention_branch(z, ret_cos, ret_sin, tables, *, batch, seq):
    tq = RET_TQ
    nt = seq // tq
    dmat, xi_b, zeta_b, decay_b = tables
    qw = RET_HEADS * RET_DK
    vw = RET_HEADS * RET_DV
    row = lambda b, i: b * nt + i
    full3 = lambda b, i: (0, 0, 0)
    return pl.pallas_call(
        _ret_kernel,
        out_shape=jax.ShapeDtypeStruct((batch * seq, vw), F32),
        grid=(batch, nt),
        in_specs=[pl.BlockSpec((tq, qw), lambda b, i: (row(b, i), Z_RQ // qw)),
                  pl.BlockSpec((tq, qw), lambda b, i: (row(b, i), Z_RK // qw)),
                  pl.BlockSpec((tq, vw), lambda b, i: (row(b, i), Z_RV // vw)),
                  pl.BlockSpec((tq, vw), lambda b, i: (row(b, i), Z_RG // vw)),
                  pl.BlockSpec((tq, RET_DK), lambda b, i: (row(b, i), 0)),
                  pl.BlockSpec((tq, RET_DK), lambda b, i: (row(b, i), 0)),
                  pl.BlockSpec(dmat.shape, full3),
                  pl.BlockSpec(xi_b.shape, full3),
                  pl.BlockSpec(zeta_b.shape, full3),
                  pl.BlockSpec(decay_b.shape, full3)],
        out_specs=pl.BlockSpec((tq, vw), lambda b, i: (row(b, i), 0)),
        scratch_shapes=[pltpu.VMEM((RET_HEADS, RET_DK, RET_DV), F32)],
        compiler_params=_cparams(("parallel", "arbitrary"), 48),
        name="retention",
    )(z, z, z, z, ret_cos, ret_sin, dmat, xi_b, zeta_b, decay_b)


def _layout_w_in(w):
    seg = lambda i: w[:, _IN_OFFS[i]:_IN_OFFS[i] + _IN_WIDTHS[i]]
    order = (14, 0, 1, 2, 3, 4, 5, 6, 7, 8, 9, 10, 11, 12, 13)
    parts = [seg(i) for i in order]
    used = sum(_IN_WIDTHS)
    parts.append(jnp.zeros((w.shape[0], Z_W - used), w.dtype))
    return jnp.concatenate(parts, axis=1).astype(BF16)


CMP_HALF = CMP_BLOCK // 2
CMP_ROW = CMP_HALF * NSA_GROUPS * NSA_DH


def _n_cmp_pad(seq):
    return seq // CMP_STRIDE


def _cmp_weights(pe, w1, w2):
    g = NSA_GROUPS
    eye = jnp.eye(g, dtype=F32)
    w = w1.reshape(2, CMP_HALF, NSA_DH, CMP_HIDDEN)
    w1ab = jnp.einsum("hldf,gk->hlgdkf", w, eye).reshape(2, CMP_ROW, g * CMP_HIDDEN).astype(BF16)
    peab = jnp.broadcast_to(pe.reshape(2, CMP_HALF, 1, NSA_DH), (2, CMP_HALF, g, NSA_DH)).reshape(2, 1, CMP_ROW)
    w2bd = jnp.einsum("fd,gk->gfkd", w2, eye).reshape(g * CMP_HIDDEN, g * NSA_DH).astype(BF16)
    return peab, w1ab, w2bd


def _cmp_mlp(x, pe_ref, w1_ref, w2_ref):
    a = jnp.dot((x + pe_ref[0]).astype(BF16), w1_ref[0], preferred_element_type=F32)
    b = jnp.dot((x + pe_ref[1]).astype(BF16), w1_ref[1], preferred_element_type=F32)
    hid = a + pltpu.roll(b, b.shape[0] - 1, 0)
    hid = hid * jax.nn.sigmoid(hid)
    return jnp.dot(hid.astype(BF16), w2_ref[...], preferred_element_type=F32)


def _compress_kernel(xk_ref, xv_ref, pek_ref, w1k_ref, w2k_ref, pev_ref, w1v_ref, w2v_ref, cos_ref, sin_ref,
                     k_ref, vt_ref):
    k = _cmp_mlp(xk_ref[...], pek_ref, w1k_ref, w2k_ref)
    k_ref[0] = _nsa_rope(k, cos_ref[...], sin_ref[...]).astype(BF16)
    vt_ref[0] = _cmp_mlp(xv_ref[...], pev_ref, w1v_ref, w2v_ref).T.astype(BF16)


def nsa_compress(xk, xv, wk, wv, cmp_cos, cmp_sin, *, batch):
    pek, w1k, w2k = wk
    pev, w1v, w2v = wv
    npad = xk.shape[0] // batch
    gw = NSA_GROUPS * NSA_DH
    c3 = lambda b: (0, 0, 0)
    c2 = lambda b: (0, 0)
    return pl.pallas_call(
        _compress_kernel,
        out_shape=(jax.ShapeDtypeStruct((batch, npad, gw), BF16), jax.ShapeDtypeStruct((batch, gw, npad), BF16)),
        grid=(batch,),
        in_specs=[pl.BlockSpec((npad, CMP_ROW), lambda b: (b, 0)),
                  pl.BlockSpec((npad, CMP_ROW), lambda b: (b, 0)),
                  pl.BlockSpec(pek.shape, c3), pl.BlockSpec(w1k.shape, c3), pl.BlockSpec(w2k.shape, c2),
                  pl.BlockSpec(pev.shape, c3), pl.BlockSpec(w1v.shape, c3), pl.BlockSpec(w2v.shape, c2),
                  pl.BlockSpec((npad, gw), lambda b: (b, 0)),
                  pl.BlockSpec((npad, gw), lambda b: (b, 0))],
        out_specs=(pl.BlockSpec((1, npad, gw), lambda b: (b, 0, 0)), pl.BlockSpec((1, gw, npad), lambda b: (b, 0, 0))),
        compiler_params=_cparams(("parallel",), 48),
        name="nsa_compress",
    )(xk, xv, pek, w1k, w2k, pev, w1v, w2v, cmp_cos, cmp_sin)


KV_TT = 512
ATT_TK = 512
ATT_KB = 128
ATT_AHEAD = 8
LOG2E = math.log2(math.e)


def _kv_prep_kernel(z_ref, cos_ref, sin_ref, ksa_ref, kw_ref, vswt_ref):
    tt = z_ref.shape[0]
    cos = cos_ref[...]
    sin = sin_ref[...]
    w = NSA_GROUPS * NSA_DH
    n_sel = ksa_ref.shape[3] - NSA_DH
    ks = _nsa_rope(z_ref[:, 2 * w:3 * w], cos, sin)
    vs = z_ref[:, 3 * w:4 * w]
    kw = _nsa_rope(z_ref[:, 4 * w:5 * w], cos, sin)
    vw = z_ref[:, 5 * w:6 * w]
    blk = (pl.program_id(1) * tt + lax.broadcasted_iota(I32, (tt, n_sel), 0)) // SEL_BLOCK
    onehot = jnp.where(blk == lax.broadcasted_iota(I32, (tt, n_sel), 1), 1.0, 0.0)
    for g in range(NSA_GROUPS):
        cols = slice(g * NSA_DH, (g + 1) * NSA_DH)
        ksa_ref[0, g] = jnp.concatenate([ks[:, cols], onehot], axis=1).astype(BF16)
        kw_ref[0, g] = kw[:, cols].astype(BF16)
        vswt_ref[0, g] = jnp.concatenate([vs[:, cols], vw[:, cols]], axis=1).T.astype(BF16)


def nsa_kv_prep(z, nsa_cos, nsa_sin, *, batch, seq):
    tt = KV_TT
    nt = seq // tt
    g = NSA_GROUPS
    w = g * NSA_DH
    n_sel = seq // SEL_BLOCK
    return pl.pallas_call(
        _kv_prep_kernel,
        out_shape=(jax.ShapeDtypeStruct((batch, g, seq, NSA_DH + n_sel), BF16),
                   jax.ShapeDtypeStruct((batch, g, seq, NSA_DH), BF16),
                   jax.ShapeDtypeStruct((batch, g, w, seq), BF16)),
        grid=(batch, nt),
        in_specs=[pl.BlockSpec((tt, 6 * w), lambda b, i: (b * nt + i, Z_NKV // (6 * w))),
                  pl.BlockSpec((tt, w), lambda b, i: (b * nt + i, 0)),
                  pl.BlockSpec((tt, w), lambda b, i: (b * nt + i, 0))],
        out_specs=(pl.BlockSpec((1, g, tt, NSA_DH + n_sel), lambda b, i: (b, 0, i, 0)),
                   pl.BlockSpec((1, g, tt, NSA_DH), lambda b, i: (b, 0, i, 0)),
                   pl.BlockSpec((1, g, w, tt), lambda b, i: (b, 0, 0, i))),
        compiler_params=_cparams(("parallel", "parallel")),
        name="nsa_kv_prep",
    )(z, nsa_cos, nsa_sin)


ATT_TQ = 256


def _overlap_matrix(seq):
    n_cmp = (seq - CMP_BLOCK) // CMP_STRIDE + 1
    n_sel = seq // SEL_BLOCK
    ii = np.arange(_n_cmp_pad(seq))[None, :]
    jj = np.arange(LANES)[:, None]
    lo = np.maximum(ii * CMP_STRIDE, jj * SEL_BLOCK)
    hi = np.minimum(ii * CMP_STRIDE + CMP_BLOCK, (jj + 1) * SEL_BLOCK)
    ov = np.maximum(hi - lo, 0).astype(np.float32) / CMP_BLOCK
    ov = np.where((ii < n_cmp) & (jj < n_sel), ov, 0.0)
    return jnp.asarray(np.tile(ov, (1, NSA_HPG)), BF16)


def _softmax_step(s, ok, vt, cols, m_ref, l_ref, a_ref):
    m_old = m_ref[:, cols]
    m_new = jnp.maximum(m_old, jnp.max(s, axis=0, keepdims=True))
    alpha = jnp.exp2(m_old - m_new)
    p = jnp.exp2(s - m_new)
    if ok is not None:
        p = jnp.where(ok, p, 0.0)
    l_ref[:, cols] = alpha * l_ref[:, cols] + jnp.sum(p, axis=0, keepdims=True)
    a_ref[:, cols] = alpha * a_ref[:, cols] + jnp.dot(vt, p.astype(BF16), preferred_element_type=F32)
    m_ref[:, cols] = m_new


def _att_kernel(q_ref, cos_ref, sin_ref, gl_ref, kc_ref, vct_ref, ksa_ref, kw_ref, vswt_ref, ovl_ref, o_ref,
                qa_s, m_sel, l_sel, a_sel, m_win, l_win, a_win):
    g = pl.program_id(1)
    qi = pl.program_id(2)
    tq = ATT_TQ
    tk = ATT_TK
    hpg = NSA_HPG
    dh = NSA_DH
    t0 = qi * tq
    seq = ksa_ref.shape[2]
    n_sel = ksa_ref.shape[3] - dh
    heads = [slice(h * tq, (h + 1) * tq) for h in range(hpg)]

    cos = jnp.concatenate([cos_ref[...]] * (hpg // 2), axis=1)
    sin = jnp.concatenate([sin_ref[...]] * (hpg // 2), axis=1)
    qt = (_nsa_rope(q_ref[...], cos, sin) * (dh ** -0.5 * LOG2E)).T
    q_t = jnp.concatenate([qt[h * dh:(h + 1) * dh] for h in range(hpg)], axis=1)
    q_tb = q_t.astype(BF16)
    qa_s[0:dh, :] = q_tb

    zero = jnp.zeros_like(q_tb)
    q2 = jnp.where(g == 0, jnp.concatenate([q_tb, zero], axis=0), jnp.concatenate([zero, q_tb], axis=0))
    s = jnp.dot(kc_ref[0], q2, preferred_element_type=F32)
    n_cmp = (seq - CMP_BLOCK) // CMP_STRIDE + 1
    nrow = lax.broadcasted_iota(I32, s.shape, 0)
    tcol = t0 + lax.broadcasted_iota(I32, s.shape, 1) % tq
    ok = (nrow * CMP_STRIDE + (CMP_BLOCK - 1) <= tcol) & (nrow < n_cmp)
    sm = jnp.where(ok, s, NEG)
    e = jnp.where(ok, jnp.exp2(sm - jnp.max(sm, axis=0, keepdims=True)), 0.0)
    den = jnp.sum(e, axis=0, keepdims=True)
    pb = (e / jnp.where(den > 0.0, den, 1.0)).astype(BF16)
    oc = jnp.dot(vct_ref[0], pb, preferred_element_type=F32)
    o_cmp = jnp.where(g == 0, oc[0:dh], oc[dh:2 * dh])
    pcat = jnp.concatenate([pb[:, hs] for hs in heads], axis=0)
    imp_t = jnp.dot(ovl_ref[...], pcat, preferred_element_type=F32)[0:n_sel]

    jrow = lax.broadcasted_iota(I32, (n_sel, tq), 0)
    cur = (t0 + lax.broadcasted_iota(I32, (n_sel, tq), 1)) // SEL_BLOCK
    forced = (jrow == 0) | (jrow == cur) | (jrow == cur - 1)
    imp_t = jnp.where(forced, jnp.inf, jnp.where(jrow <= cur, imp_t, -jnp.inf))
    rank = jnp.zeros((n_sel, tq), F32)
    for i in range(n_sel):
        ri = imp_t[i:i + 1, :]
        rank = rank + jnp.where(jrow > i, jnp.where(ri >= imp_t, 1.0, 0.0), jnp.where(ri > imp_t, 1.0, 0.0))
    bias_t = jnp.where(rank < float(SEL_TOPK), 0.0, NEG).astype(BF16)
    qa_s[dh:dh + n_sel, :] = jnp.concatenate([bias_t] * hpg, axis=1)

    m_sel[...] = jnp.full(m_sel.shape, NEG, F32)
    l_sel[...] = jnp.zeros(l_sel.shape, F32)
    a_sel[...] = jnp.zeros(a_sel.shape, F32)

    kb = ATT_KB
    krow = lax.broadcasted_iota(I32, (kb, tq), 0)
    qtime = t0 + lax.broadcasted_iota(I32, (kb, tq), 1)

    def run_blocks(n_keys, scores, mask, vt_of, m_ref, l_ref, a_ref):
        blocks = [(kk, hs) for kk in range(0, n_keys, kb) for hs in heads]
        sc = [scores(kk, hs) for kk, hs in blocks[:ATT_AHEAD]]
        for b, (kk, hs) in enumerate(blocks):
            if b + ATT_AHEAD < len(blocks):
                sc.append(scores(*blocks[b + ATT_AHEAD]))
            ok = mask(kk)
            s_b = sc[b] if ok is None else jnp.where(ok, sc[b], NEG)
            sc[b] = None
            _softmax_step(s_b, ok, vt_of(kk), hs, m_ref, l_ref, a_ref)

    def sel_tile(k0, causal):
        scores = lambda kk, hs: jnp.dot(ksa_ref[0, 0, pl.ds(k0 + kk, kb), :], qa_s[:, hs], preferred_element_type=F32)
        mask = (lambda kk: (k0 + kk + krow) <= qtime) if causal else (lambda kk: None)
        run_blocks(tk, scores, mask, lambda kk: vswt_ref[0, 0, 0:dh, pl.ds(k0 + kk, kb)], m_sel, l_sel, a_sel)

    def sel_body(j, carry):
        sel_tile(pl.multiple_of(j * tk, tk), False)
        return carry

    n_full = t0 // tk
    lax.fori_loop(0, n_full, sel_body, 0)
    sel_tile(pl.multiple_of(n_full * tk, tk), True)

    k0 = pl.multiple_of(jnp.maximum(t0 - WINDOW, 0), tq)

    def win_mask(kk):
        dist = qtime - (k0 + kk + krow)
        return (dist >= 0) & (dist < WINDOW)

    m_win[...] = jnp.full(m_win.shape, NEG, F32)
    l_win[...] = jnp.zeros(l_win.shape, F32)
    a_win[...] = jnp.zeros(a_win.shape, F32)
    run_blocks(WINDOW + tq,
               lambda kk, hs: jnp.dot(kw_ref[0, 0, pl.ds(k0 + kk, kb), :], qa_s[0:dh, hs], preferred_element_type=F32),
               win_mask, lambda kk: vswt_ref[0, 0, dh:2 * dh, pl.ds(k0 + kk, kb)], m_win, l_win, a_win)

    o_win = a_win[...] / l_win[...]
    o_slc = a_sel[...] / l_sel[...]
    gl_t = jax.nn.sigmoid(gl_ref[...]).T
    nb = NSA_N_BRANCH
    outs = []
    for h, hs in enumerate(heads):
        gate = lambda br: jnp.where(g == 0, gl_t[nb * h + br:nb * h + br + 1],
                                    gl_t[nb * (hpg + h) + br:nb * (hpg + h) + br + 1])
        outs.append(gate(0) * o_cmp[:, hs] + gate(1) * o_slc[:, hs] + gate(2) * o_win[:, hs])
    o_ref[...] = jnp.concatenate(outs, axis=0).T


def nsa_attention(z, nsa_cos, nsa_sin, kc, vct, ksa, kw, vswt, ovl, *, batch, seq):
    tq = ATT_TQ
    nt = seq // tq
    g = NSA_GROUPS
    gw = g * NSA_DH
    qw = NSA_HPG * NSA_DH
    m_cols = NSA_HPG * tq
    row = lambda b, gg, i: b * nt + i
    per_bg = lambda a: pl.BlockSpec((1, 1) + a.shape[2:], lambda b, gg, i: (b, gg, 0, 0))
    stat = pltpu.VMEM((1, m_cols), F32)
    vals = pltpu.VMEM((NSA_DH, m_cols), F32)
    return pl.pallas_call(
        _att_kernel,
        out_shape=jax.ShapeDtypeStruct((batch * seq, g * qw), F32),
        grid=(batch, g, nt),
        in_specs=[pl.BlockSpec((tq, qw), lambda b, gg, i: (row(b, gg, i), Z_NQ // qw + gg)),
                  pl.BlockSpec((tq, gw), lambda b, gg, i: (row(b, gg, i), 0)),
                  pl.BlockSpec((tq, gw), lambda b, gg, i: (row(b, gg, i), 0)),
                  pl.BlockSpec((tq, LANES), lambda b, gg, i: (row(b, gg, i), Z_NG // LANES)),
                  pl.BlockSpec((1,) + kc.shape[1:], lambda b, gg, i: (b, 0, 0)),
                  pl.BlockSpec((1,) + vct.shape[1:], lambda b, gg, i: (b, 0, 0)),
                  per_bg(ksa), per_bg(kw), per_bg(vswt),
                  pl.BlockSpec(ovl.shape, lambda b, gg, i: (0, 0))],
        out_specs=pl.BlockSpec((tq, qw), lambda b, gg, i: (row(b, gg, i), gg)),
        scratch_shapes=[pltpu.VMEM((ksa.shape[3], m_cols), BF16), stat, stat, vals, stat, stat, vals],
        compiler_params=_cparams(("parallel", "parallel", "parallel"), 48),
        name="nsa_attention",
    )(z, nsa_cos, nsa_sin, z, kc, vct, ksa, kw, vswt, ovl)


MERGE_TM = 256


def _layer_norm(y, g, b):
    mu = jnp.mean(y, axis=-1, keepdims=True)
    var = jnp.mean(jnp.square(y - mu), axis=-1, keepdims=True)
    return (y - mu) * lax.rsqrt(var + LN_EPS) * g + b


def _merge_kernel(x_ref, yc_ref, or_ref, on_ref, g0_ref, g1_ref, g2_ref, wr_ref, wn_ref, wo_ref, lg_ref, lb_ref, o_ref):
    y_ret = jnp.dot(or_ref[...].astype(BF16), wr_ref[...], preferred_element_type=F32)
    y_nsa = jnp.dot(on_ref[...].astype(BF16), wn_ref[...], preferred_element_type=F32)
    m = (jax.nn.sigmoid(g0_ref[...]) * yc_ref[...] + jax.nn.sigmoid(g1_ref[...]) * y_ret
         + jax.nn.sigmoid(g2_ref[...]) * y_nsa)
    h = jnp.dot(m.astype(BF16), wo_ref[...], preferred_element_type=F32)
    o_ref[...] = _layer_norm(DN_ALPHA * x_ref[...] + h, lg_ref[...], lb_ref[...])


def merge_block(x, y_conv, o_ret, o_nsa, z, ret_w_o, nsa_w_o, w_out, ln_g, ln_b):
    n = x.shape[0]
    tm = MERGE_TM
    d = D_MODEL
    rowd = lambda i: (i, 0)
    const = lambda i: (0, 0)
    return pl.pallas_call(
        _merge_kernel,
        out_shape=jax.ShapeDtypeStruct((n, d), F32),
        grid=(n // tm,),
        in_specs=[pl.BlockSpec((tm, d), rowd), pl.BlockSpec((tm, d), rowd),
                  pl.BlockSpec((tm, o_ret.shape[1]), rowd), pl.BlockSpec((tm, o_nsa.shape[1]), rowd),
                  pl.BlockSpec((tm, d), lambda i: (i, Z_MG // d)),
                  pl.BlockSpec((tm, d), lambda i: (i, Z_MG // d + 1)),
                  pl.BlockSpec((tm, d), lambda i: (i, Z_MG // d + 2)),
                  pl.BlockSpec(ret_w_o.shape, const), pl.BlockSpec(nsa_w_o.shape, const), pl.BlockSpec(w_out.shape, const),
                  pl.BlockSpec((1, d), const), pl.BlockSpec((1, d), const)],
        out_specs=pl.BlockSpec((tm, d), rowd),
        compiler_params=_cparams(("parallel",), 48),
        name="merge_ln1",
    )(x, y_conv, o_ret, o_nsa, z, z, z, ret_w_o, nsa_w_o, w_out, ln_g, ln_b)


RT_TM = 256


def _stable_rank(v):
    n = v.shape[0]
    row = lax.broadcasted_iota(I32, v.shape, 0)
    rank = jnp.zeros(v.shape, F32)
    for i in range(n):
        r = v[i:i + 1, :]
        rank = rank + jnp.where(row > i, jnp.where(r >= v, 1.0, 0.0), jnp.where(r > v, 1.0, 0.0))
    return rank


def _router_kernel(x_ref, wr_ref, b_ref, ltri_ref, utri_ref, eidx_ref, rnk_ref, wts_ref, cnt_ref, carry):
    @pl.when(pl.program_id(0) == 0)
    def _():
        carry[...] = jnp.zeros_like(carry)

    tm = x_ref.shape[0]
    ne = N_EXPERTS
    per = ne // N_EXPERT_GROUPS
    logits = jnp.dot(x_ref[...].astype(BF16), wr_ref[...], preferred_element_type=F32)
    s = jax.nn.sigmoid(logits.T[0:ne])
    sb = s + b_ref[...]
    sub = lax.broadcasted_iota(I32, (per, tm), 0)
    gscore = []
    for gi in range(N_EXPERT_GROUPS):
        v = sb[gi * per:(gi + 1) * per]
        m1 = jnp.max(v, axis=0, keepdims=True)
        first = jnp.min(jnp.where(v == m1, sub, per), axis=0, keepdims=True)
        m2 = jnp.max(jnp.where(sub == first, -jnp.inf, v), axis=0, keepdims=True)
        gscore.append(m1 + m2)
    gscore = jnp.concatenate(gscore, axis=0)
    gkeep = jnp.where(_stable_rank(gscore) < float(TOPK_GROUPS), 1.0, 0.0)
    ekeep = jnp.concatenate([jnp.broadcast_to(gkeep[gi:gi + 1], (per, tm)) for gi in range(N_EXPERT_GROUPS)], axis=0)
    sel = jnp.where(_stable_rank(jnp.where(ekeep > 0.0, sb, -jnp.inf)) < float(TOPK), 1.0, 0.0)
    ssel = s * sel
    gate = ssel / jnp.sum(ssel, axis=0, keepdims=True) * ROUTED_SCALE

    selb = sel.astype(BF16)
    slot = jnp.dot(ltri_ref[...], selb, preferred_element_type=F32)
    incl = jnp.dot(selb, utri_ref[...], preferred_element_type=F32)
    rnk = carry[...] + incl - 1.0
    carry[...] = carry[...] + incl[:, tm - 1:tm]
    erow = lax.broadcasted_iota(I32, (ne, tm), 0).astype(F32)
    es, rs, ws = [], [], []
    for k in range(TOPK):
        pick = jnp.where(slot == float(k), sel, 0.0)
        es.append(jnp.sum(pick * erow, axis=0, keepdims=True))
        rs.append(jnp.sum(pick * rnk, axis=0, keepdims=True))
        ws.append(jnp.sum(pick * gate, axis=0, keepdims=True))
    eidx_ref[...] = jnp.concatenate(es, axis=0).astype(I32)
    rnk_ref[...] = jnp.concatenate(rs, axis=0).astype(I32)
    wts_ref[...] = jnp.concatenate(ws + [jnp.zeros((LANES - TOPK, tm), F32)], axis=0).T
    cnt_ref[...] = jnp.broadcast_to(carry[...], cnt_ref.shape).astype(I32)


def moe_route(x, router_w_pad, router_b_col):
    n = x.shape[0]
    tm = RT_TM
    ne = N_EXPERTS
    ltri = jnp.asarray(np.tril(np.ones((ne, ne), np.float32), -1), BF16)
    utri = jnp.asarray(np.triu(np.ones((tm, tm), np.float32)), BF16)
    const = lambda i: (0, 0)
    return pl.pallas_call(
        _router_kernel,
        out_shape=(jax.ShapeDtypeStruct((TOPK, n), I32), jax.ShapeDtypeStruct((TOPK, n), I32),
                   jax.ShapeDtypeStruct((n, LANES), F32), jax.ShapeDtypeStruct((ne, LANES), I32)),
        grid=(n // tm,),
        in_specs=[pl.BlockSpec((tm, D_MODEL), lambda i: (i, 0)),
                  pl.BlockSpec(router_w_pad.shape, const), pl.BlockSpec((ne, 1), const),
                  pl.BlockSpec((ne, ne), const), pl.BlockSpec((tm, tm), const)],
        out_specs=(pl.BlockSpec((TOPK, tm), lambda i: (0, i)), pl.BlockSpec((TOPK, tm), lambda i: (0, i)),
                   pl.BlockSpec((tm, LANES), lambda i: (i, 0)), pl.BlockSpec((ne, LANES), const)),
        scratch_shapes=[pltpu.VMEM((ne, 1), F32)],
        compiler_params=_cparams(("arbitrary",)),
        name="moe_route",
    )(x, router_w_pad, router_b_col, ltri, utri)


def _moe_rows(n_tokens):
    return n_tokens * TOPK + N_EXPERTS * MOE_TILE


def _plan_kernel(cnt_ref, off_ref, texp_ref, nused_ref):
    shift = MOE_TILE.bit_length() - 1

    def per_expert(e, carry):
        off, ti = carry
        off_ref[e] = off
        ntile = lax.shift_right_logical(cnt_ref[e] + (MOE_TILE - 1), shift)

        def mark(j, c):
            texp_ref[ti + j] = e
            return c

        lax.fori_loop(0, ntile, mark, 0)
        return off + ntile * MOE_TILE, ti + ntile

    _, used = lax.fori_loop(0, N_EXPERTS, per_expert, (jnp.int32(0), jnp.int32(0)))
    nused_ref[0] = used

    def fill(j, c):
        texp_ref[j] = N_EXPERTS - 1
        return c

    lax.fori_loop(used, texp_ref.shape[0], fill, 0)


def moe_plan(counts, n_tokens):
    nt = _moe_rows(n_tokens) // MOE_TILE
    smem = pl.BlockSpec(memory_space=pltpu.SMEM)
    return pl.pallas_call(
        _plan_kernel,
        out_shape=(jax.ShapeDtypeStruct((N_EXPERTS,), I32), jax.ShapeDtypeStruct((nt,), I32),
                   jax.ShapeDtypeStruct((1,), I32)),
        in_specs=[smem],
        out_specs=(smem, smem, smem),
        name="moe_plan",
    )(counts)


DSP_TB = 256


ROW_CHUNKS = D_MODEL // LANES
assert ROW_CHUNKS == 8


def _dst_kernel(off_ref, eidx_ref, rnk_ref, dst_ref):
    e = eidx_ref[...]
    base = jnp.zeros(e.shape, I32)
    for x in range(N_EXPERTS):
        base = jnp.where(e == x, off_ref[x], base)
    dst_ref[...] = (base + rnk_ref[...]) * ROW_CHUNKS


def moe_dst(eidx, rnk, off):
    k, n = eidx.shape
    tb = 2048
    blk = pl.BlockSpec((k, tb), lambda i: (0, i))
    return pl.pallas_call(
        _dst_kernel,
        out_shape=jax.ShapeDtypeStruct((k, n), I32),
        grid=(n // tb,),
        in_specs=[pl.BlockSpec(memory_space=pltpu.SMEM), blk, blk],
        out_specs=blk,
        compiler_params=_cparams(("parallel",)),
        name="moe_dst",
    )(off, eidx, rnk)


def _tile_copy(src_ref, src_row, dst_ref, dst_row, sem):
    return pltpu.make_async_copy(src_ref.at[pl.ds(pl.multiple_of(src_row, ROW_CHUNKS), ROW_CHUNKS)],
                                 dst_ref.at[pl.ds(pl.multiple_of(dst_row, ROW_CHUNKS), ROW_CHUNKS)], sem)


def _dispatch_kernel(dst_ref, off_ref, cnt_ref, x_ref, xs_ref, zbuf, sem, zsem):
    tb = x_ref.shape[0] // ROW_CHUNKS
    zrows = MOE_TILE * ROW_CHUNKS

    def pad_copy(e):
        cnt = cnt_ref[e]
        rem = jnp.bitwise_and(cnt, MOE_TILE - 1)
        start = pl.multiple_of((off_ref[e] + cnt - rem) * ROW_CHUNKS, zrows)
        return rem != 0, pltpu.make_async_copy(zbuf, xs_ref.at[pl.ds(start, zrows)], zsem)

    @pl.when(pl.program_id(0) == 0)
    def _():
        zbuf[...] = jnp.zeros_like(zbuf)

        def start(e, c):
            has_pad, cp = pad_copy(e)

            @pl.when(has_pad)
            def _():
                cp.start()
            return c

        def wait(e, c):
            has_pad, cp = pad_copy(e)

            @pl.when(has_pad)
            def _():
                cp.wait()
            return c

        lax.fori_loop(0, N_EXPERTS, start, 0)
        lax.fori_loop(0, N_EXPERTS, wait, 0)

    def issue(t, c):
        for k in range(TOPK):
            _tile_copy(x_ref, t * ROW_CHUNKS, xs_ref, dst_ref[k, t], sem).start(priority=k % 2)
        return c

    def drain(t, c):
        for k in range(TOPK):
            _tile_copy(x_ref, 0, xs_ref, 0, sem).wait()
        return c

    lax.fori_loop(0, tb, issue, 0)
    lax.fori_loop(0, tb, drain, 0)


def moe_dispatch(xc, dst, off, counts):
    n = xc.shape[0] // ROW_CHUNKS
    tb = DSP_TB
    smem_all = pl.BlockSpec(memory_space=pltpu.SMEM)
    smem_blk = pl.BlockSpec((TOPK, tb), lambda i: (0, i), memory_space=pltpu.SMEM)
    return pl.pallas_call(
        _dispatch_kernel,
        out_shape=jax.ShapeDtypeStruct((_moe_rows(n) * ROW_CHUNKS, LANES), xc.dtype),
        grid=(n // tb,),
        in_specs=[smem_blk, smem_all, smem_all, pl.BlockSpec((tb * ROW_CHUNKS, LANES), lambda i: (i, 0))],
        out_specs=pl.BlockSpec(memory_space=pl.ANY),
        scratch_shapes=[pltpu.VMEM((MOE_TILE * ROW_CHUNKS, LANES), xc.dtype), pltpu.SemaphoreType.DMA(()),
                        pltpu.SemaphoreType.DMA(())],
        compiler_params=_cparams(("arbitrary",)),
        name="moe_dispatch",
    )(dst, off, counts, xc)


def _expert_kernel(texp_ref, nused_ref, xs_ref, w1_ref, w3_ref, w2_ref, ys_ref, w1b, w3b, w2b, last):
    i = pl.program_id(0)
    e = texp_ref[i]

    @pl.when(i == 0)
    def _():
        last[0] = -1

    @pl.when(e != last[0])
    def _():
        w1b[...] = w1_ref[0, 0].astype(BF16)
        w3b[...] = w3_ref[0, 0].astype(BF16)
        w2b[...] = w2_ref[0, 0].astype(BF16)
        last[0] = e

    @pl.when(i < nused_ref[0])
    def _():
        xb = jnp.concatenate([xs_ref[pl.ds(s, MOE_TILE, stride=ROW_CHUNKS), :] for s in range(ROW_CHUNKS)],
                             axis=1).astype(BF16)
        h1 = jnp.dot(xb, w1b[...], preferred_element_type=F32)
        h3 = jnp.dot(xb, w3b[...], preferred_element_type=F32)
        h = h1 * jax.nn.sigmoid(h1) * h3
        y = jnp.dot(h.astype(BF16), w2b[...], preferred_element_type=F32)
        for s in range(ROW_CHUNKS):
            ys_ref[pl.ds(s, MOE_TILE, stride=ROW_CHUNKS), :] = y[:, s * LANES:(s + 1) * LANES]

    @pl.when(i >= nused_ref[0])
    def _():
        ys_ref[...] = jnp.zeros_like(ys_ref)


def moe_experts(xs, texp, nused, w1, w3, w2, layer):
    crows = xs.shape[0]
    d, f = w1.shape[2], w1.shape[3]
    blk = MOE_TILE * ROW_CHUNKS
    nt = crows // blk
    grid_spec = pltpu.PrefetchScalarGridSpec(
        num_scalar_prefetch=2,
        grid=(nt,),
        in_specs=[pl.BlockSpec((blk, LANES), lambda i, te, nu: (jnp.where(i < nu[0], i, 0), 0)),
                  pl.BlockSpec((1, 1, d, f), lambda i, te, nu: (layer, te[i], 0, 0)),
                  pl.BlockSpec((1, 1, d, f), lambda i, te, nu: (layer, te[i], 0, 0)),
                  pl.BlockSpec((1, 1, f, d), lambda i, te, nu: (layer, te[i], 0, 0))],
        out_specs=pl.BlockSpec((blk, LANES), lambda i, te, nu: (i, 0)),
        scratch_shapes=[pltpu.VMEM((d, f), BF16), pltpu.VMEM((d, f), BF16), pltpu.VMEM((f, d), BF16),
                        pltpu.SMEM((1,), I32)],
    )
    return pl.pallas_call(
        _expert_kernel,
        out_shape=jax.ShapeDtypeStruct((crows, LANES), F32),
        grid_spec=grid_spec,
        compiler_params=_cparams(("arbitrary",), 48),
        name="moe_experts",
    )(texp, nused, xs, w1, w3, w2)


CMB_TB = 128


def _combine_kernel(dst_ref, x_ref, wts_ref, ws1_ref, ws3_ref, ws2_ref, lg_ref, lb_ref, ys_ref, o_ref, buf, sem):
    tb = x_ref.shape[0]

    def issue(t, c):
        for k in range(TOPK):
            _tile_copy(ys_ref, dst_ref[k, t], buf.at[k], t * ROW_CHUNKS, sem).start(priority=k % 2)
        return c

    def drain(t, c):
        for k in range(TOPK):
            _tile_copy(ys_ref, 0, buf.at[0], 0, sem).wait()
        return c

    lax.fori_loop(0, tb, issue, 0)
    x = x_ref[...]
    xb = x.astype(BF16)
    h1 = jnp.dot(xb, ws1_ref[...], preferred_element_type=F32)
    h3 = jnp.dot(xb, ws3_ref[...], preferred_element_type=F32)
    y = jnp.dot((h1 * jax.nn.sigmoid(h1) * h3).astype(BF16), ws2_ref[...], preferred_element_type=F32)
    lax.fori_loop(0, tb, drain, 0)
    w = wts_ref[...]
    routed = y
    for k in range(TOPK):
        rows = jnp.concatenate([buf[k, pl.ds(s, tb, stride=ROW_CHUNKS), :] for s in range(ROW_CHUNKS)], axis=1)
        routed = routed + w[:, k:k + 1] * rows
    o_ref[...] = _layer_norm(DN_ALPHA * x + routed, lg_ref[...], lb_ref[...])


def moe_combine(x, ys, dst, wts, ws1, ws3, ws2, ln_g, ln_b):
    n, d = x.shape
    tb = CMB_TB
    smem_blk = pl.BlockSpec((TOPK, tb), lambda i: (0, i), memory_space=pltpu.SMEM)
    const = lambda i: (0, 0)
    return pl.pallas_call(
        _combine_kernel,
        out_shape=jax.ShapeDtypeStruct((n, d), F32),
        grid=(n // tb,),
        in_specs=[smem_blk,
                  pl.BlockSpec((tb, d), lambda i: (i, 0)), pl.BlockSpec((tb, LANES), lambda i: (i, 0)),
                  pl.BlockSpec(ws1.shape, const), pl.BlockSpec(ws3.shape, const), pl.BlockSpec(ws2.shape, const),
                  pl.BlockSpec((1, d), const), pl.BlockSpec((1, d), const),
                  pl.BlockSpec(memory_space=pl.ANY)],
        out_specs=pl.BlockSpec((tb, d), lambda i: (i, 0)),
        scratch_shapes=[pltpu.VMEM((TOPK, tb * ROW_CHUNKS, LANES), F32), pltpu.SemaphoreType.DMA(())],
        compiler_params=_cparams(("parallel",), 48),
        name="moe_combine_ln2",
    )(dst, x, wts, ws1, ws3, ws2, ln_g, ln_b, ys)


def moe_block(x, router_w, router_b, w1, w3, w2, layer, ws1, ws3, ws2, ln_g, ln_b):
    n, d = x.shape
    rw = jnp.pad(router_w, ((0, 0), (0, LANES - N_EXPERTS))).astype(BF16)
    eidx, rnk, wts, cnt = moe_route(x, rw, router_b.reshape(N_EXPERTS, 1))
    counts = cnt[:, 0]
    off, texp, nused = moe_plan(counts, n)
    dst = moe_dst(eidx, rnk, off)
    xs = moe_dispatch(x.reshape(n * ROW_CHUNKS, LANES), dst, off, counts)
    ys = moe_experts(xs, texp, nused, w1, w3, w2, layer)
    return moe_combine(x, ys, dst, wts, ws1.astype(BF16), ws3.astype(BF16), ws2.astype(BF16),
                       ln_g.reshape(1, -1), ln_b.reshape(1, -1))


def nsa_rope_tables(positions):
    batch, seq = positions.shape
    inv, sgn = _nsa_inv_freq()
    posf = positions.astype(F32)
    tok = rope_tables(posf.reshape(batch * seq, 1), inv, sgn, tm=512)
    end = posf[:, CMP_BLOCK - 1::CMP_STRIDE]
    npad = _n_cmp_pad(seq)
    end = jnp.pad(end, ((0, 0), (0, npad - end.shape[1])))
    cmp = rope_tables(end.reshape(batch * npad, 1), inv, sgn, tm=npad)
    return tok, cmp


def nsa_branch(z, tok_tab, cmp_tab, wk, wv, ovl, *, batch, seq):
    n = batch * seq
    w = NSA_GROUPS * NSA_DH
    xk = z[:, Z_NKV:Z_NKV + w].reshape(n // CMP_HALF, CMP_ROW)
    xv = z[:, Z_NKV + w:Z_NKV + 2 * w].reshape(n // CMP_HALF, CMP_ROW)
    kc, vct = nsa_compress(xk, xv, wk, wv, cmp_tab[0], cmp_tab[1], batch=batch)
    ksa, kw, vswt = nsa_kv_prep(z, tok_tab[0], tok_tab[1], batch=batch, seq=seq)
    return nsa_attention(z, tok_tab[0], tok_tab[1], kc, vct, ksa, kw, vswt, ovl, batch=batch, seq=seq)


def kernel(x, positions, w_in, conv_dw, conv_db, conv_ln_g, conv_ln_b, conv_w_pw, ret_w_o, nsa_pe_k, nsa_w1_k, nsa_w2_k,
           nsa_pe_v, nsa_w1_v, nsa_w2_v, nsa_w_o, w_out, ln1_g, ln1_b, router_w, router_b, moe_w1, moe_w3, moe_w2,
           shared_w1, shared_w3, shared_w2, ln2_g, ln2_b):
    batch, seq, d = x.shape
    n = batch * seq
    xf = x.reshape(n, d)
    row = lambda v: v.reshape(1, -1)

    ret_inv, ret_sgn = _ret_inv_freq()
    ret_tab = rope_tables(positions.astype(F32).reshape(n, 1), ret_inv, ret_sgn, tm=512)
    tok_tab, cmp_tab = nsa_rope_tables(positions)
    ret_consts = _ret_tables()
    ovl = _overlap_matrix(seq)

    for l in range(w_in.shape[0]):
        z = matmul(xf, _layout_w_in(w_in[l]), tm=1024, tn=512, name="in_proj")
        y_conv = conv_branch(z, conv_dw[l], row(conv_db[l]), row(conv_ln_g[l]), row(conv_ln_b[l]),
                             conv_w_pw[l].astype(BF16), batch=batch, seq=seq)
        o_ret = retention_branch(z, ret_tab[0], ret_tab[1], ret_consts, batch=batch, seq=seq)
        o_nsa = nsa_branch(z, tok_tab, cmp_tab, _cmp_weights(nsa_pe_k[l], nsa_w1_k[l], nsa_w2_k[l]),
                           _cmp_weights(nsa_pe_v[l], nsa_w1_v[l], nsa_w2_v[l]), ovl, batch=batch, seq=seq)
        x1 = merge_block(xf, y_conv, o_ret, o_nsa, z, ret_w_o[l].astype(BF16), nsa_w_o[l].astype(BF16),
                         w_out[l].astype(BF16), row(ln1_g[l]), row(ln1_b[l]))
        xf = moe_block(x1, router_w[l], router_b[l], moe_w1, moe_w3, moe_w2, l,
                       shared_w1[l], shared_w3[l], shared_w2[l], ln2_g[l], ln2_b[l])
    return xf.reshape(batch, seq, d)
```

```python
import functools
import math

import jax
import jax.numpy as jnp
import numpy as np
from jax import lax
from jax.experimental import pallas as pl
from jax.experimental.pallas import tpu as pltpu

F32 = jnp.float32
BF16 = jnp.bfloat16
I32 = jnp.int32

D_MODEL = 1024
DEPTH = 4
CONV_CH = 512
CONV_WIDTH = 31
RET_HEADS = 4
RET_DK = 128
RET_DV = 256
RET_CHUNK = 128
RET_ROPE_BASE = 10000.0
NSA_HEADS = 8
NSA_GROUPS = 2
NSA_HPG = NSA_HEADS // NSA_GROUPS
NSA_DH = 64
NSA_N_BRANCH = 3
CMP_BLOCK = 32
CMP_STRIDE = 16
CMP_HIDDEN = 256
SEL_BLOCK = 64
SEL_TOPK = 16
WINDOW = 512
ROPE_THETA = 500000.0
ROT_DIM = NSA_DH // 4
N_EXPERTS = 64
N_EXPERT_GROUPS = 8
TOPK_GROUPS = 4
TOPK = 8
D_EXPERT = 256
D_SHARED = 256
ROUTED_SCALE = 2.5
DN_ALPHA = (2.0 * DEPTH) ** 0.25
LN_EPS = 1e-5
NEG = -1e30

LANES = 128

Z_MG = 0
Z_CA = 3072
Z_CB = 3584
Z_RQ = 4096
Z_RK = 4608
Z_RV = 5120
Z_RG = 6144
Z_NQ = 7168
Z_NKV = 7680
Z_NG = 8448
Z_W = 8704

_IN_WIDTHS = (512, 512, 512, 512, 1024, 1024, 512, 128, 128, 128, 128, 128, 128, 24, 3072)
_IN_OFFS = tuple(int(v) for v in np.concatenate([[0], np.cumsum(_IN_WIDTHS)[:-1]]))

MOE_TILE = 512
MOE_ROWS_PER_TOKEN = TOPK


def _cparams(sem, vmem_mb=None):
    kw = dict(dimension_semantics=sem)
    if vmem_mb is not None:
        kw["vmem_limit_bytes"] = vmem_mb * 1024 * 1024
    return pltpu.CompilerParams(**kw)


def _mm_kernel(x_ref, w_ref, o_ref):
    o_ref[...] = jnp.dot(x_ref[...].astype(BF16), w_ref[...], preferred_element_type=F32).astype(o_ref.dtype)


def matmul(x, w, *, tm, tn, out_dtype=F32, x_col_block=0, k=None, name="mm"):
    m = x.shape[0]
    kk, n = w.shape
    if k is None:
        k = kk
    return pl.pallas_call(
        _mm_kernel,
        out_shape=jax.ShapeDtypeStruct((m, n), out_dtype),
        grid=(m // tm, n // tn),
        in_specs=[pl.BlockSpec((tm, k), lambda i, j: (i, x_col_block)),
                  pl.BlockSpec((k, tn), lambda i, j: (0, j))],
        out_specs=pl.BlockSpec((tm, tn), lambda i, j: (i, j)),
        compiler_params=_cparams(("parallel", "arbitrary"), 48),
        name=name,
    )(x, w)


def _rope_table_kernel(pos_ref, inv_ref, sgn_ref, cos_ref, sin_ref):
    ang = pos_ref[...] * inv_ref[...]
    cos_ref[...] = jnp.cos(ang)
    sin_ref[...] = jnp.sin(ang) * sgn_ref[...]


def rope_tables(pos_col, inv, sgn, *, tm):
    n = pos_col.shape[0]
    w = inv.shape[1]
    return pl.pallas_call(
        _rope_table_kernel,
        out_shape=(jax.ShapeDtypeStruct((n, w), F32), jax.ShapeDtypeStruct((n, w), F32)),
        grid=(n // tm,),
        in_specs=[pl.BlockSpec((tm, 1), lambda i: (i, 0)),
                  pl.BlockSpec((1, w), lambda i: (0, 0)),
                  pl.BlockSpec((1, w), lambda i: (0, 0))],
        out_specs=(pl.BlockSpec((tm, w), lambda i: (i, 0)), pl.BlockSpec((tm, w), lambda i: (i, 0))),
        compiler_params=_cparams(("parallel",)),
        name="rope_tables",
    )(pos_col, inv, sgn)


def _ret_inv_freq():
    inv = 1.0 / jnp.power(jnp.float32(RET_ROPE_BASE), jnp.linspace(0.0, 1.0, RET_DK // 2, dtype=F32))
    inv = jnp.concatenate([inv, inv])[None, :]
    sgn = jnp.concatenate([-jnp.ones((RET_DK // 2,), F32), jnp.ones((RET_DK // 2,), F32)])[None, :]
    return inv, sgn


def _nsa_inv_freq():
    half = ROT_DIM // 2
    inv = jnp.power(jnp.float32(ROPE_THETA), -jnp.arange(0, ROT_DIM, 2, dtype=F32) / ROT_DIM)
    z = jnp.zeros((NSA_DH - ROT_DIM,), F32)
    inv64 = jnp.concatenate([inv, inv, z])
    sgn64 = jnp.concatenate([-jnp.ones((half,), F32), jnp.ones((half,), F32), z])
    return jnp.concatenate([inv64, inv64])[None, :], jnp.concatenate([sgn64, sgn64])[None, :]


def _nsa_rope(x, cos, sin):
    w = x.shape[1]
    half = ROT_DIM // 2
    lane = lax.broadcasted_iota(I32, x.shape, 1) % NSA_DH
    partner = jnp.where(lane < half, pltpu.roll(x, w - half, 1), pltpu.roll(x, half, 1))
    return x * cos + partner * sin


CONV_TT = 256
CONV_HALO = 32


def _conv_kernel(a_ref, b_ref, ah_ref, bh_ref, dw_ref, db_ref, g_ref, be_ref, wpw_ref, o_ref, ubuf):
    i = pl.program_id(1)
    tt = a_ref.shape[0]
    u = a_ref[...].astype(F32) * jax.nn.sigmoid(b_ref[...].astype(F32))
    uh = ah_ref[...].astype(F32) * jax.nn.sigmoid(bh_ref[...].astype(F32))
    ubuf[0:CONV_HALO, :] = jnp.where(i > 0, uh, 0.0)
    ubuf[CONV_HALO:CONV_HALO + tt, :] = u
    acc = jnp.zeros((tt, CONV_CH), F32)
    base = CONV_HALO - (CONV_WIDTH - 1)
    for k in range(CONV_WIDTH):
        acc = acc + dw_ref[k:k + 1, :] * ubuf[base + k:base + k + tt, :]
    acc = acc + db_ref[...]
    mu = jnp.mean(acc, axis=-1, keepdims=True)
    var = jnp.mean(jnp.square(acc - mu), axis=-1, keepdims=True)
    y = (acc - mu) * lax.rsqrt(var + LN_EPS) * g_ref[...] + be_ref[...]
    y = y * jax.nn.sigmoid(y)
    o_ref[...] = jnp.dot(y.astype(BF16), wpw_ref[...], preferred_element_type=F32)


def conv_branch(z, dw, db, ln_g, ln_b, w_pw_bf, *, batch, seq):
    tt = CONV_TT
    nt = seq // tt
    r = tt // CONV_HALO
    ca, cb = Z_CA // CONV_CH, Z_CB // CONV_CH

    def halo_map(col):
        return lambda b, i: (jnp.maximum((b * nt + i) * r - 1, 0), col)

    return pl.pallas_call(
        _conv_kernel,
        out_shape=jax.ShapeDtypeStruct((batch * seq, D_MODEL), F32),
        grid=(batch, nt),
        in_specs=[pl.BlockSpec((tt, CONV_CH), lambda b, i: (b * nt + i, ca)),
                  pl.BlockSpec((tt, CONV_CH), lambda b, i: (b * nt + i, cb)),
                  pl.BlockSpec((CONV_HALO, CONV_CH), halo_map(ca)),
                  pl.BlockSpec((CONV_HALO, CONV_CH), halo_map(cb)),
                  pl.BlockSpec((CONV_WIDTH, CONV_CH), lambda b, i: (0, 0)),
                  pl.BlockSpec((1, CONV_CH), lambda b, i: (0, 0)),
                  pl.BlockSpec((1, CONV_CH), lambda b, i: (0, 0)),
                  pl.BlockSpec((1, CONV_CH), lambda b, i: (0, 0)),
                  pl.BlockSpec((CONV_CH, D_MODEL), lambda b, i: (0, 0))],
        out_specs=pl.BlockSpec((tt, D_MODEL), lambda b, i: (b * nt + i, 0)),
        scratch_shapes=[pltpu.VMEM((CONV_HALO + tt, CONV_CH), F32)],
        compiler_params=_cparams(("parallel", "parallel")),
        name="conv_branch",
    )(z, z, z, z, dw, db, ln_g, ln_b, w_pw_bf)


RET_TQ = 512


def _ret_tables():
    h, c = RET_HEADS, RET_CHUNK
    log_gamma = jnp.log1p(-jnp.exp2(-5.0 - jnp.arange(h, dtype=F32)))
    idx = jnp.arange(c, dtype=F32)
    diff = idx[:, None] - idx[None, :]
    dmat = jnp.where(diff >= 0, jnp.exp(log_gamma[:, None, None] * jnp.maximum(diff, 0.0)), 0.0).astype(F32)
    xi = jnp.exp(log_gamma[:, None] * (idx + 1.0)).astype(F32)
    zeta = jnp.exp(log_gamma[:, None] * (c - 1.0 - idx)).astype(F32)
    decay = jnp.exp(log_gamma * c).astype(F32)
    xi_b = jnp.broadcast_to(xi[:, :, None], (h, c, RET_DV))
    zeta_b = jnp.broadcast_to(zeta[:, :, None], (h, c, RET_DV))
    decay_b = jnp.broadcast_to(decay[:, None, None], (h, RET_DK, RET_DV))
    return dmat, xi_b, zeta_b, decay_b


def _ret_kernel(q_ref, k_ref, v_ref, g_ref, cos_ref, sin_ref, dmat_ref, xi_ref, zeta_ref, dec_ref, o_ref, r_ref):
    @pl.when(pl.program_id(1) == 0)
    def _():
        r_ref[...] = jnp.zeros_like(r_ref)

    c = RET_CHUNK
    n_chunks = q_ref.shape[0] // c
    for ci in range(n_chunks):
        rows = slice(ci * c, (ci + 1) * c)
        cos = cos_ref[rows, :]
        sin = sin_ref[rows, :]
        for h in range(RET_HEADS):
            qk_cols = slice(h * RET_DK, (h + 1) * RET_DK)
            v_cols = slice(h * RET_DV, (h + 1) * RET_DV)
            q = q_ref[rows, qk_cols].astype(F32)
            k = k_ref[rows, qk_cols].astype(F32)
            q = q * cos + pltpu.roll(q, RET_DK // 2, 1) * sin
            k = (k * cos + pltpu.roll(k, RET_DK // 2, 1) * sin) * (RET_DK ** -0.5)
            v = v_ref[rows, v_cols].astype(F32)
            qb = q.astype(BF16)
            kb = k.astype(BF16)
            inner = lax.dot_general(qb, kb, (((1,), (1,)), ((), ())), preferred_element_type=F32) * dmat_ref[h]
            r_old = r_ref[h]
            o = (jnp.dot(inner.astype(BF16), v.astype(BF16), preferred_element_type=F32)
                 + jnp.dot(qb, r_old.astype(BF16), preferred_element_type=F32) * xi_ref[h])
            vz = (v * zeta_ref[h]).astype(BF16)
            r_ref[h] = r_old * dec_ref[h] + jnp.dot(k.T.astype(BF16), vz, preferred_element_type=F32)
            mu = jnp.mean(o, axis=-1, keepdims=True)
            var = jnp.mean(jnp.square(o - mu), axis=-1, keepdims=True)
            on = (o - mu) * lax.rsqrt(var + LN_EPS)
            g = g_ref[rows, v_cols].astype(F32)
            o_ref[rows, v_cols] = g * jax.nn.sigmoid(g) * on


def retention_branch(z, ret_cos, ret_sin, tables, *, batch, seq):
    tq = RET_TQ
    nt = seq // tq
    dmat, xi_b, zeta_b, decay_b = tables
    qw = RET_HEADS * RET_DK
    vw = RET_HEADS * RET_DV
    row = lambda b, i: b * nt + i
    full3 = lambda b, i: (0, 0, 0)
    return pl.pallas_call(
        _ret_kernel,
        out_shape=jax.ShapeDtypeStruct((batch * seq, vw), F32),
        grid=(batch, nt),
        in_specs=[pl.BlockSpec((tq, qw), lambda b, i: (row(b, i), Z_RQ // qw)),
                  pl.BlockSpec((tq, qw), lambda b, i: (row(b, i), Z_RK // qw)),
                  pl.BlockSpec((tq, vw), lambda b, i: (row(b, i), Z_RV // vw)),
                  pl.BlockSpec((tq, vw), lambda b, i: (row(b, i), Z_RG // vw)),
                  pl.BlockSpec((tq, RET_DK), lambda b, i: (row(b, i), 0)),
                  pl.BlockSpec((tq, RET_DK), lambda b, i: (row(b, i), 0)),
                  pl.BlockSpec(dmat.shape, full3),
                  pl.BlockSpec(xi_b.shape, full3),
                  pl.BlockSpec(zeta_b.shape, full3),
                  pl.BlockSpec(decay_b.shape, full3)],
        out_specs=pl.BlockSpec((tq, vw), lambda b, i: (row(b, i), 0)),
        scratch_shapes=[pltpu.VMEM((RET_HEADS, RET_DK, RET_DV), F32)],
        compiler_params=_cparams(("parallel", "arbitrary"), 48),
        name="retention",
    )(z, z, z, z, ret_cos, ret_sin, dmat, xi_b, zeta_b, decay_b)


def _layout_w_in(w):
    seg = lambda i: w[:, _IN_OFFS[i]:_IN_OFFS[i] + _IN_WIDTHS[i]]
    order = (14, 0, 1, 2, 3, 4, 5, 6, 7, 8, 9, 10, 11, 12, 13)
    parts = [seg(i) for i in order]
    used = sum(_IN_WIDTHS)
    parts.append(jnp.zeros((w.shape[0], Z_W - used), w.dtype))
    return jnp.concatenate(parts, axis=1).astype(BF16)


CMP_HALF = CMP_BLOCK // 2
CMP_ROW = CMP_HALF * NSA_GROUPS * NSA_DH


def _n_cmp_pad(seq):
    return seq // CMP_STRIDE


def _cmp_weights(pe, w1, w2):
    g = NSA_GROUPS
    eye = jnp.eye(g, dtype=F32)
    w = w1.reshape(2, CMP_HALF, NSA_DH, CMP_HIDDEN)
    w1ab = jnp.einsum("hldf,gk->hlgdkf", w, eye).reshape(2, CMP_ROW, g * CMP_HIDDEN).astype(BF16)
    peab = jnp.broadcast_to(pe.reshape(2, CMP_HALF, 1, NSA_DH), (2, CMP_HALF, g, NSA_DH)).reshape(2, 1, CMP_ROW)
    w2bd = jnp.einsum("fd,gk->gfkd", w2, eye).reshape(g * CMP_HIDDEN, g * NSA_DH).astype(BF16)
    return peab, w1ab, w2bd


def _cmp_mlp(x, pe_ref, w1_ref, w2_ref):
    x = x.astype(F32)
    a = jnp.dot((x + pe_ref[0]).astype(BF16), w1_ref[0], preferred_element_type=F32)
    b = jnp.dot((x + pe_ref[1]).astype(BF16), w1_ref[1], preferred_element_type=F32)
    hid = a + pltpu.roll(b, b.shape[0] - 1, 0)
    hid = hid * jax.nn.sigmoid(hid)
    return jnp.dot(hid.astype(BF16), w2_ref[...], preferred_element_type=F32)


def _compress_kernel(xk_ref, xv_ref, pek_ref, w1k_ref, w2k_ref, pev_ref, w1v_ref, w2v_ref, cos_ref, sin_ref,
                     k_ref, vt_ref):
    k = _cmp_mlp(xk_ref[...], pek_ref, w1k_ref, w2k_ref)
    k_ref[0] = _nsa_rope(k, cos_ref[...], sin_ref[...]).astype(BF16)
    vt_ref[0] = _cmp_mlp(xv_ref[...], pev_ref, w1v_ref, w2v_ref).T.astype(BF16)


def nsa_compress(xk, xv, wk, wv, cmp_cos, cmp_sin, *, batch):
    pek, w1k, w2k = wk
    pev, w1v, w2v = wv
    npad = xk.shape[0] // batch
    gw = NSA_GROUPS * NSA_DH
    c3 = lambda b: (0, 0, 0)
    c2 = lambda b: (0, 0)
    return pl.pallas_call(
        _compress_kernel,
        out_shape=(jax.ShapeDtypeStruct((batch, npad, gw), BF16), jax.ShapeDtypeStruct((batch, gw, npad), BF16)),
        grid=(batch,),
        in_specs=[pl.BlockSpec((npad, CMP_ROW), lambda b: (b, 0)),
                  pl.BlockSpec((npad, CMP_ROW), lambda b: (b, 0)),
                  pl.BlockSpec(pek.shape, c3), pl.BlockSpec(w1k.shape, c3), pl.BlockSpec(w2k.shape, c2),
                  pl.BlockSpec(pev.shape, c3), pl.BlockSpec(w1v.shape, c3), pl.BlockSpec(w2v.shape, c2),
                  pl.BlockSpec((npad, gw), lambda b: (b, 0)),
                  pl.BlockSpec((npad, gw), lambda b: (b, 0))],
        out_specs=(pl.BlockSpec((1, npad, gw), lambda b: (b, 0, 0)), pl.BlockSpec((1, gw, npad), lambda b: (b, 0, 0))),
        compiler_params=_cparams(("parallel",), 48),
        name="nsa_compress",
    )(xk, xv, pek, w1k, w2k, pev, w1v, w2v, cmp_cos, cmp_sin)


KV_TT = 512
ATT_TK = 512
ATT_KB = 128
ATT_AHEAD = 8
LOG2E = math.log2(math.e)


def _kv_prep_kernel(z_ref, cos_ref, sin_ref, ksa_ref, kw_ref, vswt_ref):
    tt = z_ref.shape[0]
    cos = cos_ref[...]
    sin = sin_ref[...]
    w = NSA_GROUPS * NSA_DH
    n_sel = ksa_ref.shape[3] - NSA_DH
    ks = _nsa_rope(z_ref[:, 2 * w:3 * w].astype(F32), cos, sin)
    vs = z_ref[:, 3 * w:4 * w].astype(F32)
    kw = _nsa_rope(z_ref[:, 4 * w:5 * w].astype(F32), cos, sin)
    vw = z_ref[:, 5 * w:6 * w].astype(F32)
    blk = (pl.program_id(1) * tt + lax.broadcasted_iota(I32, (tt, n_sel), 0)) // SEL_BLOCK
    onehot = jnp.where(blk == lax.broadcasted_iota(I32, (tt, n_sel), 1), 1.0, 0.0)
    for g in range(NSA_GROUPS):
        cols = slice(g * NSA_DH, (g + 1) * NSA_DH)
        ksa_ref[0, g] = jnp.concatenate([ks[:, cols], onehot], axis=1).astype(BF16)
        kw_ref[0, g] = kw[:, cols].astype(BF16)
        vswt_ref[0, g] = jnp.concatenate([vs[:, cols], vw[:, cols]], axis=1).T.astype(BF16)


def nsa_kv_prep(z, nsa_cos, nsa_sin, *, batch, seq):
    tt = KV_TT
    nt = seq // tt
    g = NSA_GROUPS
    w = g * NSA_DH
    n_sel = seq // SEL_BLOCK
    return pl.pallas_call(
        _kv_prep_kernel,
        out_shape=(jax.ShapeDtypeStruct((batch, g, seq, NSA_DH + n_sel), BF16),
                   jax.ShapeDtypeStruct((batch, g, seq, NSA_DH), BF16),
                   jax.ShapeDtypeStruct((batch, g, w, seq), BF16)),
        grid=(batch, nt),
        in_specs=[pl.BlockSpec((tt, 6 * w), lambda b, i: (b * nt + i, Z_NKV // (6 * w))),
                  pl.BlockSpec((tt, w), lambda b, i: (b * nt + i, 0)),
                  pl.BlockSpec((tt, w), lambda b, i: (b * nt + i, 0))],
        out_specs=(pl.BlockSpec((1, g, tt, NSA_DH + n_sel), lambda b, i: (b, 0, i, 0)),
                   pl.BlockSpec((1, g, tt, NSA_DH), lambda b, i: (b, 0, i, 0)),
                   pl.BlockSpec((1, g, w, tt), lambda b, i: (b, 0, 0, i))),
        compiler_params=_cparams(("parallel", "parallel")),
        name="nsa_kv_prep",
    )(z, nsa_cos, nsa_sin)


ATT_TQ = 256


def _overlap_matrix(seq):
    n_cmp = (seq - CMP_BLOCK) // CMP_STRIDE + 1
    n_sel = seq // SEL_BLOCK
    ii = np.arange(_n_cmp_pad(seq))[None, :]
    jj = np.arange(LANES)[:, None]
    lo = np.maximum(ii * CMP_STRIDE, jj * SEL_BLOCK)
    hi = np.minimum(ii * CMP_STRIDE + CMP_BLOCK, (jj + 1) * SEL_BLOCK)
    ov = np.maximum(hi - lo, 0).astype(np.float32) / CMP_BLOCK
    ov = np.where((ii < n_cmp) & (jj < n_sel), ov, 0.0)
    return jnp.asarray(np.tile(ov, (1, NSA_HPG)), BF16)


def _softmax_step(s, ok, vt, cols, m_ref, l_ref, a_ref):
    m_old = m_ref[:, cols]
    m_new = jnp.maximum(m_old, jnp.max(s, axis=0, keepdims=True))
    alpha = jnp.exp2(m_old - m_new)
    p = jnp.exp2(s - m_new)
    if ok is not None:
        p = jnp.where(ok, p, 0.0)
    l_ref[:, cols] = alpha * l_ref[:, cols] + jnp.sum(p, axis=0, keepdims=True)
    a_ref[:, cols] = alpha * a_ref[:, cols] + jnp.dot(vt, p.astype(BF16), preferred_element_type=F32)
    m_ref[:, cols] = m_new


def _att_kernel(q_ref, cos_ref, sin_ref, gl_ref, kc_ref, vct_ref, ksa_ref, kw_ref, vswt_ref, ovl_ref, o_ref,
                qa_s, m_sel, l_sel, a_sel, m_win, l_win, a_win):
    g = pl.program_id(1)
    qi = pl.program_id(2)
    tq = ATT_TQ
    tk = ATT_TK
    hpg = NSA_HPG
    dh = NSA_DH
    t0 = qi * tq
    seq = ksa_ref.shape[2]
    n_sel = ksa_ref.shape[3] - dh
    heads = [slice(h * tq, (h + 1) * tq) for h in range(hpg)]

    cos = jnp.concatenate([cos_ref[...]] * (hpg // 2), axis=1)
    sin = jnp.concatenate([sin_ref[...]] * (hpg // 2), axis=1)
    qt = (_nsa_rope(q_ref[...].astype(F32), cos, sin) * (dh ** -0.5 * LOG2E)).T
    q_t = jnp.concatenate([qt[h * dh:(h + 1) * dh] for h in range(hpg)], axis=1)
    q_tb = q_t.astype(BF16)
    qa_s[0:dh, :] = q_tb

    zero = jnp.zeros_like(q_tb)
    q2 = jnp.where(g == 0, jnp.concatenate([q_tb, zero], axis=0), jnp.concatenate([zero, q_tb], axis=0))
    s = jnp.dot(kc_ref[0], q2, preferred_element_type=F32)
    n_cmp = (seq - CMP_BLOCK) // CMP_STRIDE + 1
    nrow = lax.broadcasted_iota(I32, s.shape, 0)
    tcol = t0 + lax.broadcasted_iota(I32, s.shape, 1) % tq
    ok = (nrow * CMP_STRIDE + (CMP_BLOCK - 1) <= tcol) & (nrow < n_cmp)
    sm = jnp.where(ok, s, NEG)
    e = jnp.where(ok, jnp.exp2(sm - jnp.max(sm, axis=0, keepdims=True)), 0.0)
    den = jnp.sum(e, axis=0, keepdims=True)
    pb = (e / jnp.where(den > 0.0, den, 1.0)).astype(BF16)
    oc = jnp.dot(vct_ref[0], pb, preferred_element_type=F32)
    o_cmp = jnp.where(g == 0, oc[0:dh], oc[dh:2 * dh])
    pcat = jnp.concatenate([pb[:, hs] for hs in heads], axis=0)
    imp_t = jnp.dot(ovl_ref[...], pcat, preferred_element_type=F32)[0:n_sel]

    jrow = lax.broadcasted_iota(I32, (n_sel, tq), 0)
    cur = (t0 + lax.broadcasted_iota(I32, (n_sel, tq), 1)) // SEL_BLOCK
    forced = (jrow == 0) | (jrow == cur) | (jrow == cur - 1)
    imp_t = jnp.where(forced, jnp.inf, jnp.where(jrow <= cur, imp_t, -jnp.inf))
    rank = jnp.zeros((n_sel, tq), F32)
    for i in range(n_sel):
        ri = imp_t[i:i + 1, :]
        rank = rank + jnp.where(jrow > i, jnp.where(ri >= imp_t, 1.0, 0.0), jnp.where(ri > imp_t, 1.0, 0.0))
    bias_t = jnp.where(rank < float(SEL_TOPK), 0.0, NEG).astype(BF16)
    qa_s[dh:dh + n_sel, :] = jnp.concatenate([bias_t] * hpg, axis=1)

    m_sel[...] = jnp.full(m_sel.shape, NEG, F32)
    l_sel[...] = jnp.zeros(l_sel.shape, F32)
    a_sel[...] = jnp.zeros(a_sel.shape, F32)

    kb = ATT_KB
    assert tk == 2 * tq and tq % kb == 0
    krow = lax.broadcasted_iota(I32, (kb, tq), 0)
    qtime = t0 + lax.broadcasted_iota(I32, (kb, tq), 1)

    def run_blocks(n_keys, scores, mask, vt_of, m_ref, l_ref, a_ref):
        blocks = [(kk, hs) for kk in range(0, n_keys, kb) for hs in heads]
        sc = [scores(kk, hs) for kk, hs in blocks[:ATT_AHEAD]]
        for b, (kk, hs) in enumerate(blocks):
            if b + ATT_AHEAD < len(blocks):
                sc.append(scores(*blocks[b + ATT_AHEAD]))
            ok = mask(kk)
            s_b = sc[b] if ok is None else jnp.where(ok, sc[b], NEG)
            sc[b] = None
            _softmax_step(s_b, ok, vt_of(kk), hs, m_ref, l_ref, a_ref)

    def sel_keys(k0, n_keys, causal):
        scores = lambda kk, hs: jnp.dot(ksa_ref[0, 0, pl.ds(k0 + kk, kb), :], qa_s[:, hs], preferred_element_type=F32)
        mask = (lambda kk: (k0 + kk + krow) <= qtime) if causal else (lambda kk: None)
        run_blocks(n_keys, scores, mask, lambda kk: vswt_ref[0, 0, 0:dh, pl.ds(k0 + kk, kb)], m_sel, l_sel, a_sel)

    def sel_body(j, carry):
        sel_keys(pl.multiple_of(j * tk, tk), tk, False)
        return carry

    n_full = t0 // tk
    lax.fori_loop(0, n_full, sel_body, 0)

    @pl.when(n_full * tk < t0)
    def _():
        sel_keys(pl.multiple_of(n_full * tk, tk), tk - tq, False)

    sel_keys(pl.multiple_of(t0, tq), tq, True)

    k0 = pl.multiple_of(jnp.maximum(t0 - WINDOW, 0), tq)

    def win_mask(kk):
        dist = qtime - (k0 + kk + krow)
        return (dist >= 0) & (dist < WINDOW)

    m_win[...] = jnp.full(m_win.shape, NEG, F32)
    l_win[...] = jnp.zeros(l_win.shape, F32)
    a_win[...] = jnp.zeros(a_win.shape, F32)
    run_blocks(WINDOW + tq,
               lambda kk, hs: jnp.dot(kw_ref[0, 0, pl.ds(k0 + kk, kb), :], qa_s[0:dh, hs], preferred_element_type=F32),
               win_mask, lambda kk: vswt_ref[0, 0, dh:2 * dh, pl.ds(k0 + kk, kb)], m_win, l_win, a_win)

    o_win = a_win[...] / l_win[...]
    o_slc = a_sel[...] / l_sel[...]
    gl_t = jax.nn.sigmoid(gl_ref[...].astype(F32)).T
    nb = NSA_N_BRANCH
    outs = []
    for h, hs in enumerate(heads):
        gate = lambda br: jnp.where(g == 0, gl_t[nb * h + br:nb * h + br + 1],
                                    gl_t[nb * (hpg + h) + br:nb * (hpg + h) + br + 1])
        outs.append(gate(0) * o_cmp[:, hs] + gate(1) * o_slc[:, hs] + gate(2) * o_win[:, hs])
    o_ref[...] = jnp.concatenate(outs, axis=0).T


def nsa_attention(z, nsa_cos, nsa_sin, kc, vct, ksa, kw, vswt, ovl, *, batch, seq):
    tq = ATT_TQ
    nt = seq // tq
    g = NSA_GROUPS
    gw = g * NSA_DH
    qw = NSA_HPG * NSA_DH
    m_cols = NSA_HPG * tq
    row = lambda b, gg, i: b * nt + i
    per_bg = lambda a: pl.BlockSpec((1, 1) + a.shape[2:], lambda b, gg, i: (b, gg, 0, 0))
    stat = pltpu.VMEM((1, m_cols), F32)
    vals = pltpu.VMEM((NSA_DH, m_cols), F32)
    return pl.pallas_call(
        _att_kernel,
        out_shape=jax.ShapeDtypeStruct((batch * seq, g * qw), F32),
        grid=(batch, g, nt),
        in_specs=[pl.BlockSpec((tq, qw), lambda b, gg, i: (row(b, gg, i), Z_NQ // qw + gg)),
                  pl.BlockSpec((tq, gw), lambda b, gg, i: (row(b, gg, i), 0)),
                  pl.BlockSpec((tq, gw), lambda b, gg, i: (row(b, gg, i), 0)),
                  pl.BlockSpec((tq, LANES), lambda b, gg, i: (row(b, gg, i), Z_NG // LANES)),
                  pl.BlockSpec((1,) + kc.shape[1:], lambda b, gg, i: (b, 0, 0)),
                  pl.BlockSpec((1,) + vct.shape[1:], lambda b, gg, i: (b, 0, 0)),
                  per_bg(ksa), per_bg(kw), per_bg(vswt),
                  pl.BlockSpec(ovl.shape, lambda b, gg, i: (0, 0))],
        out_specs=pl.BlockSpec((tq, qw), lambda b, gg, i: (row(b, gg, i), gg)),
        scratch_shapes=[pltpu.VMEM((ksa.shape[3], m_cols), BF16), stat, stat, vals, stat, stat, vals],
        compiler_params=_cparams(("parallel", "parallel", "parallel"), 48),
        name="nsa_attention",
    )(z, nsa_cos, nsa_sin, z, kc, vct, ksa, kw, vswt, ovl)


MERGE_TM = 256


def _layer_norm(y, g, b):
    mu = jnp.mean(y, axis=-1, keepdims=True)
    var = jnp.mean(jnp.square(y - mu), axis=-1, keepdims=True)
    return (y - mu) * lax.rsqrt(var + LN_EPS) * g + b


def _merge_kernel(x_ref, yc_ref, or_ref, on_ref, g0_ref, g1_ref, g2_ref, wr_ref, wn_ref, wo_ref, lg_ref, lb_ref, o_ref,
                  op_ref):
    y_ret = jnp.dot(or_ref[...].astype(BF16), wr_ref[...], preferred_element_type=F32)
    y_nsa = jnp.dot(on_ref[...].astype(BF16), wn_ref[...], preferred_element_type=F32)
    gate = lambda ref: jax.nn.sigmoid(ref[...].astype(F32))
    m = gate(g0_ref) * yc_ref[...] + gate(g1_ref) * y_ret + gate(g2_ref) * y_nsa
    h = jnp.dot(m.astype(BF16), wo_ref[...], preferred_element_type=F32)
    y = _layer_norm(DN_ALPHA * x_ref[...] + h, lg_ref[...], lb_ref[...])
    o_ref[...] = y
    _store_chunked(op_ref, _pack_rows(y))


def merge_block(x, y_conv, o_ret, o_nsa, z, ret_w_o, nsa_w_o, w_out, ln_g, ln_b):
    n = x.shape[0]
    tm = MERGE_TM
    d = D_MODEL
    rowd = lambda i: (i, 0)
    const = lambda i: (0, 0)
    return pl.pallas_call(
        _merge_kernel,
        out_shape=(jax.ShapeDtypeStruct((n, d), F32), jax.ShapeDtypeStruct((n * ROW_CHUNKS, LANES), U32)),
        grid=(n // tm,),
        in_specs=[pl.BlockSpec((tm, d), rowd), pl.BlockSpec((tm, d), rowd),
                  pl.BlockSpec((tm, o_ret.shape[1]), rowd), pl.BlockSpec((tm, o_nsa.shape[1]), rowd),
                  pl.BlockSpec((tm, d), lambda i: (i, Z_MG // d)),
                  pl.BlockSpec((tm, d), lambda i: (i, Z_MG // d + 1)),
                  pl.BlockSpec((tm, d), lambda i: (i, Z_MG // d + 2)),
                  pl.BlockSpec(ret_w_o.shape, const), pl.BlockSpec(nsa_w_o.shape, const), pl.BlockSpec(w_out.shape, const),
                  pl.BlockSpec((1, d), const), pl.BlockSpec((1, d), const)],
        out_specs=(pl.BlockSpec((tm, d), rowd), pl.BlockSpec((tm * ROW_CHUNKS, LANES), rowd)),
        compiler_params=_cparams(("parallel",), 48),
        name="merge_ln1",
    )(x, y_conv, o_ret, o_nsa, z, z, z, ret_w_o, nsa_w_o, w_out, ln_g, ln_b)


RT_TM = 256


def _stable_rank(v):
    n = v.shape[0]
    row = lax.broadcasted_iota(I32, v.shape, 0)
    rank = jnp.zeros(v.shape, F32)
    for i in range(n):
        r = v[i:i + 1, :]
        rank = rank + jnp.where(row > i, jnp.where(r >= v, 1.0, 0.0), jnp.where(r > v, 1.0, 0.0))
    return rank


def _router_kernel(x_ref, wr_ref, b_ref, ltri_ref, utri_ref, eidx_ref, rnk_ref, wts_ref, cnt_ref, carry):
    @pl.when(pl.program_id(0) == 0)
    def _():
        carry[...] = jnp.zeros_like(carry)

    tm = x_ref.shape[0]
    ne = N_EXPERTS
    per = ne // N_EXPERT_GROUPS
    logits = jnp.dot(x_ref[...].astype(BF16), wr_ref[...], preferred_element_type=F32)
    s = jax.nn.sigmoid(logits.T[0:ne])
    sb = s + b_ref[...]
    sub = lax.broadcasted_iota(I32, (per, tm), 0)
    gscore = []
    for gi in range(N_EXPERT_GROUPS):
        v = sb[gi * per:(gi + 1) * per]
        m1 = jnp.max(v, axis=0, keepdims=True)
        first = jnp.min(jnp.where(v == m1, sub, per), axis=0, keepdims=True)
        m2 = jnp.max(jnp.where(sub == first, -jnp.inf, v), axis=0, keepdims=True)
        gscore.append(m1 + m2)
    gscore = jnp.concatenate(gscore, axis=0)
    gkeep = jnp.where(_stable_rank(gscore) < float(TOPK_GROUPS), 1.0, 0.0)
    ekeep = jnp.concatenate([jnp.broadcast_to(gkeep[gi:gi + 1], (per, tm)) for gi in range(N_EXPERT_GROUPS)], axis=0)
    sel = jnp.where(_stable_rank(jnp.where(ekeep > 0.0, sb, -jnp.inf)) < float(TOPK), 1.0, 0.0)
    ssel = s * sel
    gate = ssel / jnp.sum(ssel, axis=0, keepdims=True) * ROUTED_SCALE

    selb = sel.astype(BF16)
    slot = jnp.dot(ltri_ref[...], selb, preferred_element_type=F32)
    incl = jnp.dot(selb, utri_ref[...], preferred_element_type=F32)
    rnk = carry[...] + incl - 1.0
    carry[...] = carry[...] + incl[:, tm - 1:tm]
    erow = lax.broadcasted_iota(I32, (ne, tm), 0).astype(F32)
    es, rs, ws = [], [], []
    for k in range(TOPK):
        pick = jnp.where(slot == float(k), sel, 0.0)
        es.append(jnp.sum(pick * erow, axis=0, keepdims=True))
        rs.append(jnp.sum(pick * rnk, axis=0, keepdims=True))
        ws.append(jnp.sum(pick * gate, axis=0, keepdims=True))
    eidx_ref[...] = jnp.concatenate(es, axis=0).astype(I32)
    rnk_ref[...] = jnp.concatenate(rs, axis=0).astype(I32)
    wts_ref[...] = jnp.concatenate(ws + [jnp.zeros((LANES - TOPK, tm), F32)], axis=0).T
    cnt_ref[...] = jnp.broadcast_to(carry[...], cnt_ref.shape).astype(I32)


def moe_route(x, router_w_pad, router_b_col):
    n = x.shape[0]
    tm = RT_TM
    ne = N_EXPERTS
    ltri = jnp.asarray(np.tril(np.ones((ne, ne), np.float32), -1), BF16)
    utri = jnp.asarray(np.triu(np.ones((tm, tm), np.float32)), BF16)
    const = lambda i: (0, 0)
    return pl.pallas_call(
        _router_kernel,
        out_shape=(jax.ShapeDtypeStruct((TOPK, n), I32), jax.ShapeDtypeStruct((TOPK, n), I32),
                   jax.ShapeDtypeStruct((n, LANES), F32), jax.ShapeDtypeStruct((ne, LANES), I32)),
        grid=(n // tm,),
        in_specs=[pl.BlockSpec((tm, D_MODEL), lambda i: (i, 0)),
                  pl.BlockSpec(router_w_pad.shape, const), pl.BlockSpec((ne, 1), const),
                  pl.BlockSpec((ne, ne), const), pl.BlockSpec((tm, tm), const)],
        out_specs=(pl.BlockSpec((TOPK, tm), lambda i: (0, i)), pl.BlockSpec((TOPK, tm), lambda i: (0, i)),
                   pl.BlockSpec((tm, LANES), lambda i: (i, 0)), pl.BlockSpec((ne, LANES), const)),
        scratch_shapes=[pltpu.VMEM((ne, 1), F32)],
        compiler_params=_cparams(("arbitrary",)),
        name="moe_route",
    )(x, router_w_pad, router_b_col, ltri, utri)


def _moe_rows(n_tokens):
    return n_tokens * TOPK + N_EXPERTS * MOE_TILE


def _plan_kernel(cnt_ref, off_ref, texp_ref, nused_ref):
    shift = MOE_TILE.bit_length() - 1

    def per_expert(e, carry):
        off, ti = carry
        off_ref[e] = off
        ntile = lax.shift_right_logical(cnt_ref[e] + (MOE_TILE - 1), shift)

        def mark(j, c):
            texp_ref[ti + j] = e
            return c

        lax.fori_loop(0, ntile, mark, 0)
        return off + ntile * MOE_TILE, ti + ntile

    _, used = lax.fori_loop(0, N_EXPERTS, per_expert, (jnp.int32(0), jnp.int32(0)))
    nused_ref[0] = used

    def fill(j, c):
        texp_ref[j] = N_EXPERTS - 1
        return c

    lax.fori_loop(used, texp_ref.shape[0], fill, 0)


def moe_plan(counts, n_tokens):
    nt = _moe_rows(n_tokens) // MOE_TILE
    smem = pl.BlockSpec(memory_space=pltpu.SMEM)
    return pl.pallas_call(
        _plan_kernel,
        out_shape=(jax.ShapeDtypeStruct((N_EXPERTS,), I32), jax.ShapeDtypeStruct((nt,), I32),
                   jax.ShapeDtypeStruct((1,), I32)),
        in_specs=[smem],
        out_specs=(smem, smem, smem),
        name="moe_plan",
    )(counts)


DSP_TB = 256


U32 = jnp.uint32
ROW_WORDS = D_MODEL // 2
ROW_CHUNKS = ROW_WORDS // LANES
assert ROW_CHUNKS == 4


def _pack_rows(y):
    half = y.shape[1] // 2
    bits = lambda v: lax.bitcast_convert_type(v.astype(jnp.bfloat16).astype(F32), U32)
    return bits(y[:, :half]) | lax.shift_right_logical(bits(y[:, half:]), jnp.uint32(16))


def _unpack_rows(words, dtype):
    hi = lax.bitcast_convert_type(words & jnp.uint32(0xFFFF0000), F32)
    lo = lax.bitcast_convert_type(lax.shift_left(words, jnp.uint32(16)), F32)
    return jnp.concatenate([hi, lo], axis=1).astype(dtype)


def _load_chunked(ref, rows):
    return jnp.concatenate([ref[pl.ds(s, rows, stride=ROW_CHUNKS), :] for s in range(ROW_CHUNKS)], axis=1)


def _store_chunked(ref, words):
    rows = words.shape[0]
    for s in range(ROW_CHUNKS):
        ref[pl.ds(s, rows, stride=ROW_CHUNKS), :] = words[:, s * LANES:(s + 1) * LANES]


def _dst_kernel(off_ref, eidx_ref, rnk_ref, dst_ref):
    e = eidx_ref[...]
    base = jnp.zeros(e.shape, I32)
    for x in range(N_EXPERTS):
        base = jnp.where(e == x, off_ref[x], base)
    dst_ref[...] = (base + rnk_ref[...]) * ROW_CHUNKS


def moe_dst(eidx, rnk, off):
    k, n = eidx.shape
    tb = 2048
    blk = pl.BlockSpec((k, tb), lambda i: (0, i))
    return pl.pallas_call(
        _dst_kernel,
        out_shape=jax.ShapeDtypeStruct((k, n), I32),
        grid=(n // tb,),
        in_specs=[pl.BlockSpec(memory_space=pltpu.SMEM), blk, blk],
        out_specs=blk,
        compiler_params=_cparams(("parallel",)),
        name="moe_dst",
    )(off, eidx, rnk)


def _tile_copy(src_ref, src_row, dst_ref, dst_row, sem):
    return pltpu.make_async_copy(src_ref.at[pl.ds(pl.multiple_of(src_row, ROW_CHUNKS), ROW_CHUNKS)],
                                 dst_ref.at[pl.ds(pl.multiple_of(dst_row, ROW_CHUNKS), ROW_CHUNKS)], sem)


def _dispatch_kernel(dst_ref, off_ref, cnt_ref, x_ref, xs_ref, zbuf, sem, zsem):
    tb = x_ref.shape[0] // ROW_CHUNKS
    zrows = MOE_TILE * ROW_CHUNKS

    def pad_copy(e):
        cnt = cnt_ref[e]
        rem = jnp.bitwise_and(cnt, MOE_TILE - 1)
        start = pl.multiple_of((off_ref[e] + cnt - rem) * ROW_CHUNKS, zrows)
        return rem != 0, pltpu.make_async_copy(zbuf, xs_ref.at[pl.ds(start, zrows)], zsem)

    @pl.when(pl.program_id(0) == 0)
    def _():
        zbuf[...] = jnp.zeros_like(zbuf)

        def start(e, c):
            has_pad, cp = pad_copy(e)

            @pl.when(has_pad)
            def _():
                cp.start()
            return c

        def wait(e, c):
            has_pad, cp = pad_copy(e)

            @pl.when(has_pad)
            def _():
                cp.wait()
            return c

        lax.fori_loop(0, N_EXPERTS, start, 0)
        lax.fori_loop(0, N_EXPERTS, wait, 0)

    def issue(t, c):
        for k in range(TOPK):
            _tile_copy(x_ref, t * ROW_CHUNKS, xs_ref, dst_ref[k, t], sem).start(priority=k % 2)
        return c

    def drain(t, c):
        for k in range(TOPK):
            _tile_copy(x_ref, 0, xs_ref, 0, sem).wait()
        return c

    lax.fori_loop(0, tb, issue, 0)
    lax.fori_loop(0, tb, drain, 0)


def moe_dispatch(xc, dst, off, counts):
    n = xc.shape[0] // ROW_CHUNKS
    tb = DSP_TB
    smem_all = pl.BlockSpec(memory_space=pltpu.SMEM)
    smem_blk = pl.BlockSpec((TOPK, tb), lambda i: (0, i), memory_space=pltpu.SMEM)
    return pl.pallas_call(
        _dispatch_kernel,
        out_shape=jax.ShapeDtypeStruct((_moe_rows(n) * ROW_CHUNKS, LANES), xc.dtype),
        grid=(n // tb,),
        in_specs=[smem_blk, smem_all, smem_all, pl.BlockSpec((tb * ROW_CHUNKS, LANES), lambda i: (i, 0))],
        out_specs=pl.BlockSpec(memory_space=pl.ANY),
        scratch_shapes=[pltpu.VMEM((MOE_TILE * ROW_CHUNKS, LANES), xc.dtype), pltpu.SemaphoreType.DMA(()),
                        pltpu.SemaphoreType.DMA(())],
        compiler_params=_cparams(("arbitrary",)),
        name="moe_dispatch",
    )(dst, off, counts, xc)


def _expert_kernel(texp_ref, nused_ref, xs_ref, w1_ref, w3_ref, w2_ref, ys_ref, w1b, w3b, w2b, last):
    i = pl.program_id(0)
    e = texp_ref[i]

    @pl.when(i == 0)
    def _():
        last[0] = -1

    @pl.when(e != last[0])
    def _():
        w1b[...] = w1_ref[0, 0].astype(BF16)
        w3b[...] = w3_ref[0, 0].astype(BF16)
        w2b[...] = w2_ref[0, 0].astype(BF16)
        last[0] = e

    @pl.when(i < nused_ref[0])
    def _():
        xb = _unpack_rows(_load_chunked(xs_ref, MOE_TILE), BF16)
        h1 = jnp.dot(xb, w1b[...], preferred_element_type=F32)
        h3 = jnp.dot(xb, w3b[...], preferred_element_type=F32)
        h = h1 * jax.nn.sigmoid(h1) * h3
        y = jnp.dot(h.astype(BF16), w2b[...], preferred_element_type=F32)
        _store_chunked(ys_ref, _pack_rows(y))

    @pl.when(i >= nused_ref[0])
    def _():
        ys_ref[...] = jnp.zeros_like(ys_ref)


def moe_experts(xs, texp, nused, w1, w3, w2, layer):
    crows = xs.shape[0]
    d, f = w1.shape[2], w1.shape[3]
    blk = MOE_TILE * ROW_CHUNKS
    nt = crows // blk
    grid_spec = pltpu.PrefetchScalarGridSpec(
        num_scalar_prefetch=2,
        grid=(nt,),
        in_specs=[pl.BlockSpec((blk, LANES), lambda i, te, nu: (jnp.where(i < nu[0], i, 0), 0)),
                  pl.BlockSpec((1, 1, d, f), lambda i, te, nu: (layer, te[i], 0, 0)),
                  pl.BlockSpec((1, 1, d, f), lambda i, te, nu: (layer, te[i], 0, 0)),
                  pl.BlockSpec((1, 1, f, d), lambda i, te, nu: (layer, te[i], 0, 0))],
        out_specs=pl.BlockSpec((blk, LANES), lambda i, te, nu: (i, 0)),
        scratch_shapes=[pltpu.VMEM((d, f), BF16), pltpu.VMEM((d, f), BF16), pltpu.VMEM((f, d), BF16),
                        pltpu.SMEM((1,), I32)],
    )
    return pl.pallas_call(
        _expert_kernel,
        out_shape=jax.ShapeDtypeStruct((crows, LANES), U32),
        grid_spec=grid_spec,
        compiler_params=_cparams(("arbitrary",), 48),
        name="moe_experts",
    )(texp, nused, xs, w1, w3, w2)


CMB_TB = 128


def _combine_kernel(dst_ref, x_ref, wts_ref, ws1_ref, ws3_ref, ws2_ref, lg_ref, lb_ref, ys_ref, o_ref, buf, sem):
    tb = x_ref.shape[0]

    def issue(t, c):
        for k in range(TOPK):
            _tile_copy(ys_ref, dst_ref[k, t], buf.at[k], t * ROW_CHUNKS, sem).start(priority=k % 2)
        return c

    def drain(t, c):
        for k in range(TOPK):
            _tile_copy(ys_ref, 0, buf.at[0], 0, sem).wait()
        return c

    lax.fori_loop(0, tb, issue, 0)
    x = x_ref[...]
    xb = x.astype(BF16)
    h1 = jnp.dot(xb, ws1_ref[...], preferred_element_type=F32)
    h3 = jnp.dot(xb, ws3_ref[...], preferred_element_type=F32)
    y = jnp.dot((h1 * jax.nn.sigmoid(h1) * h3).astype(BF16), ws2_ref[...], preferred_element_type=F32)
    lax.fori_loop(0, tb, drain, 0)
    w = wts_ref[...]
    routed = y
    for k in range(TOPK):
        routed = routed + w[:, k:k + 1] * _unpack_rows(_load_chunked(buf.at[k], tb), F32)
    o_ref[...] = _layer_norm(DN_ALPHA * x + routed, lg_ref[...], lb_ref[...])


def moe_combine(x, ys, dst, wts, ws1, ws3, ws2, ln_g, ln_b):
    n, d = x.shape
    tb = CMB_TB
    smem_blk = pl.BlockSpec((TOPK, tb), lambda i: (0, i), memory_space=pltpu.SMEM)
    const = lambda i: (0, 0)
    return pl.pallas_call(
        _combine_kernel,
        out_shape=jax.ShapeDtypeStruct((n, d), F32),
        grid=(n // tb,),
        in_specs=[smem_blk,
                  pl.BlockSpec((tb, d), lambda i: (i, 0)), pl.BlockSpec((tb, LANES), lambda i: (i, 0)),
                  pl.BlockSpec(ws1.shape, const), pl.BlockSpec(ws3.shape, const), pl.BlockSpec(ws2.shape, const),
                  pl.BlockSpec((1, d), const), pl.BlockSpec((1, d), const),
                  pl.BlockSpec(memory_space=pl.ANY)],
        out_specs=pl.BlockSpec((tb, d), lambda i: (i, 0)),
        scratch_shapes=[pltpu.VMEM((TOPK, tb * ROW_CHUNKS, LANES), U32), pltpu.SemaphoreType.DMA(())],
        compiler_params=_cparams(("parallel",), 48),
        name="moe_combine_ln2",
    )(dst, x, wts, ws1, ws3, ws2, ln_g, ln_b, ys)


def moe_block(x, xp, router_w, router_b, w1, w3, w2, layer, ws1, ws3, ws2, ln_g, ln_b):
    n, d = x.shape
    rw = jnp.pad(router_w, ((0, 0), (0, LANES - N_EXPERTS))).astype(BF16)
    eidx, rnk, wts, cnt = moe_route(x, rw, router_b.reshape(N_EXPERTS, 1))
    counts = cnt[:, 0]
    off, texp, nused = moe_plan(counts, n)
    dst = moe_dst(eidx, rnk, off)
    xs = moe_dispatch(xp, dst, off, counts)
    ys = moe_experts(xs, texp, nused, w1, w3, w2, layer)
    return moe_combine(x, ys, dst, wts, ws1.astype(BF16), ws3.astype(BF16), ws2.astype(BF16),
                       ln_g.reshape(1, -1), ln_b.reshape(1, -1))


def nsa_rope_tables(positions):
    batch, seq = positions.shape
    inv, sgn = _nsa_inv_freq()
    posf = positions.astype(F32)
    tok = rope_tables(posf.reshape(batch * seq, 1), inv, sgn, tm=512)
    end = posf[:, CMP_BLOCK - 1::CMP_STRIDE]
    npad = _n_cmp_pad(seq)
    end = jnp.pad(end, ((0, 0), (0, npad - end.shape[1])))
    cmp = rope_tables(end.reshape(batch * npad, 1), inv, sgn, tm=npad)
    return tok, cmp


def nsa_branch(z, tok_tab, cmp_tab, wk, wv, ovl, *, batch, seq):
    n = batch * seq
    w = NSA_GROUPS * NSA_DH
    xk = z[:, Z_NKV:Z_NKV + w].reshape(n // CMP_HALF, CMP_ROW)
    xv = z[:, Z_NKV + w:Z_NKV + 2 * w].reshape(n // CMP_HALF, CMP_ROW)
    kc, vct = nsa_compress(xk, xv, wk, wv, cmp_tab[0], cmp_tab[1], batch=batch)
    ksa, kw, vswt = nsa_kv_prep(z, tok_tab[0], tok_tab[1], batch=batch, seq=seq)
    return nsa_attention(z, tok_tab[0], tok_tab[1], kc, vct, ksa, kw, vswt, ovl, batch=batch, seq=seq)


def kernel(x, positions, w_in, conv_dw, conv_db, conv_ln_g, conv_ln_b, conv_w_pw, ret_w_o, nsa_pe_k, nsa_w1_k, nsa_w2_k,
           nsa_pe_v, nsa_w1_v, nsa_w2_v, nsa_w_o, w_out, ln1_g, ln1_b, router_w, router_b, moe_w1, moe_w3, moe_w2,
           shared_w1, shared_w3, shared_w2, ln2_g, ln2_b):
    batch, seq, d = x.shape
    n = batch * seq
    xf = x.reshape(n, d)
    row = lambda v: v.reshape(1, -1)

    ret_inv, ret_sgn = _ret_inv_freq()
    ret_tab = rope_tables(positions.astype(F32).reshape(n, 1), ret_inv, ret_sgn, tm=512)
    tok_tab, cmp_tab = nsa_rope_tables(positions)
    ret_consts = _ret_tables()
    ovl = _overlap_matrix(seq)

    for l in range(w_in.shape[0]):
        z = matmul(xf, _layout_w_in(w_in[l]), tm=2048, tn=512, out_dtype=BF16, name="in_proj")
        y_conv = conv_branch(z, conv_dw[l], row(conv_db[l]), row(conv_ln_g[l]), row(conv_ln_b[l]),
                             conv_w_pw[l].astype(BF16), batch=batch, seq=seq)
        o_ret = retention_branch(z, ret_tab[0], ret_tab[1], ret_consts, batch=batch, seq=seq)
        o_nsa = nsa_branch(z, tok_tab, cmp_tab, _cmp_weights(nsa_pe_k[l], nsa_w1_k[l], nsa_w2_k[l]),
                           _cmp_weights(nsa_pe_v[l], nsa_w1_v[l], nsa_w2_v[l]), ovl, batch=batch, seq=seq)
        x1, x1p = merge_block(xf, y_conv, o_ret, o_nsa, z, ret_w_o[l].astype(BF16), nsa_w_o[l].astype(BF16),
                              w_out[l].astype(BF16), row(ln1_g[l]), row(ln1_b[l]))
        xf = moe_block(x1, x1p, router_w[l], router_b[l], moe_w1, moe_w3, moe_w2, l,
                       shared_w1[l], shared_w3[l], shared_w2[l], ln2_g[l], ln2_b[l])
    return xf.reshape(batch, seq, d)
```

```python
import functools
import math

import jax
import jax.numpy as jnp
import numpy as np
from jax import lax
from jax.experimental import pallas as pl
from jax.experimental.pallas import tpu as pltpu

F32 = jnp.float32
BF16 = jnp.bfloat16
I32 = jnp.int32

D_MODEL = 1024
DEPTH = 4
CONV_CH = 512
CONV_WIDTH = 31
RET_HEADS = 4
RET_DK = 128
RET_DV = 256
RET_CHUNK = 128
RET_ROPE_BASE = 10000.0
NSA_HEADS = 8
NSA_GROUPS = 2
NSA_HPG = NSA_HEADS // NSA_GROUPS
NSA_DH = 64
NSA_N_BRANCH = 3
CMP_BLOCK = 32
CMP_STRIDE = 16
CMP_HIDDEN = 256
SEL_BLOCK = 64
SEL_TOPK = 16
WINDOW = 512
ROPE_THETA = 500000.0
ROT_DIM = NSA_DH // 4
N_EXPERTS = 64
N_EXPERT_GROUPS = 8
TOPK_GROUPS = 4
TOPK = 8
D_EXPERT = 256
D_SHARED = 256
ROUTED_SCALE = 2.5
DN_ALPHA = (2.0 * DEPTH) ** 0.25
LN_EPS = 1e-5
NEG = -1e30

LANES = 128

Z_MG = 0
Z_CA = 3072
Z_CB = 3584
Z_RQ = 4096
Z_RK = 4608
Z_RV = 5120
Z_RG = 6144
Z_NQ = 7168
Z_NKV = 7680
Z_NG = 8448
Z_W = 8704

_IN_WIDTHS = (512, 512, 512, 512, 1024, 1024, 512, 128, 128, 128, 128, 128, 128, 24, 3072)
_IN_OFFS = tuple(int(v) for v in np.concatenate([[0], np.cumsum(_IN_WIDTHS)[:-1]]))

MOE_TILE = 512
MOE_ROWS_PER_TOKEN = TOPK


def _cparams(sem, vmem_mb=None):
    kw = dict(dimension_semantics=sem)
    if vmem_mb is not None:
        kw["vmem_limit_bytes"] = vmem_mb * 1024 * 1024
    return pltpu.CompilerParams(**kw)


def _mm_kernel(x_ref, w_ref, o_ref):
    o_ref[...] = jnp.dot(x_ref[...].astype(BF16), w_ref[...], preferred_element_type=F32).astype(o_ref.dtype)


def matmul(x, w, *, tm, tn, out_dtype=F32, x_col_block=0, k=None, name="mm"):
    m = x.shape[0]
    kk, n = w.shape
    if k is None:
        k = kk
    return pl.pallas_call(
        _mm_kernel,
        out_shape=jax.ShapeDtypeStruct((m, n), out_dtype),
        grid=(m // tm, n // tn),
        in_specs=[pl.BlockSpec((tm, k), lambda i, j: (i, x_col_block)),
                  pl.BlockSpec((k, tn), lambda i, j: (0, j))],
        out_specs=pl.BlockSpec((tm, tn), lambda i, j: (i, j)),
        compiler_params=_cparams(("parallel", "arbitrary"), 48),
        name=name,
    )(x, w)


def _rope_table_kernel(pos_ref, inv_ref, sgn_ref, cos_ref, sin_ref):
    ang = pos_ref[...] * inv_ref[...]
    cos_ref[...] = jnp.cos(ang)
    sin_ref[...] = jnp.sin(ang) * sgn_ref[...]


def rope_tables(pos_col, inv, sgn, *, tm):
    n = pos_col.shape[0]
    w = inv.shape[1]
    return pl.pallas_call(
        _rope_table_kernel,
        out_shape=(jax.ShapeDtypeStruct((n, w), F32), jax.ShapeDtypeStruct((n, w), F32)),
        grid=(n // tm,),
        in_specs=[pl.BlockSpec((tm, 1), lambda i: (i, 0)),
                  pl.BlockSpec((1, w), lambda i: (0, 0)),
                  pl.BlockSpec((1, w), lambda i: (0, 0))],
        out_specs=(pl.BlockSpec((tm, w), lambda i: (i, 0)), pl.BlockSpec((tm, w), lambda i: (i, 0))),
        compiler_params=_cparams(("parallel",)),
        name="rope_tables",
    )(pos_col, inv, sgn)


def _ret_inv_freq():
    inv = 1.0 / jnp.power(jnp.float32(RET_ROPE_BASE), jnp.linspace(0.0, 1.0, RET_DK // 2, dtype=F32))
    inv = jnp.concatenate([inv, inv])[None, :]
    sgn = jnp.concatenate([-jnp.ones((RET_DK // 2,), F32), jnp.ones((RET_DK // 2,), F32)])[None, :]
    return inv, sgn


def _nsa_inv_freq():
    half = ROT_DIM // 2
    inv = jnp.power(jnp.float32(ROPE_THETA), -jnp.arange(0, ROT_DIM, 2, dtype=F32) / ROT_DIM)
    z = jnp.zeros((NSA_DH - ROT_DIM,), F32)
    inv64 = jnp.concatenate([inv, inv, z])
    sgn64 = jnp.concatenate([-jnp.ones((half,), F32), jnp.ones((half,), F32), z])
    return jnp.concatenate([inv64, inv64])[None, :], jnp.concatenate([sgn64, sgn64])[None, :]


def _nsa_rope(x, cos, sin):
    w = x.shape[1]
    half = ROT_DIM // 2
    lane = lax.broadcasted_iota(I32, x.shape, 1) % NSA_DH
    partner = jnp.where(lane < half, pltpu.roll(x, w - half, 1), pltpu.roll(x, half, 1))
    return x * cos + partner * sin


CONV_TT = 256
CONV_HALO = 32


def _conv_kernel(a_ref, b_ref, ah_ref, bh_ref, dw_ref, db_ref, g_ref, be_ref, wpw_ref, o_ref, ubuf, sbuf):
    i = pl.program_id(1)
    tt = a_ref.shape[0]
    u = a_ref[...].astype(F32) * jax.nn.sigmoid(b_ref[...].astype(F32))
    uh = ah_ref[...].astype(F32) * jax.nn.sigmoid(bh_ref[...].astype(F32))
    ubuf[0:CONV_HALO, :] = jnp.where(i > 0, uh, 0.0)
    ubuf[CONV_HALO:CONV_HALO + tt, :] = u
    acc = jnp.zeros((tt, CONV_CH), F32)
    base = CONV_HALO - (CONV_WIDTH - 1)
    sub = 8
    for r in range(sub):
        offs = [o for o in range(base, base + CONV_WIDTH) if o % sub == r]
        if not offs:
            continue
        span = offs[-1] - r + tt
        sbuf[0:span, :] = ubuf[r:r + span, :]
        for o in offs:
            acc = acc + dw_ref[o - base:o - base + 1, :] * sbuf[o - r:o - r + tt, :]
    acc = acc + db_ref[...]
    mu = jnp.mean(acc, axis=-1, keepdims=True)
    var = jnp.mean(jnp.square(acc - mu), axis=-1, keepdims=True)
    y = (acc - mu) * lax.rsqrt(var + LN_EPS) * g_ref[...] + be_ref[...]
    y = y * jax.nn.sigmoid(y)
    o_ref[...] = jnp.dot(y.astype(BF16), wpw_ref[...], preferred_element_type=F32)


def conv_branch(z, dw, db, ln_g, ln_b, w_pw_bf, *, batch, seq):
    tt = CONV_TT
    nt = seq // tt
    r = tt // CONV_HALO
    ca, cb = Z_CA // CONV_CH, Z_CB // CONV_CH

    def halo_map(col):
        return lambda b, i: (jnp.maximum((b * nt + i) * r - 1, 0), col)

    return pl.pallas_call(
        _conv_kernel,
        out_shape=jax.ShapeDtypeStruct((batch * seq, D_MODEL), F32),
        grid=(batch, nt),
        in_specs=[pl.BlockSpec((tt, CONV_CH), lambda b, i: (b * nt + i, ca)),
                  pl.BlockSpec((tt, CONV_CH), lambda b, i: (b * nt + i, cb)),
                  pl.BlockSpec((CONV_HALO, CONV_CH), halo_map(ca)),
                  pl.BlockSpec((CONV_HALO, CONV_CH), halo_map(cb)),
                  pl.BlockSpec((CONV_WIDTH, CONV_CH), lambda b, i: (0, 0)),
                  pl.BlockSpec((1, CONV_CH), lambda b, i: (0, 0)),
                  pl.BlockSpec((1, CONV_CH), lambda b, i: (0, 0)),
                  pl.BlockSpec((1, CONV_CH), lambda b, i: (0, 0)),
                  pl.BlockSpec((CONV_CH, D_MODEL), lambda b, i: (0, 0))],
        out_specs=pl.BlockSpec((tt, D_MODEL), lambda b, i: (b * nt + i, 0)),
        scratch_shapes=[pltpu.VMEM((CONV_HALO + tt, CONV_CH), F32), pltpu.VMEM((CONV_HALO + tt, CONV_CH), F32)],
        compiler_params=_cparams(("parallel", "parallel")),
        name="conv_branch",
    )(z, z, z, z, dw, db, ln_g, ln_b, w_pw_bf)


RET_TQ = 512


def _ret_tables():
    h, c = RET_HEADS, RET_CHUNK
    log_gamma = jnp.log1p(-jnp.exp2(-5.0 - jnp.arange(h, dtype=F32)))
    idx = jnp.arange(c, dtype=F32)
    diff = idx[:, None] - idx[None, :]
    dmat = jnp.where(diff >= 0, jnp.exp(log_gamma[:, None, None] * jnp.maximum(diff, 0.0)), 0.0).astype(F32)
    xi = jnp.exp(log_gamma[:, None] * (idx + 1.0)).astype(F32)
    zeta = jnp.exp(log_gamma[:, None] * (c - 1.0 - idx)).astype(F32)
    decay = jnp.exp(log_gamma * c).astype(F32)
    xi_b = jnp.broadcast_to(xi[:, :, None], (h, c, RET_DV))
    zeta_b = jnp.broadcast_to(zeta[:, :, None], (h, c, RET_DV))
    decay_b = jnp.broadcast_to(decay[:, None, None], (h, RET_DK, RET_DV))
    return dmat, xi_b, zeta_b, decay_b


def _ret_kernel(q_ref, k_ref, v_ref, g_ref, cos_ref, sin_ref, dmat_ref, xi_ref, zeta_ref, dec_ref, o_ref, r_ref):
    @pl.when(pl.program_id(1) == 0)
    def _():
        r_ref[...] = jnp.zeros_like(r_ref)

    c = RET_CHUNK
    n_chunks = q_ref.shape[0] // c
    for ci in range(n_chunks):
        rows = slice(ci * c, (ci + 1) * c)
        cos = cos_ref[rows, :]
        sin = sin_ref[rows, :]
        for h in range(RET_HEADS):
            qk_cols = slice(h * RET_DK, (h + 1) * RET_DK)
            v_cols = slice(h * RET_DV, (h + 1) * RET_DV)
            q = q_ref[rows, qk_cols].astype(F32)
            k = k_ref[rows, qk_cols].astype(F32)
            q = q * cos + pltpu.roll(q, RET_DK // 2, 1) * sin
            k = (k * cos + pltpu.roll(k, RET_DK // 2, 1) * sin) * (RET_DK ** -0.5)
            v = v_ref[rows, v_cols].astype(F32)
            qb = q.astype(BF16)
            kb = k.astype(BF16)
            inner = lax.dot_general(qb, kb, (((1,), (1,)), ((), ())), preferred_element_type=F32) * dmat_ref[h]
            r_old = r_ref[h]
            o = (jnp.dot(inner.astype(BF16), v.astype(BF16), preferred_element_type=F32)
                 + jnp.dot(qb, r_old.astype(BF16), preferred_element_type=F32) * xi_ref[h])
            vz = (v * zeta_ref[h]).astype(BF16)
            r_ref[h] = r_old * dec_ref[h] + jnp.dot(k.T.astype(BF16), vz, preferred_element_type=F32)
            mu = jnp.mean(o, axis=-1, keepdims=True)
            var = jnp.mean(jnp.square(o - mu), axis=-1, keepdims=True)
            on = (o - mu) * lax.rsqrt(var + LN_EPS)
            g = g_ref[rows, v_cols].astype(F32)
            o_ref[rows, v_cols] = g * jax.nn.sigmoid(g) * on


def retention_branch(z, ret_cos, ret_sin, tables, *, batch, seq):
    tq = RET_TQ
    nt = seq // tq
    dmat, xi_b, zeta_b, decay_b = tables
    qw = RET_HEADS * RET_DK
    vw = RET_HEADS * RET_DV
    row = lambda b, i: b * nt + i
    full3 = lambda b, i: (0, 0, 0)
    return pl.pallas_call(
        _ret_kernel,
        out_shape=jax.ShapeDtypeStruct((batch * seq, vw), F32),
        grid=(batch, nt),
        in_specs=[pl.BlockSpec((tq, qw), lambda b, i: (row(b, i), Z_RQ // qw)),
                  pl.BlockSpec((tq, qw), lambda b, i: (row(b, i), Z_RK // qw)),
                  pl.BlockSpec((tq, vw), lambda b, i: (row(b, i), Z_RV // vw)),
                  pl.BlockSpec((tq, vw), lambda b, i: (row(b, i), Z_RG // vw)),
                  pl.BlockSpec((tq, RET_DK), lambda b, i: (row(b, i), 0)),
                  pl.BlockSpec((tq, RET_DK), lambda b, i: (row(b, i), 0)),
                  pl.BlockSpec(dmat.shape, full3),
                  pl.BlockSpec(xi_b.shape, full3),
                  pl.BlockSpec(zeta_b.shape, full3),
                  pl.BlockSpec(decay_b.shape, full3)],
        out_specs=pl.BlockSpec((tq, vw), lambda b, i: (row(b, i), 0)),
        scratch_shapes=[pltpu.VMEM((RET_HEADS, RET_DK, RET_DV), F32)],
        compiler_params=_cparams(("parallel", "arbitrary"), 48),
        name="retention",
    )(z, z, z, z, ret_cos, ret_sin, dmat, xi_b, zeta_b, decay_b)


def _layout_w_in(w):
    seg = lambda i: w[:, _IN_OFFS[i]:_IN_OFFS[i] + _IN_WIDTHS[i]]
    order = (14, 0, 1, 2, 3, 4, 5, 6, 7, 8, 9, 10, 11, 12, 13)
    parts = [seg(i) for i in order]
    used = sum(_IN_WIDTHS)
    parts.append(jnp.zeros((w.shape[0], Z_W - used), w.dtype))
    return jnp.concatenate(parts, axis=1).astype(BF16)


CMP_HALF = CMP_BLOCK // 2
CMP_ROW = CMP_HALF * NSA_GROUPS * NSA_DH


def _n_cmp_pad(seq):
    return seq // CMP_STRIDE


def _cmp_weights(pe, w1, w2):
    g = NSA_GROUPS
    eye = jnp.eye(g, dtype=F32)
    w = w1.reshape(2, CMP_HALF, NSA_DH, CMP_HIDDEN)
    w1ab = jnp.einsum("hldf,gk->hlgdkf", w, eye).reshape(2, CMP_ROW, g * CMP_HIDDEN).astype(BF16)
    peab = jnp.broadcast_to(pe.reshape(2, CMP_HALF, 1, NSA_DH), (2, CMP_HALF, g, NSA_DH)).reshape(2, 1, CMP_ROW)
    w2bd = jnp.einsum("fd,gk->gfkd", w2, eye).reshape(g * CMP_HIDDEN, g * NSA_DH).astype(BF16)
    return peab, w1ab, w2bd


def _cmp_mlp(x, pe_ref, w1_ref, w2_ref):
    x = x.astype(F32)
    a = jnp.dot((x + pe_ref[0]).astype(BF16), w1_ref[0], preferred_element_type=F32)
    b = jnp.dot((x + pe_ref[1]).astype(BF16), w1_ref[1], preferred_element_type=F32)
    hid = a + pltpu.roll(b, b.shape[0] - 1, 0)
    hid = hid * jax.nn.sigmoid(hid)
    return jnp.dot(hid.astype(BF16), w2_ref[...], preferred_element_type=F32)


def _compress_kernel(xk_ref, xv_ref, pek_ref, w1k_ref, w2k_ref, pev_ref, w1v_ref, w2v_ref, cos_ref, sin_ref,
                     k_ref, vt_ref):
    k = _cmp_mlp(xk_ref[...], pek_ref, w1k_ref, w2k_ref)
    k_ref[0] = _nsa_rope(k, cos_ref[...], sin_ref[...]).astype(BF16)
    vt_ref[0] = _cmp_mlp(xv_ref[...], pev_ref, w1v_ref, w2v_ref).T.astype(BF16)


def nsa_compress(xk, xv, wk, wv, cmp_cos, cmp_sin, *, batch):
    pek, w1k, w2k = wk
    pev, w1v, w2v = wv
    npad = xk.shape[0] // batch
    gw = NSA_GROUPS * NSA_DH
    c3 = lambda b: (0, 0, 0)
    c2 = lambda b: (0, 0)
    return pl.pallas_call(
        _compress_kernel,
        out_shape=(jax.ShapeDtypeStruct((batch, npad, gw), BF16), jax.ShapeDtypeStruct((batch, gw, npad), BF16)),
        grid=(batch,),
        in_specs=[pl.BlockSpec((npad, CMP_ROW), lambda b: (b, 0)),
                  pl.BlockSpec((npad, CMP_ROW), lambda b: (b, 0)),
                  pl.BlockSpec(pek.shape, c3), pl.BlockSpec(w1k.shape, c3), pl.BlockSpec(w2k.shape, c2),
                  pl.BlockSpec(pev.shape, c3), pl.BlockSpec(w1v.shape, c3), pl.BlockSpec(w2v.shape, c2),
                  pl.BlockSpec((npad, gw), lambda b: (b, 0)),
                  pl.BlockSpec((npad, gw), lambda b: (b, 0))],
        out_specs=(pl.BlockSpec((1, npad, gw), lambda b: (b, 0, 0)), pl.BlockSpec((1, gw, npad), lambda b: (b, 0, 0))),
        compiler_params=_cparams(("parallel",), 48),
        name="nsa_compress",
    )(xk, xv, pek, w1k, w2k, pev, w1v, w2v, cmp_cos, cmp_sin)


KV_TT = 512
ATT_TK = 512
ATT_KB = 128
ATT_AHEAD = 8
LOG2E = math.log2(math.e)


def _kv_prep_kernel(z_ref, cos_ref, sin_ref, ksa_ref, kw_ref, vswt_ref):
    tt = z_ref.shape[0]
    cos = cos_ref[...]
    sin = sin_ref[...]
    w = NSA_GROUPS * NSA_DH
    n_sel = ksa_ref.shape[3] - NSA_DH
    ks = _nsa_rope(z_ref[:, 2 * w:3 * w].astype(F32), cos, sin)
    vs = z_ref[:, 3 * w:4 * w].astype(F32)
    kw = _nsa_rope(z_ref[:, 4 * w:5 * w].astype(F32), cos, sin)
    vw = z_ref[:, 5 * w:6 * w].astype(F32)
    blk = (pl.program_id(1) * tt + lax.broadcasted_iota(I32, (tt, n_sel), 0)) // SEL_BLOCK
    onehot = jnp.where(blk == lax.broadcasted_iota(I32, (tt, n_sel), 1), 1.0, 0.0)
    for g in range(NSA_GROUPS):
        cols = slice(g * NSA_DH, (g + 1) * NSA_DH)
        ksa_ref[0, g] = jnp.concatenate([ks[:, cols], onehot], axis=1).astype(BF16)
        kw_ref[0, g] = kw[:, cols].astype(BF16)
        vswt_ref[0, g] = jnp.concatenate([vs[:, cols], vw[:, cols]], axis=1).T.astype(BF16)


def nsa_kv_prep(z, nsa_cos, nsa_sin, *, batch, seq):
    tt = KV_TT
    nt = seq // tt
    g = NSA_GROUPS
    w = g * NSA_DH
    n_sel = seq // SEL_BLOCK
    return pl.pallas_call(
        _kv_prep_kernel,
        out_shape=(jax.ShapeDtypeStruct((batch, g, seq, NSA_DH + n_sel), BF16),
                   jax.ShapeDtypeStruct((batch, g, seq, NSA_DH), BF16),
                   jax.ShapeDtypeStruct((batch, g, w, seq), BF16)),
        grid=(batch, nt),
        in_specs=[pl.BlockSpec((tt, 6 * w), lambda b, i: (b * nt + i, Z_NKV // (6 * w))),
                  pl.BlockSpec((tt, w), lambda b, i: (b * nt + i, 0)),
                  pl.BlockSpec((tt, w), lambda b, i: (b * nt + i, 0))],
        out_specs=(pl.BlockSpec((1, g, tt, NSA_DH + n_sel), lambda b, i: (b, 0, i, 0)),
                   pl.BlockSpec((1, g, tt, NSA_DH), lambda b, i: (b, 0, i, 0)),
                   pl.BlockSpec((1, g, w, tt), lambda b, i: (b, 0, 0, i))),
        compiler_params=_cparams(("parallel", "parallel")),
        name="nsa_kv_prep",
    )(z, nsa_cos, nsa_sin)


ATT_TQ = 256


def _overlap_matrix(seq):
    n_cmp = (seq - CMP_BLOCK) // CMP_STRIDE + 1
    n_sel = seq // SEL_BLOCK
    ii = np.arange(_n_cmp_pad(seq))[None, :]
    jj = np.arange(LANES)[:, None]
    lo = np.maximum(ii * CMP_STRIDE, jj * SEL_BLOCK)
    hi = np.minimum(ii * CMP_STRIDE + CMP_BLOCK, (jj + 1) * SEL_BLOCK)
    ov = np.maximum(hi - lo, 0).astype(np.float32) / CMP_BLOCK
    ov = np.where((ii < n_cmp) & (jj < n_sel), ov, 0.0)
    return jnp.asarray(np.tile(ov, (1, NSA_HPG)), BF16)


def _softmax_step(s, ok, vt, cols, m_ref, l_ref, a_ref):
    m_old = m_ref[:, cols]
    m_new = jnp.maximum(m_old, jnp.max(s, axis=0, keepdims=True))
    alpha = jnp.exp2(m_old - m_new)
    p = jnp.exp2(s - m_new)
    if ok is not None:
        p = jnp.where(ok, p, 0.0)
    l_ref[:, cols] = alpha * l_ref[:, cols] + jnp.sum(p, axis=0, keepdims=True)
    a_ref[:, cols] = alpha * a_ref[:, cols] + jnp.dot(vt, p.astype(BF16), preferred_element_type=F32)
    m_ref[:, cols] = m_new


def _att_kernel(q_ref, cos_ref, sin_ref, gl_ref, kc_ref, vct_ref, ksa_ref, kw_ref, vswt_ref, ovl_ref, o_ref,
                qa_s, m_sel, l_sel, a_sel, m_win, l_win, a_win):
    g = pl.program_id(1)
    qi = pl.program_id(2)
    tq = ATT_TQ
    tk = ATT_TK
    hpg = NSA_HPG
    dh = NSA_DH
    t0 = qi * tq
    seq = ksa_ref.shape[2]
    n_sel = ksa_ref.shape[3] - dh
    heads = [slice(h * tq, (h + 1) * tq) for h in range(hpg)]

    cos = jnp.concatenate([cos_ref[...]] * (hpg // 2), axis=1)
    sin = jnp.concatenate([sin_ref[...]] * (hpg // 2), axis=1)
    qt = (_nsa_rope(q_ref[...].astype(F32), cos, sin) * (dh ** -0.5 * LOG2E)).T
    q_t = jnp.concatenate([qt[h * dh:(h + 1) * dh] for h in range(hpg)], axis=1)
    q_tb = q_t.astype(BF16)
    qa_s[0:dh, :] = q_tb

    zero = jnp.zeros_like(q_tb)
    q2 = jnp.where(g == 0, jnp.concatenate([q_tb, zero], axis=0), jnp.concatenate([zero, q_tb], axis=0))
    s = jnp.dot(kc_ref[0], q2, preferred_element_type=F32)
    n_cmp = (seq - CMP_BLOCK) // CMP_STRIDE + 1
    nrow = lax.broadcasted_iota(I32, s.shape, 0)
    tcol = t0 + lax.broadcasted_iota(I32, s.shape, 1) % tq
    ok = (nrow * CMP_STRIDE + (CMP_BLOCK - 1) <= tcol) & (nrow < n_cmp)
    sm = jnp.where(ok, s, NEG)
    e = jnp.where(ok, jnp.exp2(sm - jnp.max(sm, axis=0, keepdims=True)), 0.0)
    den = jnp.sum(e, axis=0, keepdims=True)
    pb = (e / jnp.where(den > 0.0, den, 1.0)).astype(BF16)
    oc = jnp.dot(vct_ref[0], pb, preferred_element_type=F32)
    o_cmp = jnp.where(g == 0, oc[0:dh], oc[dh:2 * dh])
    pcat = jnp.concatenate([pb[:, hs] for hs in heads], axis=0)
    imp_t = jnp.dot(ovl_ref[...], pcat, preferred_element_type=F32)[0:n_sel]

    jrow = lax.broadcasted_iota(I32, (n_sel, tq), 0)
    cur = (t0 + lax.broadcasted_iota(I32, (n_sel, tq), 1)) // SEL_BLOCK
    forced = (jrow == 0) | (jrow == cur) | (jrow == cur - 1)
    imp_t = jnp.where(forced, jnp.inf, jnp.where(jrow <= cur, imp_t, -jnp.inf))
    rank = jnp.zeros((n_sel, tq), F32)
    for i in range(n_sel):
        ri = imp_t[i:i + 1, :]
        rank = rank + jnp.where(jrow > i, jnp.where(ri >= imp_t, 1.0, 0.0), jnp.where(ri > imp_t, 1.0, 0.0))
    bias_t = jnp.where(rank < float(SEL_TOPK), 0.0, NEG).astype(BF16)
    qa_s[dh:dh + n_sel, :] = jnp.concatenate([bias_t] * hpg, axis=1)

    m_sel[...] = jnp.full(m_sel.shape, NEG, F32)
    l_sel[...] = jnp.zeros(l_sel.shape, F32)
    a_sel[...] = jnp.zeros(a_sel.shape, F32)

    kb = ATT_KB
    assert tk == 2 * tq and tq % kb == 0
    krow = lax.broadcasted_iota(I32, (kb, tq), 0)
    qtime = t0 + lax.broadcasted_iota(I32, (kb, tq), 1)

    def run_blocks(n_keys, scores, mask, vt_of, m_ref, l_ref, a_ref):
        blocks = [(kk, hs) for kk in range(0, n_keys, kb) for hs in heads]
        sc = [scores(kk, hs) for kk, hs in blocks[:ATT_AHEAD]]
        for b, (kk, hs) in enumerate(blocks):
            if b + ATT_AHEAD < len(blocks):
                sc.append(scores(*blocks[b + ATT_AHEAD]))
            ok = mask(kk)
            s_b = sc[b] if ok is None else jnp.where(ok, sc[b], NEG)
            sc[b] = None
            _softmax_step(s_b, ok, vt_of(kk), hs, m_ref, l_ref, a_ref)

    def sel_keys(k0, n_keys, causal):
        scores = lambda kk, hs: jnp.dot(ksa_ref[0, 0, pl.ds(k0 + kk, kb), :], qa_s[:, hs], preferred_element_type=F32)
        mask = (lambda kk: (k0 + kk + krow) <= qtime) if causal else (lambda kk: None)
        run_blocks(n_keys, scores, mask, lambda kk: vswt_ref[0, 0, 0:dh, pl.ds(k0 + kk, kb)], m_sel, l_sel, a_sel)

    def sel_body(j, carry):
        sel_keys(pl.multiple_of(j * tk, tk), tk, False)
        return carry

    n_full = t0 // tk
    lax.fori_loop(0, n_full, sel_body, 0)

    @pl.when(n_full * tk < t0)
    def _():
        sel_keys(pl.multiple_of(n_full * tk, tk), tk - tq, False)

    sel_keys(pl.multiple_of(t0, tq), tq, True)

    assert WINDOW == 2 * tq
    m_win[...] = jnp.full(m_win.shape, NEG, F32)
    l_win[...] = jnp.zeros(l_win.shape, F32)
    a_win[...] = jnp.zeros(a_win.shape, F32)

    def win_keys(k0, mask_of):
        run_blocks(tq,
                   lambda kk, hs: jnp.dot(kw_ref[0, 0, pl.ds(k0 + kk, kb), :], qa_s[0:dh, hs],
                                          preferred_element_type=F32),
                   lambda kk: mask_of(k0 + kk + krow),
                   lambda kk: vswt_ref[0, 0, dh:2 * dh, pl.ds(k0 + kk, kb)], m_win, l_win, a_win)

    @pl.when(qi >= 2)
    def _():
        win_keys(pl.multiple_of(t0 - WINDOW, tq), lambda key: (qtime - key) < WINDOW)

    @pl.when(qi >= 1)
    def _():
        win_keys(pl.multiple_of(t0 - tq, tq), lambda key: None)

    win_keys(pl.multiple_of(t0, tq), lambda key: key <= qtime)

    o_win = a_win[...] / l_win[...]
    o_slc = a_sel[...] / l_sel[...]
    gl_t = jax.nn.sigmoid(gl_ref[...].astype(F32)).T
    nb = NSA_N_BRANCH
    outs = []
    for h, hs in enumerate(heads):
        gate = lambda br: jnp.where(g == 0, gl_t[nb * h + br:nb * h + br + 1],
                                    gl_t[nb * (hpg + h) + br:nb * (hpg + h) + br + 1])
        outs.append(gate(0) * o_cmp[:, hs] + gate(1) * o_slc[:, hs] + gate(2) * o_win[:, hs])
    o_ref[...] = jnp.concatenate(outs, axis=0).T


def nsa_attention(z, nsa_cos, nsa_sin, kc, vct, ksa, kw, vswt, ovl, *, batch, seq):
    tq = ATT_TQ
    nt = seq // tq
    g = NSA_GROUPS
    gw = g * NSA_DH
    qw = NSA_HPG * NSA_DH
    m_cols = NSA_HPG * tq
    row = lambda b, gg, i: b * nt + i
    per_bg = lambda a: pl.BlockSpec((1, 1) + a.shape[2:], lambda b, gg, i: (b, gg, 0, 0))
    stat = pltpu.VMEM((1, m_cols), F32)
    vals = pltpu.VMEM((NSA_DH, m_cols), F32)
    return pl.pallas_call(
        _att_kernel,
        out_shape=jax.ShapeDtypeStruct((batch * seq, g * qw), F32),
        grid=(batch, g, nt),
        in_specs=[pl.BlockSpec((tq, qw), lambda b, gg, i: (row(b, gg, i), Z_NQ // qw + gg)),
                  pl.BlockSpec((tq, gw), lambda b, gg, i: (row(b, gg, i), 0)),
                  pl.BlockSpec((tq, gw), lambda b, gg, i: (row(b, gg, i), 0)),
                  pl.BlockSpec((tq, LANES), lambda b, gg, i: (row(b, gg, i), Z_NG // LANES)),
                  pl.BlockSpec((1,) + kc.shape[1:], lambda b, gg, i: (b, 0, 0)),
                  pl.BlockSpec((1,) + vct.shape[1:], lambda b, gg, i: (b, 0, 0)),
                  per_bg(ksa), per_bg(kw), per_bg(vswt),
                  pl.BlockSpec(ovl.shape, lambda b, gg, i: (0, 0))],
        out_specs=pl.BlockSpec((tq, qw), lambda b, gg, i: (row(b, gg, i), gg)),
        scratch_shapes=[pltpu.VMEM((ksa.shape[3], m_cols), BF16), stat, stat, vals, stat, stat, vals],
        compiler_params=_cparams(("parallel", "parallel", "parallel"), 48),
        name="nsa_attention",
    )(z, nsa_cos, nsa_sin, z, kc, vct, ksa, kw, vswt, ovl)


MERGE_TM = 256


def _layer_norm(y, g, b):
    mu = jnp.mean(y, axis=-1, keepdims=True)
    var = jnp.mean(jnp.square(y - mu), axis=-1, keepdims=True)
    return (y - mu) * lax.rsqrt(var + LN_EPS) * g + b


def _merge_kernel(x_ref, yc_ref, or_ref, on_ref, g0_ref, g1_ref, g2_ref, wr_ref, wn_ref, wo_ref, lg_ref, lb_ref, o_ref,
                  op_ref):
    y_ret = jnp.dot(or_ref[...].astype(BF16), wr_ref[...], preferred_element_type=F32)
    y_nsa = jnp.dot(on_ref[...].astype(BF16), wn_ref[...], preferred_element_type=F32)
    gate = lambda ref: jax.nn.sigmoid(ref[...].astype(F32))
    m = gate(g0_ref) * yc_ref[...] + gate(g1_ref) * y_ret + gate(g2_ref) * y_nsa
    h = jnp.dot(m.astype(BF16), wo_ref[...], preferred_element_type=F32)
    y = _layer_norm(DN_ALPHA * x_ref[...] + h, lg_ref[...], lb_ref[...])
    o_ref[...] = y
    _store_chunked(op_ref, _pack_rows(y))


def merge_block(x, y_conv, o_ret, o_nsa, z, ret_w_o, nsa_w_o, w_out, ln_g, ln_b):
    n = x.shape[0]
    tm = MERGE_TM
    d = D_MODEL
    rowd = lambda i: (i, 0)
    const = lambda i: (0, 0)
    return pl.pallas_call(
        _merge_kernel,
        out_shape=(jax.ShapeDtypeStruct((n, d), F32), jax.ShapeDtypeStruct((n * ROW_CHUNKS, LANES), U32)),
        grid=(n // tm,),
        in_specs=[pl.BlockSpec((tm, d), rowd), pl.BlockSpec((tm, d), rowd),
                  pl.BlockSpec((tm, o_ret.shape[1]), rowd), pl.BlockSpec((tm, o_nsa.shape[1]), rowd),
                  pl.BlockSpec((tm, d), lambda i: (i, Z_MG // d)),
                  pl.BlockSpec((tm, d), lambda i: (i, Z_MG // d + 1)),
                  pl.BlockSpec((tm, d), lambda i: (i, Z_MG // d + 2)),
                  pl.BlockSpec(ret_w_o.shape, const), pl.BlockSpec(nsa_w_o.shape, const), pl.BlockSpec(w_out.shape, const),
                  pl.BlockSpec((1, d), const), pl.BlockSpec((1, d), const)],
        out_specs=(pl.BlockSpec((tm, d), rowd), pl.BlockSpec((tm * ROW_CHUNKS, LANES), rowd)),
        compiler_params=_cparams(("parallel",), 48),
        name="merge_ln1",
    )(x, y_conv, o_ret, o_nsa, z, z, z, ret_w_o, nsa_w_o, w_out, ln_g, ln_b)


RT_TM = 256


def _stable_rank(v):
    n = v.shape[0]
    row = lax.broadcasted_iota(I32, v.shape, 0)
    rank = jnp.zeros(v.shape, F32)
    for i in range(n):
        r = v[i:i + 1, :]
        rank = rank + jnp.where(row > i, jnp.where(r >= v, 1.0, 0.0), jnp.where(r > v, 1.0, 0.0))
    return rank


def _router_kernel(x_ref, wr_ref, b_ref, ltri_ref, utri_ref, eidx_ref, rnk_ref, wts_ref, cnt_ref, carry):
    @pl.when(pl.program_id(0) == 0)
    def _():
        carry[...] = jnp.zeros_like(carry)

    tm = x_ref.shape[0]
    ne = N_EXPERTS
    per = ne // N_EXPERT_GROUPS
    logits = jnp.dot(x_ref[...].astype(BF16), wr_ref[...], preferred_element_type=F32)
    s = jax.nn.sigmoid(logits.T[0:ne])
    sb = s + b_ref[...]
    sub = lax.broadcasted_iota(I32, (per, tm), 0)
    gscore = []
    for gi in range(N_EXPERT_GROUPS):
        v = sb[gi * per:(gi + 1) * per]
        m1 = jnp.max(v, axis=0, keepdims=True)
        first = jnp.min(jnp.where(v == m1, sub, per), axis=0, keepdims=True)
        m2 = jnp.max(jnp.where(sub == first, -jnp.inf, v), axis=0, keepdims=True)
        gscore.append(m1 + m2)
    gscore = jnp.concatenate(gscore, axis=0)
    gkeep = jnp.where(_stable_rank(gscore) < float(TOPK_GROUPS), 1.0, 0.0)
    ekeep = jnp.concatenate([jnp.broadcast_to(gkeep[gi:gi + 1], (per, tm)) for gi in range(N_EXPERT_GROUPS)], axis=0)
    sel = jnp.where(_stable_rank(jnp.where(ekeep > 0.0, sb, -jnp.inf)) < float(TOPK), 1.0, 0.0)
    ssel = s * sel
    gate = ssel / jnp.sum(ssel, axis=0, keepdims=True) * ROUTED_SCALE

    selb = sel.astype(BF16)
    slot = jnp.dot(ltri_ref[...], selb, preferred_element_type=F32)
    incl = jnp.dot(selb, utri_ref[...], preferred_element_type=F32)
    rnk = carry[...] + incl - 1.0
    carry[...] = carry[...] + incl[:, tm - 1:tm]
    erow = lax.broadcasted_iota(I32, (ne, tm), 0).astype(F32)
    es, rs, ws = [], [], []
    for k in range(TOPK):
        pick = jnp.where(slot == float(k), sel, 0.0)
        es.append(jnp.sum(pick * erow, axis=0, keepdims=True))
        rs.append(jnp.sum(pick * rnk, axis=0, keepdims=True))
        ws.append(jnp.sum(pick * gate, axis=0, keepdims=True))
    eidx_ref[...] = jnp.concatenate(es, axis=0).astype(I32)
    rnk_ref[...] = jnp.concatenate(rs, axis=0).astype(I32)
    wts_ref[...] = jnp.concatenate(ws + [jnp.zeros((LANES - TOPK, tm), F32)], axis=0).T
    cnt_ref[...] = jnp.broadcast_to(carry[...], cnt_ref.shape).astype(I32)


def moe_route(x, router_w_pad, router_b_col):
    n = x.shape[0]
    tm = RT_TM
    ne = N_EXPERTS
    ltri = jnp.asarray(np.tril(np.ones((ne, ne), np.float32), -1), BF16)
    utri = jnp.asarray(np.triu(np.ones((tm, tm), np.float32)), BF16)
    const = lambda i: (0, 0)
    return pl.pallas_call(
        _router_kernel,
        out_shape=(jax.ShapeDtypeStruct((TOPK, n), I32), jax.ShapeDtypeStruct((TOPK, n), I32),
                   jax.ShapeDtypeStruct((n, LANES), F32), jax.ShapeDtypeStruct((ne, LANES), I32)),
        grid=(n // tm,),
        in_specs=[pl.BlockSpec((tm, D_MODEL), lambda i: (i, 0)),
                  pl.BlockSpec(router_w_pad.shape, const), pl.BlockSpec((ne, 1), const),
                  pl.BlockSpec((ne, ne), const), pl.BlockSpec((tm, tm), const)],
        out_specs=(pl.BlockSpec((TOPK, tm), lambda i: (0, i)), pl.BlockSpec((TOPK, tm), lambda i: (0, i)),
                   pl.BlockSpec((tm, LANES), lambda i: (i, 0)), pl.BlockSpec((ne, LANES), const)),
        scratch_shapes=[pltpu.VMEM((ne, 1), F32)],
        compiler_params=_cparams(("arbitrary",)),
        name="moe_route",
    )(x, router_w_pad, router_b_col, ltri, utri)


def _moe_rows(n_tokens):
    return n_tokens * TOPK + N_EXPERTS * MOE_TILE


def _plan_kernel(cnt_ref, off_ref, texp_ref, nused_ref):
    shift = MOE_TILE.bit_length() - 1

    def per_expert(e, carry):
        off, ti = carry
        off_ref[e] = off
        ntile = lax.shift_right_logical(cnt_ref[e] + (MOE_TILE - 1), shift)

        def mark(j, c):
            texp_ref[ti + j] = e
            return c

        lax.fori_loop(0, ntile, mark, 0)
        return off + ntile * MOE_TILE, ti + ntile

    _, used = lax.fori_loop(0, N_EXPERTS, per_expert, (jnp.int32(0), jnp.int32(0)))
    nused_ref[0] = used

    def fill(j, c):
        texp_ref[j] = N_EXPERTS - 1
        return c

    lax.fori_loop(used, texp_ref.shape[0], fill, 0)


def moe_plan(counts, n_tokens):
    nt = _moe_rows(n_tokens) // MOE_TILE
    smem = pl.BlockSpec(memory_space=pltpu.SMEM)
    return pl.pallas_call(
        _plan_kernel,
        out_shape=(jax.ShapeDtypeStruct((N_EXPERTS,), I32), jax.ShapeDtypeStruct((nt,), I32),
                   jax.ShapeDtypeStruct((1,), I32)),
        in_specs=[smem],
        out_specs=(smem, smem, smem),
        name="moe_plan",
    )(counts)


DSP_TB = 256


U32 = jnp.uint32
ROW_WORDS = D_MODEL // 2
ROW_CHUNKS = ROW_WORDS // LANES
assert ROW_CHUNKS == 4


def _pack_rows(y):
    half = y.shape[1] // 2
    bits = lambda v: lax.bitcast_convert_type(v.astype(jnp.bfloat16).astype(F32), U32)
    return bits(y[:, :half]) | lax.shift_right_logical(bits(y[:, half:]), jnp.uint32(16))


def _unpack_rows(words, dtype):
    hi = lax.bitcast_convert_type(words & jnp.uint32(0xFFFF0000), F32)
    lo = lax.bitcast_convert_type(lax.shift_left(words, jnp.uint32(16)), F32)
    return jnp.concatenate([hi, lo], axis=1).astype(dtype)


def _load_chunked(ref, rows):
    return jnp.concatenate([ref[pl.ds(s, rows, stride=ROW_CHUNKS), :] for s in range(ROW_CHUNKS)], axis=1)


def _store_chunked(ref, words):
    rows = words.shape[0]
    for s in range(ROW_CHUNKS):
        ref[pl.ds(s, rows, stride=ROW_CHUNKS), :] = words[:, s * LANES:(s + 1) * LANES]


def _dst_kernel(off_ref, eidx_ref, rnk_ref, dst_ref):
    e = eidx_ref[...]
    base = jnp.zeros(e.shape, I32)
    for x in range(N_EXPERTS):
        base = jnp.where(e == x, off_ref[x], base)
    dst_ref[...] = (base + rnk_ref[...]) * ROW_CHUNKS


def moe_dst(eidx, rnk, off):
    k, n = eidx.shape
    tb = 2048
    blk = pl.BlockSpec((k, tb), lambda i: (0, i))
    return pl.pallas_call(
        _dst_kernel,
        out_shape=jax.ShapeDtypeStruct((k, n), I32),
        grid=(n // tb,),
        in_specs=[pl.BlockSpec(memory_space=pltpu.SMEM), blk, blk],
        out_specs=blk,
        compiler_params=_cparams(("parallel",)),
        name="moe_dst",
    )(off, eidx, rnk)


def _tile_copy(src_ref, src_row, dst_ref, dst_row, sem):
    return pltpu.make_async_copy(src_ref.at[pl.ds(pl.multiple_of(src_row, ROW_CHUNKS), ROW_CHUNKS)],
                                 dst_ref.at[pl.ds(pl.multiple_of(dst_row, ROW_CHUNKS), ROW_CHUNKS)], sem)


def _dispatch_kernel(dst_ref, off_ref, cnt_ref, x_ref, xs_ref, zbuf, sem, zsem):
    tb = x_ref.shape[0] // ROW_CHUNKS
    zrows = MOE_TILE * ROW_CHUNKS

    def pad_copy(e):
        cnt = cnt_ref[e]
        rem = jnp.bitwise_and(cnt, MOE_TILE - 1)
        start = pl.multiple_of((off_ref[e] + cnt - rem) * ROW_CHUNKS, zrows)
        return rem != 0, pltpu.make_async_copy(zbuf, xs_ref.at[pl.ds(start, zrows)], zsem)

    @pl.when(pl.program_id(0) == 0)
    def _():
        zbuf[...] = jnp.zeros_like(zbuf)

        def start(e, c):
            has_pad, cp = pad_copy(e)

            @pl.when(has_pad)
            def _():
                cp.start()
            return c

        def wait(e, c):
            has_pad, cp = pad_copy(e)

            @pl.when(has_pad)
            def _():
                cp.wait()
            return c

        lax.fori_loop(0, N_EXPERTS, start, 0)
        lax.fori_loop(0, N_EXPERTS, wait, 0)

    def issue(t, c):
        for k in range(TOPK):
            _tile_copy(x_ref, t * ROW_CHUNKS, xs_ref, dst_ref[k, t], sem).start(priority=k % 2)
        return c

    def drain(t, c):
        for k in range(TOPK):
            _tile_copy(x_ref, 0, xs_ref, 0, sem).wait()
        return c

    lax.fori_loop(0, tb, issue, 0)
    lax.fori_loop(0, tb, drain, 0)


def moe_dispatch(xc, dst, off, counts):
    n = xc.shape[0] // ROW_CHUNKS
    tb = DSP_TB
    smem_all = pl.BlockSpec(memory_space=pltpu.SMEM)
    smem_blk = pl.BlockSpec((TOPK, tb), lambda i: (0, i), memory_space=pltpu.SMEM)
    return pl.pallas_call(
        _dispatch_kernel,
        out_shape=jax.ShapeDtypeStruct((_moe_rows(n) * ROW_CHUNKS, LANES), xc.dtype),
        grid=(n // tb,),
        in_specs=[smem_blk, smem_all, smem_all, pl.BlockSpec((tb * ROW_CHUNKS, LANES), lambda i: (i, 0))],
        out_specs=pl.BlockSpec(memory_space=pl.ANY),
        scratch_shapes=[pltpu.VMEM((MOE_TILE * ROW_CHUNKS, LANES), xc.dtype), pltpu.SemaphoreType.DMA(()),
                        pltpu.SemaphoreType.DMA(())],
        compiler_params=_cparams(("arbitrary",)),
        name="moe_dispatch",
    )(dst, off, counts, xc)


def _expert_kernel(texp_ref, nused_ref, xs_ref, w1_ref, w3_ref, w2_ref, ys_ref, w1b, w3b, w2b, last):
    i = pl.program_id(0)
    e = texp_ref[i]

    @pl.when(i == 0)
    def _():
        last[0] = -1

    @pl.when(e != last[0])
    def _():
        w1b[...] = w1_ref[0, 0].astype(BF16)
        w3b[...] = w3_ref[0, 0].astype(BF16)
        w2b[...] = w2_ref[0, 0].astype(BF16)
        last[0] = e

    @pl.when(i < nused_ref[0])
    def _():
        xb = _unpack_rows(_load_chunked(xs_ref, MOE_TILE), BF16)
        h1 = jnp.dot(xb, w1b[...], preferred_element_type=F32)
        h3 = jnp.dot(xb, w3b[...], preferred_element_type=F32)
        h = h1 * jax.nn.sigmoid(h1) * h3
        y = jnp.dot(h.astype(BF16), w2b[...], preferred_element_type=F32)
        _store_chunked(ys_ref, _pack_rows(y))

    @pl.when(i >= nused_ref[0])
    def _():
        ys_ref[...] = jnp.zeros_like(ys_ref)


def moe_experts(xs, texp, nused, w1, w3, w2, layer):
    crows = xs.shape[0]
    d, f = w1.shape[2], w1.shape[3]
    blk = MOE_TILE * ROW_CHUNKS
    nt = crows // blk
    grid_spec = pltpu.PrefetchScalarGridSpec(
        num_scalar_prefetch=2,
        grid=(nt,),
        in_specs=[pl.BlockSpec((blk, LANES), lambda i, te, nu: (jnp.where(i < nu[0], i, 0), 0)),
                  pl.BlockSpec((1, 1, d, f), lambda i, te, nu: (layer, te[i], 0, 0)),
                  pl.BlockSpec((1, 1, d, f), lambda i, te, nu: (layer, te[i], 0, 0)),
                  pl.BlockSpec((1, 1, f, d), lambda i, te, nu: (layer, te[i], 0, 0))],
        out_specs=pl.BlockSpec((blk, LANES), lambda i, te, nu: (i, 0)),
        scratch_shapes=[pltpu.VMEM((d, f), BF16), pltpu.VMEM((d, f), BF16), pltpu.VMEM((f, d), BF16),
                        pltpu.SMEM((1,), I32)],
    )
    return pl.pallas_call(
        _expert_kernel,
        out_shape=jax.ShapeDtypeStruct((crows, LANES), U32),
        grid_spec=grid_spec,
        compiler_params=_cparams(("arbitrary",), 48),
        name="moe_experts",
    )(texp, nused, xs, w1, w3, w2)


CMB_TB = 128


def _combine_kernel(dst_ref, x_ref, wts_ref, ws1_ref, ws3_ref, ws2_ref, lg_ref, lb_ref, ys_ref, o_ref, buf, sem):
    tb = x_ref.shape[0]

    def issue(t, c):
        for k in range(TOPK):
            _tile_copy(ys_ref, dst_ref[k, t], buf.at[k], t * ROW_CHUNKS, sem).start(priority=k % 2)
        return c

    def drain(t, c):
        for k in range(TOPK):
            _tile_copy(ys_ref, 0, buf.at[0], 0, sem).wait()
        return c

    lax.fori_loop(0, tb, issue, 0)
    x = x_ref[...]
    xb = x.astype(BF16)
    h1 = jnp.dot(xb, ws1_ref[...], preferred_element_type=F32)
    h3 = jnp.dot(xb, ws3_ref[...], preferred_element_type=F32)
    y = jnp.dot((h1 * jax.nn.sigmoid(h1) * h3).astype(BF16), ws2_ref[...], preferred_element_type=F32)
    lax.fori_loop(0, tb, drain, 0)
    w = wts_ref[...]
    routed = y
    for k in range(TOPK):
        routed = routed + w[:, k:k + 1] * _unpack_rows(_load_chunked(buf.at[k], tb), F32)
    o_ref[...] = _layer_norm(DN_ALPHA * x + routed, lg_ref[...], lb_ref[...])


def moe_combine(x, ys, dst, wts, ws1, ws3, ws2, ln_g, ln_b):
    n, d = x.shape
    tb = CMB_TB
    smem_blk = pl.BlockSpec((TOPK, tb), lambda i: (0, i), memory_space=pltpu.SMEM)
    const = lambda i: (0, 0)
    return pl.pallas_call(
        _combine_kernel,
        out_shape=jax.ShapeDtypeStruct((n, d), F32),
        grid=(n // tb,),
        in_specs=[smem_blk,
                  pl.BlockSpec((tb, d), lambda i: (i, 0)), pl.BlockSpec((tb, LANES), lambda i: (i, 0)),
                  pl.BlockSpec(ws1.shape, const), pl.BlockSpec(ws3.shape, const), pl.BlockSpec(ws2.shape, const),
                  pl.BlockSpec((1, d), const), pl.BlockSpec((1, d), const),
                  pl.BlockSpec(memory_space=pl.ANY)],
        out_specs=pl.BlockSpec((tb, d), lambda i: (i, 0)),
        scratch_shapes=[pltpu.VMEM((TOPK, tb * ROW_CHUNKS, LANES), U32), pltpu.SemaphoreType.DMA(())],
        compiler_params=_cparams(("parallel",), 48),
        name="moe_combine_ln2",
    )(dst, x, wts, ws1, ws3, ws2, ln_g, ln_b, ys)


def moe_block(x, xp, router_w, router_b, w1, w3, w2, layer, ws1, ws3, ws2, ln_g, ln_b):
    n, d = x.shape
    rw = jnp.pad(router_w, ((0, 0), (0, LANES - N_EXPERTS))).astype(BF16)
    eidx, rnk, wts, cnt = moe_route(x, rw, router_b.reshape(N_EXPERTS, 1))
    counts = cnt[:, 0]
    off, texp, nused = moe_plan(counts, n)
    dst = moe_dst(eidx, rnk, off)
    xs = moe_dispatch(xp, dst, off, counts)
    ys = moe_experts(xs, texp, nused, w1, w3, w2, layer)
    return moe_combine(x, ys, dst, wts, ws1.astype(BF16), ws3.astype(BF16), ws2.astype(BF16),
                       ln_g.reshape(1, -1), ln_b.reshape(1, -1))


def nsa_rope_tables(positions):
    batch, seq = positions.shape
    inv, sgn = _nsa_inv_freq()
    posf = positions.astype(F32)
    tok = rope_tables(posf.reshape(batch * seq, 1), inv, sgn, tm=512)
    end = posf[:, CMP_BLOCK - 1::CMP_STRIDE]
    npad = _n_cmp_pad(seq)
    end = jnp.pad(end, ((0, 0), (0, npad - end.shape[1])))
    cmp = rope_tables(end.reshape(batch * npad, 1), inv, sgn, tm=npad)
    return tok, cmp


def nsa_branch(z, tok_tab, cmp_tab, wk, wv, ovl, *, batch, seq):
    n = batch * seq
    w = NSA_GROUPS * NSA_DH
    xk = z[:, Z_NKV:Z_NKV + w].reshape(n // CMP_HALF, CMP_ROW)
    xv = z[:, Z_NKV + w:Z_NKV + 2 * w].reshape(n // CMP_HALF, CMP_ROW)
    kc, vct = nsa_compress(xk, xv, wk, wv, cmp_tab[0], cmp_tab[1], batch=batch)
    ksa, kw, vswt = nsa_kv_prep(z, tok_tab[0], tok_tab[1], batch=batch, seq=seq)
    return nsa_attention(z, tok_tab[0], tok_tab[1], kc, vct, ksa, kw, vswt, ovl, batch=batch, seq=seq)


def kernel(x, positions, w_in, conv_dw, conv_db, conv_ln_g, conv_ln_b, conv_w_pw, ret_w_o, nsa_pe_k, nsa_w1_k, nsa_w2_k,
           nsa_pe_v, nsa_w1_v, nsa_w2_v, nsa_w_o, w_out, ln1_g, ln1_b, router_w, router_b, moe_w1, moe_w3, moe_w2,
           shared_w1, shared_w3, shared_w2, ln2_g, ln2_b):
    batch, seq, d = x.shape
    n = batch * seq
    xf = x.reshape(n, d)
    row = lambda v: v.reshape(1, -1)

    ret_inv, ret_sgn = _ret_inv_freq()
    ret_tab = rope_tables(positions.astype(F32).reshape(n, 1), ret_inv, ret_sgn, tm=512)
    tok_tab, cmp_tab = nsa_rope_tables(positions)
    ret_consts = _ret_tables()
    ovl = _overlap_matrix(seq)

    for l in range(w_in.shape[0]):
        z = matmul(xf, _layout_w_in(w_in[l]), tm=2048, tn=512, out_dtype=BF16, name="in_proj")
        y_conv = conv_branch(z, conv_dw[l], row(conv_db[l]), row(conv_ln_g[l]), row(conv_ln_b[l]),
                             conv_w_pw[l].astype(BF16), batch=batch, seq=seq)
        o_ret = retention_branch(z, ret_tab[0], ret_tab[1], ret_consts, batch=batch, seq=seq)
        o_nsa = nsa_branch(z, tok_tab, cmp_tab, _cmp_weights(nsa_pe_k[l], nsa_w1_k[l], nsa_w2_k[l]),
                           _cmp_weights(nsa_pe_v[l], nsa_w1_v[l], nsa_w2_v[l]), ovl, batch=batch, seq=seq)
        x1, x1p = merge_block(xf, y_conv, o_ret, o_nsa, z, ret_w_o[l].astype(BF16), nsa_w_o[l].astype(BF16),
                              w_out[l].astype(BF16), row(ln1_g[l]), row(ln1_b[l]))
        xf = moe_block(x1, x1p, router_w[l], router_b[l], moe_w1, moe_w3, moe_w2, l,
                       shared_w1[l], shared_w3[l], shared_w2[l], ln2_g[l], ln2_b[l])
    return xf.reshape(batch, seq, d)
```

```python
import functools
import math

import jax
import jax.numpy as jnp
import numpy as np
from jax import lax
from jax.experimental import pallas as pl
from jax.experimental.pallas import tpu as pltpu

F32 = jnp.float32
BF16 = jnp.bfloat16
I32 = jnp.int32

D_MODEL = 1024
DEPTH = 4
CONV_CH = 512
CONV_WIDTH = 31
RET_HEADS = 4
RET_DK = 128
RET_DV = 256
RET_CHUNK = 128
RET_ROPE_BASE = 10000.0
NSA_HEADS = 8
NSA_GROUPS = 2
NSA_HPG = NSA_HEADS // NSA_GROUPS
NSA_DH = 64
NSA_N_BRANCH = 3
CMP_BLOCK = 32
CMP_STRIDE = 16
CMP_HIDDEN = 256
SEL_BLOCK = 64
SEL_TOPK = 16
WINDOW = 512
ROPE_THETA = 500000.0
ROT_DIM = NSA_DH // 4
N_EXPERTS = 64
N_EXPERT_GROUPS = 8
TOPK_GROUPS = 4
TOPK = 8
D_EXPERT = 256
D_SHARED = 256
ROUTED_SCALE = 2.5
DN_ALPHA = (2.0 * DEPTH) ** 0.25
LN_EPS = 1e-5
NEG = -1e30

LANES = 128

Z_MG = 0
Z_CA = 3072
Z_CB = 3584
Z_RQ = 4096
Z_RK = 4608
Z_RV = 5120
Z_RG = 6144
Z_NQ = 7168
Z_NKV = 7680
Z_NG = 8448
Z_W = 8704

_IN_WIDTHS = (512, 512, 512, 512, 1024, 1024, 512, 128, 128, 128, 128, 128, 128, 24, 3072)
_IN_OFFS = tuple(int(v) for v in np.concatenate([[0], np.cumsum(_IN_WIDTHS)[:-1]]))

MOE_TILE = 1024
MOE_ROWS_PER_TOKEN = TOPK


def _cparams(sem, vmem_mb=None):
    kw = dict(dimension_semantics=sem)
    if vmem_mb is not None:
        kw["vmem_limit_bytes"] = vmem_mb * 1024 * 1024
    return pltpu.CompilerParams(**kw)


def _mm_kernel(x_ref, w_ref, o_ref):
    o_ref[...] = jnp.dot(x_ref[...].astype(BF16), w_ref[...], preferred_element_type=F32).astype(o_ref.dtype)


def matmul(x, w, *, tm, tn, out_dtype=F32, x_col_block=0, k=None, name="mm"):
    m = x.shape[0]
    kk, n = w.shape
    if k is None:
        k = kk
    return pl.pallas_call(
        _mm_kernel,
        out_shape=jax.ShapeDtypeStruct((m, n), out_dtype),
        grid=(m // tm, n // tn),
        in_specs=[pl.BlockSpec((tm, k), lambda i, j: (i, x_col_block)),
                  pl.BlockSpec((k, tn), lambda i, j: (0, j))],
        out_specs=pl.BlockSpec((tm, tn), lambda i, j: (i, j)),
        compiler_params=_cparams(("parallel", "arbitrary"), 48),
        name=name,
    )(x, w)


def _rope_table_kernel(pos_ref, inv_ref, sgn_ref, cos_ref, sin_ref):
    ang = pos_ref[...] * inv_ref[...]
    cos_ref[...] = jnp.cos(ang)
    sin_ref[...] = jnp.sin(ang) * sgn_ref[...]


def rope_tables(pos_col, inv, sgn, *, tm):
    n = pos_col.shape[0]
    w = inv.shape[1]
    return pl.pallas_call(
        _rope_table_kernel,
        out_shape=(jax.ShapeDtypeStruct((n, w), F32), jax.ShapeDtypeStruct((n, w), F32)),
        grid=(n // tm,),
        in_specs=[pl.BlockSpec((tm, 1), lambda i: (i, 0)),
                  pl.BlockSpec((1, w), lambda i: (0, 0)),
                  pl.BlockSpec((1, w), lambda i: (0, 0))],
        out_specs=(pl.BlockSpec((tm, w), lambda i: (i, 0)), pl.BlockSpec((tm, w), lambda i: (i, 0))),
        compiler_params=_cparams(("parallel",)),
        name="rope_tables",
    )(pos_col, inv, sgn)


def _ret_inv_freq():
    inv = 1.0 / jnp.power(jnp.float32(RET_ROPE_BASE), jnp.linspace(0.0, 1.0, RET_DK // 2, dtype=F32))
    inv = jnp.concatenate([inv, inv])[None, :]
    sgn = jnp.concatenate([-jnp.ones((RET_DK // 2,), F32), jnp.ones((RET_DK // 2,), F32)])[None, :]
    return inv, sgn


def _nsa_inv_freq():
    half = ROT_DIM // 2
    inv = jnp.power(jnp.float32(ROPE_THETA), -jnp.arange(0, ROT_DIM, 2, dtype=F32) / ROT_DIM)
    z = jnp.zeros((NSA_DH - ROT_DIM,), F32)
    inv64 = jnp.concatenate([inv, inv, z])
    sgn64 = jnp.concatenate([-jnp.ones((half,), F32), jnp.ones((half,), F32), z])
    return jnp.concatenate([inv64, inv64])[None, :], jnp.concatenate([sgn64, sgn64])[None, :]


def _nsa_rope(x, cos, sin):
    w = x.shape[1]
    half = ROT_DIM // 2
    lane = lax.broadcasted_iota(I32, x.shape, 1) % NSA_DH
    partner = jnp.where(lane < half, pltpu.roll(x, w - half, 1), pltpu.roll(x, half, 1))
    return x * cos + partner * sin


CONV_TT = 256
CONV_HALO = 32


def _conv_kernel(a_ref, b_ref, ah_ref, bh_ref, dw_ref, db_ref, g_ref, be_ref, wpw_ref, o_ref, ubuf, sbuf):
    i = pl.program_id(1)
    tt = a_ref.shape[0]
    u = a_ref[...].astype(F32) * jax.nn.sigmoid(b_ref[...].astype(F32))
    uh = ah_ref[...].astype(F32) * jax.nn.sigmoid(bh_ref[...].astype(F32))
    ubuf[0:CONV_HALO, :] = jnp.where(i > 0, uh, 0.0)
    ubuf[CONV_HALO:CONV_HALO + tt, :] = u
    acc = jnp.zeros((tt, CONV_CH), F32)
    base = CONV_HALO - (CONV_WIDTH - 1)
    sub = 8
    for r in range(sub):
        offs = [o for o in range(base, base + CONV_WIDTH) if o % sub == r]
        if not offs:
            continue
        span = offs[-1] - r + tt
        sbuf[0:span, :] = ubuf[r:r + span, :]
        for o in offs:
            acc = acc + dw_ref[o - base:o - base + 1, :] * sbuf[o - r:o - r + tt, :]
    acc = acc + db_ref[...]
    mu = jnp.mean(acc, axis=-1, keepdims=True)
    var = jnp.mean(jnp.square(acc - mu), axis=-1, keepdims=True)
    y = (acc - mu) * lax.rsqrt(var + LN_EPS) * g_ref[...] + be_ref[...]
    y = y * jax.nn.sigmoid(y)
    o_ref[...] = jnp.dot(y.astype(BF16), wpw_ref[...], preferred_element_type=F32)


def conv_branch(z, dw, db, ln_g, ln_b, w_pw_bf, *, batch, seq):
    tt = CONV_TT
    nt = seq // tt
    r = tt // CONV_HALO
    ca, cb = Z_CA // CONV_CH, Z_CB // CONV_CH

    def halo_map(col):
        return lambda b, i: (jnp.maximum((b * nt + i) * r - 1, 0), col)

    return pl.pallas_call(
        _conv_kernel,
        out_shape=jax.ShapeDtypeStruct((batch * seq, D_MODEL), F32),
        grid=(batch, nt),
        in_specs=[pl.BlockSpec((tt, CONV_CH), lambda b, i: (b * nt + i, ca)),
                  pl.BlockSpec((tt, CONV_CH), lambda b, i: (b * nt + i, cb)),
                  pl.BlockSpec((CONV_HALO, CONV_CH), halo_map(ca)),
                  pl.BlockSpec((CONV_HALO, CONV_CH), halo_map(cb)),
                  pl.BlockSpec((CONV_WIDTH, CONV_CH), lambda b, i: (0, 0)),
                  pl.BlockSpec((1, CONV_CH), lambda b, i: (0, 0)),
                  pl.BlockSpec((1, CONV_CH), lambda b, i: (0, 0)),
                  pl.BlockSpec((1, CONV_CH), lambda b, i: (0, 0)),
                  pl.BlockSpec((CONV_CH, D_MODEL), lambda b, i: (0, 0))],
        out_specs=pl.BlockSpec((tt, D_MODEL), lambda b, i: (b * nt + i, 0)),
        scratch_shapes=[pltpu.VMEM((CONV_HALO + tt, CONV_CH), F32), pltpu.VMEM((CONV_HALO + tt, CONV_CH), F32)],
        compiler_params=_cparams(("parallel", "parallel")),
        name="conv_branch",
    )(z, z, z, z, dw, db, ln_g, ln_b, w_pw_bf)


RET_TQ = 512


def _ret_tables():
    h, c = RET_HEADS, RET_CHUNK
    log_gamma = jnp.log1p(-jnp.exp2(-5.0 - jnp.arange(h, dtype=F32)))
    idx = jnp.arange(c, dtype=F32)
    diff = idx[:, None] - idx[None, :]
    dmat = jnp.where(diff >= 0, jnp.exp(log_gamma[:, None, None] * jnp.maximum(diff, 0.0)), 0.0).astype(F32)
    xi = jnp.exp(log_gamma[:, None] * (idx + 1.0)).astype(F32)
    zeta = jnp.exp(log_gamma[:, None] * (c - 1.0 - idx)).astype(F32)
    decay = jnp.exp(log_gamma * c).astype(F32)
    xi_b = jnp.broadcast_to(xi[:, :, None], (h, c, RET_DV))
    zeta_b = jnp.broadcast_to(zeta[:, :, None], (h, c, RET_DV))
    decay_b = jnp.broadcast_to(decay[:, None, None], (h, RET_DK, RET_DV))
    return dmat, xi_b, zeta_b, decay_b


def _ret_kernel(q_ref, k_ref, v_ref, g_ref, cos_ref, sin_ref, dmat_ref, xi_ref, zeta_ref, dec_ref, o_ref, r_ref):
    @pl.when(pl.program_id(1) == 0)
    def _():
        r_ref[...] = jnp.zeros_like(r_ref)

    c = RET_CHUNK
    n_chunks = q_ref.shape[0] // c
    for ci in range(n_chunks):
        rows = slice(ci * c, (ci + 1) * c)
        cos = cos_ref[rows, :]
        sin = sin_ref[rows, :]
        for h in range(RET_HEADS):
            qk_cols = slice(h * RET_DK, (h + 1) * RET_DK)
            v_cols = slice(h * RET_DV, (h + 1) * RET_DV)
            q = q_ref[rows, qk_cols].astype(F32)
            k = k_ref[rows, qk_cols].astype(F32)
            q = q * cos + pltpu.roll(q, RET_DK // 2, 1) * sin
            k = (k * cos + pltpu.roll(k, RET_DK // 2, 1) * sin) * (RET_DK ** -0.5)
            v = v_ref[rows, v_cols].astype(F32)
            qb = q.astype(BF16)
            kb = k.astype(BF16)
            inner = lax.dot_general(qb, kb, (((1,), (1,)), ((), ())), preferred_element_type=F32) * dmat_ref[h]
            r_old = r_ref[h]
            o = (jnp.dot(inner.astype(BF16), v.astype(BF16), preferred_element_type=F32)
                 + jnp.dot(qb, r_old.astype(BF16), preferred_element_type=F32) * xi_ref[h])
            vz = (v * zeta_ref[h]).astype(BF16)
            r_ref[h] = r_old * dec_ref[h] + jnp.dot(k.T.astype(BF16), vz, preferred_element_type=F32)
            mu = jnp.mean(o, axis=-1, keepdims=True)
            var = jnp.mean(jnp.square(o - mu), axis=-1, keepdims=True)
            on = (o - mu) * lax.rsqrt(var + LN_EPS)
            g = g_ref[rows, v_cols].astype(F32)
            o_ref[rows, v_cols] = g * jax.nn.sigmoid(g) * on


def retention_branch(z, ret_cos, ret_sin, tables, *, batch, seq):
    tq = RET_TQ
    nt = seq // tq
    dmat, xi_b, zeta_b, decay_b = tables
    qw = RET_HEADS * RET_DK
    vw = RET_HEADS * RET_DV
    row = lambda b, i: b * nt + i
    full3 = lambda b, i: (0, 0, 0)
    return pl.pallas_call(
        _ret_kernel,
        out_shape=jax.ShapeDtypeStruct((batch * seq, vw), F32),
        grid=(batch, nt),
        in_specs=[pl.BlockSpec((tq, qw), lambda b, i: (row(b, i), Z_RQ // qw)),
                  pl.BlockSpec((tq, qw), lambda b, i: (row(b, i), Z_RK // qw)),
                  pl.BlockSpec((tq, vw), lambda b, i: (row(b, i), Z_RV // vw)),
                  pl.BlockSpec((tq, vw), lambda b, i: (row(b, i), Z_RG // vw)),
                  pl.BlockSpec((tq, RET_DK), lambda b, i: (row(b, i), 0)),
                  pl.BlockSpec((tq, RET_DK), lambda b, i: (row(b, i), 0)),
                  pl.BlockSpec(dmat.shape, full3),
                  pl.BlockSpec(xi_b.shape, full3),
                  pl.BlockSpec(zeta_b.shape, full3),
                  pl.BlockSpec(decay_b.shape, full3)],
        out_specs=pl.BlockSpec((tq, vw), lambda b, i: (row(b, i), 0)),
        scratch_shapes=[pltpu.VMEM((RET_HEADS, RET_DK, RET_DV), F32)],
        compiler_params=_cparams(("parallel", "arbitrary"), 48),
        name="retention",
    )(z, z, z, z, ret_cos, ret_sin, dmat, xi_b, zeta_b, decay_b)


def _layout_w_in(w):
    seg = lambda i: w[:, _IN_OFFS[i]:_IN_OFFS[i] + _IN_WIDTHS[i]]
    order = (14, 0, 1, 2, 3, 4, 5, 6, 7, 8, 9, 10, 11, 12, 13)
    parts = [seg(i) for i in order]
    used = sum(_IN_WIDTHS)
    parts.append(jnp.zeros((w.shape[0], Z_W - used), w.dtype))
    return jnp.concatenate(parts, axis=1).astype(BF16)


CMP_HALF = CMP_BLOCK // 2
CMP_ROW = CMP_HALF * NSA_GROUPS * NSA_DH


def _n_cmp_pad(seq):
    return seq // CMP_STRIDE


def _cmp_weights(pe, w1, w2):
    g = NSA_GROUPS
    eye = jnp.eye(g, dtype=F32)
    w = w1.reshape(2, CMP_HALF, NSA_DH, CMP_HIDDEN)
    w1ab = jnp.einsum("hldf,gk->hlgdkf", w, eye).reshape(2, CMP_ROW, g * CMP_HIDDEN).astype(BF16)
    peab = jnp.broadcast_to(pe.reshape(2, CMP_HALF, 1, NSA_DH), (2, CMP_HALF, g, NSA_DH)).reshape(2, 1, CMP_ROW)
    w2bd = jnp.einsum("fd,gk->gfkd", w2, eye).reshape(g * CMP_HIDDEN, g * NSA_DH).astype(BF16)
    return peab, w1ab, w2bd


def _cmp_mlp(x, pe_ref, w1_ref, w2_ref):
    x = x.astype(F32)
    a = jnp.dot((x + pe_ref[0]).astype(BF16), w1_ref[0], preferred_element_type=F32)
    b = jnp.dot((x + pe_ref[1]).astype(BF16), w1_ref[1], preferred_element_type=F32)
    hid = a + pltpu.roll(b, b.shape[0] - 1, 0)
    hid = hid * jax.nn.sigmoid(hid)
    return jnp.dot(hid.astype(BF16), w2_ref[...], preferred_element_type=F32)


def _compress_kernel(xk_ref, xv_ref, pek_ref, w1k_ref, w2k_ref, pev_ref, w1v_ref, w2v_ref, cos_ref, sin_ref,
                     k_ref, vt_ref):
    k = _cmp_mlp(xk_ref[...], pek_ref, w1k_ref, w2k_ref)
    k_ref[0] = _nsa_rope(k, cos_ref[...], sin_ref[...]).astype(BF16)
    vt_ref[0] = _cmp_mlp(xv_ref[...], pev_ref, w1v_ref, w2v_ref).T.astype(BF16)


def nsa_compress(xk, xv, wk, wv, cmp_cos, cmp_sin, *, batch):
    pek, w1k, w2k = wk
    pev, w1v, w2v = wv
    npad = xk.shape[0] // batch
    gw = NSA_GROUPS * NSA_DH
    c3 = lambda b: (0, 0, 0)
    c2 = lambda b: (0, 0)
    return pl.pallas_call(
        _compress_kernel,
        out_shape=(jax.ShapeDtypeStruct((batch, npad, gw), BF16), jax.ShapeDtypeStruct((batch, gw, npad), BF16)),
        grid=(batch,),
        in_specs=[pl.BlockSpec((npad, CMP_ROW), lambda b: (b, 0)),
                  pl.BlockSpec((npad, CMP_ROW), lambda b: (b, 0)),
                  pl.BlockSpec(pek.shape, c3), pl.BlockSpec(w1k.shape, c3), pl.BlockSpec(w2k.shape, c2),
                  pl.BlockSpec(pev.shape, c3), pl.BlockSpec(w1v.shape, c3), pl.BlockSpec(w2v.shape, c2),
                  pl.BlockSpec((npad, gw), lambda b: (b, 0)),
                  pl.BlockSpec((npad, gw), lambda b: (b, 0))],
        out_specs=(pl.BlockSpec((1, npad, gw), lambda b: (b, 0, 0)), pl.BlockSpec((1, gw, npad), lambda b: (b, 0, 0))),
        compiler_params=_cparams(("parallel",), 48),
        name="nsa_compress",
    )(xk, xv, pek, w1k, w2k, pev, w1v, w2v, cmp_cos, cmp_sin)


KV_TT = 512
ATT_TK = 512
ATT_KB = 128
ATT_AHEAD = 8
LOG2E = math.log2(math.e)


def _kv_prep_kernel(z_ref, cos_ref, sin_ref, ksa_ref, kw_ref, vswt_ref):
    tt = z_ref.shape[0]
    cos = cos_ref[...]
    sin = sin_ref[...]
    w = NSA_GROUPS * NSA_DH
    n_sel = ksa_ref.shape[3] - NSA_DH
    ks = _nsa_rope(z_ref[:, 2 * w:3 * w].astype(F32), cos, sin)
    vs = z_ref[:, 3 * w:4 * w].astype(F32)
    kw = _nsa_rope(z_ref[:, 4 * w:5 * w].astype(F32), cos, sin)
    vw = z_ref[:, 5 * w:6 * w].astype(F32)
    blk = (pl.program_id(1) * tt + lax.broadcasted_iota(I32, (tt, n_sel), 0)) // SEL_BLOCK
    onehot = jnp.where(blk == lax.broadcasted_iota(I32, (tt, n_sel), 1), 1.0, 0.0)
    for g in range(NSA_GROUPS):
        cols = slice(g * NSA_DH, (g + 1) * NSA_DH)
        ksa_ref[0, g] = jnp.concatenate([ks[:, cols], onehot], axis=1).astype(BF16)
        kw_ref[0, g] = kw[:, cols].astype(BF16)
        vswt_ref[0, g] = jnp.concatenate([vs[:, cols], vw[:, cols]], axis=1).T.astype(BF16)


def nsa_kv_prep(z, nsa_cos, nsa_sin, *, batch, seq):
    tt = KV_TT
    nt = seq // tt
    g = NSA_GROUPS
    w = g * NSA_DH
    n_sel = seq // SEL_BLOCK
    return pl.pallas_call(
        _kv_prep_kernel,
        out_shape=(jax.ShapeDtypeStruct((batch, g, seq, NSA_DH + n_sel), BF16),
                   jax.ShapeDtypeStruct((batch, g, seq, NSA_DH), BF16),
                   jax.ShapeDtypeStruct((batch, g, w, seq), BF16)),
        grid=(batch, nt),
        in_specs=[pl.BlockSpec((tt, 6 * w), lambda b, i: (b * nt + i, Z_NKV // (6 * w))),
                  pl.BlockSpec((tt, w), lambda b, i: (b * nt + i, 0)),
                  pl.BlockSpec((tt, w), lambda b, i: (b * nt + i, 0))],
        out_specs=(pl.BlockSpec((1, g, tt, NSA_DH + n_sel), lambda b, i: (b, 0, i, 0)),
                   pl.BlockSpec((1, g, tt, NSA_DH), lambda b, i: (b, 0, i, 0)),
                   pl.BlockSpec((1, g, w, tt), lambda b, i: (b, 0, 0, i))),
        compiler_params=_cparams(("parallel", "parallel")),
        name="nsa_kv_prep",
    )(z, nsa_cos, nsa_sin)


ATT_TQ = 256


def _overlap_matrix(seq):
    n_cmp = (seq - CMP_BLOCK) // CMP_STRIDE + 1
    n_sel = seq // SEL_BLOCK
    ii = np.arange(_n_cmp_pad(seq))[None, :]
    jj = np.arange(LANES)[:, None]
    lo = np.maximum(ii * CMP_STRIDE, jj * SEL_BLOCK)
    hi = np.minimum(ii * CMP_STRIDE + CMP_BLOCK, (jj + 1) * SEL_BLOCK)
    ov = np.maximum(hi - lo, 0).astype(np.float32) / CMP_BLOCK
    ov = np.where((ii < n_cmp) & (jj < n_sel), ov, 0.0)
    return jnp.asarray(np.tile(ov, (1, NSA_HPG)), BF16)


def _softmax_step(s, ok, vt, cols, m_ref, l_ref, a_ref):
    m_old = m_ref[:, cols]
    m_new = jnp.maximum(m_old, jnp.max(s, axis=0, keepdims=True))
    alpha = jnp.exp2(m_old - m_new)
    p = jnp.exp2(s - m_new)
    if ok is not None:
        p = jnp.where(ok, p, 0.0)
    l_ref[:, cols] = alpha * l_ref[:, cols] + jnp.sum(p, axis=0, keepdims=True)
    a_ref[:, cols] = alpha * a_ref[:, cols] + jnp.dot(vt, p.astype(BF16), preferred_element_type=F32)
    m_ref[:, cols] = m_new


def _att_kernel(q_ref, cos_ref, sin_ref, gl_ref, kc_ref, vct_ref, ksa_ref, kw_ref, vswt_ref, ovl_ref, o_ref,
                qa_s, m_sel, l_sel, a_sel, m_win, l_win, a_win):
    g = pl.program_id(1)
    qi = pl.program_id(2)
    tq = ATT_TQ
    tk = ATT_TK
    hpg = NSA_HPG
    dh = NSA_DH
    t0 = qi * tq
    seq = ksa_ref.shape[2]
    n_sel = ksa_ref.shape[3] - dh
    heads = [slice(h * tq, (h + 1) * tq) for h in range(hpg)]

    cos = jnp.concatenate([cos_ref[...]] * (hpg // 2), axis=1)
    sin = jnp.concatenate([sin_ref[...]] * (hpg // 2), axis=1)
    qt = (_nsa_rope(q_ref[...].astype(F32), cos, sin) * (dh ** -0.5 * LOG2E)).T
    q_t = jnp.concatenate([qt[h * dh:(h + 1) * dh] for h in range(hpg)], axis=1)
    q_tb = q_t.astype(BF16)
    qa_s[0:dh, :] = q_tb

    zero = jnp.zeros_like(q_tb)
    q2 = jnp.where(g == 0, jnp.concatenate([q_tb, zero], axis=0), jnp.concatenate([zero, q_tb], axis=0))
    s = jnp.dot(kc_ref[0], q2, preferred_element_type=F32)
    n_cmp = (seq - CMP_BLOCK) // CMP_STRIDE + 1
    nrow = lax.broadcasted_iota(I32, s.shape, 0)
    tcol = t0 + lax.broadcasted_iota(I32, s.shape, 1) % tq
    ok = (nrow * CMP_STRIDE + (CMP_BLOCK - 1) <= tcol) & (nrow < n_cmp)
    sm = jnp.where(ok, s, NEG)
    e = jnp.where(ok, jnp.exp2(sm - jnp.max(sm, axis=0, keepdims=True)), 0.0)
    den = jnp.sum(e, axis=0, keepdims=True)
    pb = (e / jnp.where(den > 0.0, den, 1.0)).astype(BF16)
    oc = jnp.dot(vct_ref[0], pb, preferred_element_type=F32)
    o_cmp = jnp.where(g == 0, oc[0:dh], oc[dh:2 * dh])
    pcat = jnp.concatenate([pb[:, hs] for hs in heads], axis=0)
    imp_t = jnp.dot(ovl_ref[...], pcat, preferred_element_type=F32)[0:n_sel]

    jrow = lax.broadcasted_iota(I32, (n_sel, tq), 0)
    cur = (t0 + lax.broadcasted_iota(I32, (n_sel, tq), 1)) // SEL_BLOCK
    forced = (jrow == 0) | (jrow == cur) | (jrow == cur - 1)
    imp_t = jnp.where(forced, jnp.inf, jnp.where(jrow <= cur, imp_t, -jnp.inf))
    sub = 8
    groups = [imp_t[r:r + sub] for r in range(0, n_sel, sub)]
    ranks = [jnp.zeros((sub, tq), F32) for _ in groups]
    srow = lax.broadcasted_iota(I32, (sub, tq), 0)
    for i in range(n_sel):
        gi, si = divmod(i, sub)
        ri = groups[gi][si:si + 1, :]
        for gj, v in enumerate(groups):
            ge = jnp.where(ri >= v, 1.0, 0.0)
            if gj > gi:
                ranks[gj] = ranks[gj] + ge
            else:
                gt = jnp.where(ri > v, 1.0, 0.0)
                ranks[gj] = ranks[gj] + (gt if gj < gi else jnp.where(srow > si, ge, gt))
    rank = jnp.concatenate(ranks, axis=0)
    bias_t = jnp.where(rank < float(SEL_TOPK), 0.0, NEG).astype(BF16)
    qa_s[dh:dh + n_sel, :] = jnp.concatenate([bias_t] * hpg, axis=1)

    m_sel[...] = jnp.full(m_sel.shape, NEG, F32)
    l_sel[...] = jnp.zeros(l_sel.shape, F32)
    a_sel[...] = jnp.zeros(a_sel.shape, F32)

    kb = ATT_KB
    assert tk == 2 * tq and tq % kb == 0
    krow = lax.broadcasted_iota(I32, (kb, tq), 0)
    qtime = t0 + lax.broadcasted_iota(I32, (kb, tq), 1)

    def run_blocks(blocks):
        sc = [blk[0]() for blk in blocks[:ATT_AHEAD]]
        for b, (_, mask, vt, hs, refs) in enumerate(blocks):
            if b + ATT_AHEAD < len(blocks):
                sc.append(blocks[b + ATT_AHEAD][0]())
            ok = mask()
            s_b = sc[b] if ok is None else jnp.where(ok, sc[b], NEG)
            sc[b] = None
            _softmax_step(s_b, ok, vt(), hs, *refs)

    def sel_blocks(k0, n_keys, causal):
        def block(kk, hs):
            return (lambda: jnp.dot(ksa_ref[0, 0, pl.ds(k0 + kk, kb), :], qa_s[:, hs], preferred_element_type=F32),
                    (lambda: (k0 + kk + krow) <= qtime) if causal else (lambda: None),
                    lambda: vswt_ref[0, 0, 0:dh, pl.ds(k0 + kk, kb)], hs, (m_sel, l_sel, a_sel))
        return [block(kk, hs) for kk in range(0, n_keys, kb) for hs in heads]

    def sel_body(j, carry):
        run_blocks(sel_blocks(pl.multiple_of(j * tk, tk), tk, False))
        return carry

    n_full = t0 // tk
    lax.fori_loop(0, n_full, sel_body, 0)

    @pl.when(n_full * tk < t0)
    def _():
        run_blocks(sel_blocks(pl.multiple_of(n_full * tk, tk), tk - tq, False))

    kw0 = pl.multiple_of(jnp.maximum(t0 - WINDOW, 0), tq)

    def win_block(kk, hs):
        def mask():
            dist = qtime - (kw0 + kk + krow)
            return (dist >= 0) & (dist < WINDOW)
        return (lambda: jnp.dot(kw_ref[0, 0, pl.ds(kw0 + kk, kb), :], qa_s[0:dh, hs], preferred_element_type=F32),
                mask, lambda: vswt_ref[0, 0, dh:2 * dh, pl.ds(kw0 + kk, kb)], hs, (m_win, l_win, a_win))

    m_win[...] = jnp.full(m_win.shape, NEG, F32)
    l_win[...] = jnp.zeros(l_win.shape, F32)
    a_win[...] = jnp.zeros(a_win.shape, F32)
    win = [win_block(kk, hs) for kk in range(0, WINDOW + tq, kb) for hs in heads]
    diag = sel_blocks(pl.multiple_of(t0, tq), tq, True)
    per = len(win) // len(diag)
    mixed = []
    for i, blk in enumerate(diag):
        mixed += win[i * per:(i + 1) * per] + [blk]
    run_blocks(mixed + win[len(diag) * per:])

    o_win = a_win[...] / l_win[...]
    o_slc = a_sel[...] / l_sel[...]
    gl_t = jax.nn.sigmoid(gl_ref[...].astype(F32)).T
    nb = NSA_N_BRANCH
    outs = []
    for h, hs in enumerate(heads):
        gate = lambda br: jnp.where(g == 0, gl_t[nb * h + br:nb * h + br + 1],
                                    gl_t[nb * (hpg + h) + br:nb * (hpg + h) + br + 1])
        outs.append(gate(0) * o_cmp[:, hs] + gate(1) * o_slc[:, hs] + gate(2) * o_win[:, hs])
    o_ref[...] = jnp.concatenate(outs, axis=0).T


def nsa_attention(z, nsa_cos, nsa_sin, kc, vct, ksa, kw, vswt, ovl, *, batch, seq):
    tq = ATT_TQ
    nt = seq // tq
    g = NSA_GROUPS
    gw = g * NSA_DH
    qw = NSA_HPG * NSA_DH
    m_cols = NSA_HPG * tq
    row = lambda b, gg, i: b * nt + i
    per_bg = lambda a: pl.BlockSpec((1, 1) + a.shape[2:], lambda b, gg, i: (b, gg, 0, 0))
    stat = pltpu.VMEM((1, m_cols), F32)
    vals = pltpu.VMEM((NSA_DH, m_cols), F32)
    return pl.pallas_call(
        _att_kernel,
        out_shape=jax.ShapeDtypeStruct((batch * seq, g * qw), F32),
        grid=(batch, g, nt),
        in_specs=[pl.BlockSpec((tq, qw), lambda b, gg, i: (row(b, gg, i), Z_NQ // qw + gg)),
                  pl.BlockSpec((tq, gw), lambda b, gg, i: (row(b, gg, i), 0)),
                  pl.BlockSpec((tq, gw), lambda b, gg, i: (row(b, gg, i), 0)),
                  pl.BlockSpec((tq, LANES), lambda b, gg, i: (row(b, gg, i), Z_NG // LANES)),
                  pl.BlockSpec((1,) + kc.shape[1:], lambda b, gg, i: (b, 0, 0)),
                  pl.BlockSpec((1,) + vct.shape[1:], lambda b, gg, i: (b, 0, 0)),
                  per_bg(ksa), per_bg(kw), per_bg(vswt),
                  pl.BlockSpec(ovl.shape, lambda b, gg, i: (0, 0))],
        out_specs=pl.BlockSpec((tq, qw), lambda b, gg, i: (row(b, gg, i), gg)),
        scratch_shapes=[pltpu.VMEM((ksa.shape[3], m_cols), BF16), stat, stat, vals, stat, stat, vals],
        compiler_params=_cparams(("parallel", "parallel", "parallel"), 48),
        name="nsa_attention",
    )(z, nsa_cos, nsa_sin, z, kc, vct, ksa, kw, vswt, ovl)


MERGE_TM = 256


def _layer_norm(y, g, b):
    mu = jnp.mean(y, axis=-1, keepdims=True)
    var = jnp.mean(jnp.square(y - mu), axis=-1, keepdims=True)
    return (y - mu) * lax.rsqrt(var + LN_EPS) * g + b


def _merge_kernel(x_ref, yc_ref, or_ref, on_ref, g0_ref, g1_ref, g2_ref, wr_ref, wn_ref, wo_ref, lg_ref, lb_ref, o_ref,
                  op_ref):
    y_ret = jnp.dot(or_ref[...].astype(BF16), wr_ref[...], preferred_element_type=F32)
    y_nsa = jnp.dot(on_ref[...].astype(BF16), wn_ref[...], preferred_element_type=F32)
    gate = lambda ref: jax.nn.sigmoid(ref[...].astype(F32))
    m = gate(g0_ref) * yc_ref[...] + gate(g1_ref) * y_ret + gate(g2_ref) * y_nsa
    h = jnp.dot(m.astype(BF16), wo_ref[...], preferred_element_type=F32)
    y = _layer_norm(DN_ALPHA * x_ref[...] + h, lg_ref[...], lb_ref[...])
    o_ref[...] = y
    _store_chunked(op_ref, _pack_rows(y))


def merge_block(x, y_conv, o_ret, o_nsa, z, ret_w_o, nsa_w_o, w_out, ln_g, ln_b):
    n = x.shape[0]
    tm = MERGE_TM
    d = D_MODEL
    rowd = lambda i: (i, 0)
    const = lambda i: (0, 0)
    return pl.pallas_call(
        _merge_kernel,
        out_shape=(jax.ShapeDtypeStruct((n, d), F32), jax.ShapeDtypeStruct((n * ROW_CHUNKS, LANES), U32)),
        grid=(n // tm,),
        in_specs=[pl.BlockSpec((tm, d), rowd), pl.BlockSpec((tm, d), rowd),
                  pl.BlockSpec((tm, o_ret.shape[1]), rowd), pl.BlockSpec((tm, o_nsa.shape[1]), rowd),
                  pl.BlockSpec((tm, d), lambda i: (i, Z_MG // d)),
                  pl.BlockSpec((tm, d), lambda i: (i, Z_MG // d + 1)),
                  pl.BlockSpec((tm, d), lambda i: (i, Z_MG // d + 2)),
                  pl.BlockSpec(ret_w_o.shape, const), pl.BlockSpec(nsa_w_o.shape, const), pl.BlockSpec(w_out.shape, const),
                  pl.BlockSpec((1, d), const), pl.BlockSpec((1, d), const)],
        out_specs=(pl.BlockSpec((tm, d), rowd), pl.BlockSpec((tm * ROW_CHUNKS, LANES), rowd)),
        compiler_params=_cparams(("parallel",), 48),
        name="merge_ln1",
    )(x, y_conv, o_ret, o_nsa, z, z, z, ret_w_o, nsa_w_o, w_out, ln_g, ln_b)


RT_TM = 256


def _stable_rank(v):
    n = v.shape[0]
    row = lax.broadcasted_iota(I32, v.shape, 0)
    rank = jnp.zeros(v.shape, F32)
    for i in range(n):
        r = v[i:i + 1, :]
        rank = rank + jnp.where(row > i, jnp.where(r >= v, 1.0, 0.0), jnp.where(r > v, 1.0, 0.0))
    return rank


def _router_kernel(x_ref, wr_ref, b_ref, ltri_ref, utri_ref, eidx_ref, rnk_ref, wts_ref, cnt_ref, carry):
    @pl.when(pl.program_id(0) == 0)
    def _():
        carry[...] = jnp.zeros_like(carry)

    tm = x_ref.shape[0]
    ne = N_EXPERTS
    per = ne // N_EXPERT_GROUPS
    logits = jnp.dot(x_ref[...].astype(BF16), wr_ref[...], preferred_element_type=F32)
    s = jax.nn.sigmoid(logits.T[0:ne])
    sb = s + b_ref[...]
    sub = lax.broadcasted_iota(I32, (per, tm), 0)
    gscore = []
    for gi in range(N_EXPERT_GROUPS):
        v = sb[gi * per:(gi + 1) * per]
        m1 = jnp.max(v, axis=0, keepdims=True)
        first = jnp.min(jnp.where(v == m1, sub, per), axis=0, keepdims=True)
        m2 = jnp.max(jnp.where(sub == first, -jnp.inf, v), axis=0, keepdims=True)
        gscore.append(m1 + m2)
    gscore = jnp.concatenate(gscore, axis=0)
    gkeep = jnp.where(_stable_rank(gscore) < float(TOPK_GROUPS), 1.0, 0.0)
    ekeep = jnp.concatenate([jnp.broadcast_to(gkeep[gi:gi + 1], (per, tm)) for gi in range(N_EXPERT_GROUPS)], axis=0)
    sel = jnp.where(_stable_rank(jnp.where(ekeep > 0.0, sb, -jnp.inf)) < float(TOPK), 1.0, 0.0)
    ssel = s * sel
    gate = ssel / jnp.sum(ssel, axis=0, keepdims=True) * ROUTED_SCALE

    selb = sel.astype(BF16)
    slot = jnp.dot(ltri_ref[...], selb, preferred_element_type=F32)
    incl = jnp.dot(selb, utri_ref[...], preferred_element_type=F32)
    rnk = carry[...] + incl - 1.0
    carry[...] = carry[...] + incl[:, tm - 1:tm]
    erow = lax.broadcasted_iota(I32, (ne, tm), 0).astype(F32)
    es, rs, ws = [], [], []
    for k in range(TOPK):
        pick = jnp.where(slot == float(k), sel, 0.0)
        es.append(jnp.sum(pick * erow, axis=0, keepdims=True))
        rs.append(jnp.sum(pick * rnk, axis=0, keepdims=True))
        ws.append(jnp.sum(pick * gate, axis=0, keepdims=True))
    eidx_ref[...] = jnp.concatenate(es, axis=0).astype(I32)
    rnk_ref[...] = jnp.concatenate(rs, axis=0).astype(I32)
    wts_ref[...] = jnp.concatenate(ws + [jnp.zeros((LANES - TOPK, tm), F32)], axis=0).T
    cnt_ref[...] = jnp.broadcast_to(carry[...], cnt_ref.shape).astype(I32)


def moe_route(x, router_w_pad, router_b_col):
    n = x.shape[0]
    tm = RT_TM
    ne = N_EXPERTS
    ltri = jnp.asarray(np.tril(np.ones((ne, ne), np.float32), -1), BF16)
    utri = jnp.asarray(np.triu(np.ones((tm, tm), np.float32)), BF16)
    const = lambda i: (0, 0)
    return pl.pallas_call(
        _router_kernel,
        out_shape=(jax.ShapeDtypeStruct((TOPK, n), I32), jax.ShapeDtypeStruct((TOPK, n), I32),
                   jax.ShapeDtypeStruct((n, LANES), F32), jax.ShapeDtypeStruct((ne, LANES), I32)),
        grid=(n // tm,),
        in_specs=[pl.BlockSpec((tm, D_MODEL), lambda i: (i, 0)),
                  pl.BlockSpec(router_w_pad.shape, const), pl.BlockSpec((ne, 1), const),
                  pl.BlockSpec((ne, ne), const), pl.BlockSpec((tm, tm), const)],
        out_specs=(pl.BlockSpec((TOPK, tm), lambda i: (0, i)), pl.BlockSpec((TOPK, tm), lambda i: (0, i)),
                   pl.BlockSpec((tm, LANES), lambda i: (i, 0)), pl.BlockSpec((ne, LANES), const)),
        scratch_shapes=[pltpu.VMEM((ne, 1), F32)],
        compiler_params=_cparams(("arbitrary",)),
        name="moe_route",
    )(x, router_w_pad, router_b_col, ltri, utri)


def _moe_rows(n_tokens):
    return n_tokens * TOPK + N_EXPERTS * MOE_TILE


def _plan_kernel(cnt_ref, off_ref, texp_ref, nused_ref):
    shift = MOE_TILE.bit_length() - 1

    def per_expert(e, carry):
        off, ti = carry
        off_ref[e] = off
        ntile = lax.shift_right_logical(cnt_ref[e] + (MOE_TILE - 1), shift)

        def mark(j, c):
            texp_ref[ti + j] = e
            return c

        lax.fori_loop(0, ntile, mark, 0)
        return off + ntile * MOE_TILE, ti + ntile

    _, used = lax.fori_loop(0, N_EXPERTS, per_expert, (jnp.int32(0), jnp.int32(0)))
    nused_ref[0] = used

    def fill(j, c):
        texp_ref[j] = N_EXPERTS - 1
        return c

    lax.fori_loop(used, texp_ref.shape[0], fill, 0)


def moe_plan(counts, n_tokens):
    nt = _moe_rows(n_tokens) // MOE_TILE
    smem = pl.BlockSpec(memory_space=pltpu.SMEM)
    return pl.pallas_call(
        _plan_kernel,
        out_shape=(jax.ShapeDtypeStruct((N_EXPERTS,), I32), jax.ShapeDtypeStruct((nt,), I32),
                   jax.ShapeDtypeStruct((1,), I32)),
        in_specs=[smem],
        out_specs=(smem, smem, smem),
        name="moe_plan",
    )(counts)


DSP_TB = 256


U32 = jnp.uint32
ROW_WORDS = D_MODEL // 2
ROW_CHUNKS = ROW_WORDS // LANES
assert ROW_CHUNKS == 4


def _pack_rows(y):
    half = y.shape[1] // 2
    bits = lambda v: lax.bitcast_convert_type(v.astype(jnp.bfloat16).astype(F32), U32)
    return bits(y[:, :half]) | lax.shift_right_logical(bits(y[:, half:]), jnp.uint32(16))


def _unpack_rows(words, dtype):
    hi = lax.bitcast_convert_type(words & jnp.uint32(0xFFFF0000), F32)
    lo = lax.bitcast_convert_type(lax.shift_left(words, jnp.uint32(16)), F32)
    return jnp.concatenate([hi, lo], axis=1).astype(dtype)


def _load_chunked(ref, rows):
    return jnp.concatenate([ref[pl.ds(s, rows, stride=ROW_CHUNKS), :] for s in range(ROW_CHUNKS)], axis=1)


def _store_chunked(ref, words):
    rows = words.shape[0]
    for s in range(ROW_CHUNKS):
        ref[pl.ds(s, rows, stride=ROW_CHUNKS), :] = words[:, s * LANES:(s + 1) * LANES]


def _dst_kernel(off_ref, eidx_ref, rnk_ref, dst_ref):
    e = eidx_ref[...]
    base = jnp.zeros(e.shape, I32)
    for x in range(N_EXPERTS):
        base = jnp.where(e == x, off_ref[x], base)
    dst_ref[...] = (base + rnk_ref[...]) * ROW_CHUNKS


def moe_dst(eidx, rnk, off):
    k, n = eidx.shape
    tb = 2048
    blk = pl.BlockSpec((k, tb), lambda i: (0, i))
    return pl.pallas_call(
        _dst_kernel,
        out_shape=jax.ShapeDtypeStruct((k, n), I32),
        grid=(n // tb,),
        in_specs=[pl.BlockSpec(memory_space=pltpu.SMEM), blk, blk],
        out_specs=blk,
        compiler_params=_cparams(("parallel",)),
        name="moe_dst",
    )(off, eidx, rnk)


def _tile_copy(src_ref, src_row, dst_ref, dst_row, sem):
    return pltpu.make_async_copy(src_ref.at[pl.ds(pl.multiple_of(src_row, ROW_CHUNKS), ROW_CHUNKS)],
                                 dst_ref.at[pl.ds(pl.multiple_of(dst_row, ROW_CHUNKS), ROW_CHUNKS)], sem)


def _dispatch_kernel(dst_ref, off_ref, cnt_ref, x_ref, xs_ref, zbuf, sem, zsem):
    tb = x_ref.shape[0] // ROW_CHUNKS
    zrows = MOE_TILE * ROW_CHUNKS

    def pad_copy(e):
        cnt = cnt_ref[e]
        rem = jnp.bitwise_and(cnt, MOE_TILE - 1)
        start = pl.multiple_of((off_ref[e] + cnt - rem) * ROW_CHUNKS, zrows)
        return rem != 0, pltpu.make_async_copy(zbuf, xs_ref.at[pl.ds(start, zrows)], zsem)

    @pl.when(pl.program_id(0) == 0)
    def _():
        zbuf[...] = jnp.zeros_like(zbuf)

        def start(e, c):
            has_pad, cp = pad_copy(e)

            @pl.when(has_pad)
            def _():
                cp.start()
            return c

        def wait(e, c):
            has_pad, cp = pad_copy(e)

            @pl.when(has_pad)
            def _():
                cp.wait()
            return c

        lax.fori_loop(0, N_EXPERTS, start, 0)
        lax.fori_loop(0, N_EXPERTS, wait, 0)

    def issue(t, c):
        for k in range(TOPK):
            _tile_copy(x_ref, t * ROW_CHUNKS, xs_ref, dst_ref[k, t], sem).start(priority=k % 2)
        return c

    def drain(t, c):
        for k in range(TOPK):
            _tile_copy(x_ref, 0, xs_ref, 0, sem).wait()
        return c

    lax.fori_loop(0, tb, issue, 0)
    lax.fori_loop(0, tb, drain, 0)


def moe_dispatch(xc, dst, off, counts):
    n = xc.shape[0] // ROW_CHUNKS
    tb = DSP_TB
    smem_all = pl.BlockSpec(memory_space=pltpu.SMEM)
    smem_blk = pl.BlockSpec((TOPK, tb), lambda i: (0, i), memory_space=pltpu.SMEM)
    return pl.pallas_call(
        _dispatch_kernel,
        out_shape=jax.ShapeDtypeStruct((_moe_rows(n) * ROW_CHUNKS, LANES), xc.dtype),
        grid=(n // tb,),
        in_specs=[smem_blk, smem_all, smem_all, pl.BlockSpec((tb * ROW_CHUNKS, LANES), lambda i: (i, 0))],
        out_specs=pl.BlockSpec(memory_space=pl.ANY),
        scratch_shapes=[pltpu.VMEM((MOE_TILE * ROW_CHUNKS, LANES), xc.dtype), pltpu.SemaphoreType.DMA(()),
                        pltpu.SemaphoreType.DMA(())],
        compiler_params=_cparams(("arbitrary",)),
        name="moe_dispatch",
    )(dst, off, counts, xc)


def _expert_kernel(texp_ref, nused_ref, xs_ref, w1_ref, w3_ref, w2_ref, ys_ref, w1b, w3b, w2b, last):
    i = pl.program_id(0)
    e = texp_ref[i]

    @pl.when(i == 0)
    def _():
        last[0] = -1

    @pl.when(e != last[0])
    def _():
        w1b[...] = w1_ref[0, 0].astype(BF16)
        w3b[...] = w3_ref[0, 0].astype(BF16)
        w2b[...] = w2_ref[0, 0].astype(BF16)
        last[0] = e

    @pl.when(i < nused_ref[0])
    def _():
        xb = _unpack_rows(_load_chunked(xs_ref, MOE_TILE), BF16)
        h1 = jnp.dot(xb, w1b[...], preferred_element_type=F32)
        h3 = jnp.dot(xb, w3b[...], preferred_element_type=F32)
        h = h1 * jax.nn.sigmoid(h1) * h3
        y = jnp.dot(h.astype(BF16), w2b[...], preferred_element_type=F32)
        _store_chunked(ys_ref, _pack_rows(y))

    @pl.when(i >= nused_ref[0])
    def _():
        ys_ref[...] = jnp.zeros_like(ys_ref)


def moe_experts(xs, texp, nused, w1, w3, w2, layer):
    crows = xs.shape[0]
    d, f = w1.shape[2], w1.shape[3]
    blk = MOE_TILE * ROW_CHUNKS
    nt = crows // blk
    grid_spec = pltpu.PrefetchScalarGridSpec(
        num_scalar_prefetch=2,
        grid=(nt,),
        in_specs=[pl.BlockSpec((blk, LANES), lambda i, te, nu: (jnp.where(i < nu[0], i, 0), 0)),
                  pl.BlockSpec((1, 1, d, f), lambda i, te, nu: (layer, te[i], 0, 0)),
                  pl.BlockSpec((1, 1, d, f), lambda i, te, nu: (layer, te[i], 0, 0)),
                  pl.BlockSpec((1, 1, f, d), lambda i, te, nu: (layer, te[i], 0, 0))],
        out_specs=pl.BlockSpec((blk, LANES), lambda i, te, nu: (i, 0)),
        scratch_shapes=[pltpu.VMEM((d, f), BF16), pltpu.VMEM((d, f), BF16), pltpu.VMEM((f, d), BF16),
                        pltpu.SMEM((1,), I32)],
    )
    return pl.pallas_call(
        _expert_kernel,
        out_shape=jax.ShapeDtypeStruct((crows, LANES), U32),
        grid_spec=grid_spec,
        compiler_params=_cparams(("arbitrary",), 48),
        name="moe_experts",
    )(texp, nused, xs, w1, w3, w2)


CMB_TB = 128


def _combine_kernel(dst_ref, dst_next_ref, x_ref, wts_ref, ws1_ref, ws3_ref, ws2_ref, lg_ref, lb_ref, ys_ref, o_ref,
                    buf, sem):
    tb = x_ref.shape[0]
    i = pl.program_id(0)
    slot = lax.rem(i, 2)

    def fetch(table_ref, into):
        def issue(t, c):
            for k in range(TOPK):
                _tile_copy(ys_ref, table_ref[k, t], buf.at[into, k], t * ROW_CHUNKS,
                           sem.at[into]).start(priority=k % 2)
            return c
        lax.fori_loop(0, tb, issue, 0)

    def drain(t, c):
        for k in range(TOPK):
            _tile_copy(ys_ref, 0, buf.at[slot, 0], 0, sem.at[slot]).wait()
        return c

    @pl.when(i == 0)
    def _():
        fetch(dst_ref, slot)

    @pl.when(i + 1 < pl.num_programs(0))
    def _():
        fetch(dst_next_ref, 1 - slot)

    x = x_ref[...]
    xb = x.astype(BF16)
    h1 = jnp.dot(xb, ws1_ref[...], preferred_element_type=F32)
    h3 = jnp.dot(xb, ws3_ref[...], preferred_element_type=F32)
    y = jnp.dot((h1 * jax.nn.sigmoid(h1) * h3).astype(BF16), ws2_ref[...], preferred_element_type=F32)
    lax.fori_loop(0, tb, drain, 0)
    w = wts_ref[...]
    routed = y
    for k in range(TOPK):
        routed = routed + w[:, k:k + 1] * _unpack_rows(_load_chunked(buf.at[slot, k], tb), F32)
    o_ref[...] = _layer_norm(DN_ALPHA * x + routed, lg_ref[...], lb_ref[...])


def moe_combine(x, ys, dst, wts, ws1, ws3, ws2, ln_g, ln_b):
    n, d = x.shape
    tb = CMB_TB
    steps = n // tb
    smem_blk = pl.BlockSpec((TOPK, tb), lambda i: (0, i), memory_space=pltpu.SMEM)
    smem_next = pl.BlockSpec((TOPK, tb), lambda i: (0, jnp.minimum(i + 1, steps - 1)), memory_space=pltpu.SMEM)
    const = lambda i: (0, 0)
    return pl.pallas_call(
        _combine_kernel,
        out_shape=jax.ShapeDtypeStruct((n, d), F32),
        grid=(steps,),
        in_specs=[smem_blk, smem_next,
                  pl.BlockSpec((tb, d), lambda i: (i, 0)), pl.BlockSpec((tb, LANES), lambda i: (i, 0)),
                  pl.BlockSpec(ws1.shape, const), pl.BlockSpec(ws3.shape, const), pl.BlockSpec(ws2.shape, const),
                  pl.BlockSpec((1, d), const), pl.BlockSpec((1, d), const),
                  pl.BlockSpec(memory_space=pl.ANY)],
        out_specs=pl.BlockSpec((tb, d), lambda i: (i, 0)),
        scratch_shapes=[pltpu.VMEM((2, TOPK, tb * ROW_CHUNKS, LANES), U32), pltpu.SemaphoreType.DMA((2,))],
        compiler_params=_cparams(("arbitrary",), 48),
        name="moe_combine_ln2",
    )(dst, dst, x, wts, ws1, ws3, ws2, ln_g, ln_b, ys)


def moe_block(x, xp, router_w, router_b, w1, w3, w2, layer, ws1, ws3, ws2, ln_g, ln_b):
    n, d = x.shape
    rw = jnp.pad(router_w, ((0, 0), (0, LANES - N_EXPERTS))).astype(BF16)
    eidx, rnk, wts, cnt = moe_route(x, rw, router_b.reshape(N_EXPERTS, 1))
    counts = cnt[:, 0]
    off, texp, nused = moe_plan(counts, n)
    dst = moe_dst(eidx, rnk, off)
    xs = moe_dispatch(xp, dst, off, counts)
    ys = moe_experts(xs, texp, nused, w1, w3, w2, layer)
    return moe_combine(x, ys, dst, wts, ws1.astype(BF16), ws3.astype(BF16), ws2.astype(BF16),
                       ln_g.reshape(1, -1), ln_b.reshape(1, -1))


def nsa_rope_tables(positions):
    batch, seq = positions.shape
    inv, sgn = _nsa_inv_freq()
    posf = positions.astype(F32)
    tok = rope_tables(posf.reshape(batch * seq, 1), inv, sgn, tm=512)
    end = posf[:, CMP_BLOCK - 1::CMP_STRIDE]
    npad = _n_cmp_pad(seq)
    end = jnp.pad(end, ((0, 0), (0, npad - end.shape[1])))
    cmp = rope_tables(end.reshape(batch * npad, 1), inv, sgn, tm=npad)
    return tok, cmp


def nsa_branch(z, tok_tab, cmp_tab, wk, wv, ovl, *, batch, seq):
    n = batch * seq
    w = NSA_GROUPS * NSA_DH
    xk = z[:, Z_NKV:Z_NKV + w].reshape(n // CMP_HALF, CMP_ROW)
    xv = z[:, Z_NKV + w:Z_NKV + 2 * w].reshape(n // CMP_HALF, CMP_ROW)
    kc, vct = nsa_compress(xk, xv, wk, wv, cmp_tab[0], cmp_tab[1], batch=batch)
    ksa, kw, vswt = nsa_kv_prep(z, tok_tab[0], tok_tab[1], batch=batch, seq=seq)
    return nsa_attention(z, tok_tab[0], tok_tab[1], kc, vct, ksa, kw, vswt, ovl, batch=batch, seq=seq)


def kernel(x, positions, w_in, conv_dw, conv_db, conv_ln_g, conv_ln_b, conv_w_pw, ret_w_o, nsa_pe_k, nsa_w1_k, nsa_w2_k,
           nsa_pe_v, nsa_w1_v, nsa_w2_v, nsa_w_o, w_out, ln1_g, ln1_b, router_w, router_b, moe_w1, moe_w3, moe_w2,
           shared_w1, shared_w3, shared_w2, ln2_g, ln2_b):
    batch, seq, d = x.shape
    n = batch * seq
    xf = x.reshape(n, d)
    row = lambda v: v.reshape(1, -1)

    ret_inv, ret_sgn = _ret_inv_freq()
    ret_tab = rope_tables(positions.astype(F32).reshape(n, 1), ret_inv, ret_sgn, tm=512)
    tok_tab, cmp_tab = nsa_rope_tables(positions)
    ret_consts = _ret_tables()
    ovl = _overlap_matrix(seq)

    for l in range(w_in.shape[0]):
        z = matmul(xf, _layout_w_in(w_in[l]), tm=2048, tn=512, out_dtype=BF16, name="in_proj")
        y_conv = conv_branch(z, conv_dw[l], row(conv_db[l]), row(conv_ln_g[l]), row(conv_ln_b[l]),
                             conv_w_pw[l].astype(BF16), batch=batch, seq=seq)
        o_ret = retention_branch(z, ret_tab[0], ret_tab[1], ret_consts, batch=batch, seq=seq)
        o_nsa = nsa_branch(z, tok_tab, cmp_tab, _cmp_weights(nsa_pe_k[l], nsa_w1_k[l], nsa_w2_k[l]),
                           _cmp_weights(nsa_pe_v[l], nsa_w1_v[l], nsa_w2_v[l]), ovl, batch=batch, seq=seq)
        x1, x1p = merge_block(xf, y_conv, o_ret, o_nsa, z, ret_w_o[l].astype(BF16), nsa_w_o[l].astype(BF16),
                              w_out[l].astype(BF16), row(ln1_g[l]), row(ln1_b[l]))
        xf = moe_block(x1, x1p, router_w[l], router_b[l], moe_w1, moe_w3, moe_w2, l,
                       shared_w1[l], shared_w3[l], shared_w2[l], ln2_g[l], ln2_b[l])
    return xf.reshape(batch, seq, d)
```

```python
import functools
import math

import jax
import jax.numpy as jnp
import numpy as np
from jax import lax
from jax.experimental import pallas as pl
from jax.experimental.pallas import tpu as pltpu

F32 = jnp.float32
BF16 = jnp.bfloat16
I32 = jnp.int32

D_MODEL = 1024
DEPTH = 4
CONV_CH = 512
CONV_WIDTH = 31
RET_HEADS = 4
RET_DK = 128
RET_DV = 256
RET_CHUNK = 128
RET_ROPE_BASE = 10000.0
NSA_HEADS = 8
NSA_GROUPS = 2
NSA_HPG = NSA_HEADS // NSA_GROUPS
NSA_DH = 64
NSA_N_BRANCH = 3
CMP_BLOCK = 32
CMP_STRIDE = 16
CMP_HIDDEN = 256
SEL_BLOCK = 64
SEL_TOPK = 16
WINDOW = 512
ROPE_THETA = 500000.0
ROT_DIM = NSA_DH // 4
N_EXPERTS = 64
N_EXPERT_GROUPS = 8
TOPK_GROUPS = 4
TOPK = 8
D_EXPERT = 256
D_SHARED = 256
ROUTED_SCALE = 2.5
DN_ALPHA = (2.0 * DEPTH) ** 0.25
LN_EPS = 1e-5
NEG = -1e30

LANES = 128

Z_MG = 0
Z_CA = 3072
Z_CB = 3584
Z_RQ = 4096
Z_RK = 4608
Z_RV = 5120
Z_RG = 6144
Z_NQ = 7168
Z_NKV = 7680
Z_NG = 8448
Z_W = 8704

_IN_WIDTHS = (512, 512, 512, 512, 1024, 1024, 512, 128, 128, 128, 128, 128, 128, 24, 3072)
_IN_OFFS = tuple(int(v) for v in np.concatenate([[0], np.cumsum(_IN_WIDTHS)[:-1]]))

MOE_TILE = 1024
MOE_ROWS_PER_TOKEN = TOPK


def _cparams(sem, vmem_mb=None):
    kw = dict(dimension_semantics=sem)
    if vmem_mb is not None:
        kw["vmem_limit_bytes"] = vmem_mb * 1024 * 1024
    return pltpu.CompilerParams(**kw)


def _mm_kernel(x_ref, w_ref, o_ref):
    o_ref[...] = jnp.dot(x_ref[...].astype(BF16), w_ref[...], preferred_element_type=F32).astype(o_ref.dtype)


def matmul(x, w, *, tm, tn, out_dtype=F32, x_col_block=0, k=None, name="mm"):
    m = x.shape[0]
    kk, n = w.shape
    if k is None:
        k = kk
    return pl.pallas_call(
        _mm_kernel,
        out_shape=jax.ShapeDtypeStruct((m, n), out_dtype),
        grid=(m // tm, n // tn),
        in_specs=[pl.BlockSpec((tm, k), lambda i, j: (i, x_col_block)),
                  pl.BlockSpec((k, tn), lambda i, j: (0, j))],
        out_specs=pl.BlockSpec((tm, tn), lambda i, j: (i, j)),
        compiler_params=_cparams(("parallel", "arbitrary"), 48),
        name=name,
    )(x, w)


def _rope_table_kernel(pos_ref, inv_ref, sgn_ref, cos_ref, sin_ref):
    ang = pos_ref[...] * inv_ref[...]
    cos_ref[...] = jnp.cos(ang)
    sin_ref[...] = jnp.sin(ang) * sgn_ref[...]


def rope_tables(pos_col, inv, sgn, *, tm):
    n = pos_col.shape[0]
    w = inv.shape[1]
    return pl.pallas_call(
        _rope_table_kernel,
        out_shape=(jax.ShapeDtypeStruct((n, w), F32), jax.ShapeDtypeStruct((n, w), F32)),
        grid=(n // tm,),
        in_specs=[pl.BlockSpec((tm, 1), lambda i: (i, 0)),
                  pl.BlockSpec((1, w), lambda i: (0, 0)),
                  pl.BlockSpec((1, w), lambda i: (0, 0))],
        out_specs=(pl.BlockSpec((tm, w), lambda i: (i, 0)), pl.BlockSpec((tm, w), lambda i: (i, 0))),
        compiler_params=_cparams(("parallel",)),
        name="rope_tables",
    )(pos_col, inv, sgn)


def _ret_inv_freq():
    inv = 1.0 / jnp.power(jnp.float32(RET_ROPE_BASE), jnp.linspace(0.0, 1.0, RET_DK // 2, dtype=F32))
    inv = jnp.concatenate([inv, inv])[None, :]
    sgn = jnp.concatenate([-jnp.ones((RET_DK // 2,), F32), jnp.ones((RET_DK // 2,), F32)])[None, :]
    return inv, sgn


def _nsa_inv_freq():
    half = ROT_DIM // 2
    inv = jnp.power(jnp.float32(ROPE_THETA), -jnp.arange(0, ROT_DIM, 2, dtype=F32) / ROT_DIM)
    z = jnp.zeros((NSA_DH - ROT_DIM,), F32)
    inv64 = jnp.concatenate([inv, inv, z])
    sgn64 = jnp.concatenate([-jnp.ones((half,), F32), jnp.ones((half,), F32), z])
    return jnp.concatenate([inv64, inv64])[None, :], jnp.concatenate([sgn64, sgn64])[None, :]


def _nsa_rope(x, cos, sin):
    w = x.shape[1]
    half = ROT_DIM // 2
    lane = lax.broadcasted_iota(I32, x.shape, 1) % NSA_DH
    partner = jnp.where(lane < half, pltpu.roll(x, w - half, 1), pltpu.roll(x, half, 1))
    return x * cos + partner * sin


CONV_TT = 256
CONV_HALO = 32


def _conv_kernel(a_ref, b_ref, ah_ref, bh_ref, dw_ref, db_ref, g_ref, be_ref, wpw_ref, o_ref, ubuf, sbuf):
    i = pl.program_id(1)
    tt = a_ref.shape[0]
    u = a_ref[...].astype(F32) * jax.nn.sigmoid(b_ref[...].astype(F32))
    uh = ah_ref[...].astype(F32) * jax.nn.sigmoid(bh_ref[...].astype(F32))
    ubuf[0:CONV_HALO, :] = jnp.where(i > 0, uh, 0.0)
    ubuf[CONV_HALO:CONV_HALO + tt, :] = u
    acc = jnp.zeros((tt, CONV_CH), F32)
    base = CONV_HALO - (CONV_WIDTH - 1)
    sub = 8
    for r in range(sub):
        offs = [o for o in range(base, base + CONV_WIDTH) if o % sub == r]
        if not offs:
            continue
        span = offs[-1] - r + tt
        sbuf[0:span, :] = ubuf[r:r + span, :]
        for o in offs:
            acc = acc + dw_ref[o - base:o - base + 1, :] * sbuf[o - r:o - r + tt, :]
    acc = acc + db_ref[...]
    mu = jnp.mean(acc, axis=-1, keepdims=True)
    var = jnp.mean(jnp.square(acc - mu), axis=-1, keepdims=True)
    y = (acc - mu) * lax.rsqrt(var + LN_EPS) * g_ref[...] + be_ref[...]
    y = y * jax.nn.sigmoid(y)
    o_ref[...] = jnp.dot(y.astype(BF16), wpw_ref[...], preferred_element_type=F32)


def conv_branch(z, dw, db, ln_g, ln_b, w_pw_bf, *, batch, seq):
    tt = CONV_TT
    nt = seq // tt
    r = tt // CONV_HALO
    ca, cb = Z_CA // CONV_CH, Z_CB // CONV_CH

    def halo_map(col):
        return lambda b, i: (jnp.maximum((b * nt + i) * r - 1, 0), col)

    return pl.pallas_call(
        _conv_kernel,
        out_shape=jax.ShapeDtypeStruct((batch * seq, D_MODEL), F32),
        grid=(batch, nt),
        in_specs=[pl.BlockSpec((tt, CONV_CH), lambda b, i: (b * nt + i, ca)),
                  pl.BlockSpec((tt, CONV_CH), lambda b, i: (b * nt + i, cb)),
                  pl.BlockSpec((CONV_HALO, CONV_CH), halo_map(ca)),
                  pl.BlockSpec((CONV_HALO, CONV_CH), halo_map(cb)),
                  pl.BlockSpec((CONV_WIDTH, CONV_CH), lambda b, i: (0, 0)),
                  pl.BlockSpec((1, CONV_CH), lambda b, i: (0, 0)),
                  pl.BlockSpec((1, CONV_CH), lambda b, i: (0, 0)),
                  pl.BlockSpec((1, CONV_CH), lambda b, i: (0, 0)),
                  pl.BlockSpec((CONV_CH, D_MODEL), lambda b, i: (0, 0))],
        out_specs=pl.BlockSpec((tt, D_MODEL), lambda b, i: (b * nt + i, 0)),
        scratch_shapes=[pltpu.VMEM((CONV_HALO + tt, CONV_CH), F32), pltpu.VMEM((CONV_HALO + tt, CONV_CH), F32)],
        compiler_params=_cparams(("parallel", "parallel")),
        name="conv_branch",
    )(z, z, z, z, dw, db, ln_g, ln_b, w_pw_bf)


RET_TQ = 512


def _ret_tables():
    h, c = RET_HEADS, RET_CHUNK
    log_gamma = jnp.log1p(-jnp.exp2(-5.0 - jnp.arange(h, dtype=F32)))
    idx = jnp.arange(c, dtype=F32)
    diff = idx[:, None] - idx[None, :]
    dmat = jnp.where(diff >= 0, jnp.exp(log_gamma[:, None, None] * jnp.maximum(diff, 0.0)), 0.0).astype(F32)
    xi = jnp.exp(log_gamma[:, None] * (idx + 1.0)).astype(F32)
    zeta = jnp.exp(log_gamma[:, None] * (c - 1.0 - idx)).astype(F32)
    decay = jnp.exp(log_gamma * c).astype(F32)
    xi_b = jnp.broadcast_to(xi[:, :, None], (h, c, RET_DV))
    zeta_b = jnp.broadcast_to(zeta[:, :, None], (h, c, RET_DV))
    decay_b = jnp.broadcast_to(decay[:, None, None], (h, RET_DK, RET_DV))
    return dmat, xi_b, zeta_b, decay_b


def _ret_kernel(q_ref, k_ref, v_ref, g_ref, cos_ref, sin_ref, dmat_ref, xi_ref, zeta_ref, dec_ref, o_ref, r_ref):
    @pl.when(pl.program_id(1) == 0)
    def _():
        r_ref[...] = jnp.zeros_like(r_ref)

    c = RET_CHUNK
    n_chunks = q_ref.shape[0] // c
    for ci in range(n_chunks):
        rows = slice(ci * c, (ci + 1) * c)
        cos = cos_ref[rows, :]
        sin = sin_ref[rows, :]
        for h in range(RET_HEADS):
            qk_cols = slice(h * RET_DK, (h + 1) * RET_DK)
            v_cols = slice(h * RET_DV, (h + 1) * RET_DV)
            q = q_ref[rows, qk_cols].astype(F32)
            k = k_ref[rows, qk_cols].astype(F32)
            q = q * cos + pltpu.roll(q, RET_DK // 2, 1) * sin
            k = (k * cos + pltpu.roll(k, RET_DK // 2, 1) * sin) * (RET_DK ** -0.5)
            v = v_ref[rows, v_cols].astype(F32)
            qb = q.astype(BF16)
            kb = k.astype(BF16)
            inner = lax.dot_general(qb, kb, (((1,), (1,)), ((), ())), preferred_element_type=F32) * dmat_ref[h]
            r_old = r_ref[h]
            o = (jnp.dot(inner.astype(BF16), v.astype(BF16), preferred_element_type=F32)
                 + jnp.dot(qb, r_old.astype(BF16), preferred_element_type=F32) * xi_ref[h])
            vz = (v * zeta_ref[h]).astype(BF16)
            r_ref[h] = r_old * dec_ref[h] + jnp.dot(k.T.astype(BF16), vz, preferred_element_type=F32)
            mu = jnp.mean(o, axis=-1, keepdims=True)
            var = jnp.mean(jnp.square(o - mu), axis=-1, keepdims=True)
            on = (o - mu) * lax.rsqrt(var + LN_EPS)
            g = g_ref[rows, v_cols].astype(F32)
            o_ref[rows, v_cols] = g * jax.nn.sigmoid(g) * on


def retention_branch(z, ret_cos, ret_sin, tables, *, batch, seq):
    tq = RET_TQ
    nt = seq // tq
    dmat, xi_b, zeta_b, decay_b = tables
    qw = RET_HEADS * RET_DK
    vw = RET_HEADS * RET_DV
    row = lambda b, i: b * nt + i
    full3 = lambda b, i: (0, 0, 0)
    return pl.pallas_call(
        _ret_kernel,
        out_shape=jax.ShapeDtypeStruct((batch * seq, vw), F32),
        grid=(batch, nt),
        in_specs=[pl.BlockSpec((tq, qw), lambda b, i: (row(b, i), Z_RQ // qw)),
                  pl.BlockSpec((tq, qw), lambda b, i: (row(b, i), Z_RK // qw)),
                  pl.BlockSpec((tq, vw), lambda b, i: (row(b, i), Z_RV // vw)),
                  pl.BlockSpec((tq, vw), lambda b, i: (row(b, i), Z_RG // vw)),
                  pl.BlockSpec((tq, RET_DK), lambda b, i: (row(b, i), 0)),
                  pl.BlockSpec((tq, RET_DK), lambda b, i: (row(b, i), 0)),
                  pl.BlockSpec(dmat.shape, full3),
                  pl.BlockSpec(xi_b.shape, full3),
                  pl.BlockSpec(zeta_b.shape, full3),
                  pl.BlockSpec(decay_b.shape, full3)],
        out_specs=pl.BlockSpec((tq, vw), lambda b, i: (row(b, i), 0)),
        scratch_shapes=[pltpu.VMEM((RET_HEADS, RET_DK, RET_DV), F32)],
        compiler_params=_cparams(("parallel", "arbitrary"), 48),
        name="retention",
    )(z, z, z, z, ret_cos, ret_sin, dmat, xi_b, zeta_b, decay_b)


def _layout_w_in(w):
    seg = lambda i: w[:, _IN_OFFS[i]:_IN_OFFS[i] + _IN_WIDTHS[i]]
    order = (14, 0, 1, 2, 3, 4, 5, 6, 7, 8, 9, 10, 11, 12, 13)
    parts = [seg(i) for i in order]
    used = sum(_IN_WIDTHS)
    parts.append(jnp.zeros((w.shape[0], Z_W - used), w.dtype))
    return jnp.concatenate(parts, axis=1).astype(BF16)


CMP_HALF = CMP_BLOCK // 2
CMP_ROW = CMP_HALF * NSA_GROUPS * NSA_DH


def _n_cmp_pad(seq):
    return seq // CMP_STRIDE


def _cmp_weights(pe, w1, w2):
    g = NSA_GROUPS
    eye = jnp.eye(g, dtype=F32)
    w = w1.reshape(2, CMP_HALF, NSA_DH, CMP_HIDDEN)
    w1ab = jnp.einsum("hldf,gk->hlgdkf", w, eye).reshape(2, CMP_ROW, g * CMP_HIDDEN).astype(BF16)
    peab = jnp.broadcast_to(pe.reshape(2, CMP_HALF, 1, NSA_DH), (2, CMP_HALF, g, NSA_DH)).reshape(2, 1, CMP_ROW)
    w2bd = jnp.einsum("fd,gk->gfkd", w2, eye).reshape(g * CMP_HIDDEN, g * NSA_DH).astype(BF16)
    return peab, w1ab, w2bd


def _cmp_mlp(x, pe_ref, w1_ref, w2_ref):
    x = x.astype(F32)
    a = jnp.dot((x + pe_ref[0]).astype(BF16), w1_ref[0], preferred_element_type=F32)
    b = jnp.dot((x + pe_ref[1]).astype(BF16), w1_ref[1], preferred_element_type=F32)
    hid = a + pltpu.roll(b, b.shape[0] - 1, 0)
    hid = hid * jax.nn.sigmoid(hid)
    return jnp.dot(hid.astype(BF16), w2_ref[...], preferred_element_type=F32)


def _compress_kernel(xk_ref, xv_ref, pek_ref, w1k_ref, w2k_ref, pev_ref, w1v_ref, w2v_ref, cos_ref, sin_ref,
                     k_ref, vt_ref):
    k = _cmp_mlp(xk_ref[...], pek_ref, w1k_ref, w2k_ref)
    k_ref[0] = _nsa_rope(k, cos_ref[...], sin_ref[...]).astype(BF16)
    vt_ref[0] = _cmp_mlp(xv_ref[...], pev_ref, w1v_ref, w2v_ref).T.astype(BF16)


def nsa_compress(xk, xv, wk, wv, cmp_cos, cmp_sin, *, batch):
    pek, w1k, w2k = wk
    pev, w1v, w2v = wv
    npad = xk.shape[0] // batch
    gw = NSA_GROUPS * NSA_DH
    c3 = lambda b: (0, 0, 0)
    c2 = lambda b: (0, 0)
    return pl.pallas_call(
        _compress_kernel,
        out_shape=(jax.ShapeDtypeStruct((batch, npad, gw), BF16), jax.ShapeDtypeStruct((batch, gw, npad), BF16)),
        grid=(batch,),
        in_specs=[pl.BlockSpec((npad, CMP_ROW), lambda b: (b, 0)),
                  pl.BlockSpec((npad, CMP_ROW), lambda b: (b, 0)),
                  pl.BlockSpec(pek.shape, c3), pl.BlockSpec(w1k.shape, c3), pl.BlockSpec(w2k.shape, c2),
                  pl.BlockSpec(pev.shape, c3), pl.BlockSpec(w1v.shape, c3), pl.BlockSpec(w2v.shape, c2),
                  pl.BlockSpec((npad, gw), lambda b: (b, 0)),
                  pl.BlockSpec((npad, gw), lambda b: (b, 0))],
        out_specs=(pl.BlockSpec((1, npad, gw), lambda b: (b, 0, 0)), pl.BlockSpec((1, gw, npad), lambda b: (b, 0, 0))),
        compiler_params=_cparams(("parallel",), 48),
        name="nsa_compress",
    )(xk, xv, pek, w1k, w2k, pev, w1v, w2v, cmp_cos, cmp_sin)


KV_TT = 512
ATT_TK = 512
ATT_KB = 256
ATT_AHEAD = 8
LOG2E = math.log2(math.e)


def _kv_prep_kernel(z_ref, cos_ref, sin_ref, ksa_ref, kw_ref, vswt_ref):
    tt = z_ref.shape[0]
    cos = cos_ref[...]
    sin = sin_ref[...]
    w = NSA_GROUPS * NSA_DH
    n_sel = ksa_ref.shape[3] - NSA_DH
    ks = _nsa_rope(z_ref[:, 2 * w:3 * w].astype(F32), cos, sin)
    vs = z_ref[:, 3 * w:4 * w].astype(F32)
    kw = _nsa_rope(z_ref[:, 4 * w:5 * w].astype(F32), cos, sin)
    vw = z_ref[:, 5 * w:6 * w].astype(F32)
    blk = (pl.program_id(1) * tt + lax.broadcasted_iota(I32, (tt, n_sel), 0)) // SEL_BLOCK
    onehot = jnp.where(blk == lax.broadcasted_iota(I32, (tt, n_sel), 1), 1.0, 0.0)
    for g in range(NSA_GROUPS):
        cols = slice(g * NSA_DH, (g + 1) * NSA_DH)
        ksa_ref[0, g] = jnp.concatenate([ks[:, cols], onehot], axis=1).astype(BF16)
        kw_ref[0, g] = kw[:, cols].astype(BF16)
        vswt_ref[0, g] = jnp.concatenate([vs[:, cols], vw[:, cols]], axis=1).T.astype(BF16)


def nsa_kv_prep(z, nsa_cos, nsa_sin, *, batch, seq):
    tt = KV_TT
    nt = seq // tt
    g = NSA_GROUPS
    w = g * NSA_DH
    n_sel = seq // SEL_BLOCK
    return pl.pallas_call(
        _kv_prep_kernel,
        out_shape=(jax.ShapeDtypeStruct((batch, g, seq, NSA_DH + n_sel), BF16),
                   jax.ShapeDtypeStruct((batch, g, seq, NSA_DH), BF16),
                   jax.ShapeDtypeStruct((batch, g, w, seq), BF16)),
        grid=(batch, nt),
        in_specs=[pl.BlockSpec((tt, 6 * w), lambda b, i: (b * nt + i, Z_NKV // (6 * w))),
                  pl.BlockSpec((tt, w), lambda b, i: (b * nt + i, 0)),
                  pl.BlockSpec((tt, w), lambda b, i: (b * nt + i, 0))],
        out_specs=(pl.BlockSpec((1, g, tt, NSA_DH + n_sel), lambda b, i: (b, 0, i, 0)),
                   pl.BlockSpec((1, g, tt, NSA_DH), lambda b, i: (b, 0, i, 0)),
                   pl.BlockSpec((1, g, w, tt), lambda b, i: (b, 0, 0, i))),
        compiler_params=_cparams(("parallel", "parallel")),
        name="nsa_kv_prep",
    )(z, nsa_cos, nsa_sin)


ATT_TQ = 256


def _overlap_matrix(seq):
    n_cmp = (seq - CMP_BLOCK) // CMP_STRIDE + 1
    n_sel = seq // SEL_BLOCK
    ii = np.arange(_n_cmp_pad(seq))[None, :]
    jj = np.arange(LANES)[:, None]
    lo = np.maximum(ii * CMP_STRIDE, jj * SEL_BLOCK)
    hi = np.minimum(ii * CMP_STRIDE + CMP_BLOCK, (jj + 1) * SEL_BLOCK)
    ov = np.maximum(hi - lo, 0).astype(np.float32) / CMP_BLOCK
    ov = np.where((ii < n_cmp) & (jj < n_sel), ov, 0.0)
    return jnp.asarray(np.tile(ov, (1, NSA_HPG)), BF16)


def _softmax_step(s, ok, vt, cols, m_ref, l_ref, a_ref):
    m_old = m_ref[:, cols]
    m_new = jnp.maximum(m_old, jnp.max(s, axis=0, keepdims=True))
    alpha = jnp.exp2(m_old - m_new)
    p = jnp.exp2(s - m_new)
    if ok is not None:
        p = jnp.where(ok, p, 0.0)
    l_ref[:, cols] = alpha * l_ref[:, cols] + jnp.sum(p, axis=0, keepdims=True)
    a_ref[:, cols] = alpha * a_ref[:, cols] + jnp.dot(vt, p.astype(BF16), preferred_element_type=F32)
    m_ref[:, cols] = m_new


def _att_kernel(q_ref, cos_ref, sin_ref, gl_ref, kc_ref, vct_ref, ksa_ref, kw_ref, vswt_ref, ovl_ref, o_ref,
                qa_s, m_sel, l_sel, a_sel, m_win, l_win, a_win):
    g = pl.program_id(1)
    qi = pl.program_id(2)
    tq = ATT_TQ
    tk = ATT_TK
    hpg = NSA_HPG
    dh = NSA_DH
    t0 = qi * tq
    seq = ksa_ref.shape[2]
    n_sel = ksa_ref.shape[3] - dh
    heads = [slice(h * tq, (h + 1) * tq) for h in range(hpg)]

    cos = jnp.concatenate([cos_ref[...]] * (hpg // 2), axis=1)
    sin = jnp.concatenate([sin_ref[...]] * (hpg // 2), axis=1)
    qt = (_nsa_rope(q_ref[...].astype(F32), cos, sin) * (dh ** -0.5 * LOG2E)).T
    q_t = jnp.concatenate([qt[h * dh:(h + 1) * dh] for h in range(hpg)], axis=1)
    q_tb = q_t.astype(BF16)
    qa_s[0:dh, :] = q_tb

    zero = jnp.zeros_like(q_tb)
    q2 = jnp.where(g == 0, jnp.concatenate([q_tb, zero], axis=0), jnp.concatenate([zero, q_tb], axis=0))
    s = jnp.dot(kc_ref[0], q2, preferred_element_type=F32)
    n_cmp = (seq - CMP_BLOCK) // CMP_STRIDE + 1
    nrow = lax.broadcasted_iota(I32, s.shape, 0)
    tcol = t0 + lax.broadcasted_iota(I32, s.shape, 1) % tq
    ok = (nrow * CMP_STRIDE + (CMP_BLOCK - 1) <= tcol) & (nrow < n_cmp)
    sm = jnp.where(ok, s, NEG)
    e = jnp.where(ok, jnp.exp2(sm - jnp.max(sm, axis=0, keepdims=True)), 0.0)
    den = jnp.sum(e, axis=0, keepdims=True)
    pb = (e / jnp.where(den > 0.0, den, 1.0)).astype(BF16)
    oc = jnp.dot(vct_ref[0], pb, preferred_element_type=F32)
    o_cmp = jnp.where(g == 0, oc[0:dh], oc[dh:2 * dh])
    pcat = jnp.concatenate([pb[:, hs] for hs in heads], axis=0)
    imp_t = jnp.dot(ovl_ref[...], pcat, preferred_element_type=F32)[0:n_sel]

    jrow = lax.broadcasted_iota(I32, (n_sel, tq), 0)
    cur = (t0 + lax.broadcasted_iota(I32, (n_sel, tq), 1)) // SEL_BLOCK
    forced = (jrow == 0) | (jrow == cur) | (jrow == cur - 1)
    imp_t = jnp.where(forced, jnp.inf, jnp.where(jrow <= cur, imp_t, -jnp.inf))
    sub = 8
    groups = [imp_t[r:r + sub] for r in range(0, n_sel, sub)]
    ranks = [jnp.zeros((sub, tq), F32) for _ in groups]
    srow = lax.broadcasted_iota(I32, (sub, tq), 0)
    for i in range(n_sel):
        gi, si = divmod(i, sub)
        ri = groups[gi][si:si + 1, :]
        for gj, v in enumerate(groups):
            ge = jnp.where(ri >= v, 1.0, 0.0)
            if gj > gi:
                ranks[gj] = ranks[gj] + ge
            else:
                gt = jnp.where(ri > v, 1.0, 0.0)
                ranks[gj] = ranks[gj] + (gt if gj < gi else jnp.where(srow > si, ge, gt))
    rank = jnp.concatenate(ranks, axis=0)
    bias_t = jnp.where(rank < float(SEL_TOPK), 0.0, NEG).astype(BF16)
    qa_s[dh:dh + n_sel, :] = jnp.concatenate([bias_t] * hpg, axis=1)

    m_sel[...] = jnp.full(m_sel.shape, NEG, F32)
    l_sel[...] = jnp.zeros(l_sel.shape, F32)
    a_sel[...] = jnp.zeros(a_sel.shape, F32)

    kb = ATT_KB
    assert tk == 2 * tq and tq % kb == 0
    krow = lax.broadcasted_iota(I32, (kb, tq), 0)
    qtime = t0 + lax.broadcasted_iota(I32, (kb, tq), 1)

    def run_blocks(blocks):
        sc = [blk[0]() for blk in blocks[:ATT_AHEAD]]
        for b, (_, mask, vt, hs, refs) in enumerate(blocks):
            if b + ATT_AHEAD < len(blocks):
                sc.append(blocks[b + ATT_AHEAD][0]())
            ok = mask()
            s_b = sc[b] if ok is None else jnp.where(ok, sc[b], NEG)
            sc[b] = None
            _softmax_step(s_b, ok, vt(), hs, *refs)

    def sel_blocks(k0, n_keys, causal):
        def block(kk, hs):
            return (lambda: jnp.dot(ksa_ref[0, 0, pl.ds(k0 + kk, kb), :], qa_s[:, hs], preferred_element_type=F32),
                    (lambda: (k0 + kk + krow) <= qtime) if causal else (lambda: None),
                    lambda: vswt_ref[0, 0, 0:dh, pl.ds(k0 + kk, kb)], hs, (m_sel, l_sel, a_sel))
        return [block(kk, hs) for kk in range(0, n_keys, kb) for hs in heads]

    def sel_body(j, carry):
        run_blocks(sel_blocks(pl.multiple_of(j * tk, tk), tk, False))
        return carry

    n_full = t0 // tk
    lax.fori_loop(0, n_full, sel_body, 0)

    @pl.when(n_full * tk < t0)
    def _():
        run_blocks(sel_blocks(pl.multiple_of(n_full * tk, tk), tk - tq, False))

    kw0 = pl.multiple_of(jnp.maximum(t0 - WINDOW, 0), tq)

    def win_block(kk, hs):
        def mask():
            dist = qtime - (kw0 + kk + krow)
            return (dist >= 0) & (dist < WINDOW)
        return (lambda: jnp.dot(kw_ref[0, 0, pl.ds(kw0 + kk, kb), :], qa_s[0:dh, hs], preferred_element_type=F32),
                mask, lambda: vswt_ref[0, 0, dh:2 * dh, pl.ds(kw0 + kk, kb)], hs, (m_win, l_win, a_win))

    m_win[...] = jnp.full(m_win.shape, NEG, F32)
    l_win[...] = jnp.zeros(l_win.shape, F32)
    a_win[...] = jnp.zeros(a_win.shape, F32)
    win = [win_block(kk, hs) for kk in range(0, WINDOW + tq, kb) for hs in heads]
    diag = sel_blocks(pl.multiple_of(t0, tq), tq, True)
    per = len(win) // len(diag)
    mixed = []
    for i, blk in enumerate(diag):
        mixed += win[i * per:(i + 1) * per] + [blk]
    run_blocks(mixed + win[len(diag) * per:])

    o_win = a_win[...] / l_win[...]
    o_slc = a_sel[...] / l_sel[...]
    gl_t = jax.nn.sigmoid(gl_ref[...].astype(F32)).T
    nb = NSA_N_BRANCH
    outs = []
    for h, hs in enumerate(heads):
        gate = lambda br: jnp.where(g == 0, gl_t[nb * h + br:nb * h + br + 1],
                                    gl_t[nb * (hpg + h) + br:nb * (hpg + h) + br + 1])
        outs.append(gate(0) * o_cmp[:, hs] + gate(1) * o_slc[:, hs] + gate(2) * o_win[:, hs])
    o_ref[...] = jnp.concatenate(outs, axis=0).T


def nsa_attention(z, nsa_cos, nsa_sin, kc, vct, ksa, kw, vswt, ovl, *, batch, seq):
    tq = ATT_TQ
    nt = seq // tq
    g = NSA_GROUPS
    gw = g * NSA_DH
    qw = NSA_HPG * NSA_DH
    m_cols = NSA_HPG * tq
    row = lambda b, gg, i: b * nt + i
    per_bg = lambda a: pl.BlockSpec((1, 1) + a.shape[2:], lambda b, gg, i: (b, gg, 0, 0))
    stat = pltpu.VMEM((1, m_cols), F32)
    vals = pltpu.VMEM((NSA_DH, m_cols), F32)
    return pl.pallas_call(
        _att_kernel,
        out_shape=jax.ShapeDtypeStruct((batch * seq, g * qw), F32),
        grid=(batch, g, nt),
        in_specs=[pl.BlockSpec((tq, qw), lambda b, gg, i: (row(b, gg, i), Z_NQ // qw + gg)),
                  pl.BlockSpec((tq, gw), lambda b, gg, i: (row(b, gg, i), 0)),
                  pl.BlockSpec((tq, gw), lambda b, gg, i: (row(b, gg, i), 0)),
                  pl.BlockSpec((tq, LANES), lambda b, gg, i: (row(b, gg, i), Z_NG // LANES)),
                  pl.BlockSpec((1,) + kc.shape[1:], lambda b, gg, i: (b, 0, 0)),
                  pl.BlockSpec((1,) + vct.shape[1:], lambda b, gg, i: (b, 0, 0)),
                  per_bg(ksa), per_bg(kw), per_bg(vswt),
                  pl.BlockSpec(ovl.shape, lambda b, gg, i: (0, 0))],
        out_specs=pl.BlockSpec((tq, qw), lambda b, gg, i: (row(b, gg, i), gg)),
        scratch_shapes=[pltpu.VMEM((ksa.shape[3], m_cols), BF16), stat, stat, vals, stat, stat, vals],
        compiler_params=_cparams(("parallel", "parallel", "parallel"), 48),
        name="nsa_attention",
    )(z, nsa_cos, nsa_sin, z, kc, vct, ksa, kw, vswt, ovl)


MERGE_TM = 256


def _layer_norm(y, g, b):
    mu = jnp.mean(y, axis=-1, keepdims=True)
    var = jnp.mean(jnp.square(y - mu), axis=-1, keepdims=True)
    return (y - mu) * lax.rsqrt(var + LN_EPS) * g + b


def _merge_kernel(x_ref, yc_ref, or_ref, on_ref, g0_ref, g1_ref, g2_ref, wr_ref, wn_ref, wo_ref, lg_ref, lb_ref, o_ref,
                  op_ref):
    y_ret = jnp.dot(or_ref[...].astype(BF16), wr_ref[...], preferred_element_type=F32)
    y_nsa = jnp.dot(on_ref[...].astype(BF16), wn_ref[...], preferred_element_type=F32)
    gate = lambda ref: jax.nn.sigmoid(ref[...].astype(F32))
    m = gate(g0_ref) * yc_ref[...] + gate(g1_ref) * y_ret + gate(g2_ref) * y_nsa
    h = jnp.dot(m.astype(BF16), wo_ref[...], preferred_element_type=F32)
    y = _layer_norm(DN_ALPHA * x_ref[...] + h, lg_ref[...], lb_ref[...])
    o_ref[...] = y
    _store_chunked(op_ref, _pack_rows(y))


def merge_block(x, y_conv, o_ret, o_nsa, z, ret_w_o, nsa_w_o, w_out, ln_g, ln_b):
    n = x.shape[0]
    tm = MERGE_TM
    d = D_MODEL
    rowd = lambda i: (i, 0)
    const = lambda i: (0, 0)
    return pl.pallas_call(
        _merge_kernel,
        out_shape=(jax.ShapeDtypeStruct((n, d), F32), jax.ShapeDtypeStruct((n * ROW_CHUNKS, LANES), U32)),
        grid=(n // tm,),
        in_specs=[pl.BlockSpec((tm, d), rowd), pl.BlockSpec((tm, d), rowd),
                  pl.BlockSpec((tm, o_ret.shape[1]), rowd), pl.BlockSpec((tm, o_nsa.shape[1]), rowd),
                  pl.BlockSpec((tm, d), lambda i: (i, Z_MG // d)),
                  pl.BlockSpec((tm, d), lambda i: (i, Z_MG // d + 1)),
                  pl.BlockSpec((tm, d), lambda i: (i, Z_MG // d + 2)),
                  pl.BlockSpec(ret_w_o.shape, const), pl.BlockSpec(nsa_w_o.shape, const), pl.BlockSpec(w_out.shape, const),
                  pl.BlockSpec((1, d), const), pl.BlockSpec((1, d), const)],
        out_specs=(pl.BlockSpec((tm, d), rowd), pl.BlockSpec((tm * ROW_CHUNKS, LANES), rowd)),
        compiler_params=_cparams(("parallel",), 48),
        name="merge_ln1",
    )(x, y_conv, o_ret, o_nsa, z, z, z, ret_w_o, nsa_w_o, w_out, ln_g, ln_b)


RT_TM = 256


def _stable_rank(v):
    n = v.shape[0]
    row = lax.broadcasted_iota(I32, v.shape, 0)
    rank = jnp.zeros(v.shape, F32)
    for i in range(n):
        r = v[i:i + 1, :]
        rank = rank + jnp.where(row > i, jnp.where(r >= v, 1.0, 0.0), jnp.where(r > v, 1.0, 0.0))
    return rank


def _router_kernel(x_ref, wr_ref, b_ref, ltri_ref, utri_ref, eidx_ref, rnk_ref, wts_ref, cnt_ref, carry):
    @pl.when(pl.program_id(0) == 0)
    def _():
        carry[...] = jnp.zeros_like(carry)

    tm = x_ref.shape[0]
    ne = N_EXPERTS
    per = ne // N_EXPERT_GROUPS
    logits = jnp.dot(x_ref[...].astype(BF16), wr_ref[...], preferred_element_type=F32)
    s = jax.nn.sigmoid(logits.T[0:ne])
    sb = s + b_ref[...]
    sub = lax.broadcasted_iota(I32, (per, tm), 0)
    gscore = []
    for gi in range(N_EXPERT_GROUPS):
        v = sb[gi * per:(gi + 1) * per]
        m1 = jnp.max(v, axis=0, keepdims=True)
        first = jnp.min(jnp.where(v == m1, sub, per), axis=0, keepdims=True)
        m2 = jnp.max(jnp.where(sub == first, -jnp.inf, v), axis=0, keepdims=True)
        gscore.append(m1 + m2)
    gscore = jnp.concatenate(gscore, axis=0)
    gkeep = jnp.where(_stable_rank(gscore) < float(TOPK_GROUPS), 1.0, 0.0)
    ekeep = jnp.concatenate([jnp.broadcast_to(gkeep[gi:gi + 1], (per, tm)) for gi in range(N_EXPERT_GROUPS)], axis=0)
    sel = jnp.where(_stable_rank(jnp.where(ekeep > 0.0, sb, -jnp.inf)) < float(TOPK), 1.0, 0.0)
    ssel = s * sel
    gate = ssel / jnp.sum(ssel, axis=0, keepdims=True) * ROUTED_SCALE

    selb = sel.astype(BF16)
    slot = jnp.dot(ltri_ref[...], selb, preferred_element_type=F32)
    incl = jnp.dot(selb, utri_ref[...], preferred_element_type=F32)
    rnk = carry[...] + incl - 1.0
    carry[...] = carry[...] + incl[:, tm - 1:tm]
    erow = lax.broadcasted_iota(I32, (ne, tm), 0).astype(F32)
    es, rs, ws = [], [], []
    for k in range(TOPK):
        pick = jnp.where(slot == float(k), sel, 0.0)
        es.append(jnp.sum(pick * erow, axis=0, keepdims=True))
        rs.append(jnp.sum(pick * rnk, axis=0, keepdims=True))
        ws.append(jnp.sum(pick * gate, axis=0, keepdims=True))
    eidx_ref[...] = jnp.concatenate(es, axis=0).astype(I32)
    rnk_ref[...] = jnp.concatenate(rs, axis=0).astype(I32)
    wts_ref[...] = jnp.concatenate(ws + [jnp.zeros((LANES - TOPK, tm), F32)], axis=0).T
    cnt_ref[...] = jnp.broadcast_to(carry[...], cnt_ref.shape).astype(I32)


def moe_route(x, router_w_pad, router_b_col):
    n = x.shape[0]
    tm = RT_TM
    ne = N_EXPERTS
    ltri = jnp.asarray(np.tril(np.ones((ne, ne), np.float32), -1), BF16)
    utri = jnp.asarray(np.triu(np.ones((tm, tm), np.float32)), BF16)
    const = lambda i: (0, 0)
    return pl.pallas_call(
        _router_kernel,
        out_shape=(jax.ShapeDtypeStruct((TOPK, n), I32), jax.ShapeDtypeStruct((TOPK, n), I32),
                   jax.ShapeDtypeStruct((n, LANES), F32), jax.ShapeDtypeStruct((ne, LANES), I32)),
        grid=(n // tm,),
        in_specs=[pl.BlockSpec((tm, D_MODEL), lambda i: (i, 0)),
                  pl.BlockSpec(router_w_pad.shape, const), pl.BlockSpec((ne, 1), const),
                  pl.BlockSpec((ne, ne), const), pl.BlockSpec((tm, tm), const)],
        out_specs=(pl.BlockSpec((TOPK, tm), lambda i: (0, i)), pl.BlockSpec((TOPK, tm), lambda i: (0, i)),
                   pl.BlockSpec((tm, LANES), lambda i: (i, 0)), pl.BlockSpec((ne, LANES), const)),
        scratch_shapes=[pltpu.VMEM((ne, 1), F32)],
        compiler_params=_cparams(("arbitrary",)),
        name="moe_route",
    )(x, router_w_pad, router_b_col, ltri, utri)


def _moe_rows(n_tokens):
    return n_tokens * TOPK + N_EXPERTS * MOE_TILE


def _plan_kernel(cnt_ref, off_ref, texp_ref, nused_ref):
    shift = MOE_TILE.bit_length() - 1

    def per_expert(e, carry):
        off, ti = carry
        off_ref[e] = off
        ntile = lax.shift_right_logical(cnt_ref[e] + (MOE_TILE - 1), shift)

        def mark(j, c):
            texp_ref[ti + j] = e
            return c

        lax.fori_loop(0, ntile, mark, 0)
        return off + ntile * MOE_TILE, ti + ntile

    _, used = lax.fori_loop(0, N_EXPERTS, per_expert, (jnp.int32(0), jnp.int32(0)))
    nused_ref[0] = used

    def fill(j, c):
        texp_ref[j] = N_EXPERTS - 1
        return c

    lax.fori_loop(used, texp_ref.shape[0], fill, 0)


def moe_plan(counts, n_tokens):
    nt = _moe_rows(n_tokens) // MOE_TILE
    smem = pl.BlockSpec(memory_space=pltpu.SMEM)
    return pl.pallas_call(
        _plan_kernel,
        out_shape=(jax.ShapeDtypeStruct((N_EXPERTS,), I32), jax.ShapeDtypeStruct((nt,), I32),
                   jax.ShapeDtypeStruct((1,), I32)),
        in_specs=[smem],
        out_specs=(smem, smem, smem),
        name="moe_plan",
    )(counts)


DSP_TB = 512


U32 = jnp.uint32
ROW_WORDS = D_MODEL // 2
ROW_CHUNKS = ROW_WORDS // LANES
assert ROW_CHUNKS == 4


def _pack_rows(y):
    half = y.shape[1] // 2
    bits = lambda v: lax.bitcast_convert_type(v.astype(jnp.bfloat16).astype(F32), U32)
    return bits(y[:, :half]) | lax.shift_right_logical(bits(y[:, half:]), jnp.uint32(16))


def _unpack_rows(words, dtype):
    hi = lax.bitcast_convert_type(words & jnp.uint32(0xFFFF0000), F32)
    lo = lax.bitcast_convert_type(lax.shift_left(words, jnp.uint32(16)), F32)
    return jnp.concatenate([hi, lo], axis=1).astype(dtype)


def _load_chunked(ref, rows):
    return jnp.concatenate([ref[pl.ds(s, rows, stride=ROW_CHUNKS), :] for s in range(ROW_CHUNKS)], axis=1)


def _store_chunked(ref, words):
    rows = words.shape[0]
    for s in range(ROW_CHUNKS):
        ref[pl.ds(s, rows, stride=ROW_CHUNKS), :] = words[:, s * LANES:(s + 1) * LANES]


def _dst_kernel(off_ref, eidx_ref, rnk_ref, dst_ref):
    e = eidx_ref[...]
    base = jnp.zeros(e.shape, I32)
    for x in range(N_EXPERTS):
        base = jnp.where(e == x, off_ref[x], base)
    dst_ref[...] = (base + rnk_ref[...]) * ROW_CHUNKS


def moe_dst(eidx, rnk, off):
    k, n = eidx.shape
    tb = 2048
    blk = pl.BlockSpec((k, tb), lambda i: (0, i))
    return pl.pallas_call(
        _dst_kernel,
        out_shape=jax.ShapeDtypeStruct((k, n), I32),
        grid=(n // tb,),
        in_specs=[pl.BlockSpec(memory_space=pltpu.SMEM), blk, blk],
        out_specs=blk,
        compiler_params=_cparams(("parallel",)),
        name="moe_dst",
    )(off, eidx, rnk)


def _tile_copy(src_ref, src_row, dst_ref, dst_row, sem):
    return pltpu.make_async_copy(src_ref.at[pl.ds(pl.multiple_of(src_row, ROW_CHUNKS), ROW_CHUNKS)],
                                 dst_ref.at[pl.ds(pl.multiple_of(dst_row, ROW_CHUNKS), ROW_CHUNKS)], sem)


def _dispatch_kernel(dst_ref, off_ref, cnt_ref, x_ref, xs_ref, zbuf, sem, zsem):
    tb = x_ref.shape[0] // ROW_CHUNKS
    zrows = MOE_TILE * ROW_CHUNKS

    def pad_copy(e):
        cnt = cnt_ref[e]
        rem = jnp.bitwise_and(cnt, MOE_TILE - 1)
        start = pl.multiple_of((off_ref[e] + cnt - rem) * ROW_CHUNKS, zrows)
        return rem != 0, pltpu.make_async_copy(zbuf, xs_ref.at[pl.ds(start, zrows)], zsem)

    @pl.when(pl.program_id(0) == 0)
    def _():
        zbuf[...] = jnp.zeros_like(zbuf)

        def start(e, c):
            has_pad, cp = pad_copy(e)

            @pl.when(has_pad)
            def _():
                cp.start()
            return c

        def wait(e, c):
            has_pad, cp = pad_copy(e)

            @pl.when(has_pad)
            def _():
                cp.wait()
            return c

        lax.fori_loop(0, N_EXPERTS, start, 0)
        lax.fori_loop(0, N_EXPERTS, wait, 0)

    def issue(t, c):
        for k in range(TOPK):
            _tile_copy(x_ref, t * ROW_CHUNKS, xs_ref, dst_ref[k, t], sem).start(priority=k % 2)
        return c

    def drain(t, c):
        for k in range(TOPK):
            _tile_copy(x_ref, 0, xs_ref, 0, sem).wait()
        return c

    lax.fori_loop(0, tb, issue, 0)
    lax.fori_loop(0, tb, drain, 0)


def moe_dispatch(xc, dst, off, counts):
    n = xc.shape[0] // ROW_CHUNKS
    tb = DSP_TB
    smem_all = pl.BlockSpec(memory_space=pltpu.SMEM)
    smem_blk = pl.BlockSpec((TOPK, tb), lambda i: (0, i), memory_space=pltpu.SMEM)
    return pl.pallas_call(
        _dispatch_kernel,
        out_shape=jax.ShapeDtypeStruct((_moe_rows(n) * ROW_CHUNKS, LANES), xc.dtype),
        grid=(n // tb,),
        in_specs=[smem_blk, smem_all, smem_all, pl.BlockSpec((tb * ROW_CHUNKS, LANES), lambda i: (i, 0))],
        out_specs=pl.BlockSpec(memory_space=pl.ANY),
        scratch_shapes=[pltpu.VMEM((MOE_TILE * ROW_CHUNKS, LANES), xc.dtype), pltpu.SemaphoreType.DMA(()),
                        pltpu.SemaphoreType.DMA(())],
        compiler_params=_cparams(("arbitrary",)),
        name="moe_dispatch",
    )(dst, off, counts, xc)


def _expert_kernel(texp_ref, nused_ref, xs_ref, w1_ref, w3_ref, w2_ref, ys_ref, w1b, w3b, w2b, last):
    i = pl.program_id(0)
    e = texp_ref[i]

    @pl.when(i == 0)
    def _():
        last[0] = -1

    @pl.when(e != last[0])
    def _():
        w1b[...] = w1_ref[0, 0].astype(BF16)
        w3b[...] = w3_ref[0, 0].astype(BF16)
        w2b[...] = w2_ref[0, 0].astype(BF16)
        last[0] = e

    @pl.when(i < nused_ref[0])
    def _():
        xb = _unpack_rows(_load_chunked(xs_ref, MOE_TILE), BF16)
        h1 = jnp.dot(xb, w1b[...], preferred_element_type=F32)
        h3 = jnp.dot(xb, w3b[...], preferred_element_type=F32)
        h = h1 * jax.nn.sigmoid(h1) * h3
        y = jnp.dot(h.astype(BF16), w2b[...], preferred_element_type=F32)
        _store_chunked(ys_ref, _pack_rows(y))

    @pl.when(i >= nused_ref[0])
    def _():
        ys_ref[...] = jnp.zeros_like(ys_ref)


def moe_experts(xs, texp, nused, w1, w3, w2, layer):
    crows = xs.shape[0]
    d, f = w1.shape[2], w1.shape[3]
    blk = MOE_TILE * ROW_CHUNKS
    nt = crows // blk
    grid_spec = pltpu.PrefetchScalarGridSpec(
        num_scalar_prefetch=2,
        grid=(nt,),
        in_specs=[pl.BlockSpec((blk, LANES), lambda i, te, nu: (jnp.where(i < nu[0], i, 0), 0)),
                  pl.BlockSpec((1, 1, d, f), lambda i, te, nu: (layer, te[i], 0, 0)),
                  pl.BlockSpec((1, 1, d, f), lambda i, te, nu: (layer, te[i], 0, 0)),
                  pl.BlockSpec((1, 1, f, d), lambda i, te, nu: (layer, te[i], 0, 0))],
        out_specs=pl.BlockSpec((blk, LANES), lambda i, te, nu: (i, 0)),
        scratch_shapes=[pltpu.VMEM((d, f), BF16), pltpu.VMEM((d, f), BF16), pltpu.VMEM((f, d), BF16),
                        pltpu.SMEM((1,), I32)],
    )
    return pl.pallas_call(
        _expert_kernel,
        out_shape=jax.ShapeDtypeStruct((crows, LANES), U32),
        grid_spec=grid_spec,
        compiler_params=_cparams(("arbitrary",), 48),
        name="moe_experts",
    )(texp, nused, xs, w1, w3, w2)


CMB_TB = 256


def _combine_kernel(dst_ref, x_ref, wts_ref, ws1_ref, ws3_ref, ws2_ref, lg_ref, lb_ref, ys_ref, o_ref, buf, sem):
    tb = x_ref.shape[0]

    def issue(t, c):
        for k in range(TOPK):
            _tile_copy(ys_ref, dst_ref[k, t], buf.at[k], t * ROW_CHUNKS, sem).start(priority=k % 2)
        return c

    def drain(t, c):
        for k in range(TOPK):
            _tile_copy(ys_ref, 0, buf.at[0], 0, sem).wait()
        return c

    lax.fori_loop(0, tb, issue, 0)
    x = x_ref[...]
    xb = x.astype(BF16)
    h1 = jnp.dot(xb, ws1_ref[...], preferred_element_type=F32)
    h3 = jnp.dot(xb, ws3_ref[...], preferred_element_type=F32)
    y = jnp.dot((h1 * jax.nn.sigmoid(h1) * h3).astype(BF16), ws2_ref[...], preferred_element_type=F32)
    lax.fori_loop(0, tb, drain, 0)
    w = wts_ref[...]
    routed = y
    for k in range(TOPK):
        routed = routed + w[:, k:k + 1] * _unpack_rows(_load_chunked(buf.at[k], tb), F32)
    o_ref[...] = _layer_norm(DN_ALPHA * x + routed, lg_ref[...], lb_ref[...])


def moe_combine(x, ys, dst, wts, ws1, ws3, ws2, ln_g, ln_b):
    n, d = x.shape
    tb = CMB_TB
    smem_blk = pl.BlockSpec((TOPK, tb), lambda i: (0, i), memory_space=pltpu.SMEM)
    const = lambda i: (0, 0)
    return pl.pallas_call(
        _combine_kernel,
        out_shape=jax.ShapeDtypeStruct((n, d), F32),
        grid=(n // tb,),
        in_specs=[smem_blk,
                  pl.BlockSpec((tb, d), lambda i: (i, 0)), pl.BlockSpec((tb, LANES), lambda i: (i, 0)),
                  pl.BlockSpec(ws1.shape, const), pl.BlockSpec(ws3.shape, const), pl.BlockSpec(ws2.shape, const),
                  pl.BlockSpec((1, d), const), pl.BlockSpec((1, d), const),
                  pl.BlockSpec(memory_space=pl.ANY)],
        out_specs=pl.BlockSpec((tb, d), lambda i: (i, 0)),
        scratch_shapes=[pltpu.VMEM((TOPK, tb * ROW_CHUNKS, LANES), U32), pltpu.SemaphoreType.DMA(())],
        compiler_params=_cparams(("parallel",), 48),
        name="moe_combine_ln2",
    )(dst, x, wts, ws1, ws3, ws2, ln_g, ln_b, ys)


def moe_block(x, xp, router_w, router_b, w1, w3, w2, layer, ws1, ws3, ws2, ln_g, ln_b):
    n, d = x.shape
    rw = jnp.pad(router_w, ((0, 0), (0, LANES - N_EXPERTS))).astype(BF16)
    eidx, rnk, wts, cnt = moe_route(x, rw, router_b.reshape(N_EXPERTS, 1))
    counts = cnt[:, 0]
    off, texp, nused = moe_plan(counts, n)
    dst = moe_dst(eidx, rnk, off)
    xs = moe_dispatch(xp, dst, off, counts)
    ys = moe_experts(xs, texp, nused, w1, w3, w2, layer)
    return moe_combine(x, ys, dst, wts, ws1.astype(BF16), ws3.astype(BF16), ws2.astype(BF16),
                       ln_g.reshape(1, -1), ln_b.reshape(1, -1))


def nsa_rope_tables(positions):
    batch, seq = positions.shape
    inv, sgn = _nsa_inv_freq()
    posf = positions.astype(F32)
    tok = rope_tables(posf.reshape(batch * seq, 1), inv, sgn, tm=512)
    end = posf[:, CMP_BLOCK - 1::CMP_STRIDE]
    npad = _n_cmp_pad(seq)
    end = jnp.pad(end, ((0, 0), (0, npad - end.shape[1])))
    cmp = rope_tables(end.reshape(batch * npad, 1), inv, sgn, tm=npad)
    return tok, cmp


def nsa_branch(z, tok_tab, cmp_tab, wk, wv, ovl, *, batch, seq):
    n = batch * seq
    w = NSA_GROUPS * NSA_DH
    xk = z[:, Z_NKV:Z_NKV + w].reshape(n // CMP_HALF, CMP_ROW)
    xv = z[:, Z_NKV + w:Z_NKV + 2 * w].reshape(n // CMP_HALF, CMP_ROW)
    kc, vct = nsa_compress(xk, xv, wk, wv, cmp_tab[0], cmp_tab[1], batch=batch)
    ksa, kw, vswt = nsa_kv_prep(z, tok_tab[0], tok_tab[1], batch=batch, seq=seq)
    return nsa_attention(z, tok_tab[0], tok_tab[1], kc, vct, ksa, kw, vswt, ovl, batch=batch, seq=seq)


def kernel(x, positions, w_in, conv_dw, conv_db, conv_ln_g, conv_ln_b, conv_w_pw, ret_w_o, nsa_pe_k, nsa_w1_k, nsa_w2_k,
           nsa_pe_v, nsa_w1_v, nsa_w2_v, nsa_w_o, w_out, ln1_g, ln1_b, router_w, router_b, moe_w1, moe_w3, moe_w2,
           shared_w1, shared_w3, shared_w2, ln2_g, ln2_b):
    batch, seq, d = x.shape
    n = batch * seq
    xf = x.reshape(n, d)
    row = lambda v: v.reshape(1, -1)

    ret_inv, ret_sgn = _ret_inv_freq()
    ret_tab = rope_tables(positions.astype(F32).reshape(n, 1), ret_inv, ret_sgn, tm=512)
    tok_tab, cmp_tab = nsa_rope_tables(positions)
    ret_consts = _ret_tables()
    ovl = _overlap_matrix(seq)

    for l in range(w_in.shape[0]):
        z = matmul(xf, _layout_w_in(w_in[l]), tm=2048, tn=512, out_dtype=BF16, name="in_proj")
        y_conv = conv_branch(z, conv_dw[l], row(conv_db[l]), row(conv_ln_g[l]), row(conv_ln_b[l]),
                             conv_w_pw[l].astype(BF16), batch=batch, seq=seq)
        o_ret = retention_branch(z, ret_tab[0], ret_tab[1], ret_consts, batch=batch, seq=seq)
        o_nsa = nsa_branch(z, tok_tab, cmp_tab, _cmp_weights(nsa_pe_k[l], nsa_w1_k[l], nsa_w2_k[l]),
                           _cmp_weights(nsa_pe_v[l], nsa_w1_v[l], nsa_w2_v[l]), ovl, batch=batch, seq=seq)
        x1, x1p = merge_block(xf, y_conv, o_ret, o_nsa, z, ret_w_o[l].astype(BF16), nsa_w_o[l].astype(BF16),
                              w_out[l].astype(BF16), row(ln1_g[l]), row(ln1_b[l]))
        xf = moe_block(x1, x1p, router_w[l], router_b[l], moe_w1, moe_w3, moe_w2, l,
                       shared_w1[l], shared_w3[l], shared_w2[l], ln2_g[l], ln2_b[l])
    return xf.reshape(batch, seq, d)
```

```python
import math

import jax
import jax.numpy as jnp
import numpy as np
from jax import lax
from jax.experimental import pallas as pl
from jax.experimental.pallas import tpu as pltpu

F32 = jnp.float32
BF16 = jnp.bfloat16
I32 = jnp.int32

D_MODEL = 1024
DEPTH = 4
CONV_CH = 512
CONV_WIDTH = 31
RET_HEADS = 4
RET_DK = 128
RET_DV = 256
RET_CHUNK = 128
RET_ROPE_BASE = 10000.0
NSA_HEADS = 8
NSA_GROUPS = 2
NSA_HPG = NSA_HEADS // NSA_GROUPS
NSA_DH = 64
NSA_N_BRANCH = 3
CMP_BLOCK = 32
CMP_STRIDE = 16
CMP_HIDDEN = 256
SEL_BLOCK = 64
SEL_TOPK = 16
WINDOW = 512
ROPE_THETA = 500000.0
ROT_DIM = NSA_DH // 4
N_EXPERTS = 64
N_EXPERT_GROUPS = 8
TOPK_GROUPS = 4
TOPK = 8
D_EXPERT = 256
D_SHARED = 256
ROUTED_SCALE = 2.5
DN_ALPHA = (2.0 * DEPTH) ** 0.25
LN_EPS = 1e-5
NEG = -1e30

LANES = 128

Z_MG = 0
Z_CA = 3072
Z_CB = 3584
Z_RQ = 4096
Z_RK = 4608
Z_RV = 5120
Z_RG = 6144
Z_NQ = 7168
Z_NKV = 7680
Z_NG = 8448
Z_W = 8704

_IN_WIDTHS = (512, 512, 512, 512, 1024, 1024, 512, 128, 128, 128, 128, 128, 128, 24, 3072)
_IN_OFFS = tuple(int(v) for v in np.concatenate([[0], np.cumsum(_IN_WIDTHS)[:-1]]))

MOE_TILE = 1024


def _cparams(sem, vmem_mb=None):
    kw = dict(dimension_semantics=sem)
    if vmem_mb is not None:
        kw["vmem_limit_bytes"] = vmem_mb * 1024 * 1024
    return pltpu.CompilerParams(**kw)


def _mm_kernel(x_ref, w_ref, o_ref):
    o_ref[...] = jnp.dot(x_ref[...].astype(BF16), w_ref[...], preferred_element_type=F32).astype(o_ref.dtype)


def matmul(x, w, *, tm, tn, out_dtype, name):
    m, k = x.shape
    n = w.shape[1]
    return pl.pallas_call(
        _mm_kernel,
        out_shape=jax.ShapeDtypeStruct((m, n), out_dtype),
        grid=(m // tm, n // tn),
        in_specs=[pl.BlockSpec((tm, k), lambda i, j: (i, 0)),
                  pl.BlockSpec((k, tn), lambda i, j: (0, j))],
        out_specs=pl.BlockSpec((tm, tn), lambda i, j: (i, j)),
        compiler_params=_cparams(("parallel", "arbitrary"), 48),
        name=name,
    )(x, w)


def _rope_table_kernel(pos_ref, inv_ref, sgn_ref, cos_ref, sin_ref):
    ang = pos_ref[...] * inv_ref[...]
    cos_ref[...] = jnp.cos(ang)
    sin_ref[...] = jnp.sin(ang) * sgn_ref[...]


def rope_tables(pos_col, inv, sgn, *, tm):
    n = pos_col.shape[0]
    w = inv.shape[1]
    return pl.pallas_call(
        _rope_table_kernel,
        out_shape=(jax.ShapeDtypeStruct((n, w), F32), jax.ShapeDtypeStruct((n, w), F32)),
        grid=(n // tm,),
        in_specs=[pl.BlockSpec((tm, 1), lambda i: (i, 0)),
                  pl.BlockSpec((1, w), lambda i: (0, 0)),
                  pl.BlockSpec((1, w), lambda i: (0, 0))],
        out_specs=(pl.BlockSpec((tm, w), lambda i: (i, 0)), pl.BlockSpec((tm, w), lambda i: (i, 0))),
        compiler_params=_cparams(("parallel",)),
        name="rope_tables",
    )(pos_col, inv, sgn)


def _ret_inv_freq():
    inv = 1.0 / jnp.power(jnp.float32(RET_ROPE_BASE), jnp.linspace(0.0, 1.0, RET_DK // 2, dtype=F32))
    inv = jnp.concatenate([inv, inv])[None, :]
    sgn = jnp.concatenate([-jnp.ones((RET_DK // 2,), F32), jnp.ones((RET_DK // 2,), F32)])[None, :]
    return inv, sgn


def _nsa_inv_freq():
    half = ROT_DIM // 2
    inv = jnp.power(jnp.float32(ROPE_THETA), -jnp.arange(0, ROT_DIM, 2, dtype=F32) / ROT_DIM)
    z = jnp.zeros((NSA_DH - ROT_DIM,), F32)
    inv64 = jnp.concatenate([inv, inv, z])
    sgn64 = jnp.concatenate([-jnp.ones((half,), F32), jnp.ones((half,), F32), z])
    return jnp.concatenate([inv64, inv64])[None, :], jnp.concatenate([sgn64, sgn64])[None, :]


def _nsa_rope(x, cos, sin):
    w = x.shape[1]
    half = ROT_DIM // 2
    lane = lax.broadcasted_iota(I32, x.shape, 1) % NSA_DH
    partner = jnp.where(lane < half, pltpu.roll(x, w - half, 1), pltpu.roll(x, half, 1))
    return x * cos + partner * sin


CONV_TT = 256
CONV_HALO = 32


def _conv_kernel(a_ref, b_ref, ah_ref, bh_ref, dw_ref, db_ref, g_ref, be_ref, wpw_ref, o_ref, ubuf, sbuf):
    i = pl.program_id(1)
    tt = a_ref.shape[0]
    u = a_ref[...].astype(F32) * jax.nn.sigmoid(b_ref[...].astype(F32))
    uh = ah_ref[...].astype(F32) * jax.nn.sigmoid(bh_ref[...].astype(F32))
    ubuf[0:CONV_HALO, :] = jnp.where(i > 0, uh, 0.0)
    ubuf[CONV_HALO:CONV_HALO + tt, :] = u
    acc = jnp.zeros((tt, CONV_CH), F32)
    base = CONV_HALO - (CONV_WIDTH - 1)
    sub = 8
    for r in range(sub):
        offs = [o for o in range(base, base + CONV_WIDTH) if o % sub == r]
        if not offs:
            continue
        span = offs[-1] - r + tt
        sbuf[0:span, :] = ubuf[r:r + span, :]
        for o in offs:
            acc = acc + dw_ref[o - base:o - base + 1, :] * sbuf[o - r:o - r + tt, :]
    acc = acc + db_ref[...]
    mu = jnp.mean(acc, axis=-1, keepdims=True)
    var = jnp.mean(jnp.square(acc - mu), axis=-1, keepdims=True)
    y = (acc - mu) * lax.rsqrt(var + LN_EPS) * g_ref[...] + be_ref[...]
    y = y * jax.nn.sigmoid(y)
    o_ref[...] = jnp.dot(y.astype(BF16), wpw_ref[...], preferred_element_type=F32)


def conv_branch(z, dw, db, ln_g, ln_b, w_pw_bf, *, batch, seq):
    tt = CONV_TT
    nt = seq // tt
    r = tt // CONV_HALO
    ca, cb = Z_CA // CONV_CH, Z_CB // CONV_CH

    def halo_map(col):
        return lambda b, i: (jnp.maximum((b * nt + i) * r - 1, 0), col)

    return pl.pallas_call(
        _conv_kernel,
        out_shape=jax.ShapeDtypeStruct((batch * seq, D_MODEL), F32),
        grid=(batch, nt),
        in_specs=[pl.BlockSpec((tt, CONV_CH), lambda b, i: (b * nt + i, ca)),
                  pl.BlockSpec((tt, CONV_CH), lambda b, i: (b * nt + i, cb)),
                  pl.BlockSpec((CONV_HALO, CONV_CH), halo_map(ca)),
                  pl.BlockSpec((CONV_HALO, CONV_CH), halo_map(cb)),
                  pl.BlockSpec((CONV_WIDTH, CONV_CH), lambda b, i: (0, 0)),
                  pl.BlockSpec((1, CONV_CH), lambda b, i: (0, 0)),
                  pl.BlockSpec((1, CONV_CH), lambda b, i: (0, 0)),
                  pl.BlockSpec((1, CONV_CH), lambda b, i: (0, 0)),
                  pl.BlockSpec((CONV_CH, D_MODEL), lambda b, i: (0, 0))],
        out_specs=pl.BlockSpec((tt, D_MODEL), lambda b, i: (b * nt + i, 0)),
        scratch_shapes=[pltpu.VMEM((CONV_HALO + tt, CONV_CH), F32), pltpu.VMEM((CONV_HALO + tt, CONV_CH), F32)],
        compiler_params=_cparams(("parallel", "parallel")),
        name="conv_branch",
    )(z, z, z, z, dw, db, ln_g, ln_b, w_pw_bf)


RET_TQ = 512


def _ret_tables():
    h, c = RET_HEADS, RET_CHUNK
    log_gamma = jnp.log1p(-jnp.exp2(-5.0 - jnp.arange(h, dtype=F32)))
    idx = jnp.arange(c, dtype=F32)
    diff = idx[:, None] - idx[None, :]
    dmat = jnp.where(diff >= 0, jnp.exp(log_gamma[:, None, None] * jnp.maximum(diff, 0.0)), 0.0).astype(F32)
    xi = jnp.exp(log_gamma[:, None] * (idx + 1.0)).astype(F32)
    zeta = jnp.exp(log_gamma[:, None] * (c - 1.0 - idx)).astype(F32)
    decay = jnp.exp(log_gamma * c).astype(F32)
    xi_b = jnp.broadcast_to(xi[:, :, None], (h, c, RET_DV))
    zeta_b = jnp.broadcast_to(zeta[:, :, None], (h, c, RET_DV))
    decay_b = jnp.broadcast_to(decay[:, None, None], (h, RET_DK, RET_DV))
    return dmat, xi_b, zeta_b, decay_b


def _ret_kernel(q_ref, k_ref, v_ref, g_ref, cos_ref, sin_ref, dmat_ref, xi_ref, zeta_ref, dec_ref, o_ref, r_ref):
    @pl.when(pl.program_id(1) == 0)
    def _():
        r_ref[...] = jnp.zeros_like(r_ref)

    c = RET_CHUNK
    n_chunks = q_ref.shape[0] // c
    for ci in range(n_chunks):
        rows = slice(ci * c, (ci + 1) * c)
        cos = cos_ref[rows, :]
        sin = sin_ref[rows, :]
        for h in range(RET_HEADS):
            qk_cols = slice(h * RET_DK, (h + 1) * RET_DK)
            v_cols = slice(h * RET_DV, (h + 1) * RET_DV)
            q = q_ref[rows, qk_cols].astype(F32)
            k = k_ref[rows, qk_cols].astype(F32)
            q = q * cos + pltpu.roll(q, RET_DK // 2, 1) * sin
            k = (k * cos + pltpu.roll(k, RET_DK // 2, 1) * sin) * (RET_DK ** -0.5)
            v = v_ref[rows, v_cols].astype(F32)
            qb = q.astype(BF16)
            kb = k.astype(BF16)
            inner = lax.dot_general(qb, kb, (((1,), (1,)), ((), ())), preferred_element_type=F32) * dmat_ref[h]
            r_old = r_ref[h]
            o = (jnp.dot(inner.astype(BF16), v.astype(BF16), preferred_element_type=F32)
                 + jnp.dot(qb, r_old.astype(BF16), preferred_element_type=F32) * xi_ref[h])
            vz = (v * zeta_ref[h]).astype(BF16)
            r_ref[h] = r_old * dec_ref[h] + jnp.dot(k.T.astype(BF16), vz, preferred_element_type=F32)
            mu = jnp.mean(o, axis=-1, keepdims=True)
            var = jnp.mean(jnp.square(o - mu), axis=-1, keepdims=True)
            on = (o - mu) * lax.rsqrt(var + LN_EPS)
            g = g_ref[rows, v_cols].astype(F32)
            o_ref[rows, v_cols] = g * jax.nn.sigmoid(g) * on


def retention_branch(z, ret_cos, ret_sin, tables, *, batch, seq):
    tq = RET_TQ
    nt = seq // tq
    dmat, xi_b, zeta_b, decay_b = tables
    qw = RET_HEADS * RET_DK
    vw = RET_HEADS * RET_DV
    row = lambda b, i: b * nt + i
    full3 = lambda b, i: (0, 0, 0)
    return pl.pallas_call(
        _ret_kernel,
        out_shape=jax.ShapeDtypeStruct((batch * seq, vw), F32),
        grid=(batch, nt),
        in_specs=[pl.BlockSpec((tq, qw), lambda b, i: (row(b, i), Z_RQ // qw)),
                  pl.BlockSpec((tq, qw), lambda b, i: (row(b, i), Z_RK // qw)),
                  pl.BlockSpec((tq, vw), lambda b, i: (row(b, i), Z_RV // vw)),
                  pl.BlockSpec((tq, vw), lambda b, i: (row(b, i), Z_RG // vw)),
                  pl.BlockSpec((tq, RET_DK), lambda b, i: (row(b, i), 0)),
                  pl.BlockSpec((tq, RET_DK), lambda b, i: (row(b, i), 0)),
                  pl.BlockSpec(dmat.shape, full3),
                  pl.BlockSpec(xi_b.shape, full3),
                  pl.BlockSpec(zeta_b.shape, full3),
                  pl.BlockSpec(decay_b.shape, full3)],
        out_specs=pl.BlockSpec((tq, vw), lambda b, i: (row(b, i), 0)),
        scratch_shapes=[pltpu.VMEM((RET_HEADS, RET_DK, RET_DV), F32)],
        compiler_params=_cparams(("parallel", "arbitrary"), 48),
        name="retention",
    )(z, z, z, z, ret_cos, ret_sin, dmat, xi_b, zeta_b, decay_b)


def _layout_w_in(w):
    seg = lambda i: w[:, _IN_OFFS[i]:_IN_OFFS[i] + _IN_WIDTHS[i]]
    order = (14, 0, 1, 2, 3, 4, 5, 6, 7, 8, 9, 10, 11, 12, 13)
    parts = [seg(i) for i in order]
    used = sum(_IN_WIDTHS)
    parts.append(jnp.zeros((w.shape[0], Z_W - used), w.dtype))
    return jnp.concatenate(parts, axis=1).astype(BF16)


CMP_HALF = CMP_BLOCK // 2
CMP_ROW = CMP_HALF * NSA_GROUPS * NSA_DH


def _n_cmp_pad(seq):
    return seq // CMP_STRIDE


def _cmp_weights(pe, w1, w2):
    g = NSA_GROUPS
    eye = jnp.eye(g, dtype=F32)
    w = w1.reshape(2, CMP_HALF, NSA_DH, CMP_HIDDEN)
    w1ab = jnp.einsum("hldf,gk->hlgdkf", w, eye).reshape(2, CMP_ROW, g * CMP_HIDDEN).astype(BF16)
    peab = jnp.broadcast_to(pe.reshape(2, CMP_HALF, 1, NSA_DH), (2, CMP_HALF, g, NSA_DH)).reshape(2, 1, CMP_ROW)
    w2bd = jnp.einsum("fd,gk->gfkd", w2, eye).reshape(g * CMP_HIDDEN, g * NSA_DH).astype(BF16)
    return peab, w1ab, w2bd


def _cmp_mlp(x, pe_ref, w1_ref, w2_ref):
    x = x.astype(F32)
    a = jnp.dot((x + pe_ref[0]).astype(BF16), w1_ref[0], preferred_element_type=F32)
    b = jnp.dot((x + pe_ref[1]).astype(BF16), w1_ref[1], preferred_element_type=F32)
    hid = a + pltpu.roll(b, b.shape[0] - 1, 0)
    hid = hid * jax.nn.sigmoid(hid)
    return jnp.dot(hid.astype(BF16), w2_ref[...], preferred_element_type=F32)


def _compress_kernel(xk_ref, xv_ref, pek_ref, w1k_ref, w2k_ref, pev_ref, w1v_ref, w2v_ref, cos_ref, sin_ref,
                     k_ref, vt_ref):
    k = _cmp_mlp(xk_ref[...], pek_ref, w1k_ref, w2k_ref)
    k_ref[0] = _nsa_rope(k, cos_ref[...], sin_ref[...]).astype(BF16)
    vt_ref[0] = _cmp_mlp(xv_ref[...], pev_ref, w1v_ref, w2v_ref).T.astype(BF16)


def nsa_compress(xk, xv, wk, wv, cmp_cos, cmp_sin, *, batch):
    pek, w1k, w2k = wk
    pev, w1v, w2v = wv
    npad = xk.shape[0] // batch
    gw = NSA_GROUPS * NSA_DH
    c3 = lambda b: (0, 0, 0)
    c2 = lambda b: (0, 0)
    return pl.pallas_call(
        _compress_kernel,
        out_shape=(jax.ShapeDtypeStruct((batch, npad, gw), BF16), jax.ShapeDtypeStruct((batch, gw, npad), BF16)),
        grid=(batch,),
        in_specs=[pl.BlockSpec((npad, CMP_ROW), lambda b: (b, 0)),
                  pl.BlockSpec((npad, CMP_ROW), lambda b: (b, 0)),
                  pl.BlockSpec(pek.shape, c3), pl.BlockSpec(w1k.shape, c3), pl.BlockSpec(w2k.shape, c2),
                  pl.BlockSpec(pev.shape, c3), pl.BlockSpec(w1v.shape, c3), pl.BlockSpec(w2v.shape, c2),
                  pl.BlockSpec((npad, gw), lambda b: (b, 0)),
                  pl.BlockSpec((npad, gw), lambda b: (b, 0))],
        out_specs=(pl.BlockSpec((1, npad, gw), lambda b: (b, 0, 0)), pl.BlockSpec((1, gw, npad), lambda b: (b, 0, 0))),
        compiler_params=_cparams(("parallel",), 48),
        name="nsa_compress",
    )(xk, xv, pek, w1k, w2k, pev, w1v, w2v, cmp_cos, cmp_sin)


KV_TT = 512
ATT_TK = 512
ATT_KB = 256
ATT_AHEAD = 8
LOG2E = math.log2(math.e)


def _kv_prep_kernel(z_ref, cos_ref, sin_ref, ksa_ref, kw_ref, vswt_ref):
    tt = z_ref.shape[0]
    cos = cos_ref[...]
    sin = sin_ref[...]
    w = NSA_GROUPS * NSA_DH
    n_sel = ksa_ref.shape[3] - NSA_DH
    ks = _nsa_rope(z_ref[:, 2 * w:3 * w].astype(F32), cos, sin)
    vs = z_ref[:, 3 * w:4 * w].astype(F32)
    kw = _nsa_rope(z_ref[:, 4 * w:5 * w].astype(F32), cos, sin)
    vw = z_ref[:, 5 * w:6 * w].astype(F32)
    blk = (pl.program_id(1) * tt + lax.broadcasted_iota(I32, (tt, n_sel), 0)) // SEL_BLOCK
    onehot = jnp.where(blk == lax.broadcasted_iota(I32, (tt, n_sel), 1), 1.0, 0.0)
    for g in range(NSA_GROUPS):
        cols = slice(g * NSA_DH, (g + 1) * NSA_DH)
        ksa_ref[0, g] = jnp.concatenate([ks[:, cols], onehot], axis=1).astype(BF16)
        kw_ref[0, g] = kw[:, cols].astype(BF16)
        vswt_ref[0, g] = jnp.concatenate([vs[:, cols], vw[:, cols]], axis=1).T.astype(BF16)


def nsa_kv_prep(z, nsa_cos, nsa_sin, *, batch, seq):
    tt = KV_TT
    nt = seq // tt
    g = NSA_GROUPS
    w = g * NSA_DH
    n_sel = seq // SEL_BLOCK
    return pl.pallas_call(
        _kv_prep_kernel,
        out_shape=(jax.ShapeDtypeStruct((batch, g, seq, NSA_DH + n_sel), BF16),
                   jax.ShapeDtypeStruct((batch, g, seq, NSA_DH), BF16),
                   jax.ShapeDtypeStruct((batch, g, w, seq), BF16)),
        grid=(batch, nt),
        in_specs=[pl.BlockSpec((tt, 6 * w), lambda b, i: (b * nt + i, Z_NKV // (6 * w))),
                  pl.BlockSpec((tt, w), lambda b, i: (b * nt + i, 0)),
                  pl.BlockSpec((tt, w), lambda b, i: (b * nt + i, 0))],
        out_specs=(pl.BlockSpec((1, g, tt, NSA_DH + n_sel), lambda b, i: (b, 0, i, 0)),
                   pl.BlockSpec((1, g, tt, NSA_DH), lambda b, i: (b, 0, i, 0)),
                   pl.BlockSpec((1, g, w, tt), lambda b, i: (b, 0, 0, i))),
        compiler_params=_cparams(("parallel", "parallel")),
        name="nsa_kv_prep",
    )(z, nsa_cos, nsa_sin)


ATT_TQ = 256


def _overlap_matrix(seq):
    n_cmp = (seq - CMP_BLOCK) // CMP_STRIDE + 1
    n_sel = seq // SEL_BLOCK
    ii = np.arange(_n_cmp_pad(seq))[None, :]
    jj = np.arange(LANES)[:, None]
    lo = np.maximum(ii * CMP_STRIDE, jj * SEL_BLOCK)
    hi = np.minimum(ii * CMP_STRIDE + CMP_BLOCK, (jj + 1) * SEL_BLOCK)
    ov = np.maximum(hi - lo, 0).astype(np.float32) / CMP_BLOCK
    ov = np.where((ii < n_cmp) & (jj < n_sel), ov, 0.0)
    return jnp.asarray(np.tile(ov, (1, NSA_HPG)), BF16)


def _softmax_step(s, ok, vt, cols, m_ref, l_ref, a_ref):
    m_old = m_ref[:, cols]
    m_new = jnp.maximum(m_old, jnp.max(s, axis=0, keepdims=True))
    alpha = jnp.exp2(m_old - m_new)
    p = jnp.exp2(s - m_new)
    if ok is not None:
        p = jnp.where(ok, p, 0.0)
    l_ref[:, cols] = alpha * l_ref[:, cols] + jnp.sum(p, axis=0, keepdims=True)
    a_ref[:, cols] = alpha * a_ref[:, cols] + jnp.dot(vt, p.astype(BF16), preferred_element_type=F32)
    m_ref[:, cols] = m_new


def _att_kernel(q_ref, cos_ref, sin_ref, gl_ref, kc_ref, vct_ref, ksa_ref, kw_ref, vswt_ref, ovl_ref, o_ref,
                qa_s, m_sel, l_sel, a_sel, m_win, l_win, a_win):
    g = pl.program_id(1)
    qi = pl.program_id(2)
    tq = ATT_TQ
    tk = ATT_TK
    hpg = NSA_HPG
    dh = NSA_DH
    t0 = qi * tq
    seq = ksa_ref.shape[2]
    n_sel = ksa_ref.shape[3] - dh
    heads = [slice(h * tq, (h + 1) * tq) for h in range(hpg)]

    cos = jnp.concatenate([cos_ref[...]] * (hpg // 2), axis=1)
    sin = jnp.concatenate([sin_ref[...]] * (hpg // 2), axis=1)
    qt = (_nsa_rope(q_ref[...].astype(F32), cos, sin) * (dh ** -0.5 * LOG2E)).T
    q_t = jnp.concatenate([qt[h * dh:(h + 1) * dh] for h in range(hpg)], axis=1)
    q_tb = q_t.astype(BF16)
    qa_s[0:dh, :] = q_tb

    zero = jnp.zeros_like(q_tb)
    q2 = jnp.where(g == 0, jnp.concatenate([q_tb, zero], axis=0), jnp.concatenate([zero, q_tb], axis=0))
    s = jnp.dot(kc_ref[0], q2, preferred_element_type=F32)
    n_cmp = (seq - CMP_BLOCK) // CMP_STRIDE + 1
    nrow = lax.broadcasted_iota(I32, s.shape, 0)
    tcol = t0 + lax.broadcasted_iota(I32, s.shape, 1) % tq
    ok = (nrow * CMP_STRIDE + (CMP_BLOCK - 1) <= tcol) & (nrow < n_cmp)
    sm = jnp.where(ok, s, NEG)
    e = jnp.where(ok, jnp.exp2(sm - jnp.max(sm, axis=0, keepdims=True)), 0.0)
    den = jnp.sum(e, axis=0, keepdims=True)
    pb = (e / jnp.where(den > 0.0, den, 1.0)).astype(BF16)
    oc = jnp.dot(vct_ref[0], pb, preferred_element_type=F32)
    o_cmp = jnp.where(g == 0, oc[0:dh], oc[dh:2 * dh])
    pcat = jnp.concatenate([pb[:, hs] for hs in heads], axis=0)
    imp_t = jnp.dot(ovl_ref[...], pcat, preferred_element_type=F32)[0:n_sel]

    jrow = lax.broadcasted_iota(I32, (n_sel, tq), 0)
    cur = (t0 + lax.broadcasted_iota(I32, (n_sel, tq), 1)) // SEL_BLOCK
    forced = (jrow == 0) | (jrow == cur) | (jrow == cur - 1)
    imp_t = jnp.where(forced, jnp.inf, jnp.where(jrow <= cur, imp_t, -jnp.inf))
    sub = 8
    groups = [imp_t[r:r + sub] for r in range(0, n_sel, sub)]
    ranks = [jnp.zeros((sub, tq), F32) for _ in groups]
    srow = lax.broadcasted_iota(I32, (sub, tq), 0)
    for i in range(n_sel):
        gi, si = divmod(i, sub)
        ri = groups[gi][si:si + 1, :]
        for gj, v in enumerate(groups):
            ge = jnp.where(ri >= v, 1.0, 0.0)
            if gj > gi:
                ranks[gj] = ranks[gj] + ge
            else:
                gt = jnp.where(ri > v, 1.0, 0.0)
                ranks[gj] = ranks[gj] + (gt if gj < gi else jnp.where(srow > si, ge, gt))
    rank = jnp.concatenate(ranks, axis=0)
    bias_t = jnp.where(rank < float(SEL_TOPK), 0.0, NEG).astype(BF16)
    qa_s[dh:dh + n_sel, :] = jnp.concatenate([bias_t] * hpg, axis=1)

    m_sel[...] = jnp.full(m_sel.shape, NEG, F32)
    l_sel[...] = jnp.zeros(l_sel.shape, F32)
    a_sel[...] = jnp.zeros(a_sel.shape, F32)

    kb = ATT_KB
    assert tk == 2 * tq and tq % kb == 0
    krow = lax.broadcasted_iota(I32, (kb, tq), 0)
    qtime = t0 + lax.broadcasted_iota(I32, (kb, tq), 1)

    def run_blocks(blocks):
        sc = [blk[0]() for blk in blocks[:ATT_AHEAD]]
        for b, (_, mask, vt, hs, refs) in enumerate(blocks):
            if b + ATT_AHEAD < len(blocks):
                sc.append(blocks[b + ATT_AHEAD][0]())
            ok = mask()
            s_b = sc[b] if ok is None else jnp.where(ok, sc[b], NEG)
            sc[b] = None
            _softmax_step(s_b, ok, vt(), hs, *refs)

    def sel_blocks(k0, n_keys, causal):
        def block(kk, hs):
            return (lambda: jnp.dot(ksa_ref[0, 0, pl.ds(k0 + kk, kb), :], qa_s[:, hs], preferred_element_type=F32),
                    (lambda: (k0 + kk + krow) <= qtime) if causal else (lambda: None),
                    lambda: vswt_ref[0, 0, 0:dh, pl.ds(k0 + kk, kb)], hs, (m_sel, l_sel, a_sel))
        return [block(kk, hs) for kk in range(0, n_keys, kb) for hs in heads]

    def sel_body(j, carry):
        run_blocks(sel_blocks(pl.multiple_of(j * tk, tk), tk, False))
        return carry

    n_full = t0 // tk
    lax.fori_loop(0, n_full, sel_body, 0)

    @pl.when(n_full * tk < t0)
    def _():
        run_blocks(sel_blocks(pl.multiple_of(n_full * tk, tk), tk - tq, False))

    kw0 = pl.multiple_of(jnp.maximum(t0 - WINDOW, 0), tq)

    def win_block(kk, hs):
        def mask():
            dist = qtime - (kw0 + kk + krow)
            return (dist >= 0) & (dist < WINDOW)
        return (lambda: jnp.dot(kw_ref[0, 0, pl.ds(kw0 + kk, kb), :], qa_s[0:dh, hs], preferred_element_type=F32),
                mask, lambda: vswt_ref[0, 0, dh:2 * dh, pl.ds(kw0 + kk, kb)], hs, (m_win, l_win, a_win))

    m_win[...] = jnp.full(m_win.shape, NEG, F32)
    l_win[...] = jnp.zeros(l_win.shape, F32)
    a_win[...] = jnp.zeros(a_win.shape, F32)
    win = [win_block(kk, hs) for kk in range(0, WINDOW + tq, kb) for hs in heads]
    diag = sel_blocks(pl.multiple_of(t0, tq), tq, True)
    per = len(win) // len(diag)
    mixed = []
    for i, blk in enumerate(diag):
        mixed += win[i * per:(i + 1) * per] + [blk]
    run_blocks(mixed + win[len(diag) * per:])

    o_win = a_win[...] / l_win[...]
    o_slc = a_sel[...] / l_sel[...]
    gl_t = jax.nn.sigmoid(gl_ref[...].astype(F32)).T
    nb = NSA_N_BRANCH
    outs = []
    for h, hs in enumerate(heads):
        gate = lambda br: jnp.where(g == 0, gl_t[nb * h + br:nb * h + br + 1],
                                    gl_t[nb * (hpg + h) + br:nb * (hpg + h) + br + 1])
        outs.append(gate(0) * o_cmp[:, hs] + gate(1) * o_slc[:, hs] + gate(2) * o_win[:, hs])
    o_ref[...] = jnp.concatenate(outs, axis=0).T


def nsa_attention(z, nsa_cos, nsa_sin, kc, vct, ksa, kw, vswt, ovl, *, batch, seq):
    tq = ATT_TQ
    nt = seq // tq
    g = NSA_GROUPS
    gw = g * NSA_DH
    qw = NSA_HPG * NSA_DH
    m_cols = NSA_HPG * tq
    row = lambda b, gg, i: b * nt + i
    per_bg = lambda a: pl.BlockSpec((1, 1) + a.shape[2:], lambda b, gg, i: (b, gg, 0, 0))
    stat = pltpu.VMEM((1, m_cols), F32)
    vals = pltpu.VMEM((NSA_DH, m_cols), F32)
    return pl.pallas_call(
        _att_kernel,
        out_shape=jax.ShapeDtypeStruct((batch * seq, g * qw), F32),
        grid=(batch, g, nt),
        in_specs=[pl.BlockSpec((tq, qw), lambda b, gg, i: (row(b, gg, i), Z_NQ // qw + gg)),
                  pl.BlockSpec((tq, gw), lambda b, gg, i: (row(b, gg, i), 0)),
                  pl.BlockSpec((tq, gw), lambda b, gg, i: (row(b, gg, i), 0)),
                  pl.BlockSpec((tq, LANES), lambda b, gg, i: (row(b, gg, i), Z_NG // LANES)),
                  pl.BlockSpec((1,) + kc.shape[1:], lambda b, gg, i: (b, 0, 0)),
                  pl.BlockSpec((1,) + vct.shape[1:], lambda b, gg, i: (b, 0, 0)),
                  per_bg(ksa), per_bg(kw), per_bg(vswt),
                  pl.BlockSpec(ovl.shape, lambda b, gg, i: (0, 0))],
        out_specs=pl.BlockSpec((tq, qw), lambda b, gg, i: (row(b, gg, i), gg)),
        scratch_shapes=[pltpu.VMEM((ksa.shape[3], m_cols), BF16), stat, stat, vals, stat, stat, vals],
        compiler_params=_cparams(("parallel", "parallel", "parallel"), 48),
        name="nsa_attention",
    )(z, nsa_cos, nsa_sin, z, kc, vct, ksa, kw, vswt, ovl)


MERGE_TM = 256


def _layer_norm(y, g, b):
    mu = jnp.mean(y, axis=-1, keepdims=True)
    var = jnp.mean(jnp.square(y - mu), axis=-1, keepdims=True)
    return (y - mu) * lax.rsqrt(var + LN_EPS) * g + b


def _merge_kernel(x_ref, yc_ref, or_ref, on_ref, g0_ref, g1_ref, g2_ref, wr_ref, wn_ref, wo_ref, lg_ref, lb_ref, o_ref,
                  op_ref):
    y_ret = jnp.dot(or_ref[...].astype(BF16), wr_ref[...], preferred_element_type=F32)
    y_nsa = jnp.dot(on_ref[...].astype(BF16), wn_ref[...], preferred_element_type=F32)
    gate = lambda ref: jax.nn.sigmoid(ref[...].astype(F32))
    m = gate(g0_ref) * yc_ref[...] + gate(g1_ref) * y_ret + gate(g2_ref) * y_nsa
    h = jnp.dot(m.astype(BF16), wo_ref[...], preferred_element_type=F32)
    y = _layer_norm(DN_ALPHA * x_ref[...] + h, lg_ref[...], lb_ref[...])
    o_ref[...] = y
    _store_chunked(op_ref, _pack_rows(y))


def merge_block(x, y_conv, o_ret, o_nsa, z, ret_w_o, nsa_w_o, w_out, ln_g, ln_b):
    n = x.shape[0]
    tm = MERGE_TM
    d = D_MODEL
    rowd = lambda i: (i, 0)
    const = lambda i: (0, 0)
    return pl.pallas_call(
        _merge_kernel,
        out_shape=(jax.ShapeDtypeStruct((n, d), F32), jax.ShapeDtypeStruct((n * ROW_CHUNKS, LANES), U32)),
        grid=(n // tm,),
        in_specs=[pl.BlockSpec((tm, d), rowd), pl.BlockSpec((tm, d), rowd),
                  pl.BlockSpec((tm, o_ret.shape[1]), rowd), pl.BlockSpec((tm, o_nsa.shape[1]), rowd),
                  pl.BlockSpec((tm, d), lambda i: (i, Z_MG // d)),
                  pl.BlockSpec((tm, d), lambda i: (i, Z_MG // d + 1)),
                  pl.BlockSpec((tm, d), lambda i: (i, Z_MG // d + 2)),
                  pl.BlockSpec(ret_w_o.shape, const), pl.BlockSpec(nsa_w_o.shape, const), pl.BlockSpec(w_out.shape, const),
                  pl.BlockSpec((1, d), const), pl.BlockSpec((1, d), const)],
        out_specs=(pl.BlockSpec((tm, d), rowd), pl.BlockSpec((tm * ROW_CHUNKS, LANES), rowd)),
        compiler_params=_cparams(("parallel",), 48),
        name="merge_ln1",
    )(x, y_conv, o_ret, o_nsa, z, z, z, ret_w_o, nsa_w_o, w_out, ln_g, ln_b)


RT_TM = 256


def _stable_rank(v):
    n = v.shape[0]
    row = lax.broadcasted_iota(I32, v.shape, 0)
    rank = jnp.zeros(v.shape, F32)
    for i in range(n):
        r = v[i:i + 1, :]
        rank = rank + jnp.where(row > i, jnp.where(r >= v, 1.0, 0.0), jnp.where(r > v, 1.0, 0.0))
    return rank


def _router_kernel(x_ref, wr_ref, b_ref, ltri_ref, utri_ref, eidx_ref, rnk_ref, wts_ref, cnt_ref, carry):
    @pl.when(pl.program_id(0) == 0)
    def _():
        carry[...] = jnp.zeros_like(carry)

    tm = x_ref.shape[0]
    ne = N_EXPERTS
    per = ne // N_EXPERT_GROUPS
    logits = jnp.dot(x_ref[...].astype(BF16), wr_ref[...], preferred_element_type=F32)
    s = jax.nn.sigmoid(logits.T[0:ne])
    sb = s + b_ref[...]
    sub = lax.broadcasted_iota(I32, (per, tm), 0)
    gscore = []
    for gi in range(N_EXPERT_GROUPS):
        v = sb[gi * per:(gi + 1) * per]
        m1 = jnp.max(v, axis=0, keepdims=True)
        first = jnp.min(jnp.where(v == m1, sub, per), axis=0, keepdims=True)
        m2 = jnp.max(jnp.where(sub == first, -jnp.inf, v), axis=0, keepdims=True)
        gscore.append(m1 + m2)
    gscore = jnp.concatenate(gscore, axis=0)
    gkeep = jnp.where(_stable_rank(gscore) < float(TOPK_GROUPS), 1.0, 0.0)
    ekeep = jnp.concatenate([jnp.broadcast_to(gkeep[gi:gi + 1], (per, tm)) for gi in range(N_EXPERT_GROUPS)], axis=0)
    sel = jnp.where(_stable_rank(jnp.where(ekeep > 0.0, sb, -jnp.inf)) < float(TOPK), 1.0, 0.0)
    ssel = s * sel
    gate = ssel / jnp.sum(ssel, axis=0, keepdims=True) * ROUTED_SCALE

    selb = sel.astype(BF16)
    slot = jnp.dot(ltri_ref[...], selb, preferred_element_type=F32)
    incl = jnp.dot(selb, utri_ref[...], preferred_element_type=F32)
    rnk = carry[...] + incl - 1.0
    carry[...] = carry[...] + incl[:, tm - 1:tm]
    erow = lax.broadcasted_iota(I32, (ne, tm), 0).astype(F32)
    es, rs, ws = [], [], []
    for k in range(TOPK):
        pick = jnp.where(slot == float(k), sel, 0.0)
        es.append(jnp.sum(pick * erow, axis=0, keepdims=True))
        rs.append(jnp.sum(pick * rnk, axis=0, keepdims=True))
        ws.append(jnp.sum(pick * gate, axis=0, keepdims=True))
    eidx_ref[...] = jnp.concatenate(es, axis=0).astype(I32)
    rnk_ref[...] = jnp.concatenate(rs, axis=0).astype(I32)
    wts_ref[...] = jnp.concatenate(ws + [jnp.zeros((LANES - TOPK, tm), F32)], axis=0).T
    cnt_ref[...] = jnp.broadcast_to(carry[...], cnt_ref.shape).astype(I32)


def moe_route(x, router_w_pad, router_b_col):
    n = x.shape[0]
    tm = RT_TM
    ne = N_EXPERTS
    ltri = jnp.asarray(np.tril(np.ones((ne, ne), np.float32), -1), BF16)
    utri = jnp.asarray(np.triu(np.ones((tm, tm), np.float32)), BF16)
    const = lambda i: (0, 0)
    return pl.pallas_call(
        _router_kernel,
        out_shape=(jax.ShapeDtypeStruct((TOPK, n), I32), jax.ShapeDtypeStruct((TOPK, n), I32),
                   jax.ShapeDtypeStruct((n, LANES), F32), jax.ShapeDtypeStruct((ne, LANES), I32)),
        grid=(n // tm,),
        in_specs=[pl.BlockSpec((tm, D_MODEL), lambda i: (i, 0)),
                  pl.BlockSpec(router_w_pad.shape, const), pl.BlockSpec((ne, 1), const),
                  pl.BlockSpec((ne, ne), const), pl.BlockSpec((tm, tm), const)],
        out_specs=(pl.BlockSpec((TOPK, tm), lambda i: (0, i)), pl.BlockSpec((TOPK, tm), lambda i: (0, i)),
                   pl.BlockSpec((tm, LANES), lambda i: (i, 0)), pl.BlockSpec((ne, LANES), const)),
        scratch_shapes=[pltpu.VMEM((ne, 1), F32)],
        compiler_params=_cparams(("arbitrary",)),
        name="moe_route",
    )(x, router_w_pad, router_b_col, ltri, utri)


def _moe_rows(n_tokens):
    return n_tokens * TOPK + N_EXPERTS * MOE_TILE


def _plan_kernel(cnt_ref, off_ref, texp_ref, nused_ref):
    shift = MOE_TILE.bit_length() - 1

    def per_expert(e, carry):
        off, ti = carry
        off_ref[e] = off
        ntile = lax.shift_right_logical(cnt_ref[e] + (MOE_TILE - 1), shift)

        def mark(j, c):
            texp_ref[ti + j] = e
            return c

        lax.fori_loop(0, ntile, mark, 0)
        return off + ntile * MOE_TILE, ti + ntile

    _, used = lax.fori_loop(0, N_EXPERTS, per_expert, (jnp.int32(0), jnp.int32(0)))
    nused_ref[0] = used

    def fill(j, c):
        texp_ref[j] = N_EXPERTS - 1
        return c

    lax.fori_loop(used, texp_ref.shape[0], fill, 0)


def moe_plan(counts, n_tokens):
    nt = _moe_rows(n_tokens) // MOE_TILE
    smem = pl.BlockSpec(memory_space=pltpu.SMEM)
    return pl.pallas_call(
        _plan_kernel,
        out_shape=(jax.ShapeDtypeStruct((N_EXPERTS,), I32), jax.ShapeDtypeStruct((nt,), I32),
                   jax.ShapeDtypeStruct((1,), I32)),
        in_specs=[smem],
        out_specs=(smem, smem, smem),
        name="moe_plan",
    )(counts)


DSP_TB = 512


U32 = jnp.uint32
ROW_WORDS = D_MODEL // 2
ROW_CHUNKS = ROW_WORDS // LANES
assert ROW_CHUNKS == 4


def _pack_rows(y):
    half = y.shape[1] // 2
    bits = lambda v: lax.bitcast_convert_type(v.astype(jnp.bfloat16).astype(F32), U32)
    return bits(y[:, :half]) | lax.shift_right_logical(bits(y[:, half:]), jnp.uint32(16))


def _unpack_rows(words, dtype):
    hi = lax.bitcast_convert_type(words & jnp.uint32(0xFFFF0000), F32)
    lo = lax.bitcast_convert_type(lax.shift_left(words, jnp.uint32(16)), F32)
    return jnp.concatenate([hi, lo], axis=1).astype(dtype)


def _load_chunked(ref, rows):
    return jnp.concatenate([ref[pl.ds(s, rows, stride=ROW_CHUNKS), :] for s in range(ROW_CHUNKS)], axis=1)


def _store_chunked(ref, words):
    rows = words.shape[0]
    for s in range(ROW_CHUNKS):
        ref[pl.ds(s, rows, stride=ROW_CHUNKS), :] = words[:, s * LANES:(s + 1) * LANES]


def _dst_kernel(off_ref, eidx_ref, rnk_ref, dst_ref):
    e = eidx_ref[...]
    base = jnp.zeros(e.shape, I32)
    for x in range(N_EXPERTS):
        base = jnp.where(e == x, off_ref[x], base)
    dst_ref[...] = (base + rnk_ref[...]) * ROW_CHUNKS


def moe_dst(eidx, rnk, off):
    k, n = eidx.shape
    tb = 2048
    blk = pl.BlockSpec((k, tb), lambda i: (0, i))
    return pl.pallas_call(
        _dst_kernel,
        out_shape=jax.ShapeDtypeStruct((k, n), I32),
        grid=(n // tb,),
        in_specs=[pl.BlockSpec(memory_space=pltpu.SMEM), blk, blk],
        out_specs=blk,
        compiler_params=_cparams(("parallel",)),
        name="moe_dst",
    )(off, eidx, rnk)


def _tile_copy(src_ref, src_row, dst_ref, dst_row, sem):
    return pltpu.make_async_copy(src_ref.at[pl.ds(pl.multiple_of(src_row, ROW_CHUNKS), ROW_CHUNKS)],
                                 dst_ref.at[pl.ds(pl.multiple_of(dst_row, ROW_CHUNKS), ROW_CHUNKS)], sem)


def _dispatch_kernel(dst_ref, off_ref, cnt_ref, x_ref, xs_ref, zbuf, sem, zsem):
    tb = x_ref.shape[0] // ROW_CHUNKS
    zrows = MOE_TILE * ROW_CHUNKS

    def pad_copy(e):
        cnt = cnt_ref[e]
        rem = jnp.bitwise_and(cnt, MOE_TILE - 1)
        start = pl.multiple_of((off_ref[e] + cnt - rem) * ROW_CHUNKS, zrows)
        return rem != 0, pltpu.make_async_copy(zbuf, xs_ref.at[pl.ds(start, zrows)], zsem)

    @pl.when(pl.program_id(0) == 0)
    def _():
        zbuf[...] = jnp.zeros_like(zbuf)

        def start(e, c):
            has_pad, cp = pad_copy(e)

            @pl.when(has_pad)
            def _():
                cp.start()
            return c

        def wait(e, c):
            has_pad, cp = pad_copy(e)

            @pl.when(has_pad)
            def _():
                cp.wait()
            return c

        lax.fori_loop(0, N_EXPERTS, start, 0)
        lax.fori_loop(0, N_EXPERTS, wait, 0)

    def issue(t, c):
        for k in range(TOPK):
            _tile_copy(x_ref, t * ROW_CHUNKS, xs_ref, dst_ref[k, t], sem).start(priority=k % 2)
        return c

    def drain(t, c):
        for k in range(TOPK):
            _tile_copy(x_ref, 0, xs_ref, 0, sem).wait()
        return c

    lax.fori_loop(0, tb, issue, 0)
    lax.fori_loop(0, tb, drain, 0)


def moe_dispatch(xc, dst, off, counts):
    n = xc.shape[0] // ROW_CHUNKS
    tb = DSP_TB
    smem_all = pl.BlockSpec(memory_space=pltpu.SMEM)
    smem_blk = pl.BlockSpec((TOPK, tb), lambda i: (0, i), memory_space=pltpu.SMEM)
    return pl.pallas_call(
        _dispatch_kernel,
        out_shape=jax.ShapeDtypeStruct((_moe_rows(n) * ROW_CHUNKS, LANES), xc.dtype),
        grid=(n // tb,),
        in_specs=[smem_blk, smem_all, smem_all, pl.BlockSpec((tb * ROW_CHUNKS, LANES), lambda i: (i, 0))],
        out_specs=pl.BlockSpec(memory_space=pl.ANY),
        scratch_shapes=[pltpu.VMEM((MOE_TILE * ROW_CHUNKS, LANES), xc.dtype), pltpu.SemaphoreType.DMA(()),
                        pltpu.SemaphoreType.DMA(())],
        compiler_params=_cparams(("arbitrary",)),
        name="moe_dispatch",
    )(dst, off, counts, xc)


def _expert_kernel(texp_ref, nused_ref, xs_ref, w1_ref, w3_ref, w2_ref, ys_ref, w1b, w3b, w2b, last):
    i = pl.program_id(0)
    e = texp_ref[i]

    @pl.when(i == 0)
    def _():
        last[0] = -1

    @pl.when(e != last[0])
    def _():
        w1b[...] = w1_ref[0, 0].astype(BF16)
        w3b[...] = w3_ref[0, 0].astype(BF16)
        w2b[...] = w2_ref[0, 0].astype(BF16)
        last[0] = e

    @pl.when(i < nused_ref[0])
    def _():
        xb = _unpack_rows(_load_chunked(xs_ref, MOE_TILE), BF16)
        h1 = jnp.dot(xb, w1b[...], preferred_element_type=F32)
        h3 = jnp.dot(xb, w3b[...], preferred_element_type=F32)
        h = h1 * jax.nn.sigmoid(h1) * h3
        y = jnp.dot(h.astype(BF16), w2b[...], preferred_element_type=F32)
        _store_chunked(ys_ref, _pack_rows(y))

    @pl.when(i >= nused_ref[0])
    def _():
        ys_ref[...] = jnp.zeros_like(ys_ref)


def moe_experts(xs, texp, nused, w1, w3, w2, layer):
    crows = xs.shape[0]
    d, f = w1.shape[2], w1.shape[3]
    blk = MOE_TILE * ROW_CHUNKS
    nt = crows // blk
    grid_spec = pltpu.PrefetchScalarGridSpec(
        num_scalar_prefetch=2,
        grid=(nt,),
        in_specs=[pl.BlockSpec((blk, LANES), lambda i, te, nu: (jnp.where(i < nu[0], i, 0), 0)),
                  pl.BlockSpec((1, 1, d, f), lambda i, te, nu: (layer, te[i], 0, 0)),
                  pl.BlockSpec((1, 1, d, f), lambda i, te, nu: (layer, te[i], 0, 0)),
                  pl.BlockSpec((1, 1, f, d), lambda i, te, nu: (layer, te[i], 0, 0))],
        out_specs=pl.BlockSpec((blk, LANES), lambda i, te, nu: (i, 0)),
        scratch_shapes=[pltpu.VMEM((d, f), BF16), pltpu.VMEM((d, f), BF16), pltpu.VMEM((f, d), BF16),
                        pltpu.SMEM((1,), I32)],
    )
    return pl.pallas_call(
        _expert_kernel,
        out_shape=jax.ShapeDtypeStruct((crows, LANES), U32),
        grid_spec=grid_spec,
        compiler_params=_cparams(("arbitrary",), 48),
        name="moe_experts",
    )(texp, nused, xs, w1, w3, w2)


CMB_TB = 256


def _combine_kernel(dst_ref, x_ref, wts_ref, ws1_ref, ws3_ref, ws2_ref, lg_ref, lb_ref, ys_ref, o_ref, buf, sem):
    tb = x_ref.shape[0]

    def issue(t, c):
        for k in range(TOPK):
            _tile_copy(ys_ref, dst_ref[k, t], buf.at[k], t * ROW_CHUNKS, sem).start(priority=k % 2)
        return c

    def drain(t, c):
        for k in range(TOPK):
            _tile_copy(ys_ref, 0, buf.at[0], 0, sem).wait()
        return c

    lax.fori_loop(0, tb, issue, 0)
    x = x_ref[...]
    xb = x.astype(BF16)
    h1 = jnp.dot(xb, ws1_ref[...], preferred_element_type=F32)
    h3 = jnp.dot(xb, ws3_ref[...], preferred_element_type=F32)
    y = jnp.dot((h1 * jax.nn.sigmoid(h1) * h3).astype(BF16), ws2_ref[...], preferred_element_type=F32)
    lax.fori_loop(0, tb, drain, 0)
    w = wts_ref[...]
    routed = y
    for k in range(TOPK):
        routed = routed + w[:, k:k + 1] * _unpack_rows(_load_chunked(buf.at[k], tb), F32)
    o_ref[...] = _layer_norm(DN_ALPHA * x + routed, lg_ref[...], lb_ref[...])


def moe_combine(x, ys, dst, wts, ws1, ws3, ws2, ln_g, ln_b):
    n, d = x.shape
    tb = CMB_TB
    smem_blk = pl.BlockSpec((TOPK, tb), lambda i: (0, i), memory_space=pltpu.SMEM)
    const = lambda i: (0, 0)
    return pl.pallas_call(
        _combine_kernel,
        out_shape=jax.ShapeDtypeStruct((n, d), F32),
        grid=(n // tb,),
        in_specs=[smem_blk,
                  pl.BlockSpec((tb, d), lambda i: (i, 0)), pl.BlockSpec((tb, LANES), lambda i: (i, 0)),
                  pl.BlockSpec(ws1.shape, const), pl.BlockSpec(ws3.shape, const), pl.BlockSpec(ws2.shape, const),
                  pl.BlockSpec((1, d), const), pl.BlockSpec((1, d), const),
                  pl.BlockSpec(memory_space=pl.ANY)],
        out_specs=pl.BlockSpec((tb, d), lambda i: (i, 0)),
        scratch_shapes=[pltpu.VMEM((TOPK, tb * ROW_CHUNKS, LANES), U32), pltpu.SemaphoreType.DMA(())],
        compiler_params=_cparams(("parallel",), 48),
        name="moe_combine_ln2",
    )(dst, x, wts, ws1, ws3, ws2, ln_g, ln_b, ys)


def moe_block(x, xp, router_w, router_b, w1, w3, w2, layer, ws1, ws3, ws2, ln_g, ln_b):
    n, d = x.shape
    rw = jnp.pad(router_w, ((0, 0), (0, LANES - N_EXPERTS))).astype(BF16)
    eidx, rnk, wts, cnt = moe_route(x, rw, router_b.reshape(N_EXPERTS, 1))
    counts = cnt[:, 0]
    off, texp, nused = moe_plan(counts, n)
    dst = moe_dst(eidx, rnk, off)
    xs = moe_dispatch(xp, dst, off, counts)
    ys = moe_experts(xs, texp, nused, w1, w3, w2, layer)
    return moe_combine(x, ys, dst, wts, ws1.astype(BF16), ws3.astype(BF16), ws2.astype(BF16),
                       ln_g.reshape(1, -1), ln_b.reshape(1, -1))


def nsa_rope_tables(positions):
    batch, seq = positions.shape
    inv, sgn = _nsa_inv_freq()
    posf = positions.astype(F32)
    tok = rope_tables(posf.reshape(batch * seq, 1), inv, sgn, tm=512)
    end = posf[:, CMP_BLOCK - 1::CMP_STRIDE]
    npad = _n_cmp_pad(seq)
    end = jnp.pad(end, ((0, 0), (0, npad - end.shape[1])))
    cmp = rope_tables(end.reshape(batch * npad, 1), inv, sgn, tm=npad)
    return tok, cmp


def nsa_branch(z, tok_tab, cmp_tab, wk, wv, ovl, *, batch, seq):
    n = batch * seq
    w = NSA_GROUPS * NSA_DH
    xk = z[:, Z_NKV:Z_NKV + w].reshape(n // CMP_HALF, CMP_ROW)
    xv = z[:, Z_NKV + w:Z_NKV + 2 * w].reshape(n // CMP_HALF, CMP_ROW)
    kc, vct = nsa_compress(xk, xv, wk, wv, cmp_tab[0], cmp_tab[1], batch=batch)
    ksa, kw, vswt = nsa_kv_prep(z, tok_tab[0], tok_tab[1], batch=batch, seq=seq)
    return nsa_attention(z, tok_tab[0], tok_tab[1], kc, vct, ksa, kw, vswt, ovl, batch=batch, seq=seq)


def kernel(x, positions, w_in, conv_dw, conv_db, conv_ln_g, conv_ln_b, conv_w_pw, ret_w_o, nsa_pe_k, nsa_w1_k, nsa_w2_k,
           nsa_pe_v, nsa_w1_v, nsa_w2_v, nsa_w_o, w_out, ln1_g, ln1_b, router_w, router_b, moe_w1, moe_w3, moe_w2,
           shared_w1, shared_w3, shared_w2, ln2_g, ln2_b):
    batch, seq, d = x.shape
    n = batch * seq
    xf = x.reshape(n, d)
    row = lambda v: v.reshape(1, -1)

    ret_inv, ret_sgn = _ret_inv_freq()
    ret_tab = rope_tables(positions.astype(F32).reshape(n, 1), ret_inv, ret_sgn, tm=512)
    tok_tab, cmp_tab = nsa_rope_tables(positions)
    ret_consts = _ret_tables()
    ovl = _overlap_matrix(seq)

    for l in range(w_in.shape[0]):
        z = matmul(xf, _layout_w_in(w_in[l]), tm=2048, tn=512, out_dtype=BF16, name="in_proj")
        y_conv = conv_branch(z, conv_dw[l], row(conv_db[l]), row(conv_ln_g[l]), row(conv_ln_b[l]),
                             conv_w_pw[l].astype(BF16), batch=batch, seq=seq)
        o_ret = retention_branch(z, ret_tab[0], ret_tab[1], ret_consts, batch=batch, seq=seq)
        o_nsa = nsa_branch(z, tok_tab, cmp_tab, _cmp_weights(nsa_pe_k[l], nsa_w1_k[l], nsa_w2_k[l]),
                           _cmp_weights(nsa_pe_v[l], nsa_w1_v[l], nsa_w2_v[l]), ovl, batch=batch, seq=seq)
        x1, x1p = merge_block(xf, y_conv, o_ret, o_nsa, z, ret_w_o[l].astype(BF16), nsa_w_o[l].astype(BF16),
                              w_out[l].astype(BF16), row(ln1_g[l]), row(ln1_b[l]))
        xf = moe_block(x1, x1p, router_w[l], router_b[l], moe_w1, moe_w3, moe_w2, l,
                       shared_w1[l], shared_w3[l], shared_w2[l], ln2_g[l], ln2_b[l])
    return xf.reshape(batch, seq, d)
```

```python
import math

import jax
import jax.numpy as jnp
import numpy as np
from jax import lax
from jax.experimental import pallas as pl
from jax.experimental.pallas import tpu as pltpu

F32 = jnp.float32
BF16 = jnp.bfloat16
I32 = jnp.int32

D_MODEL = 1024
DEPTH = 4
CONV_CH = 512
CONV_WIDTH = 31
RET_HEADS = 4
RET_DK = 128
RET_DV = 256
RET_CHUNK = 128
RET_ROPE_BASE = 10000.0
NSA_HEADS = 8
NSA_GROUPS = 2
NSA_HPG = NSA_HEADS // NSA_GROUPS
NSA_DH = 64
NSA_N_BRANCH = 3
CMP_BLOCK = 32
CMP_STRIDE = 16
CMP_HIDDEN = 256
SEL_BLOCK = 64
SEL_TOPK = 16
WINDOW = 512
ROPE_THETA = 500000.0
ROT_DIM = NSA_DH // 4
N_EXPERTS = 64
N_EXPERT_GROUPS = 8
TOPK_GROUPS = 4
TOPK = 8
D_EXPERT = 256
D_SHARED = 256
ROUTED_SCALE = 2.5
DN_ALPHA = (2.0 * DEPTH) ** 0.25
LN_EPS = 1e-5
NEG = -1e30

LANES = 128

Z_MG = 0
Z_CA = 3072
Z_CB = 3584
Z_RQ = 4096
Z_RK = 4608
Z_RV = 5120
Z_RG = 6144
Z_NQ = 7168
Z_NKV = 7680
Z_NG = 8448
Z_W = 8704

_IN_WIDTHS = (512, 512, 512, 512, 1024, 1024, 512, 128, 128, 128, 128, 128, 128, 24, 3072)
_IN_OFFS = tuple(int(v) for v in np.concatenate([[0], np.cumsum(_IN_WIDTHS)[:-1]]))

MOE_TILE = 1024


def _cparams(sem, vmem_mb=None):
    kw = dict(dimension_semantics=sem)
    if vmem_mb is not None:
        kw["vmem_limit_bytes"] = vmem_mb * 1024 * 1024
    return pltpu.CompilerParams(**kw)


def _mm_kernel(x_ref, w_ref, o_ref):
    o_ref[...] = jnp.dot(x_ref[...].astype(BF16), w_ref[...], preferred_element_type=F32).astype(o_ref.dtype)


def matmul(x, w, *, tm, tn, out_dtype, name):
    m, k = x.shape
    n = w.shape[1]
    return pl.pallas_call(
        _mm_kernel,
        out_shape=jax.ShapeDtypeStruct((m, n), out_dtype),
        grid=(m // tm, n // tn),
        in_specs=[pl.BlockSpec((tm, k), lambda i, j: (i, 0)),
                  pl.BlockSpec((k, tn), lambda i, j: (0, j))],
        out_specs=pl.BlockSpec((tm, tn), lambda i, j: (i, j)),
        compiler_params=_cparams(("parallel", "arbitrary"), 48),
        name=name,
    )(x, w)


def _rope_table_kernel(pos_ref, inv_ref, sgn_ref, cos_ref, sin_ref):
    ang = pos_ref[...] * inv_ref[...]
    cos_ref[...] = jnp.cos(ang)
    sin_ref[...] = jnp.sin(ang) * sgn_ref[...]


def rope_tables(pos_col, inv, sgn, *, tm):
    n = pos_col.shape[0]
    w = inv.shape[1]
    return pl.pallas_call(
        _rope_table_kernel,
        out_shape=(jax.ShapeDtypeStruct((n, w), F32), jax.ShapeDtypeStruct((n, w), F32)),
        grid=(n // tm,),
        in_specs=[pl.BlockSpec((tm, 1), lambda i: (i, 0)),
                  pl.BlockSpec((1, w), lambda i: (0, 0)),
                  pl.BlockSpec((1, w), lambda i: (0, 0))],
        out_specs=(pl.BlockSpec((tm, w), lambda i: (i, 0)), pl.BlockSpec((tm, w), lambda i: (i, 0))),
        compiler_params=_cparams(("parallel",)),
        name="rope_tables",
    )(pos_col, inv, sgn)


def _ret_inv_freq():
    inv = 1.0 / jnp.power(jnp.float32(RET_ROPE_BASE), jnp.linspace(0.0, 1.0, RET_DK // 2, dtype=F32))
    inv = jnp.concatenate([inv, inv])[None, :]
    sgn = jnp.concatenate([-jnp.ones((RET_DK // 2,), F32), jnp.ones((RET_DK // 2,), F32)])[None, :]
    return inv, sgn


def _nsa_inv_freq():
    half = ROT_DIM // 2
    inv = jnp.power(jnp.float32(ROPE_THETA), -jnp.arange(0, ROT_DIM, 2, dtype=F32) / ROT_DIM)
    z = jnp.zeros((NSA_DH - ROT_DIM,), F32)
    inv64 = jnp.concatenate([inv, inv, z])
    sgn64 = jnp.concatenate([-jnp.ones((half,), F32), jnp.ones((half,), F32), z])
    return jnp.concatenate([inv64, inv64])[None, :], jnp.concatenate([sgn64, sgn64])[None, :]


def _nsa_rope(x, cos, sin):
    w = x.shape[1]
    half = ROT_DIM // 2
    lane = lax.broadcasted_iota(I32, x.shape, 1) % NSA_DH
    partner = jnp.where(lane < half, pltpu.roll(x, w - half, 1), pltpu.roll(x, half, 1))
    return x * cos + partner * sin


CONV_TT = 512
CONV_HALO = 32


def _conv_kernel(a_ref, b_ref, ah_ref, bh_ref, dw_ref, db_ref, g_ref, be_ref, wpw_ref, o_ref, ubuf, sbuf):
    i = pl.program_id(1)
    tt = a_ref.shape[0]
    u = a_ref[...].astype(F32) * jax.nn.sigmoid(b_ref[...].astype(F32))
    uh = ah_ref[...].astype(F32) * jax.nn.sigmoid(bh_ref[...].astype(F32))
    ubuf[0:CONV_HALO, :] = jnp.where(i > 0, uh, 0.0)
    ubuf[CONV_HALO:CONV_HALO + tt, :] = u
    acc = jnp.zeros((tt, CONV_CH), F32)
    base = CONV_HALO - (CONV_WIDTH - 1)
    sub = 8
    for r in range(sub):
        offs = [o for o in range(base, base + CONV_WIDTH) if o % sub == r]
        if not offs:
            continue
        span = offs[-1] - r + tt
        sbuf[0:span, :] = ubuf[r:r + span, :]
        for o in offs:
            acc = acc + dw_ref[o - base:o - base + 1, :] * sbuf[o - r:o - r + tt, :]
    acc = acc + db_ref[...]
    mu = jnp.mean(acc, axis=-1, keepdims=True)
    var = jnp.mean(jnp.square(acc - mu), axis=-1, keepdims=True)
    y = (acc - mu) * lax.rsqrt(var + LN_EPS) * g_ref[...] + be_ref[...]
    y = y * jax.nn.sigmoid(y)
    o_ref[...] = jnp.dot(y.astype(BF16), wpw_ref[...], preferred_element_type=F32)


def conv_branch(z, dw, db, ln_g, ln_b, w_pw_bf, *, batch, seq):
    tt = CONV_TT
    nt = seq // tt
    r = tt // CONV_HALO
    ca, cb = Z_CA // CONV_CH, Z_CB // CONV_CH

    def halo_map(col):
        return lambda b, i: (jnp.maximum((b * nt + i) * r - 1, 0), col)

    return pl.pallas_call(
        _conv_kernel,
        out_shape=jax.ShapeDtypeStruct((batch * seq, D_MODEL), F32),
        grid=(batch, nt),
        in_specs=[pl.BlockSpec((tt, CONV_CH), lambda b, i: (b * nt + i, ca)),
                  pl.BlockSpec((tt, CONV_CH), lambda b, i: (b * nt + i, cb)),
                  pl.BlockSpec((CONV_HALO, CONV_CH), halo_map(ca)),
                  pl.BlockSpec((CONV_HALO, CONV_CH), halo_map(cb)),
                  pl.BlockSpec((CONV_WIDTH, CONV_CH), lambda b, i: (0, 0)),
                  pl.BlockSpec((1, CONV_CH), lambda b, i: (0, 0)),
                  pl.BlockSpec((1, CONV_CH), lambda b, i: (0, 0)),
                  pl.BlockSpec((1, CONV_CH), lambda b, i: (0, 0)),
                  pl.BlockSpec((CONV_CH, D_MODEL), lambda b, i: (0, 0))],
        out_specs=pl.BlockSpec((tt, D_MODEL), lambda b, i: (b * nt + i, 0)),
        scratch_shapes=[pltpu.VMEM((CONV_HALO + tt, CONV_CH), F32), pltpu.VMEM((CONV_HALO + tt, CONV_CH), F32)],
        compiler_params=_cparams(("parallel", "parallel")),
        name="conv_branch",
    )(z, z, z, z, dw, db, ln_g, ln_b, w_pw_bf)


RET_TQ = 1024


def _ret_tables():
    h, c = RET_HEADS, RET_CHUNK
    log_gamma = jnp.log1p(-jnp.exp2(-5.0 - jnp.arange(h, dtype=F32)))
    idx = jnp.arange(c, dtype=F32)
    diff = idx[:, None] - idx[None, :]
    dmat = jnp.where(diff >= 0, jnp.exp(log_gamma[:, None, None] * jnp.maximum(diff, 0.0)), 0.0).astype(F32)
    xi = jnp.exp(log_gamma[:, None] * (idx + 1.0)).astype(F32)
    zeta = jnp.exp(log_gamma[:, None] * (c - 1.0 - idx)).astype(F32)
    decay = jnp.exp(log_gamma * c).astype(F32)
    xi_b = jnp.broadcast_to(xi[:, :, None], (h, c, RET_DV))
    zeta_b = jnp.broadcast_to(zeta[:, :, None], (h, c, RET_DV))
    decay_b = jnp.broadcast_to(decay[:, None, None], (h, RET_DK, RET_DV))
    return dmat, xi_b, zeta_b, decay_b


def _ret_kernel(q_ref, k_ref, v_ref, g_ref, cos_ref, sin_ref, dmat_ref, xi_ref, zeta_ref, dec_ref, o_ref, r_ref):
    @pl.when(pl.program_id(1) == 0)
    def _():
        r_ref[...] = jnp.zeros_like(r_ref)

    c = RET_CHUNK
    n_chunks = q_ref.shape[0] // c
    for ci in range(n_chunks):
        rows = slice(ci * c, (ci + 1) * c)
        cos = cos_ref[rows, :]
        sin = sin_ref[rows, :]
        for h in range(RET_HEADS):
            qk_cols = slice(h * RET_DK, (h + 1) * RET_DK)
            v_cols = slice(h * RET_DV, (h + 1) * RET_DV)
            q = q_ref[rows, qk_cols].astype(F32)
            k = k_ref[rows, qk_cols].astype(F32)
            q = q * cos + pltpu.roll(q, RET_DK // 2, 1) * sin
            k = (k * cos + pltpu.roll(k, RET_DK // 2, 1) * sin) * (RET_DK ** -0.5)
            v = v_ref[rows, v_cols].astype(F32)
            qb = q.astype(BF16)
            kb = k.astype(BF16)
            inner = lax.dot_general(qb, kb, (((1,), (1,)), ((), ())), preferred_element_type=F32) * dmat_ref[h]
            r_old = r_ref[h]
            o = (jnp.dot(inner.astype(BF16), v.astype(BF16), preferred_element_type=F32)
                 + jnp.dot(qb, r_old.astype(BF16), preferred_element_type=F32) * xi_ref[h])
            vz = (v * zeta_ref[h]).astype(BF16)
            r_ref[h] = r_old * dec_ref[h] + jnp.dot(k.T.astype(BF16), vz, preferred_element_type=F32)
            mu = jnp.mean(o, axis=-1, keepdims=True)
            var = jnp.mean(jnp.square(o - mu), axis=-1, keepdims=True)
            on = (o - mu) * lax.rsqrt(var + LN_EPS)
            g = g_ref[rows, v_cols].astype(F32)
            o_ref[rows, v_cols] = g * jax.nn.sigmoid(g) * on


def retention_branch(z, ret_cos, ret_sin, tables, *, batch, seq):
    tq = RET_TQ
    nt = seq // tq
    dmat, xi_b, zeta_b, decay_b = tables
    qw = RET_HEADS * RET_DK
    vw = RET_HEADS * RET_DV
    row = lambda b, i: b * nt + i
    full3 = lambda b, i: (0, 0, 0)
    return pl.pallas_call(
        _ret_kernel,
        out_shape=jax.ShapeDtypeStruct((batch * seq, vw), F32),
        grid=(batch, nt),
        in_specs=[pl.BlockSpec((tq, qw), lambda b, i: (row(b, i), Z_RQ // qw)),
                  pl.BlockSpec((tq, qw), lambda b, i: (row(b, i), Z_RK // qw)),
                  pl.BlockSpec((tq, vw), lambda b, i: (row(b, i), Z_RV // vw)),
                  pl.BlockSpec((tq, vw), lambda b, i: (row(b, i), Z_RG // vw)),
                  pl.BlockSpec((tq, RET_DK), lambda b, i: (row(b, i), 0)),
                  pl.BlockSpec((tq, RET_DK), lambda b, i: (row(b, i), 0)),
                  pl.BlockSpec(dmat.shape, full3),
                  pl.BlockSpec(xi_b.shape, full3),
                  pl.BlockSpec(zeta_b.shape, full3),
                  pl.BlockSpec(decay_b.shape, full3)],
        out_specs=pl.BlockSpec((tq, vw), lambda b, i: (row(b, i), 0)),
        scratch_shapes=[pltpu.VMEM((RET_HEADS, RET_DK, RET_DV), F32)],
        compiler_params=_cparams(("parallel", "arbitrary"), 48),
        name="retention",
    )(z, z, z, z, ret_cos, ret_sin, dmat, xi_b, zeta_b, decay_b)


def _layout_w_in(w):
    seg = lambda i: w[:, _IN_OFFS[i]:_IN_OFFS[i] + _IN_WIDTHS[i]]
    order = (14, 0, 1, 2, 3, 4, 5, 6, 7, 8, 9, 10, 11, 12, 13)
    parts = [seg(i) for i in order]
    used = sum(_IN_WIDTHS)
    parts.append(jnp.zeros((w.shape[0], Z_W - used), w.dtype))
    return jnp.concatenate(parts, axis=1).astype(BF16)


CMP_HALF = CMP_BLOCK // 2
CMP_ROW = CMP_HALF * NSA_GROUPS * NSA_DH


def _n_cmp_pad(seq):
    return seq // CMP_STRIDE


def _cmp_weights(pe, w1, w2):
    g = NSA_GROUPS
    eye = jnp.eye(g, dtype=F32)
    w = w1.reshape(2, CMP_HALF, NSA_DH, CMP_HIDDEN)
    w1ab = jnp.einsum("hldf,gk->hlgdkf", w, eye).reshape(2, CMP_ROW, g * CMP_HIDDEN).astype(BF16)
    peab = jnp.broadcast_to(pe.reshape(2, CMP_HALF, 1, NSA_DH), (2, CMP_HALF, g, NSA_DH)).reshape(2, 1, CMP_ROW)
    w2bd = jnp.einsum("fd,gk->gfkd", w2, eye).reshape(g * CMP_HIDDEN, g * NSA_DH).astype(BF16)
    return peab, w1ab, w2bd


def _cmp_mlp(x, pe_ref, w1_ref, w2_ref):
    x = x.astype(F32)
    a = jnp.dot((x + pe_ref[0]).astype(BF16), w1_ref[0], preferred_element_type=F32)
    b = jnp.dot((x + pe_ref[1]).astype(BF16), w1_ref[1], preferred_element_type=F32)
    hid = a + pltpu.roll(b, b.shape[0] - 1, 0)
    hid = hid * jax.nn.sigmoid(hid)
    return jnp.dot(hid.astype(BF16), w2_ref[...], preferred_element_type=F32)


def _compress_kernel(xk_ref, xv_ref, pek_ref, w1k_ref, w2k_ref, pev_ref, w1v_ref, w2v_ref, cos_ref, sin_ref,
                     k_ref, vt_ref):
    k = _cmp_mlp(xk_ref[...], pek_ref, w1k_ref, w2k_ref)
    k_ref[0] = _nsa_rope(k, cos_ref[...], sin_ref[...]).astype(BF16)
    vt_ref[0] = _cmp_mlp(xv_ref[...], pev_ref, w1v_ref, w2v_ref).T.astype(BF16)


def nsa_compress(xk, xv, wk, wv, cmp_cos, cmp_sin, *, batch):
    pek, w1k, w2k = wk
    pev, w1v, w2v = wv
    npad = xk.shape[0] // batch
    gw = NSA_GROUPS * NSA_DH
    c3 = lambda b: (0, 0, 0)
    c2 = lambda b: (0, 0)
    return pl.pallas_call(
        _compress_kernel,
        out_shape=(jax.ShapeDtypeStruct((batch, npad, gw), BF16), jax.ShapeDtypeStruct((batch, gw, npad), BF16)),
        grid=(batch,),
        in_specs=[pl.BlockSpec((npad, CMP_ROW), lambda b: (b, 0)),
                  pl.BlockSpec((npad, CMP_ROW), lambda b: (b, 0)),
                  pl.BlockSpec(pek.shape, c3), pl.BlockSpec(w1k.shape, c3), pl.BlockSpec(w2k.shape, c2),
                  pl.BlockSpec(pev.shape, c3), pl.BlockSpec(w1v.shape, c3), pl.BlockSpec(w2v.shape, c2),
                  pl.BlockSpec((npad, gw), lambda b: (b, 0)),
                  pl.BlockSpec((npad, gw), lambda b: (b, 0))],
        out_specs=(pl.BlockSpec((1, npad, gw), lambda b: (b, 0, 0)), pl.BlockSpec((1, gw, npad), lambda b: (b, 0, 0))),
        compiler_params=_cparams(("parallel",), 48),
        name="nsa_compress",
    )(xk, xv, pek, w1k, w2k, pev, w1v, w2v, cmp_cos, cmp_sin)


KV_TT = 512
ATT_TK = 512
ATT_KB = 256
ATT_AHEAD = 8
LOG2E = math.log2(math.e)


def _kv_prep_kernel(z_ref, cos_ref, sin_ref, ksa_ref, kw_ref, vswt_ref):
    tt = z_ref.shape[0]
    cos = cos_ref[...]
    sin = sin_ref[...]
    w = NSA_GROUPS * NSA_DH
    n_sel = ksa_ref.shape[3] - NSA_DH
    ks = _nsa_rope(z_ref[:, 2 * w:3 * w].astype(F32), cos, sin)
    vs = z_ref[:, 3 * w:4 * w].astype(F32)
    kw = _nsa_rope(z_ref[:, 4 * w:5 * w].astype(F32), cos, sin)
    vw = z_ref[:, 5 * w:6 * w].astype(F32)
    blk = (pl.program_id(1) * tt + lax.broadcasted_iota(I32, (tt, n_sel), 0)) // SEL_BLOCK
    onehot = jnp.where(blk == lax.broadcasted_iota(I32, (tt, n_sel), 1), 1.0, 0.0)
    for g in range(NSA_GROUPS):
        cols = slice(g * NSA_DH, (g + 1) * NSA_DH)
        ksa_ref[0, g] = jnp.concatenate([ks[:, cols], onehot], axis=1).astype(BF16)
        kw_ref[0, g] = kw[:, cols].astype(BF16)
        vswt_ref[0, g] = jnp.concatenate([vs[:, cols], vw[:, cols]], axis=1).T.astype(BF16)


def nsa_kv_prep(z, nsa_cos, nsa_sin, *, batch, seq):
    tt = KV_TT
    nt = seq // tt
    g = NSA_GROUPS
    w = g * NSA_DH
    n_sel = seq // SEL_BLOCK
    return pl.pallas_call(
        _kv_prep_kernel,
        out_shape=(jax.ShapeDtypeStruct((batch, g, seq, NSA_DH + n_sel), BF16),
                   jax.ShapeDtypeStruct((batch, g, seq, NSA_DH), BF16),
                   jax.ShapeDtypeStruct((batch, g, w, seq), BF16)),
        grid=(batch, nt),
        in_specs=[pl.BlockSpec((tt, 6 * w), lambda b, i: (b * nt + i, Z_NKV // (6 * w))),
                  pl.BlockSpec((tt, w), lambda b, i: (b * nt + i, 0)),
                  pl.BlockSpec((tt, w), lambda b, i: (b * nt + i, 0))],
        out_specs=(pl.BlockSpec((1, g, tt, NSA_DH + n_sel), lambda b, i: (b, 0, i, 0)),
                   pl.BlockSpec((1, g, tt, NSA_DH), lambda b, i: (b, 0, i, 0)),
                   pl.BlockSpec((1, g, w, tt), lambda b, i: (b, 0, 0, i))),
        compiler_params=_cparams(("parallel", "parallel")),
        name="nsa_kv_prep",
    )(z, nsa_cos, nsa_sin)


ATT_TQ = 256


def _overlap_matrix(seq):
    n_cmp = (seq - CMP_BLOCK) // CMP_STRIDE + 1
    n_sel = seq // SEL_BLOCK
    ii = np.arange(_n_cmp_pad(seq))[None, :]
    jj = np.arange(LANES)[:, None]
    lo = np.maximum(ii * CMP_STRIDE, jj * SEL_BLOCK)
    hi = np.minimum(ii * CMP_STRIDE + CMP_BLOCK, (jj + 1) * SEL_BLOCK)
    ov = np.maximum(hi - lo, 0).astype(np.float32) / CMP_BLOCK
    ov = np.where((ii < n_cmp) & (jj < n_sel), ov, 0.0)
    return jnp.asarray(np.tile(ov, (1, NSA_HPG)), BF16)


def _softmax_step(s, ok, vt, cols, m_ref, l_ref, a_ref):
    m_old = m_ref[:, cols]
    m_new = jnp.maximum(m_old, jnp.max(s, axis=0, keepdims=True))
    alpha = jnp.exp2(m_old - m_new)
    p = jnp.exp2(s - m_new)
    if ok is not None:
        p = jnp.where(ok, p, 0.0)
    l_ref[:, cols] = alpha * l_ref[:, cols] + jnp.sum(p, axis=0, keepdims=True)
    a_ref[:, cols] = alpha * a_ref[:, cols] + jnp.dot(vt, p.astype(BF16), preferred_element_type=F32)
    m_ref[:, cols] = m_new


def _att_kernel(q_ref, cos_ref, sin_ref, gl_ref, kc_ref, vct_ref, ksa_ref, kw_ref, vswt_ref, ovl_ref, o_ref,
                qa_s, m_sel, l_sel, a_sel, m_win, l_win, a_win):
    g = pl.program_id(1)
    qi = pl.program_id(2)
    tq = ATT_TQ
    tk = ATT_TK
    hpg = NSA_HPG
    dh = NSA_DH
    t0 = qi * tq
    seq = ksa_ref.shape[2]
    n_sel = ksa_ref.shape[3] - dh
    heads = [slice(h * tq, (h + 1) * tq) for h in range(hpg)]

    cos = jnp.concatenate([cos_ref[...]] * (hpg // 2), axis=1)
    sin = jnp.concatenate([sin_ref[...]] * (hpg // 2), axis=1)
    qt = (_nsa_rope(q_ref[...].astype(F32), cos, sin) * (dh ** -0.5 * LOG2E)).T
    q_t = jnp.concatenate([qt[h * dh:(h + 1) * dh] for h in range(hpg)], axis=1)
    q_tb = q_t.astype(BF16)
    qa_s[0:dh, :] = q_tb

    zero = jnp.zeros_like(q_tb)
    q2 = jnp.where(g == 0, jnp.concatenate([q_tb, zero], axis=0), jnp.concatenate([zero, q_tb], axis=0))
    s = jnp.dot(kc_ref[0], q2, preferred_element_type=F32)
    n_cmp = (seq - CMP_BLOCK) // CMP_STRIDE + 1
    nrow = lax.broadcasted_iota(I32, s.shape, 0)
    tcol = t0 + lax.broadcasted_iota(I32, s.shape, 1) % tq
    ok = (nrow * CMP_STRIDE + (CMP_BLOCK - 1) <= tcol) & (nrow < n_cmp)
    sm = jnp.where(ok, s, NEG)
    e = jnp.where(ok, jnp.exp2(sm - jnp.max(sm, axis=0, keepdims=True)), 0.0)
    den = jnp.sum(e, axis=0, keepdims=True)
    pb = (e / jnp.where(den > 0.0, den, 1.0)).astype(BF16)
    oc = jnp.dot(vct_ref[0], pb, preferred_element_type=F32)
    o_cmp = jnp.where(g == 0, oc[0:dh], oc[dh:2 * dh])
    pcat = jnp.concatenate([pb[:, hs] for hs in heads], axis=0)
    imp_t = jnp.dot(ovl_ref[...], pcat, preferred_element_type=F32)[0:n_sel]

    jrow = lax.broadcasted_iota(I32, (n_sel, tq), 0)
    cur = (t0 + lax.broadcasted_iota(I32, (n_sel, tq), 1)) // SEL_BLOCK
    forced = (jrow == 0) | (jrow == cur) | (jrow == cur - 1)
    imp_t = jnp.where(forced, jnp.inf, jnp.where(jrow <= cur, imp_t, -jnp.inf))
    sub = 8
    groups = [imp_t[r:r + sub] for r in range(0, n_sel, sub)]
    ranks = [jnp.zeros((sub, tq), F32) for _ in groups]
    srow = lax.broadcasted_iota(I32, (sub, tq), 0)
    for i in range(n_sel):
        gi, si = divmod(i, sub)
        ri = groups[gi][si:si + 1, :]
        for gj, v in enumerate(groups):
            ge = jnp.where(ri >= v, 1.0, 0.0)
            if gj > gi:
                ranks[gj] = ranks[gj] + ge
            else:
                gt = jnp.where(ri > v, 1.0, 0.0)
                ranks[gj] = ranks[gj] + (gt if gj < gi else jnp.where(srow > si, ge, gt))
    rank = jnp.concatenate(ranks, axis=0)
    bias_t = jnp.where(rank < float(SEL_TOPK), 0.0, NEG).astype(BF16)
    qa_s[dh:dh + n_sel, :] = jnp.concatenate([bias_t] * hpg, axis=1)

    m_sel[...] = jnp.full(m_sel.shape, NEG, F32)
    l_sel[...] = jnp.zeros(l_sel.shape, F32)
    a_sel[...] = jnp.zeros(a_sel.shape, F32)

    kb = ATT_KB
    assert tk == 2 * tq and tq % kb == 0
    krow = lax.broadcasted_iota(I32, (kb, tq), 0)
    qtime = t0 + lax.broadcasted_iota(I32, (kb, tq), 1)

    def run_blocks(blocks):
        sc = [blk[0]() for blk in blocks[:ATT_AHEAD]]
        for b, (_, mask, vt, hs, refs) in enumerate(blocks):
            if b + ATT_AHEAD < len(blocks):
                sc.append(blocks[b + ATT_AHEAD][0]())
            ok = mask()
            s_b = sc[b] if ok is None else jnp.where(ok, sc[b], NEG)
            sc[b] = None
            _softmax_step(s_b, ok, vt(), hs, *refs)

    def sel_blocks(k0, n_keys, causal):
        def block(kk, hs):
            return (lambda: jnp.dot(ksa_ref[0, 0, pl.ds(k0 + kk, kb), :], qa_s[:, hs], preferred_element_type=F32),
                    (lambda: (k0 + kk + krow) <= qtime) if causal else (lambda: None),
                    lambda: vswt_ref[0, 0, 0:dh, pl.ds(k0 + kk, kb)], hs, (m_sel, l_sel, a_sel))
        return [block(kk, hs) for kk in range(0, n_keys, kb) for hs in heads]

    def sel_body(j, carry):
        run_blocks(sel_blocks(pl.multiple_of(j * tk, tk), tk, False))
        return carry

    n_full = t0 // tk
    lax.fori_loop(0, n_full, sel_body, 0)

    @pl.when(n_full * tk < t0)
    def _():
        run_blocks(sel_blocks(pl.multiple_of(n_full * tk, tk), tk - tq, False))

    kw0 = pl.multiple_of(jnp.maximum(t0 - WINDOW, 0), tq)

    def win_block(kk, hs):
        def mask():
            dist = qtime - (kw0 + kk + krow)
            return (dist >= 0) & (dist < WINDOW)
        return (lambda: jnp.dot(kw_ref[0, 0, pl.ds(kw0 + kk, kb), :], qa_s[0:dh, hs], preferred_element_type=F32),
                mask, lambda: vswt_ref[0, 0, dh:2 * dh, pl.ds(kw0 + kk, kb)], hs, (m_win, l_win, a_win))

    m_win[...] = jnp.full(m_win.shape, NEG, F32)
    l_win[...] = jnp.zeros(l_win.shape, F32)
    a_win[...] = jnp.zeros(a_win.shape, F32)
    win = [win_block(kk, hs) for kk in range(0, WINDOW + tq, kb) for hs in heads]
    diag = sel_blocks(pl.multiple_of(t0, tq), tq, True)
    per = len(win) // len(diag)
    mixed = []
    for i, blk in enumerate(diag):
        mixed += win[i * per:(i + 1) * per] + [blk]
    run_blocks(mixed + win[len(diag) * per:])

    o_win = a_win[...] / l_win[...]
    o_slc = a_sel[...] / l_sel[...]
    gl_t = jax.nn.sigmoid(gl_ref[...].astype(F32)).T
    nb = NSA_N_BRANCH
    outs = []
    for h, hs in enumerate(heads):
        gate = lambda br: jnp.where(g == 0, gl_t[nb * h + br:nb * h + br + 1],
                                    gl_t[nb * (hpg + h) + br:nb * (hpg + h) + br + 1])
        outs.append(gate(0) * o_cmp[:, hs] + gate(1) * o_slc[:, hs] + gate(2) * o_win[:, hs])
    o_ref[...] = jnp.concatenate(outs, axis=0).T


def nsa_attention(z, nsa_cos, nsa_sin, kc, vct, ksa, kw, vswt, ovl, *, batch, seq):
    tq = ATT_TQ
    nt = seq // tq
    g = NSA_GROUPS
    gw = g * NSA_DH
    qw = NSA_HPG * NSA_DH
    m_cols = NSA_HPG * tq
    row = lambda b, gg, i: b * nt + i
    per_bg = lambda a: pl.BlockSpec((1, 1) + a.shape[2:], lambda b, gg, i: (b, gg, 0, 0))
    stat = pltpu.VMEM((1, m_cols), F32)
    vals = pltpu.VMEM((NSA_DH, m_cols), F32)
    return pl.pallas_call(
        _att_kernel,
        out_shape=jax.ShapeDtypeStruct((batch * seq, g * qw), F32),
        grid=(batch, g, nt),
        in_specs=[pl.BlockSpec((tq, qw), lambda b, gg, i: (row(b, gg, i), Z_NQ // qw + gg)),
                  pl.BlockSpec((tq, gw), lambda b, gg, i: (row(b, gg, i), 0)),
                  pl.BlockSpec((tq, gw), lambda b, gg, i: (row(b, gg, i), 0)),
                  pl.BlockSpec((tq, LANES), lambda b, gg, i: (row(b, gg, i), Z_NG // LANES)),
                  pl.BlockSpec((1,) + kc.shape[1:], lambda b, gg, i: (b, 0, 0)),
                  pl.BlockSpec((1,) + vct.shape[1:], lambda b, gg, i: (b, 0, 0)),
                  per_bg(ksa), per_bg(kw), per_bg(vswt),
                  pl.BlockSpec(ovl.shape, lambda b, gg, i: (0, 0))],
        out_specs=pl.BlockSpec((tq, qw), lambda b, gg, i: (row(b, gg, i), gg)),
        scratch_shapes=[pltpu.VMEM((ksa.shape[3], m_cols), BF16), stat, stat, vals, stat, stat, vals],
        compiler_params=_cparams(("parallel", "parallel", "parallel"), 48),
        name="nsa_attention",
    )(z, nsa_cos, nsa_sin, z, kc, vct, ksa, kw, vswt, ovl)


MERGE_TM = 256


def _layer_norm(y, g, b):
    mu = jnp.mean(y, axis=-1, keepdims=True)
    var = jnp.mean(jnp.square(y - mu), axis=-1, keepdims=True)
    return (y - mu) * lax.rsqrt(var + LN_EPS) * g + b


def _merge_kernel(x_ref, yc_ref, or_ref, on_ref, g0_ref, g1_ref, g2_ref, wr_ref, wn_ref, wo_ref, lg_ref, lb_ref, o_ref,
                  op_ref):
    y_ret = jnp.dot(or_ref[...].astype(BF16), wr_ref[...], preferred_element_type=F32)
    y_nsa = jnp.dot(on_ref[...].astype(BF16), wn_ref[...], preferred_element_type=F32)
    gate = lambda ref: jax.nn.sigmoid(ref[...].astype(F32))
    m = gate(g0_ref) * yc_ref[...] + gate(g1_ref) * y_ret + gate(g2_ref) * y_nsa
    h = jnp.dot(m.astype(BF16), wo_ref[...], preferred_element_type=F32)
    y = _layer_norm(DN_ALPHA * x_ref[...] + h, lg_ref[...], lb_ref[...])
    o_ref[...] = y
    _store_chunked(op_ref, _pack_rows(y))


def merge_block(x, y_conv, o_ret, o_nsa, z, ret_w_o, nsa_w_o, w_out, ln_g, ln_b):
    n = x.shape[0]
    tm = MERGE_TM
    d = D_MODEL
    rowd = lambda i: (i, 0)
    const = lambda i: (0, 0)
    return pl.pallas_call(
        _merge_kernel,
        out_shape=(jax.ShapeDtypeStruct((n, d), F32), jax.ShapeDtypeStruct((n * ROW_CHUNKS, LANES), U32)),
        grid=(n // tm,),
        in_specs=[pl.BlockSpec((tm, d), rowd), pl.BlockSpec((tm, d), rowd),
                  pl.BlockSpec((tm, o_ret.shape[1]), rowd), pl.BlockSpec((tm, o_nsa.shape[1]), rowd),
                  pl.BlockSpec((tm, d), lambda i: (i, Z_MG // d)),
                  pl.BlockSpec((tm, d), lambda i: (i, Z_MG // d + 1)),
                  pl.BlockSpec((tm, d), lambda i: (i, Z_MG // d + 2)),
                  pl.BlockSpec(ret_w_o.shape, const), pl.BlockSpec(nsa_w_o.shape, const), pl.BlockSpec(w_out.shape, const),
                  pl.BlockSpec((1, d), const), pl.BlockSpec((1, d), const)],
        out_specs=(pl.BlockSpec((tm, d), rowd), pl.BlockSpec((tm * ROW_CHUNKS, LANES), rowd)),
        compiler_params=_cparams(("parallel",), 48),
        name="merge_ln1",
    )(x, y_conv, o_ret, o_nsa, z, z, z, ret_w_o, nsa_w_o, w_out, ln_g, ln_b)


RT_TM = 512


def _stable_rank(v):
    n = v.shape[0]
    row = lax.broadcasted_iota(I32, v.shape, 0)
    rank = jnp.zeros(v.shape, F32)
    for i in range(n):
        r = v[i:i + 1, :]
        rank = rank + jnp.where(row > i, jnp.where(r >= v, 1.0, 0.0), jnp.where(r > v, 1.0, 0.0))
    return rank


def _router_kernel(x_ref, wr_ref, b_ref, ltri_ref, utri_ref, eidx_ref, rnk_ref, wts_ref, cnt_ref, carry):
    @pl.when(pl.program_id(0) == 0)
    def _():
        carry[...] = jnp.zeros_like(carry)

    tm = x_ref.shape[0]
    ne = N_EXPERTS
    per = ne // N_EXPERT_GROUPS
    logits = jnp.dot(x_ref[...].astype(BF16), wr_ref[...], preferred_element_type=F32)
    s = jax.nn.sigmoid(logits.T[0:ne])
    sb = s + b_ref[...]
    sub = lax.broadcasted_iota(I32, (per, tm), 0)
    gscore = []
    for gi in range(N_EXPERT_GROUPS):
        v = sb[gi * per:(gi + 1) * per]
        m1 = jnp.max(v, axis=0, keepdims=True)
        first = jnp.min(jnp.where(v == m1, sub, per), axis=0, keepdims=True)
        m2 = jnp.max(jnp.where(sub == first, -jnp.inf, v), axis=0, keepdims=True)
        gscore.append(m1 + m2)
    gscore = jnp.concatenate(gscore, axis=0)
    gkeep = jnp.where(_stable_rank(gscore) < float(TOPK_GROUPS), 1.0, 0.0)
    ekeep = jnp.concatenate([jnp.broadcast_to(gkeep[gi:gi + 1], (per, tm)) for gi in range(N_EXPERT_GROUPS)], axis=0)
    sel = jnp.where(_stable_rank(jnp.where(ekeep > 0.0, sb, -jnp.inf)) < float(TOPK), 1.0, 0.0)
    ssel = s * sel
    gate = ssel / jnp.sum(ssel, axis=0, keepdims=True) * ROUTED_SCALE

    selb = sel.astype(BF16)
    slot = jnp.dot(ltri_ref[...], selb, preferred_element_type=F32)
    incl = jnp.dot(selb, utri_ref[...], preferred_element_type=F32)
    rnk = carry[...] + incl - 1.0
    carry[...] = carry[...] + incl[:, tm - 1:tm]
    erow = lax.broadcasted_iota(I32, (ne, tm), 0).astype(F32)
    es, rs, ws = [], [], []
    for k in range(TOPK):
        pick = jnp.where(slot == float(k), sel, 0.0)
        es.append(jnp.sum(pick * erow, axis=0, keepdims=True))
        rs.append(jnp.sum(pick * rnk, axis=0, keepdims=True))
        ws.append(jnp.sum(pick * gate, axis=0, keepdims=True))
    eidx_ref[...] = jnp.concatenate(es, axis=0).astype(I32)
    rnk_ref[...] = jnp.concatenate(rs, axis=0).astype(I32)
    wts_ref[...] = jnp.concatenate(ws + [jnp.zeros((LANES - TOPK, tm), F32)], axis=0).T
    cnt_ref[...] = jnp.broadcast_to(carry[...], cnt_ref.shape).astype(I32)


def moe_route(x, router_w_pad, router_b_col):
    n = x.shape[0]
    tm = RT_TM
    ne = N_EXPERTS
    ltri = jnp.asarray(np.tril(np.ones((ne, ne), np.float32), -1), BF16)
    utri = jnp.asarray(np.triu(np.ones((tm, tm), np.float32)), BF16)
    const = lambda i: (0, 0)
    return pl.pallas_call(
        _router_kernel,
        out_shape=(jax.ShapeDtypeStruct((TOPK, n), I32), jax.ShapeDtypeStruct((TOPK, n), I32),
                   jax.ShapeDtypeStruct((n, LANES), F32), jax.ShapeDtypeStruct((ne, LANES), I32)),
        grid=(n // tm,),
        in_specs=[pl.BlockSpec((tm, D_MODEL), lambda i: (i, 0)),
                  pl.BlockSpec(router_w_pad.shape, const), pl.BlockSpec((ne, 1), const),
                  pl.BlockSpec((ne, ne), const), pl.BlockSpec((tm, tm), const)],
        out_specs=(pl.BlockSpec((TOPK, tm), lambda i: (0, i)), pl.BlockSpec((TOPK, tm), lambda i: (0, i)),
                   pl.BlockSpec((tm, LANES), lambda i: (i, 0)), pl.BlockSpec((ne, LANES), const)),
        scratch_shapes=[pltpu.VMEM((ne, 1), F32)],
        compiler_params=_cparams(("arbitrary",)),
        name="moe_route",
    )(x, router_w_pad, router_b_col, ltri, utri)


def _moe_rows(n_tokens):
    return n_tokens * TOPK + N_EXPERTS * MOE_TILE


def _plan_kernel(cnt_ref, off_ref, texp_ref, nused_ref):
    shift = MOE_TILE.bit_length() - 1

    def per_expert(e, carry):
        off, ti = carry
        off_ref[e] = off
        ntile = lax.shift_right_logical(cnt_ref[e] + (MOE_TILE - 1), shift)

        def mark(j, c):
            texp_ref[ti + j] = e
            return c

        lax.fori_loop(0, ntile, mark, 0)
        return off + ntile * MOE_TILE, ti + ntile

    _, used = lax.fori_loop(0, N_EXPERTS, per_expert, (jnp.int32(0), jnp.int32(0)))
    nused_ref[0] = used

    def fill(j, c):
        texp_ref[j] = N_EXPERTS - 1
        return c

    lax.fori_loop(used, texp_ref.shape[0], fill, 0)


def moe_plan(counts, n_tokens):
    nt = _moe_rows(n_tokens) // MOE_TILE
    smem = pl.BlockSpec(memory_space=pltpu.SMEM)
    return pl.pallas_call(
        _plan_kernel,
        out_shape=(jax.ShapeDtypeStruct((N_EXPERTS,), I32), jax.ShapeDtypeStruct((nt,), I32),
                   jax.ShapeDtypeStruct((1,), I32)),
        in_specs=[smem],
        out_specs=(smem, smem, smem),
        name="moe_plan",
    )(counts)


DSP_TB = 512


U32 = jnp.uint32
ROW_WORDS = D_MODEL // 2
ROW_CHUNKS = ROW_WORDS // LANES
assert ROW_CHUNKS == 4


def _pack_rows(y):
    half = y.shape[1] // 2
    bits = lambda v: lax.bitcast_convert_type(v.astype(jnp.bfloat16).astype(F32), U32)
    return bits(y[:, :half]) | lax.shift_right_logical(bits(y[:, half:]), jnp.uint32(16))


def _unpack_rows(words, dtype):
    hi = lax.bitcast_convert_type(words & jnp.uint32(0xFFFF0000), F32)
    lo = lax.bitcast_convert_type(lax.shift_left(words, jnp.uint32(16)), F32)
    return jnp.concatenate([hi, lo], axis=1).astype(dtype)


def _load_chunked(ref, rows):
    return jnp.concatenate([ref[pl.ds(s, rows, stride=ROW_CHUNKS), :] for s in range(ROW_CHUNKS)], axis=1)


def _store_chunked(ref, words):
    rows = words.shape[0]
    for s in range(ROW_CHUNKS):
        ref[pl.ds(s, rows, stride=ROW_CHUNKS), :] = words[:, s * LANES:(s + 1) * LANES]


def _dst_kernel(off_ref, eidx_ref, rnk_ref, dst_ref):
    e = eidx_ref[...]
    base = jnp.zeros(e.shape, I32)
    for x in range(N_EXPERTS):
        base = jnp.where(e == x, off_ref[x], base)
    dst_ref[...] = (base + rnk_ref[...]) * ROW_CHUNKS


def moe_dst(eidx, rnk, off):
    k, n = eidx.shape
    tb = 2048
    blk = pl.BlockSpec((k, tb), lambda i: (0, i))
    return pl.pallas_call(
        _dst_kernel,
        out_shape=jax.ShapeDtypeStruct((k, n), I32),
        grid=(n // tb,),
        in_specs=[pl.BlockSpec(memory_space=pltpu.SMEM), blk, blk],
        out_specs=blk,
        compiler_params=_cparams(("parallel",)),
        name="moe_dst",
    )(off, eidx, rnk)


def _tile_copy(src_ref, src_row, dst_ref, dst_row, sem):
    return pltpu.make_async_copy(src_ref.at[pl.ds(pl.multiple_of(src_row, ROW_CHUNKS), ROW_CHUNKS)],
                                 dst_ref.at[pl.ds(pl.multiple_of(dst_row, ROW_CHUNKS), ROW_CHUNKS)], sem)


def _dispatch_kernel(dst_ref, off_ref, cnt_ref, x_ref, xs_ref, zbuf, sem, zsem):
    tb = x_ref.shape[0] // ROW_CHUNKS
    zrows = MOE_TILE * ROW_CHUNKS

    def pad_copy(e):
        cnt = cnt_ref[e]
        rem = jnp.bitwise_and(cnt, MOE_TILE - 1)
        start = pl.multiple_of((off_ref[e] + cnt - rem) * ROW_CHUNKS, zrows)
        return rem != 0, pltpu.make_async_copy(zbuf, xs_ref.at[pl.ds(start, zrows)], zsem)

    @pl.when(pl.program_id(0) == 0)
    def _():
        zbuf[...] = jnp.zeros_like(zbuf)

        def start(e, c):
            has_pad, cp = pad_copy(e)

            @pl.when(has_pad)
            def _():
                cp.start()
            return c

        def wait(e, c):
            has_pad, cp = pad_copy(e)

            @pl.when(has_pad)
            def _():
                cp.wait()
            return c

        lax.fori_loop(0, N_EXPERTS, start, 0)
        lax.fori_loop(0, N_EXPERTS, wait, 0)

    def issue(t, c):
        for k in range(TOPK):
            _tile_copy(x_ref, t * ROW_CHUNKS, xs_ref, dst_ref[k, t], sem).start(priority=k % 2)
        return c

    def drain(t, c):
        for k in range(TOPK):
            _tile_copy(x_ref, 0, xs_ref, 0, sem).wait()
        return c

    lax.fori_loop(0, tb, issue, 0)
    lax.fori_loop(0, tb, drain, 0)


def moe_dispatch(xc, dst, off, counts):
    n = xc.shape[0] // ROW_CHUNKS
    tb = DSP_TB
    smem_all = pl.BlockSpec(memory_space=pltpu.SMEM)
    smem_blk = pl.BlockSpec((TOPK, tb), lambda i: (0, i), memory_space=pltpu.SMEM)
    return pl.pallas_call(
        _dispatch_kernel,
        out_shape=jax.ShapeDtypeStruct((_moe_rows(n) * ROW_CHUNKS, LANES), xc.dtype),
        grid=(n // tb,),
        in_specs=[smem_blk, smem_all, smem_all, pl.BlockSpec((tb * ROW_CHUNKS, LANES), lambda i: (i, 0))],
        out_specs=pl.BlockSpec(memory_space=pl.ANY),
        scratch_shapes=[pltpu.VMEM((MOE_TILE * ROW_CHUNKS, LANES), xc.dtype), pltpu.SemaphoreType.DMA(()),
                        pltpu.SemaphoreType.DMA(())],
        compiler_params=_cparams(("arbitrary",)),
        name="moe_dispatch",
    )(dst, off, counts, xc)


def _expert_kernel(texp_ref, nused_ref, xs_ref, w1_ref, w3_ref, w2_ref, ys_ref, w1b, w3b, w2b, last):
    i = pl.program_id(0)
    e = texp_ref[i]

    @pl.when(i == 0)
    def _():
        last[0] = -1

    @pl.when(e != last[0])
    def _():
        w1b[...] = w1_ref[0, 0].astype(BF16)
        w3b[...] = w3_ref[0, 0].astype(BF16)
        w2b[...] = w2_ref[0, 0].astype(BF16)
        last[0] = e

    @pl.when(i < nused_ref[0])
    def _():
        xb = _unpack_rows(_load_chunked(xs_ref, MOE_TILE), BF16)
        h1 = jnp.dot(xb, w1b[...], preferred_element_type=F32)
        h3 = jnp.dot(xb, w3b[...], preferred_element_type=F32)
        h = h1 * jax.nn.sigmoid(h1) * h3
        y = jnp.dot(h.astype(BF16), w2b[...], preferred_element_type=F32)
        _store_chunked(ys_ref, _pack_rows(y))

    @pl.when(i >= nused_ref[0])
    def _():
        ys_ref[...] = jnp.zeros_like(ys_ref)


def moe_experts(xs, texp, nused, w1, w3, w2, layer):
    crows = xs.shape[0]
    d, f = w1.shape[2], w1.shape[3]
    blk = MOE_TILE * ROW_CHUNKS
    nt = crows // blk
    grid_spec = pltpu.PrefetchScalarGridSpec(
        num_scalar_prefetch=2,
        grid=(nt,),
        in_specs=[pl.BlockSpec((blk, LANES), lambda i, te, nu: (jnp.where(i < nu[0], i, 0), 0)),
                  pl.BlockSpec((1, 1, d, f), lambda i, te, nu: (layer, te[i], 0, 0)),
                  pl.BlockSpec((1, 1, d, f), lambda i, te, nu: (layer, te[i], 0, 0)),
                  pl.BlockSpec((1, 1, f, d), lambda i, te, nu: (layer, te[i], 0, 0))],
        out_specs=pl.BlockSpec((blk, LANES), lambda i, te, nu: (i, 0)),
        scratch_shapes=[pltpu.VMEM((d, f), BF16), pltpu.VMEM((d, f), BF16), pltpu.VMEM((f, d), BF16),
                        pltpu.SMEM((1,), I32)],
    )
    return pl.pallas_call(
        _expert_kernel,
        out_shape=jax.ShapeDtypeStruct((crows, LANES), U32),
        grid_spec=grid_spec,
        compiler_params=_cparams(("arbitrary",), 48),
        name="moe_experts",
    )(texp, nused, xs, w1, w3, w2)


CMB_TB = 512


def _combine_kernel(dst_ref, x_ref, wts_ref, ws1_ref, ws3_ref, ws2_ref, lg_ref, lb_ref, ys_ref, o_ref, buf, sem):
    tb = x_ref.shape[0]

    def issue(t, c):
        for k in range(TOPK):
            _tile_copy(ys_ref, dst_ref[k, t], buf.at[k], t * ROW_CHUNKS, sem).start(priority=k % 2)
        return c

    def drain(t, c):
        for k in range(TOPK):
            _tile_copy(ys_ref, 0, buf.at[0], 0, sem).wait()
        return c

    lax.fori_loop(0, tb, issue, 0)
    x = x_ref[...]
    xb = x.astype(BF16)
    h1 = jnp.dot(xb, ws1_ref[...], preferred_element_type=F32)
    h3 = jnp.dot(xb, ws3_ref[...], preferred_element_type=F32)
    y = jnp.dot((h1 * jax.nn.sigmoid(h1) * h3).astype(BF16), ws2_ref[...], preferred_element_type=F32)
    lax.fori_loop(0, tb, drain, 0)
    w = wts_ref[...]
    routed = y
    for k in range(TOPK):
        routed = routed + w[:, k:k + 1] * _unpack_rows(_load_chunked(buf.at[k], tb), F32)
    o_ref[...] = _layer_norm(DN_ALPHA * x + routed, lg_ref[...], lb_ref[...])


def moe_combine(x, ys, dst, wts, ws1, ws3, ws2, ln_g, ln_b):
    n, d = x.shape
    tb = CMB_TB
    smem_blk = pl.BlockSpec((TOPK, tb), lambda i: (0, i), memory_space=pltpu.SMEM)
    const = lambda i: (0, 0)
    return pl.pallas_call(
        _combine_kernel,
        out_shape=jax.ShapeDtypeStruct((n, d), F32),
        grid=(n // tb,),
        in_specs=[smem_blk,
                  pl.BlockSpec((tb, d), lambda i: (i, 0)), pl.BlockSpec((tb, LANES), lambda i: (i, 0)),
                  pl.BlockSpec(ws1.shape, const), pl.BlockSpec(ws3.shape, const), pl.BlockSpec(ws2.shape, const),
                  pl.BlockSpec((1, d), const), pl.BlockSpec((1, d), const),
                  pl.BlockSpec(memory_space=pl.ANY)],
        out_specs=pl.BlockSpec((tb, d), lambda i: (i, 0)),
        scratch_shapes=[pltpu.VMEM((TOPK, tb * ROW_CHUNKS, LANES), U32), pltpu.SemaphoreType.DMA(())],
        compiler_params=_cparams(("parallel",), 48),
        name="moe_combine_ln2",
    )(dst, x, wts, ws1, ws3, ws2, ln_g, ln_b, ys)


def moe_block(x, xp, router_w, router_b, w1, w3, w2, layer, ws1, ws3, ws2, ln_g, ln_b):
    n, d = x.shape
    rw = jnp.pad(router_w, ((0, 0), (0, LANES - N_EXPERTS))).astype(BF16)
    eidx, rnk, wts, cnt = moe_route(x, rw, router_b.reshape(N_EXPERTS, 1))
    counts = cnt[:, 0]
    off, texp, nused = moe_plan(counts, n)
    dst = moe_dst(eidx, rnk, off)
    xs = moe_dispatch(xp, dst, off, counts)
    ys = moe_experts(xs, texp, nused, w1, w3, w2, layer)
    return moe_combine(x, ys, dst, wts, ws1.astype(BF16), ws3.astype(BF16), ws2.astype(BF16),
                       ln_g.reshape(1, -1), ln_b.reshape(1, -1))


def nsa_rope_tables(positions):
    batch, seq = positions.shape
    inv, sgn = _nsa_inv_freq()
    posf = positions.astype(F32)
    tok = rope_tables(posf.reshape(batch * seq, 1), inv, sgn, tm=512)
    end = posf[:, CMP_BLOCK - 1::CMP_STRIDE]
    npad = _n_cmp_pad(seq)
    end = jnp.pad(end, ((0, 0), (0, npad - end.shape[1])))
    cmp = rope_tables(end.reshape(batch * npad, 1), inv, sgn, tm=npad)
    return tok, cmp


def nsa_branch(z, tok_tab, cmp_tab, wk, wv, ovl, *, batch, seq):
    n = batch * seq
    w = NSA_GROUPS * NSA_DH
    xk = z[:, Z_NKV:Z_NKV + w].reshape(n // CMP_HALF, CMP_ROW)
    xv = z[:, Z_NKV + w:Z_NKV + 2 * w].reshape(n // CMP_HALF, CMP_ROW)
    kc, vct = nsa_compress(xk, xv, wk, wv, cmp_tab[0], cmp_tab[1], batch=batch)
    ksa, kw, vswt = nsa_kv_prep(z, tok_tab[0], tok_tab[1], batch=batch, seq=seq)
    return nsa_attention(z, tok_tab[0], tok_tab[1], kc, vct, ksa, kw, vswt, ovl, batch=batch, seq=seq)


def kernel(x, positions, w_in, conv_dw, conv_db, conv_ln_g, conv_ln_b, conv_w_pw, ret_w_o, nsa_pe_k, nsa_w1_k, nsa_w2_k,
           nsa_pe_v, nsa_w1_v, nsa_w2_v, nsa_w_o, w_out, ln1_g, ln1_b, router_w, router_b, moe_w1, moe_w3, moe_w2,
           shared_w1, shared_w3, shared_w2, ln2_g, ln2_b):
    batch, seq, d = x.shape
    n = batch * seq
    xf = x.reshape(n, d)
    row = lambda v: v.reshape(1, -1)

    ret_inv, ret_sgn = _ret_inv_freq()
    ret_tab = rope_tables(positions.astype(F32).reshape(n, 1), ret_inv, ret_sgn, tm=512)
    tok_tab, cmp_tab = nsa_rope_tables(positions)
    ret_consts = _ret_tables()
    ovl = _overlap_matrix(seq)

    for l in range(w_in.shape[0]):
        z = matmul(xf, _layout_w_in(w_in[l]), tm=2048, tn=512, out_dtype=BF16, name="in_proj")
        y_conv = conv_branch(z, conv_dw[l], row(conv_db[l]), row(conv_ln_g[l]), row(conv_ln_b[l]),
                             conv_w_pw[l].astype(BF16), batch=batch, seq=seq)
        o_ret = retention_branch(z, ret_tab[0], ret_tab[1], ret_consts, batch=batch, seq=seq)
        o_nsa = nsa_branch(z, tok_tab, cmp_tab, _cmp_weights(nsa_pe_k[l], nsa_w1_k[l], nsa_w2_k[l]),
                           _cmp_weights(nsa_pe_v[l], nsa_w1_v[l], nsa_w2_v[l]), ovl, batch=batch, seq=seq)
        x1, x1p = merge_block(xf, y_conv, o_ret, o_nsa, z, ret_w_o[l].astype(BF16), nsa_w_o[l].astype(BF16),
                              w_out[l].astype(BF16), row(ln1_g[l]), row(ln1_b[l]))
        xf = moe_block(x1, x1p, router_w[l], router_b[l], moe_w1, moe_w3, moe_w2, l,
                       shared_w1[l], shared_w3[l], shared_w2[l], ln2_g[l], ln2_b[l])
    return xf.reshape(batch, seq, d)
```

```python
import math

import jax
import jax.numpy as jnp
import numpy as np
from jax import lax
from jax.experimental import pallas as pl
from jax.experimental.pallas import tpu as pltpu

F32 = jnp.float32
BF16 = jnp.bfloat16
I32 = jnp.int32

D_MODEL = 1024
DEPTH = 4
CONV_CH = 512
CONV_WIDTH = 31
RET_HEADS = 4
RET_DK = 128
RET_DV = 256
RET_CHUNK = 128
RET_ROPE_BASE = 10000.0
NSA_HEADS = 8
NSA_GROUPS = 2
NSA_HPG = NSA_HEADS // NSA_GROUPS
NSA_DH = 64
NSA_N_BRANCH = 3
CMP_BLOCK = 32
CMP_STRIDE = 16
CMP_HIDDEN = 256
SEL_BLOCK = 64
SEL_TOPK = 16
WINDOW = 512
ROPE_THETA = 500000.0
ROT_DIM = NSA_DH // 4
N_EXPERTS = 64
N_EXPERT_GROUPS = 8
TOPK_GROUPS = 4
TOPK = 8
D_EXPERT = 256
D_SHARED = 256
ROUTED_SCALE = 2.5
DN_ALPHA = (2.0 * DEPTH) ** 0.25
LN_EPS = 1e-5
NEG = -1e30

LANES = 128

Z_MG = 0
Z_CA = 3072
Z_CB = 3584
Z_RQ = 4096
Z_RK = 4608
Z_RV = 5120
Z_RG = 6144
Z_NQ = 7168
Z_NKV = 7680
Z_NG = 8448
Z_W = 8704

_IN_WIDTHS = (512, 512, 512, 512, 1024, 1024, 512, 128, 128, 128, 128, 128, 128, 24, 3072)
_IN_OFFS = tuple(int(v) for v in np.concatenate([[0], np.cumsum(_IN_WIDTHS)[:-1]]))

MOE_TILE = 1024


def _cparams(sem, vmem_mb=None):
    kw = dict(dimension_semantics=sem)
    if vmem_mb is not None:
        kw["vmem_limit_bytes"] = vmem_mb * 1024 * 1024
    return pltpu.CompilerParams(**kw)


def _mm_kernel(x_ref, w_ref, o_ref):
    o_ref[...] = jnp.dot(x_ref[...].astype(BF16), w_ref[...], preferred_element_type=F32).astype(o_ref.dtype)


def matmul(x, w, *, tm, tn, out_dtype, name):
    m, k = x.shape
    n = w.shape[1]
    return pl.pallas_call(
        _mm_kernel,
        out_shape=jax.ShapeDtypeStruct((m, n), out_dtype),
        grid=(m // tm, n // tn),
        in_specs=[pl.BlockSpec((tm, k), lambda i, j: (i, 0)),
                  pl.BlockSpec((k, tn), lambda i, j: (0, j))],
        out_specs=pl.BlockSpec((tm, tn), lambda i, j: (i, j)),
        compiler_params=_cparams(("parallel", "arbitrary"), 48),
        name=name,
    )(x, w)


def _rope_table_kernel(pos_ref, inv_ref, sgn_ref, cos_ref, sin_ref):
    ang = pos_ref[...] * inv_ref[...]
    cos_ref[...] = jnp.cos(ang)
    sin_ref[...] = jnp.sin(ang) * sgn_ref[...]


def rope_tables(pos_col, inv, sgn, *, tm):
    n = pos_col.shape[0]
    w = inv.shape[1]
    return pl.pallas_call(
        _rope_table_kernel,
        out_shape=(jax.ShapeDtypeStruct((n, w), F32), jax.ShapeDtypeStruct((n, w), F32)),
        grid=(n // tm,),
        in_specs=[pl.BlockSpec((tm, 1), lambda i: (i, 0)),
                  pl.BlockSpec((1, w), lambda i: (0, 0)),
                  pl.BlockSpec((1, w), lambda i: (0, 0))],
        out_specs=(pl.BlockSpec((tm, w), lambda i: (i, 0)), pl.BlockSpec((tm, w), lambda i: (i, 0))),
        compiler_params=_cparams(("parallel",)),
        name="rope_tables",
    )(pos_col, inv, sgn)


def _ret_inv_freq():
    inv = 1.0 / jnp.power(jnp.float32(RET_ROPE_BASE), jnp.linspace(0.0, 1.0, RET_DK // 2, dtype=F32))
    inv = jnp.concatenate([inv, inv])[None, :]
    sgn = jnp.concatenate([-jnp.ones((RET_DK // 2,), F32), jnp.ones((RET_DK // 2,), F32)])[None, :]
    return inv, sgn


def _nsa_inv_freq():
    half = ROT_DIM // 2
    inv = jnp.power(jnp.float32(ROPE_THETA), -jnp.arange(0, ROT_DIM, 2, dtype=F32) / ROT_DIM)
    z = jnp.zeros((NSA_DH - ROT_DIM,), F32)
    inv64 = jnp.concatenate([inv, inv, z])
    sgn64 = jnp.concatenate([-jnp.ones((half,), F32), jnp.ones((half,), F32), z])
    return jnp.concatenate([inv64, inv64])[None, :], jnp.concatenate([sgn64, sgn64])[None, :]


def _nsa_rope(x, cos, sin):
    w = x.shape[1]
    half = ROT_DIM // 2
    lane = lax.broadcasted_iota(I32, x.shape, 1) % NSA_DH
    partner = jnp.where(lane < half, pltpu.roll(x, w - half, 1), pltpu.roll(x, half, 1))
    return x * cos + partner * sin


CONV_TT = 512
CONV_HALO = 32


def _conv_kernel(a_ref, b_ref, ah_ref, bh_ref, dw_ref, db_ref, g_ref, be_ref, wpw_ref, o_ref, ubuf, sbuf):
    i = pl.program_id(1)
    tt = a_ref.shape[0]
    u = a_ref[...].astype(F32) * jax.nn.sigmoid(b_ref[...].astype(F32))
    uh = ah_ref[...].astype(F32) * jax.nn.sigmoid(bh_ref[...].astype(F32))
    ubuf[0:CONV_HALO, :] = jnp.where(i > 0, uh, 0.0)
    ubuf[CONV_HALO:CONV_HALO + tt, :] = u
    acc = jnp.zeros((tt, CONV_CH), F32)
    base = CONV_HALO - (CONV_WIDTH - 1)
    sub = 8
    for r in range(sub):
        offs = [o for o in range(base, base + CONV_WIDTH) if o % sub == r]
        if not offs:
            continue
        span = offs[-1] - r + tt
        sbuf[0:span, :] = ubuf[r:r + span, :]
        for o in offs:
            acc = acc + dw_ref[o - base:o - base + 1, :] * sbuf[o - r:o - r + tt, :]
    acc = acc + db_ref[...]
    mu = jnp.mean(acc, axis=-1, keepdims=True)
    var = jnp.mean(jnp.square(acc - mu), axis=-1, keepdims=True)
    y = (acc - mu) * lax.rsqrt(var + LN_EPS) * g_ref[...] + be_ref[...]
    y = y * jax.nn.sigmoid(y)
    o_ref[...] = jnp.dot(y.astype(BF16), wpw_ref[...], preferred_element_type=F32).astype(o_ref.dtype)


def conv_branch(z, dw, db, ln_g, ln_b, w_pw_bf, *, batch, seq):
    tt = CONV_TT
    nt = seq // tt
    r = tt // CONV_HALO
    ca, cb = Z_CA // CONV_CH, Z_CB // CONV_CH

    def halo_map(col):
        return lambda b, i: (jnp.maximum((b * nt + i) * r - 1, 0), col)

    return pl.pallas_call(
        _conv_kernel,
        out_shape=jax.ShapeDtypeStruct((batch * seq, D_MODEL), BF16),
        grid=(batch, nt),
        in_specs=[pl.BlockSpec((tt, CONV_CH), lambda b, i: (b * nt + i, ca)),
                  pl.BlockSpec((tt, CONV_CH), lambda b, i: (b * nt + i, cb)),
                  pl.BlockSpec((CONV_HALO, CONV_CH), halo_map(ca)),
                  pl.BlockSpec((CONV_HALO, CONV_CH), halo_map(cb)),
                  pl.BlockSpec((CONV_WIDTH, CONV_CH), lambda b, i: (0, 0)),
                  pl.BlockSpec((1, CONV_CH), lambda b, i: (0, 0)),
                  pl.BlockSpec((1, CONV_CH), lambda b, i: (0, 0)),
                  pl.BlockSpec((1, CONV_CH), lambda b, i: (0, 0)),
                  pl.BlockSpec((CONV_CH, D_MODEL), lambda b, i: (0, 0))],
        out_specs=pl.BlockSpec((tt, D_MODEL), lambda b, i: (b * nt + i, 0)),
        scratch_shapes=[pltpu.VMEM((CONV_HALO + tt, CONV_CH), F32), pltpu.VMEM((CONV_HALO + tt, CONV_CH), F32)],
        compiler_params=_cparams(("parallel", "parallel")),
        name="conv_branch",
    )(z, z, z, z, dw, db, ln_g, ln_b, w_pw_bf)


RET_TQ = 1024


def _ret_tables():
    h, c = RET_HEADS, RET_CHUNK
    log_gamma = jnp.log1p(-jnp.exp2(-5.0 - jnp.arange(h, dtype=F32)))
    idx = jnp.arange(c, dtype=F32)
    diff = idx[:, None] - idx[None, :]
    dmat = jnp.where(diff >= 0, jnp.exp(log_gamma[:, None, None] * jnp.maximum(diff, 0.0)), 0.0).astype(F32)
    xi = jnp.exp(log_gamma[:, None] * (idx + 1.0)).astype(F32)
    zeta = jnp.exp(log_gamma[:, None] * (c - 1.0 - idx)).astype(F32)
    decay = jnp.exp(log_gamma * c).astype(F32)
    xi_b = jnp.broadcast_to(xi[:, :, None], (h, c, RET_DV))
    zeta_b = jnp.broadcast_to(zeta[:, :, None], (h, c, RET_DV))
    decay_b = jnp.broadcast_to(decay[:, None, None], (h, RET_DK, RET_DV))
    return dmat, xi_b, zeta_b, decay_b


def _ret_kernel(q_ref, k_ref, v_ref, g_ref, cos_ref, sin_ref, dmat_ref, xi_ref, zeta_ref, dec_ref, o_ref, r_ref):
    @pl.when(pl.program_id(1) == 0)
    def _():
        r_ref[...] = jnp.zeros_like(r_ref)

    c = RET_CHUNK
    n_chunks = q_ref.shape[0] // c
    for ci in range(n_chunks):
        rows = slice(ci * c, (ci + 1) * c)
        cos = cos_ref[rows, :]
        sin = sin_ref[rows, :]
        for h in range(RET_HEADS):
            qk_cols = slice(h * RET_DK, (h + 1) * RET_DK)
            v_cols = slice(h * RET_DV, (h + 1) * RET_DV)
            q = q_ref[rows, qk_cols].astype(F32)
            k = k_ref[rows, qk_cols].astype(F32)
            q = q * cos + pltpu.roll(q, RET_DK // 2, 1) * sin
            k = (k * cos + pltpu.roll(k, RET_DK // 2, 1) * sin) * (RET_DK ** -0.5)
            v = v_ref[rows, v_cols].astype(F32)
            qb = q.astype(BF16)
            kb = k.astype(BF16)
            inner = lax.dot_general(qb, kb, (((1,), (1,)), ((), ())), preferred_element_type=F32) * dmat_ref[h]
            r_old = r_ref[h]
            o = (jnp.dot(inner.astype(BF16), v.astype(BF16), preferred_element_type=F32)
                 + jnp.dot(qb, r_old.astype(BF16), preferred_element_type=F32) * xi_ref[h])
            vz = (v * zeta_ref[h]).astype(BF16)
            r_ref[h] = r_old * dec_ref[h] + jnp.dot(k.T.astype(BF16), vz, preferred_element_type=F32)
            mu = jnp.mean(o, axis=-1, keepdims=True)
            var = jnp.mean(jnp.square(o - mu), axis=-1, keepdims=True)
            on = (o - mu) * lax.rsqrt(var + LN_EPS)
            g = g_ref[rows, v_cols].astype(F32)
            o_ref[rows, v_cols] = (g * jax.nn.sigmoid(g) * on).astype(o_ref.dtype)


def retention_branch(z, ret_cos, ret_sin, tables, *, batch, seq):
    tq = RET_TQ
    nt = seq // tq
    dmat, xi_b, zeta_b, decay_b = tables
    qw = RET_HEADS * RET_DK
    vw = RET_HEADS * RET_DV
    row = lambda b, i: b * nt + i
    full3 = lambda b, i: (0, 0, 0)
    return pl.pallas_call(
        _ret_kernel,
        out_shape=jax.ShapeDtypeStruct((batch * seq, vw), BF16),
        grid=(batch, nt),
        in_specs=[pl.BlockSpec((tq, qw), lambda b, i: (row(b, i), Z_RQ // qw)),
                  pl.BlockSpec((tq, qw), lambda b, i: (row(b, i), Z_RK // qw)),
                  pl.BlockSpec((tq, vw), lambda b, i: (row(b, i), Z_RV // vw)),
                  pl.BlockSpec((tq, vw), lambda b, i: (row(b, i), Z_RG // vw)),
                  pl.BlockSpec((tq, RET_DK), lambda b, i: (row(b, i), 0)),
                  pl.BlockSpec((tq, RET_DK), lambda b, i: (row(b, i), 0)),
                  pl.BlockSpec(dmat.shape, full3),
                  pl.BlockSpec(xi_b.shape, full3),
                  pl.BlockSpec(zeta_b.shape, full3),
                  pl.BlockSpec(decay_b.shape, full3)],
        out_specs=pl.BlockSpec((tq, vw), lambda b, i: (row(b, i), 0)),
        scratch_shapes=[pltpu.VMEM((RET_HEADS, RET_DK, RET_DV), F32)],
        compiler_params=_cparams(("parallel", "arbitrary"), 48),
        name="retention",
    )(z, z, z, z, ret_cos, ret_sin, dmat, xi_b, zeta_b, decay_b)


def _layout_w_in(w):
    seg = lambda i: w[:, _IN_OFFS[i]:_IN_OFFS[i] + _IN_WIDTHS[i]]
    order = (14, 0, 1, 2, 3, 4, 5, 6, 7, 8, 9, 10, 11, 12, 13)
    parts = [seg(i) for i in order]
    used = sum(_IN_WIDTHS)
    parts.append(jnp.zeros((w.shape[0], Z_W - used), w.dtype))
    return jnp.concatenate(parts, axis=1).astype(BF16)


CMP_HALF = CMP_BLOCK // 2
CMP_ROW = CMP_HALF * NSA_GROUPS * NSA_DH


def _n_cmp_pad(seq):
    return seq // CMP_STRIDE


def _cmp_weights(pe, w1, w2):
    g = NSA_GROUPS
    eye = jnp.eye(g, dtype=F32)
    w = w1.reshape(2, CMP_HALF, NSA_DH, CMP_HIDDEN)
    w1ab = jnp.einsum("hldf,gk->hlgdkf", w, eye).reshape(2, CMP_ROW, g * CMP_HIDDEN).astype(BF16)
    peab = jnp.broadcast_to(pe.reshape(2, CMP_HALF, 1, NSA_DH), (2, CMP_HALF, g, NSA_DH)).reshape(2, 1, CMP_ROW)
    w2bd = jnp.einsum("fd,gk->gfkd", w2, eye).reshape(g * CMP_HIDDEN, g * NSA_DH).astype(BF16)
    return peab, w1ab, w2bd


def _cmp_mlp(x, pe_ref, w1_ref, w2_ref):
    x = x.astype(F32)
    a = jnp.dot((x + pe_ref[0]).astype(BF16), w1_ref[0], preferred_element_type=F32)
    b = jnp.dot((x + pe_ref[1]).astype(BF16), w1_ref[1], preferred_element_type=F32)
    hid = a + pltpu.roll(b, b.shape[0] - 1, 0)
    hid = hid * jax.nn.sigmoid(hid)
    return jnp.dot(hid.astype(BF16), w2_ref[...], preferred_element_type=F32)


def _compress_kernel(xk_ref, xv_ref, pek_ref, w1k_ref, w2k_ref, pev_ref, w1v_ref, w2v_ref, cos_ref, sin_ref,
                     k_ref, vt_ref):
    k = _cmp_mlp(xk_ref[...], pek_ref, w1k_ref, w2k_ref)
    k_ref[0] = _nsa_rope(k, cos_ref[...], sin_ref[...]).astype(BF16)
    vt_ref[0] = _cmp_mlp(xv_ref[...], pev_ref, w1v_ref, w2v_ref).T.astype(BF16)


def nsa_compress(xk, xv, wk, wv, cmp_cos, cmp_sin, *, batch):
    pek, w1k, w2k = wk
    pev, w1v, w2v = wv
    npad = xk.shape[0] // batch
    gw = NSA_GROUPS * NSA_DH
    c3 = lambda b: (0, 0, 0)
    c2 = lambda b: (0, 0)
    return pl.pallas_call(
        _compress_kernel,
        out_shape=(jax.ShapeDtypeStruct((batch, npad, gw), BF16), jax.ShapeDtypeStruct((batch, gw, npad), BF16)),
        grid=(batch,),
        in_specs=[pl.BlockSpec((npad, CMP_ROW), lambda b: (b, 0)),
                  pl.BlockSpec((npad, CMP_ROW), lambda b: (b, 0)),
                  pl.BlockSpec(pek.shape, c3), pl.BlockSpec(w1k.shape, c3), pl.BlockSpec(w2k.shape, c2),
                  pl.BlockSpec(pev.shape, c3), pl.BlockSpec(w1v.shape, c3), pl.BlockSpec(w2v.shape, c2),
                  pl.BlockSpec((npad, gw), lambda b: (b, 0)),
                  pl.BlockSpec((npad, gw), lambda b: (b, 0))],
        out_specs=(pl.BlockSpec((1, npad, gw), lambda b: (b, 0, 0)), pl.BlockSpec((1, gw, npad), lambda b: (b, 0, 0))),
        compiler_params=_cparams(("parallel",), 48),
        name="nsa_compress",
    )(xk, xv, pek, w1k, w2k, pev, w1v, w2v, cmp_cos, cmp_sin)


KV_TT = 512
ATT_TK = 512
ATT_KB = 256
ATT_AHEAD = 8
LOG2E = math.log2(math.e)


def _kv_prep_kernel(z_ref, cos_ref, sin_ref, ksa_ref, kw_ref, vswt_ref):
    tt = z_ref.shape[0]
    cos = cos_ref[...]
    sin = sin_ref[...]
    w = NSA_GROUPS * NSA_DH
    n_sel = ksa_ref.shape[3] - NSA_DH
    ks = _nsa_rope(z_ref[:, 2 * w:3 * w].astype(F32), cos, sin)
    vs = z_ref[:, 3 * w:4 * w].astype(F32)
    kw = _nsa_rope(z_ref[:, 4 * w:5 * w].astype(F32), cos, sin)
    vw = z_ref[:, 5 * w:6 * w].astype(F32)
    blk = (pl.program_id(1) * tt + lax.broadcasted_iota(I32, (tt, n_sel), 0)) // SEL_BLOCK
    onehot = jnp.where(blk == lax.broadcasted_iota(I32, (tt, n_sel), 1), 1.0, 0.0)
    for g in range(NSA_GROUPS):
        cols = slice(g * NSA_DH, (g + 1) * NSA_DH)
        ksa_ref[0, g] = jnp.concatenate([ks[:, cols], onehot], axis=1).astype(BF16)
        kw_ref[0, g] = kw[:, cols].astype(BF16)
        vswt_ref[0, g] = jnp.concatenate([vs[:, cols], vw[:, cols]], axis=1).T.astype(BF16)


def nsa_kv_prep(z, nsa_cos, nsa_sin, *, batch, seq):
    tt = KV_TT
    nt = seq // tt
    g = NSA_GROUPS
    w = g * NSA_DH
    n_sel = seq // SEL_BLOCK
    return pl.pallas_call(
        _kv_prep_kernel,
        out_shape=(jax.ShapeDtypeStruct((batch, g, seq, NSA_DH + n_sel), BF16),
                   jax.ShapeDtypeStruct((batch, g, seq, NSA_DH), BF16),
                   jax.ShapeDtypeStruct((batch, g, w, seq), BF16)),
        grid=(batch, nt),
        in_specs=[pl.BlockSpec((tt, 6 * w), lambda b, i: (b * nt + i, Z_NKV // (6 * w))),
                  pl.BlockSpec((tt, w), lambda b, i: (b * nt + i, 0)),
                  pl.BlockSpec((tt, w), lambda b, i: (b * nt + i, 0))],
        out_specs=(pl.BlockSpec((1, g, tt, NSA_DH + n_sel), lambda b, i: (b, 0, i, 0)),
                   pl.BlockSpec((1, g, tt, NSA_DH), lambda b, i: (b, 0, i, 0)),
                   pl.BlockSpec((1, g, w, tt), lambda b, i: (b, 0, 0, i))),
        compiler_params=_cparams(("parallel", "parallel")),
        name="nsa_kv_prep",
    )(z, nsa_cos, nsa_sin)


ATT_TQ = 256


def _overlap_matrix(seq):
    n_cmp = (seq - CMP_BLOCK) // CMP_STRIDE + 1
    n_sel = seq // SEL_BLOCK
    ii = np.arange(_n_cmp_pad(seq))[None, :]
    jj = np.arange(LANES)[:, None]
    lo = np.maximum(ii * CMP_STRIDE, jj * SEL_BLOCK)
    hi = np.minimum(ii * CMP_STRIDE + CMP_BLOCK, (jj + 1) * SEL_BLOCK)
    ov = np.maximum(hi - lo, 0).astype(np.float32) / CMP_BLOCK
    ov = np.where((ii < n_cmp) & (jj < n_sel), ov, 0.0)
    return jnp.asarray(np.tile(ov, (1, NSA_HPG)), BF16)


def _softmax_step(s, ok, vt, cols, m_ref, l_ref, a_ref):
    m_old = m_ref[:, cols]
    m_new = jnp.maximum(m_old, jnp.max(s, axis=0, keepdims=True))
    alpha = jnp.exp2(m_old - m_new)
    p = jnp.exp2(s - m_new)
    if ok is not None:
        p = jnp.where(ok, p, 0.0)
    l_ref[:, cols] = alpha * l_ref[:, cols] + jnp.sum(p, axis=0, keepdims=True)
    a_ref[:, cols] = alpha * a_ref[:, cols] + jnp.dot(vt, p.astype(BF16), preferred_element_type=F32)
    m_ref[:, cols] = m_new


def _att_kernel(q_ref, cos_ref, sin_ref, gl_ref, kc_ref, vct_ref, ksa_ref, kw_ref, vswt_ref, ovl_ref, o_ref,
                qa_s, m_sel, l_sel, a_sel, m_win, l_win, a_win):
    g = pl.program_id(1)
    qi = pl.program_id(2)
    tq = ATT_TQ
    tk = ATT_TK
    hpg = NSA_HPG
    dh = NSA_DH
    t0 = qi * tq
    seq = ksa_ref.shape[2]
    n_sel = ksa_ref.shape[3] - dh
    heads = [slice(h * tq, (h + 1) * tq) for h in range(hpg)]

    cos = jnp.concatenate([cos_ref[...]] * (hpg // 2), axis=1)
    sin = jnp.concatenate([sin_ref[...]] * (hpg // 2), axis=1)
    qt = (_nsa_rope(q_ref[...].astype(F32), cos, sin) * (dh ** -0.5 * LOG2E)).T
    q_t = jnp.concatenate([qt[h * dh:(h + 1) * dh] for h in range(hpg)], axis=1)
    q_tb = q_t.astype(BF16)
    qa_s[0:dh, :] = q_tb

    zero = jnp.zeros_like(q_tb)
    q2 = jnp.where(g == 0, jnp.concatenate([q_tb, zero], axis=0), jnp.concatenate([zero, q_tb], axis=0))
    s = jnp.dot(kc_ref[0], q2, preferred_element_type=F32)
    n_cmp = (seq - CMP_BLOCK) // CMP_STRIDE + 1
    nrow = lax.broadcasted_iota(I32, s.shape, 0)
    tcol = t0 + lax.broadcasted_iota(I32, s.shape, 1) % tq
    ok = (nrow * CMP_STRIDE + (CMP_BLOCK - 1) <= tcol) & (nrow < n_cmp)
    sm = jnp.where(ok, s, NEG)
    e = jnp.where(ok, jnp.exp2(sm - jnp.max(sm, axis=0, keepdims=True)), 0.0)
    den = jnp.sum(e, axis=0, keepdims=True)
    pb = (e / jnp.where(den > 0.0, den, 1.0)).astype(BF16)
    oc = jnp.dot(vct_ref[0], pb, preferred_element_type=F32)
    o_cmp = jnp.where(g == 0, oc[0:dh], oc[dh:2 * dh])
    pcat = jnp.concatenate([pb[:, hs] for hs in heads], axis=0)
    imp_t = jnp.dot(ovl_ref[...], pcat, preferred_element_type=F32)[0:n_sel]

    jrow = lax.broadcasted_iota(I32, (n_sel, tq), 0)
    cur = (t0 + lax.broadcasted_iota(I32, (n_sel, tq), 1)) // SEL_BLOCK
    forced = (jrow == 0) | (jrow == cur) | (jrow == cur - 1)
    imp_t = jnp.where(forced, jnp.inf, jnp.where(jrow <= cur, imp_t, -jnp.inf))
    sub = 8
    groups = [imp_t[r:r + sub] for r in range(0, n_sel, sub)]
    ranks = [jnp.zeros((sub, tq), F32) for _ in groups]
    srow = lax.broadcasted_iota(I32, (sub, tq), 0)
    for i in range(n_sel):
        gi, si = divmod(i, sub)
        ri = groups[gi][si:si + 1, :]
        for gj, v in enumerate(groups):
            ge = jnp.where(ri >= v, 1.0, 0.0)
            if gj > gi:
                ranks[gj] = ranks[gj] + ge
            else:
                gt = jnp.where(ri > v, 1.0, 0.0)
                ranks[gj] = ranks[gj] + (gt if gj < gi else jnp.where(srow > si, ge, gt))
    rank = jnp.concatenate(ranks, axis=0)
    bias_t = jnp.where(rank < float(SEL_TOPK), 0.0, NEG).astype(BF16)
    qa_s[dh:dh + n_sel, :] = jnp.concatenate([bias_t] * hpg, axis=1)

    m_sel[...] = jnp.full(m_sel.shape, NEG, F32)
    l_sel[...] = jnp.zeros(l_sel.shape, F32)
    a_sel[...] = jnp.zeros(a_sel.shape, F32)

    kb = ATT_KB
    assert tk == 2 * tq and tq % kb == 0
    krow = lax.broadcasted_iota(I32, (kb, tq), 0)
    qtime = t0 + lax.broadcasted_iota(I32, (kb, tq), 1)

    def run_blocks(blocks):
        sc = [blk[0]() for blk in blocks[:ATT_AHEAD]]
        for b, (_, mask, vt, hs, refs) in enumerate(blocks):
            if b + ATT_AHEAD < len(blocks):
                sc.append(blocks[b + ATT_AHEAD][0]())
            ok = mask()
            s_b = sc[b] if ok is None else jnp.where(ok, sc[b], NEG)
            sc[b] = None
            _softmax_step(s_b, ok, vt(), hs, *refs)

    def sel_blocks(k0, n_keys, causal):
        def block(kk, hs):
            return (lambda: jnp.dot(ksa_ref[0, 0, pl.ds(k0 + kk, kb), :], qa_s[:, hs], preferred_element_type=F32),
                    (lambda: (k0 + kk + krow) <= qtime) if causal else (lambda: None),
                    lambda: vswt_ref[0, 0, 0:dh, pl.ds(k0 + kk, kb)], hs, (m_sel, l_sel, a_sel))
        return [block(kk, hs) for kk in range(0, n_keys, kb) for hs in heads]

    def sel_body(j, carry):
        run_blocks(sel_blocks(pl.multiple_of(j * tk, tk), tk, False))
        return carry

    n_full = t0 // tk
    lax.fori_loop(0, n_full, sel_body, 0)

    @pl.when(n_full * tk < t0)
    def _():
        run_blocks(sel_blocks(pl.multiple_of(n_full * tk, tk), tk - tq, False))

    kw0 = pl.multiple_of(jnp.maximum(t0 - WINDOW, 0), tq)

    def win_block(kk, hs):
        def mask():
            dist = qtime - (kw0 + kk + krow)
            return (dist >= 0) & (dist < WINDOW)
        return (lambda: jnp.dot(kw_ref[0, 0, pl.ds(kw0 + kk, kb), :], qa_s[0:dh, hs], preferred_element_type=F32),
                mask, lambda: vswt_ref[0, 0, dh:2 * dh, pl.ds(kw0 + kk, kb)], hs, (m_win, l_win, a_win))

    m_win[...] = jnp.full(m_win.shape, NEG, F32)
    l_win[...] = jnp.zeros(l_win.shape, F32)
    a_win[...] = jnp.zeros(a_win.shape, F32)
    win = [win_block(kk, hs) for kk in range(0, WINDOW + tq, kb) for hs in heads]
    diag = sel_blocks(pl.multiple_of(t0, tq), tq, True)
    per = len(win) // len(diag)
    mixed = []
    for i, blk in enumerate(diag):
        mixed += win[i * per:(i + 1) * per] + [blk]
    run_blocks(mixed + win[len(diag) * per:])

    o_win = a_win[...] / l_win[...]
    o_slc = a_sel[...] / l_sel[...]
    gl_t = jax.nn.sigmoid(gl_ref[...].astype(F32)).T
    nb = NSA_N_BRANCH
    outs = []
    for h, hs in enumerate(heads):
        gate = lambda br: jnp.where(g == 0, gl_t[nb * h + br:nb * h + br + 1],
                                    gl_t[nb * (hpg + h) + br:nb * (hpg + h) + br + 1])
        outs.append(gate(0) * o_cmp[:, hs] + gate(1) * o_slc[:, hs] + gate(2) * o_win[:, hs])
    o_ref[...] = jnp.concatenate(outs, axis=0).T.astype(o_ref.dtype)


def nsa_attention(z, nsa_cos, nsa_sin, kc, vct, ksa, kw, vswt, ovl, *, batch, seq):
    tq = ATT_TQ
    nt = seq // tq
    g = NSA_GROUPS
    gw = g * NSA_DH
    qw = NSA_HPG * NSA_DH
    m_cols = NSA_HPG * tq
    row = lambda b, gg, i: b * nt + i
    per_bg = lambda a: pl.BlockSpec((1, 1) + a.shape[2:], lambda b, gg, i: (b, gg, 0, 0))
    stat = pltpu.VMEM((1, m_cols), F32)
    vals = pltpu.VMEM((NSA_DH, m_cols), F32)
    return pl.pallas_call(
        _att_kernel,
        out_shape=jax.ShapeDtypeStruct((batch * seq, g * qw), BF16),
        grid=(batch, g, nt),
        in_specs=[pl.BlockSpec((tq, qw), lambda b, gg, i: (row(b, gg, i), Z_NQ // qw + gg)),
                  pl.BlockSpec((tq, gw), lambda b, gg, i: (row(b, gg, i), 0)),
                  pl.BlockSpec((tq, gw), lambda b, gg, i: (row(b, gg, i), 0)),
                  pl.BlockSpec((tq, LANES), lambda b, gg, i: (row(b, gg, i), Z_NG // LANES)),
                  pl.BlockSpec((1,) + kc.shape[1:], lambda b, gg, i: (b, 0, 0)),
                  pl.BlockSpec((1,) + vct.shape[1:], lambda b, gg, i: (b, 0, 0)),
                  per_bg(ksa), per_bg(kw), per_bg(vswt),
                  pl.BlockSpec(ovl.shape, lambda b, gg, i: (0, 0))],
        out_specs=pl.BlockSpec((tq, qw), lambda b, gg, i: (row(b, gg, i), gg)),
        scratch_shapes=[pltpu.VMEM((ksa.shape[3], m_cols), BF16), stat, stat, vals, stat, stat, vals],
        compiler_params=_cparams(("parallel", "parallel", "parallel"), 48),
        name="nsa_attention",
    )(z, nsa_cos, nsa_sin, z, kc, vct, ksa, kw, vswt, ovl)


MERGE_TM = 256


def _layer_norm(y, g, b):
    mu = jnp.mean(y, axis=-1, keepdims=True)
    var = jnp.mean(jnp.square(y - mu), axis=-1, keepdims=True)
    return (y - mu) * lax.rsqrt(var + LN_EPS) * g + b


def _merge_kernel(x_ref, yc_ref, or_ref, on_ref, g0_ref, g1_ref, g2_ref, wr_ref, wn_ref, wo_ref, lg_ref, lb_ref, o_ref,
                  op_ref):
    y_ret = jnp.dot(or_ref[...].astype(BF16), wr_ref[...], preferred_element_type=F32)
    y_nsa = jnp.dot(on_ref[...].astype(BF16), wn_ref[...], preferred_element_type=F32)
    gate = lambda ref: jax.nn.sigmoid(ref[...].astype(F32))
    m = gate(g0_ref) * yc_ref[...].astype(F32) + gate(g1_ref) * y_ret + gate(g2_ref) * y_nsa
    h = jnp.dot(m.astype(BF16), wo_ref[...], preferred_element_type=F32)
    y = _layer_norm(DN_ALPHA * x_ref[...] + h, lg_ref[...], lb_ref[...])
    o_ref[...] = y
    _store_chunked(op_ref, _pack_rows(y))


def merge_block(x, y_conv, o_ret, o_nsa, z, ret_w_o, nsa_w_o, w_out, ln_g, ln_b):
    n = x.shape[0]
    tm = MERGE_TM
    d = D_MODEL
    rowd = lambda i: (i, 0)
    const = lambda i: (0, 0)
    return pl.pallas_call(
        _merge_kernel,
        out_shape=(jax.ShapeDtypeStruct((n, d), F32), jax.ShapeDtypeStruct((n * ROW_CHUNKS, LANES), U32)),
        grid=(n // tm,),
        in_specs=[pl.BlockSpec((tm, d), rowd), pl.BlockSpec((tm, d), rowd),
                  pl.BlockSpec((tm, o_ret.shape[1]), rowd), pl.BlockSpec((tm, o_nsa.shape[1]), rowd),
                  pl.BlockSpec((tm, d), lambda i: (i, Z_MG // d)),
                  pl.BlockSpec((tm, d), lambda i: (i, Z_MG // d + 1)),
                  pl.BlockSpec((tm, d), lambda i: (i, Z_MG // d + 2)),
                  pl.BlockSpec(ret_w_o.shape, const), pl.BlockSpec(nsa_w_o.shape, const), pl.BlockSpec(w_out.shape, const),
                  pl.BlockSpec((1, d), const), pl.BlockSpec((1, d), const)],
        out_specs=(pl.BlockSpec((tm, d), rowd), pl.BlockSpec((tm * ROW_CHUNKS, LANES), rowd)),
        compiler_params=_cparams(("parallel",), 48),
        name="merge_ln1",
    )(x, y_conv, o_ret, o_nsa, z, z, z, ret_w_o, nsa_w_o, w_out, ln_g, ln_b)


RT_TM = 512


def _stable_rank(v):
    n = v.shape[0]
    row = lax.broadcasted_iota(I32, v.shape, 0)
    rank = jnp.zeros(v.shape, F32)
    for i in range(n):
        r = v[i:i + 1, :]
        rank = rank + jnp.where(row > i, jnp.where(r >= v, 1.0, 0.0), jnp.where(r > v, 1.0, 0.0))
    return rank


def _router_kernel(x_ref, wr_ref, b_ref, ltri_ref, utri_ref, eidx_ref, rnk_ref, wts_ref, cnt_ref, carry):
    @pl.when(pl.program_id(0) == 0)
    def _():
        carry[...] = jnp.zeros_like(carry)

    tm = x_ref.shape[0]
    ne = N_EXPERTS
    per = ne // N_EXPERT_GROUPS
    logits = jnp.dot(x_ref[...].astype(BF16), wr_ref[...], preferred_element_type=F32)
    s = jax.nn.sigmoid(logits.T[0:ne])
    sb = s + b_ref[...]
    sub = lax.broadcasted_iota(I32, (per, tm), 0)
    gscore = []
    for gi in range(N_EXPERT_GROUPS):
        v = sb[gi * per:(gi + 1) * per]
        m1 = jnp.max(v, axis=0, keepdims=True)
        first = jnp.min(jnp.where(v == m1, sub, per), axis=0, keepdims=True)
        m2 = jnp.max(jnp.where(sub == first, -jnp.inf, v), axis=0, keepdims=True)
        gscore.append(m1 + m2)
    gscore = jnp.concatenate(gscore, axis=0)
    gkeep = jnp.where(_stable_rank(gscore) < float(TOPK_GROUPS), 1.0, 0.0)
    ekeep = jnp.concatenate([jnp.broadcast_to(gkeep[gi:gi + 1], (per, tm)) for gi in range(N_EXPERT_GROUPS)], axis=0)
    sel = jnp.where(_stable_rank(jnp.where(ekeep > 0.0, sb, -jnp.inf)) < float(TOPK), 1.0, 0.0)
    ssel = s * sel
    gate = ssel / jnp.sum(ssel, axis=0, keepdims=True) * ROUTED_SCALE

    selb = sel.astype(BF16)
    slot = jnp.dot(ltri_ref[...], selb, preferred_element_type=F32)
    incl = jnp.dot(selb, utri_ref[...], preferred_element_type=F32)
    rnk = carry[...] + incl - 1.0
    carry[...] = carry[...] + incl[:, tm - 1:tm]
    erow = lax.broadcasted_iota(I32, (ne, tm), 0).astype(F32)
    es, rs, ws = [], [], []
    for k in range(TOPK):
        pick = jnp.where(slot == float(k), sel, 0.0)
        es.append(jnp.sum(pick * erow, axis=0, keepdims=True))
        rs.append(jnp.sum(pick * rnk, axis=0, keepdims=True))
        ws.append(jnp.sum(pick * gate, axis=0, keepdims=True))
    eidx_ref[...] = jnp.concatenate(es, axis=0).astype(I32)
    rnk_ref[...] = jnp.concatenate(rs, axis=0).astype(I32)
    wts_ref[...] = jnp.concatenate(ws + [jnp.zeros((LANES - TOPK, tm), F32)], axis=0).T
    cnt_ref[...] = jnp.broadcast_to(carry[...], cnt_ref.shape).astype(I32)


def moe_route(x, router_w_pad, router_b_col):
    n = x.shape[0]
    tm = RT_TM
    ne = N_EXPERTS
    ltri = jnp.asarray(np.tril(np.ones((ne, ne), np.float32), -1), BF16)
    utri = jnp.asarray(np.triu(np.ones((tm, tm), np.float32)), BF16)
    const = lambda i: (0, 0)
    return pl.pallas_call(
        _router_kernel,
        out_shape=(jax.ShapeDtypeStruct((TOPK, n), I32), jax.ShapeDtypeStruct((TOPK, n), I32),
                   jax.ShapeDtypeStruct((n, LANES), F32), jax.ShapeDtypeStruct((ne, LANES), I32)),
        grid=(n // tm,),
        in_specs=[pl.BlockSpec((tm, D_MODEL), lambda i: (i, 0)),
                  pl.BlockSpec(router_w_pad.shape, const), pl.BlockSpec((ne, 1), const),
                  pl.BlockSpec((ne, ne), const), pl.BlockSpec((tm, tm), const)],
        out_specs=(pl.BlockSpec((TOPK, tm), lambda i: (0, i)), pl.BlockSpec((TOPK, tm), lambda i: (0, i)),
                   pl.BlockSpec((tm, LANES), lambda i: (i, 0)), pl.BlockSpec((ne, LANES), const)),
        scratch_shapes=[pltpu.VMEM((ne, 1), F32)],
        compiler_params=_cparams(("arbitrary",)),
        name="moe_route",
    )(x, router_w_pad, router_b_col, ltri, utri)


def _moe_rows(n_tokens):
    return n_tokens * TOPK + N_EXPERTS * MOE_TILE


def _plan_kernel(cnt_ref, off_ref, texp_ref, nused_ref):
    shift = MOE_TILE.bit_length() - 1

    def per_expert(e, carry):
        off, ti = carry
        off_ref[e] = off
        ntile = lax.shift_right_logical(cnt_ref[e] + (MOE_TILE - 1), shift)

        def mark(j, c):
            texp_ref[ti + j] = e
            return c

        lax.fori_loop(0, ntile, mark, 0)
        return off + ntile * MOE_TILE, ti + ntile

    _, used = lax.fori_loop(0, N_EXPERTS, per_expert, (jnp.int32(0), jnp.int32(0)))
    nused_ref[0] = used

    def fill(j, c):
        texp_ref[j] = N_EXPERTS - 1
        return c

    lax.fori_loop(used, texp_ref.shape[0], fill, 0)


def moe_plan(counts, n_tokens):
    nt = _moe_rows(n_tokens) // MOE_TILE
    smem = pl.BlockSpec(memory_space=pltpu.SMEM)
    return pl.pallas_call(
        _plan_kernel,
        out_shape=(jax.ShapeDtypeStruct((N_EXPERTS,), I32), jax.ShapeDtypeStruct((nt,), I32),
                   jax.ShapeDtypeStruct((1,), I32)),
        in_specs=[smem],
        out_specs=(smem, smem, smem),
        name="moe_plan",
    )(counts)


DSP_TB = 512


U32 = jnp.uint32
ROW_WORDS = D_MODEL // 2
ROW_CHUNKS = ROW_WORDS // LANES
assert ROW_CHUNKS == 4


def _pack_rows(y):
    half = y.shape[1] // 2
    bits = lambda v: lax.bitcast_convert_type(v.astype(jnp.bfloat16).astype(F32), U32)
    return bits(y[:, :half]) | lax.shift_right_logical(bits(y[:, half:]), jnp.uint32(16))


def _unpack_rows(words, dtype):
    hi = lax.bitcast_convert_type(words & jnp.uint32(0xFFFF0000), F32)
    lo = lax.bitcast_convert_type(lax.shift_left(words, jnp.uint32(16)), F32)
    return jnp.concatenate([hi, lo], axis=1).astype(dtype)


def _load_chunked(ref, rows):
    return jnp.concatenate([ref[pl.ds(s, rows, stride=ROW_CHUNKS), :] for s in range(ROW_CHUNKS)], axis=1)


def _store_chunked(ref, words):
    rows = words.shape[0]
    for s in range(ROW_CHUNKS):
        ref[pl.ds(s, rows, stride=ROW_CHUNKS), :] = words[:, s * LANES:(s + 1) * LANES]


def _dst_kernel(off_ref, eidx_ref, rnk_ref, dst_ref):
    e = eidx_ref[...]
    base = jnp.zeros(e.shape, I32)
    for x in range(N_EXPERTS):
        base = jnp.where(e == x, off_ref[x], base)
    dst_ref[...] = (base + rnk_ref[...]) * ROW_CHUNKS


def moe_dst(eidx, rnk, off):
    k, n = eidx.shape
    tb = 2048
    blk = pl.BlockSpec((k, tb), lambda i: (0, i))
    return pl.pallas_call(
        _dst_kernel,
        out_shape=jax.ShapeDtypeStruct((k, n), I32),
        grid=(n // tb,),
        in_specs=[pl.BlockSpec(memory_space=pltpu.SMEM), blk, blk],
        out_specs=blk,
        compiler_params=_cparams(("parallel",)),
        name="moe_dst",
    )(off, eidx, rnk)


def _tile_copy(src_ref, src_row, dst_ref, dst_row, sem):
    return pltpu.make_async_copy(src_ref.at[pl.ds(pl.multiple_of(src_row, ROW_CHUNKS), ROW_CHUNKS)],
                                 dst_ref.at[pl.ds(pl.multiple_of(dst_row, ROW_CHUNKS), ROW_CHUNKS)], sem)


def _dispatch_kernel(dst_ref, off_ref, cnt_ref, x_ref, xs_ref, zbuf, sem, zsem):
    tb = x_ref.shape[0] // ROW_CHUNKS
    zrows = MOE_TILE * ROW_CHUNKS

    def pad_copy(e):
        cnt = cnt_ref[e]
        rem = jnp.bitwise_and(cnt, MOE_TILE - 1)
        start = pl.multiple_of((off_ref[e] + cnt - rem) * ROW_CHUNKS, zrows)
        return rem != 0, pltpu.make_async_copy(zbuf, xs_ref.at[pl.ds(start, zrows)], zsem)

    @pl.when(pl.program_id(0) == 0)
    def _():
        zbuf[...] = jnp.zeros_like(zbuf)

        def start(e, c):
            has_pad, cp = pad_copy(e)

            @pl.when(has_pad)
            def _():
                cp.start()
            return c

        def wait(e, c):
            has_pad, cp = pad_copy(e)

            @pl.when(has_pad)
            def _():
                cp.wait()
            return c

        lax.fori_loop(0, N_EXPERTS, start, 0)
        lax.fori_loop(0, N_EXPERTS, wait, 0)

    def issue(t, c):
        for k in range(TOPK):
            _tile_copy(x_ref, t * ROW_CHUNKS, xs_ref, dst_ref[k, t], sem).start(priority=k % 2)
        return c

    def drain(t, c):
        for k in range(TOPK):
            _tile_copy(x_ref, 0, xs_ref, 0, sem).wait()
        return c

    lax.fori_loop(0, tb, issue, 0)
    lax.fori_loop(0, tb, drain, 0)


def moe_dispatch(xc, dst, off, counts):
    n = xc.shape[0] // ROW_CHUNKS
    tb = DSP_TB
    smem_all = pl.BlockSpec(memory_space=pltpu.SMEM)
    smem_blk = pl.BlockSpec((TOPK, tb), lambda i: (0, i), memory_space=pltpu.SMEM)
    return pl.pallas_call(
        _dispatch_kernel,
        out_shape=jax.ShapeDtypeStruct((_moe_rows(n) * ROW_CHUNKS, LANES), xc.dtype),
        grid=(n // tb,),
        in_specs=[smem_blk, smem_all, smem_all, pl.BlockSpec((tb * ROW_CHUNKS, LANES), lambda i: (i, 0))],
        out_specs=pl.BlockSpec(memory_space=pl.ANY),
        scratch_shapes=[pltpu.VMEM((MOE_TILE * ROW_CHUNKS, LANES), xc.dtype), pltpu.SemaphoreType.DMA(()),
                        pltpu.SemaphoreType.DMA(())],
        compiler_params=_cparams(("arbitrary",)),
        name="moe_dispatch",
    )(dst, off, counts, xc)


def _expert_kernel(texp_ref, nused_ref, xs_ref, w1_ref, w3_ref, w2_ref, ys_ref, w1b, w3b, w2b, last):
    i = pl.program_id(0)
    e = texp_ref[i]

    @pl.when(i == 0)
    def _():
        last[0] = -1

    @pl.when(e != last[0])
    def _():
        w1b[...] = w1_ref[0, 0].astype(BF16)
        w3b[...] = w3_ref[0, 0].astype(BF16)
        w2b[...] = w2_ref[0, 0].astype(BF16)
        last[0] = e

    @pl.when(i < nused_ref[0])
    def _():
        xb = _unpack_rows(_load_chunked(xs_ref, MOE_TILE), BF16)
        h1 = jnp.dot(xb, w1b[...], preferred_element_type=F32)
        h3 = jnp.dot(xb, w3b[...], preferred_element_type=F32)
        h = h1 * jax.nn.sigmoid(h1) * h3
        y = jnp.dot(h.astype(BF16), w2b[...], preferred_element_type=F32)
        _store_chunked(ys_ref, _pack_rows(y))

    @pl.when(i >= nused_ref[0])
    def _():
        ys_ref[...] = jnp.zeros_like(ys_ref)


def moe_experts(xs, texp, nused, w1, w3, w2, layer):
    crows = xs.shape[0]
    d, f = w1.shape[2], w1.shape[3]
    blk = MOE_TILE * ROW_CHUNKS
    nt = crows // blk
    grid_spec = pltpu.PrefetchScalarGridSpec(
        num_scalar_prefetch=2,
        grid=(nt,),
        in_specs=[pl.BlockSpec((blk, LANES), lambda i, te, nu: (jnp.where(i < nu[0], i, 0), 0)),
                  pl.BlockSpec((1, 1, d, f), lambda i, te, nu: (layer, te[i], 0, 0)),
                  pl.BlockSpec((1, 1, d, f), lambda i, te, nu: (layer, te[i], 0, 0)),
                  pl.BlockSpec((1, 1, f, d), lambda i, te, nu: (layer, te[i], 0, 0))],
        out_specs=pl.BlockSpec((blk, LANES), lambda i, te, nu: (i, 0)),
        scratch_shapes=[pltpu.VMEM((d, f), BF16), pltpu.VMEM((d, f), BF16), pltpu.VMEM((f, d), BF16),
                        pltpu.SMEM((1,), I32)],
    )
    return pl.pallas_call(
        _expert_kernel,
        out_shape=jax.ShapeDtypeStruct((crows, LANES), U32),
        grid_spec=grid_spec,
        compiler_params=_cparams(("arbitrary",), 48),
        name="moe_experts",
    )(texp, nused, xs, w1, w3, w2)


CMB_TB = 512


def _combine_kernel(dst_ref, x_ref, wts_ref, ws1_ref, ws3_ref, ws2_ref, lg_ref, lb_ref, ys_ref, o_ref, buf, sem):
    tb = x_ref.shape[0]

    def issue(t, c):
        for k in range(TOPK):
            _tile_copy(ys_ref, dst_ref[k, t], buf.at[k], t * ROW_CHUNKS, sem).start(priority=k % 2)
        return c

    def drain(t, c):
        for k in range(TOPK):
            _tile_copy(ys_ref, 0, buf.at[0], 0, sem).wait()
        return c

    lax.fori_loop(0, tb, issue, 0)
    x = x_ref[...]
    xb = x.astype(BF16)
    h1 = jnp.dot(xb, ws1_ref[...], preferred_element_type=F32)
    h3 = jnp.dot(xb, ws3_ref[...], preferred_element_type=F32)
    y = jnp.dot((h1 * jax.nn.sigmoid(h1) * h3).astype(BF16), ws2_ref[...], preferred_element_type=F32)
    lax.fori_loop(0, tb, drain, 0)
    w = wts_ref[...]
    routed = y
    for k in range(TOPK):
        routed = routed + w[:, k:k + 1] * _unpack_rows(_load_chunked(buf.at[k], tb), F32)
    o_ref[...] = _layer_norm(DN_ALPHA * x + routed, lg_ref[...], lb_ref[...])


def moe_combine(x, ys, dst, wts, ws1, ws3, ws2, ln_g, ln_b):
    n, d = x.shape
    tb = CMB_TB
    smem_blk = pl.BlockSpec((TOPK, tb), lambda i: (0, i), memory_space=pltpu.SMEM)
    const = lambda i: (0, 0)
    return pl.pallas_call(
        _combine_kernel,
        out_shape=jax.ShapeDtypeStruct((n, d), F32),
        grid=(n // tb,),
        in_specs=[smem_blk,
                  pl.BlockSpec((tb, d), lambda i: (i, 0)), pl.BlockSpec((tb, LANES), lambda i: (i, 0)),
                  pl.BlockSpec(ws1.shape, const), pl.BlockSpec(ws3.shape, const), pl.BlockSpec(ws2.shape, const),
                  pl.BlockSpec((1, d), const), pl.BlockSpec((1, d), const),
                  pl.BlockSpec(memory_space=pl.ANY)],
        out_specs=pl.BlockSpec((tb, d), lambda i: (i, 0)),
        scratch_shapes=[pltpu.VMEM((TOPK, tb * ROW_CHUNKS, LANES), U32), pltpu.SemaphoreType.DMA(())],
        compiler_params=_cparams(("parallel",), 48),
        name="moe_combine_ln2",
    )(dst, x, wts, ws1, ws3, ws2, ln_g, ln_b, ys)


def moe_block(x, xp, router_w, router_b, w1, w3, w2, layer, ws1, ws3, ws2, ln_g, ln_b):
    n, d = x.shape
    rw = jnp.pad(router_w, ((0, 0), (0, LANES - N_EXPERTS))).astype(BF16)
    eidx, rnk, wts, cnt = moe_route(x, rw, router_b.reshape(N_EXPERTS, 1))
    counts = cnt[:, 0]
    off, texp, nused = moe_plan(counts, n)
    dst = moe_dst(eidx, rnk, off)
    xs = moe_dispatch(xp, dst, off, counts)
    ys = moe_experts(xs, texp, nused, w1, w3, w2, layer)
    return moe_combine(x, ys, dst, wts, ws1.astype(BF16), ws3.astype(BF16), ws2.astype(BF16),
                       ln_g.reshape(1, -1), ln_b.reshape(1, -1))


def nsa_rope_tables(positions):
    batch, seq = positions.shape
    inv, sgn = _nsa_inv_freq()
    posf = positions.astype(F32)
    tok = rope_tables(posf.reshape(batch * seq, 1), inv, sgn, tm=512)
    end = posf[:, CMP_BLOCK - 1::CMP_STRIDE]
    npad = _n_cmp_pad(seq)
    end = jnp.pad(end, ((0, 0), (0, npad - end.shape[1])))
    cmp = rope_tables(end.reshape(batch * npad, 1), inv, sgn, tm=npad)
    return tok, cmp


def nsa_branch(z, tok_tab, cmp_tab, wk, wv, ovl, *, batch, seq):
    n = batch * seq
    w = NSA_GROUPS * NSA_DH
    xk = z[:, Z_NKV:Z_NKV + w].reshape(n // CMP_HALF, CMP_ROW)
    xv = z[:, Z_NKV + w:Z_NKV + 2 * w].reshape(n // CMP_HALF, CMP_ROW)
    kc, vct = nsa_compress(xk, xv, wk, wv, cmp_tab[0], cmp_tab[1], batch=batch)
    ksa, kw, vswt = nsa_kv_prep(z, tok_tab[0], tok_tab[1], batch=batch, seq=seq)
    return nsa_attention(z, tok_tab[0], tok_tab[1], kc, vct, ksa, kw, vswt, ovl, batch=batch, seq=seq)


def kernel(x, positions, w_in, conv_dw, conv_db, conv_ln_g, conv_ln_b, conv_w_pw, ret_w_o, nsa_pe_k, nsa_w1_k, nsa_w2_k,
           nsa_pe_v, nsa_w1_v, nsa_w2_v, nsa_w_o, w_out, ln1_g, ln1_b, router_w, router_b, moe_w1, moe_w3, moe_w2,
           shared_w1, shared_w3, shared_w2, ln2_g, ln2_b):
    batch, seq, d = x.shape
    n = batch * seq
    xf = x.reshape(n, d)
    row = lambda v: v.reshape(1, -1)

    ret_inv, ret_sgn = _ret_inv_freq()
    ret_tab = rope_tables(positions.astype(F32).reshape(n, 1), ret_inv, ret_sgn, tm=512)
    tok_tab, cmp_tab = nsa_rope_tables(positions)
    ret_consts = _ret_tables()
    ovl = _overlap_matrix(seq)

    for l in range(w_in.shape[0]):
        z = matmul(xf, _layout_w_in(w_in[l]), tm=2048, tn=512, out_dtype=BF16, name="in_proj")
        y_conv = conv_branch(z, conv_dw[l], row(conv_db[l]), row(conv_ln_g[l]), row(conv_ln_b[l]),
                             conv_w_pw[l].astype(BF16), batch=batch, seq=seq)
        o_ret = retention_branch(z, ret_tab[0], ret_tab[1], ret_consts, batch=batch, seq=seq)
        o_nsa = nsa_branch(z, tok_tab, cmp_tab, _cmp_weights(nsa_pe_k[l], nsa_w1_k[l], nsa_w2_k[l]),
                           _cmp_weights(nsa_pe_v[l], nsa_w1_v[l], nsa_w2_v[l]), ovl, batch=batch, seq=seq)
        x1, x1p = merge_block(xf, y_conv, o_ret, o_nsa, z, ret_w_o[l].astype(BF16), nsa_w_o[l].astype(BF16),
                              w_out[l].astype(BF16), row(ln1_g[l]), row(ln1_b[l]))
        xf = moe_block(x1, x1p, router_w[l], router_b[l], moe_w1, moe_w3, moe_w2, l,
                       shared_w1[l], shared_w3[l], shared_w2[l], ln2_g[l], ln2_b[l])
    return xf.reshape(batch, seq, d)
```

```python
import math

import jax
import jax.numpy as jnp
import numpy as np
from jax import lax
from jax.experimental import pallas as pl
from jax.experimental.pallas import tpu as pltpu

F32 = jnp.float32
BF16 = jnp.bfloat16
I32 = jnp.int32

D_MODEL = 1024
DEPTH = 4
CONV_CH = 512
CONV_WIDTH = 31
RET_HEADS = 4
RET_DK = 128
RET_DV = 256
RET_CHUNK = 128
RET_ROPE_BASE = 10000.0
NSA_HEADS = 8
NSA_GROUPS = 2
NSA_HPG = NSA_HEADS // NSA_GROUPS
NSA_DH = 64
NSA_N_BRANCH = 3
CMP_BLOCK = 32
CMP_STRIDE = 16
CMP_HIDDEN = 256
SEL_BLOCK = 64
SEL_TOPK = 16
WINDOW = 512
ROPE_THETA = 500000.0
ROT_DIM = NSA_DH // 4
N_EXPERTS = 64
N_EXPERT_GROUPS = 8
TOPK_GROUPS = 4
TOPK = 8
D_EXPERT = 256
D_SHARED = 256
ROUTED_SCALE = 2.5
DN_ALPHA = (2.0 * DEPTH) ** 0.25
LN_EPS = 1e-5
NEG = -1e30

LANES = 128

Z_MG = 0
Z_CA = 3072
Z_CB = 3584
Z_RQ = 4096
Z_RK = 4608
Z_RV = 5120
Z_RG = 6144
Z_NQ = 7168
Z_NKV = 7680
Z_NG = 8448
Z_W = 8704

_IN_WIDTHS = (512, 512, 512, 512, 1024, 1024, 512, 128, 128, 128, 128, 128, 128, 24, 3072)
_IN_OFFS = tuple(int(v) for v in np.concatenate([[0], np.cumsum(_IN_WIDTHS)[:-1]]))

MOE_TILE = 1024


def _cparams(sem, vmem_mb=None):
    kw = dict(dimension_semantics=sem)
    if vmem_mb is not None:
        kw["vmem_limit_bytes"] = vmem_mb * 1024 * 1024
    return pltpu.CompilerParams(**kw)


def _mm_kernel(x_ref, w_ref, o_ref):
    o_ref[...] = jnp.dot(x_ref[...].astype(BF16), w_ref[...], preferred_element_type=F32).astype(o_ref.dtype)


def matmul(x, w, *, tm, tn, out_dtype, name):
    m, k = x.shape
    n = w.shape[1]
    return pl.pallas_call(
        _mm_kernel,
        out_shape=jax.ShapeDtypeStruct((m, n), out_dtype),
        grid=(m // tm, n // tn),
        in_specs=[pl.BlockSpec((tm, k), lambda i, j: (i, 0)),
                  pl.BlockSpec((k, tn), lambda i, j: (0, j))],
        out_specs=pl.BlockSpec((tm, tn), lambda i, j: (i, j)),
        compiler_params=_cparams(("parallel", "arbitrary"), 48),
        name=name,
    )(x, w)


def _rope_table_kernel(pos_ref, inv_ref, sgn_ref, cos_ref, sin_ref):
    ang = pos_ref[...] * inv_ref[...]
    cos_ref[...] = jnp.cos(ang)
    sin_ref[...] = jnp.sin(ang) * sgn_ref[...]


def rope_tables(pos_col, inv, sgn, *, tm):
    n = pos_col.shape[0]
    w = inv.shape[1]
    return pl.pallas_call(
        _rope_table_kernel,
        out_shape=(jax.ShapeDtypeStruct((n, w), F32), jax.ShapeDtypeStruct((n, w), F32)),
        grid=(n // tm,),
        in_specs=[pl.BlockSpec((tm, 1), lambda i: (i, 0)),
                  pl.BlockSpec((1, w), lambda i: (0, 0)),
                  pl.BlockSpec((1, w), lambda i: (0, 0))],
        out_specs=(pl.BlockSpec((tm, w), lambda i: (i, 0)), pl.BlockSpec((tm, w), lambda i: (i, 0))),
        compiler_params=_cparams(("parallel",)),
        name="rope_tables",
    )(pos_col, inv, sgn)


def _ret_inv_freq():
    inv = 1.0 / jnp.power(jnp.float32(RET_ROPE_BASE), jnp.linspace(0.0, 1.0, RET_DK // 2, dtype=F32))
    inv = jnp.concatenate([inv, inv])[None, :]
    sgn = jnp.concatenate([-jnp.ones((RET_DK // 2,), F32), jnp.ones((RET_DK // 2,), F32)])[None, :]
    return inv, sgn


def _nsa_inv_freq():
    half = ROT_DIM // 2
    inv = jnp.power(jnp.float32(ROPE_THETA), -jnp.arange(0, ROT_DIM, 2, dtype=F32) / ROT_DIM)
    z = jnp.zeros((NSA_DH - ROT_DIM,), F32)
    inv64 = jnp.concatenate([inv, inv, z])
    sgn64 = jnp.concatenate([-jnp.ones((half,), F32), jnp.ones((half,), F32), z])
    return jnp.concatenate([inv64, inv64])[None, :], jnp.concatenate([sgn64, sgn64])[None, :]


def _nsa_rope(x, cos, sin):
    w = x.shape[1]
    half = ROT_DIM // 2
    lane = lax.broadcasted_iota(I32, x.shape, 1) % NSA_DH
    partner = jnp.where(lane < half, pltpu.roll(x, w - half, 1), pltpu.roll(x, half, 1))
    return x * cos + partner * sin


CONV_TT = 512
CONV_HALO = 32


def _conv_kernel(a_ref, b_ref, ah_ref, bh_ref, dw_ref, db_ref, g_ref, be_ref, wpw_ref, o_ref, ubuf, sbuf):
    i = pl.program_id(1)
    tt = a_ref.shape[0]
    u = a_ref[...].astype(F32) * jax.nn.sigmoid(b_ref[...].astype(F32))
    uh = ah_ref[...].astype(F32) * jax.nn.sigmoid(bh_ref[...].astype(F32))
    ubuf[0:CONV_HALO, :] = jnp.where(i > 0, uh, 0.0)
    ubuf[CONV_HALO:CONV_HALO + tt, :] = u
    acc = jnp.zeros((tt, CONV_CH), F32)
    base = CONV_HALO - (CONV_WIDTH - 1)
    sub = 8
    for r in range(sub):
        offs = [o for o in range(base, base + CONV_WIDTH) if o % sub == r]
        if not offs:
            continue
        span = offs[-1] - r + tt
        sbuf[0:span, :] = ubuf[r:r + span, :]
        for o in offs:
            acc = acc + dw_ref[o - base:o - base + 1, :] * sbuf[o - r:o - r + tt, :]
    acc = acc + db_ref[...]
    mu = jnp.mean(acc, axis=-1, keepdims=True)
    var = jnp.mean(jnp.square(acc - mu), axis=-1, keepdims=True)
    y = (acc - mu) * lax.rsqrt(var + LN_EPS) * g_ref[...] + be_ref[...]
    y = y * jax.nn.sigmoid(y)
    o_ref[...] = jnp.dot(y.astype(BF16), wpw_ref[...], preferred_element_type=F32).astype(o_ref.dtype)


def conv_branch(z, dw, db, ln_g, ln_b, w_pw_bf, *, batch, seq):
    tt = CONV_TT
    nt = seq // tt
    r = tt // CONV_HALO
    ca, cb = Z_CA // CONV_CH, Z_CB // CONV_CH

    def halo_map(col):
        return lambda b, i: (jnp.maximum((b * nt + i) * r - 1, 0), col)

    return pl.pallas_call(
        _conv_kernel,
        out_shape=jax.ShapeDtypeStruct((batch * seq, D_MODEL), BF16),
        grid=(batch, nt),
        in_specs=[pl.BlockSpec((tt, CONV_CH), lambda b, i: (b * nt + i, ca)),
                  pl.BlockSpec((tt, CONV_CH), lambda b, i: (b * nt + i, cb)),
                  pl.BlockSpec((CONV_HALO, CONV_CH), halo_map(ca)),
                  pl.BlockSpec((CONV_HALO, CONV_CH), halo_map(cb)),
                  pl.BlockSpec((CONV_WIDTH, CONV_CH), lambda b, i: (0, 0)),
                  pl.BlockSpec((1, CONV_CH), lambda b, i: (0, 0)),
                  pl.BlockSpec((1, CONV_CH), lambda b, i: (0, 0)),
                  pl.BlockSpec((1, CONV_CH), lambda b, i: (0, 0)),
                  pl.BlockSpec((CONV_CH, D_MODEL), lambda b, i: (0, 0))],
        out_specs=pl.BlockSpec((tt, D_MODEL), lambda b, i: (b * nt + i, 0)),
        scratch_shapes=[pltpu.VMEM((CONV_HALO + tt, CONV_CH), F32), pltpu.VMEM((CONV_HALO + tt, CONV_CH), F32)],
        compiler_params=_cparams(("parallel", "parallel")),
        name="conv_branch",
    )(z, z, z, z, dw, db, ln_g, ln_b, w_pw_bf)


RET_TQ = 1024


def _ret_tables():
    h, c = RET_HEADS, RET_CHUNK
    log_gamma = jnp.log1p(-jnp.exp2(-5.0 - jnp.arange(h, dtype=F32)))
    idx = jnp.arange(c, dtype=F32)
    diff = idx[:, None] - idx[None, :]
    dmat = jnp.where(diff >= 0, jnp.exp(log_gamma[:, None, None] * jnp.maximum(diff, 0.0)), 0.0).astype(F32)
    xi = jnp.exp(log_gamma[:, None] * (idx + 1.0)).astype(F32)
    zeta = jnp.exp(log_gamma[:, None] * (c - 1.0 - idx)).astype(F32)
    decay = jnp.exp(log_gamma * c).astype(F32)
    xi_b = jnp.broadcast_to(xi[:, :, None], (h, c, RET_DV))
    zeta_b = jnp.broadcast_to(zeta[:, :, None], (h, c, RET_DV))
    decay_b = jnp.broadcast_to(decay[:, None, None], (h, RET_DK, RET_DV))
    return dmat, xi_b, zeta_b, decay_b


def _ret_kernel(q_ref, k_ref, v_ref, g_ref, cos_ref, sin_ref, dmat_ref, xi_ref, zeta_ref, dec_ref, o_ref, r_ref):
    @pl.when(pl.program_id(1) == 0)
    def _():
        r_ref[...] = jnp.zeros_like(r_ref)

    c = RET_CHUNK
    n_chunks = q_ref.shape[0] // c
    for ci in range(n_chunks):
        rows = slice(ci * c, (ci + 1) * c)
        cos = cos_ref[rows, :]
        sin = sin_ref[rows, :]
        for h in range(RET_HEADS):
            qk_cols = slice(h * RET_DK, (h + 1) * RET_DK)
            v_cols = slice(h * RET_DV, (h + 1) * RET_DV)
            q = q_ref[rows, qk_cols].astype(F32)
            k = k_ref[rows, qk_cols].astype(F32)
            q = q * cos + pltpu.roll(q, RET_DK // 2, 1) * sin
            k = (k * cos + pltpu.roll(k, RET_DK // 2, 1) * sin) * (RET_DK ** -0.5)
            v = v_ref[rows, v_cols].astype(F32)
            qb = q.astype(BF16)
            kb = k.astype(BF16)
            inner = lax.dot_general(qb, kb, (((1,), (1,)), ((), ())), preferred_element_type=F32) * dmat_ref[h]
            r_old = r_ref[h]
            o = (jnp.dot(inner.astype(BF16), v.astype(BF16), preferred_element_type=F32)
                 + jnp.dot(qb, r_old.astype(BF16), preferred_element_type=F32) * xi_ref[h])
            vz = (v * zeta_ref[h]).astype(BF16)
            r_ref[h] = r_old * dec_ref[h] + jnp.dot(k.T.astype(BF16), vz, preferred_element_type=F32)
            mu = jnp.mean(o, axis=-1, keepdims=True)
            var = jnp.mean(jnp.square(o - mu), axis=-1, keepdims=True)
            on = (o - mu) * lax.rsqrt(var + LN_EPS)
            g = g_ref[rows, v_cols].astype(F32)
            o_ref[rows, v_cols] = (g * jax.nn.sigmoid(g) * on).astype(o_ref.dtype)


def retention_branch(z, ret_cos, ret_sin, tables, *, batch, seq):
    tq = RET_TQ
    nt = seq // tq
    dmat, xi_b, zeta_b, decay_b = tables
    qw = RET_HEADS * RET_DK
    vw = RET_HEADS * RET_DV
    row = lambda b, i: b * nt + i
    full3 = lambda b, i: (0, 0, 0)
    return pl.pallas_call(
        _ret_kernel,
        out_shape=jax.ShapeDtypeStruct((batch * seq, vw), BF16),
        grid=(batch, nt),
        in_specs=[pl.BlockSpec((tq, qw), lambda b, i: (row(b, i), Z_RQ // qw)),
                  pl.BlockSpec((tq, qw), lambda b, i: (row(b, i), Z_RK // qw)),
                  pl.BlockSpec((tq, vw), lambda b, i: (row(b, i), Z_RV // vw)),
                  pl.BlockSpec((tq, vw), lambda b, i: (row(b, i), Z_RG // vw)),
                  pl.BlockSpec((tq, RET_DK), lambda b, i: (row(b, i), 0)),
                  pl.BlockSpec((tq, RET_DK), lambda b, i: (row(b, i), 0)),
                  pl.BlockSpec(dmat.shape, full3),
                  pl.BlockSpec(xi_b.shape, full3),
                  pl.BlockSpec(zeta_b.shape, full3),
                  pl.BlockSpec(decay_b.shape, full3)],
        out_specs=pl.BlockSpec((tq, vw), lambda b, i: (row(b, i), 0)),
        scratch_shapes=[pltpu.VMEM((RET_HEADS, RET_DK, RET_DV), F32)],
        compiler_params=_cparams(("parallel", "arbitrary"), 48),
        name="retention",
    )(z, z, z, z, ret_cos, ret_sin, dmat, xi_b, zeta_b, decay_b)


def _layout_w_in(w):
    seg = lambda i: w[:, _IN_OFFS[i]:_IN_OFFS[i] + _IN_WIDTHS[i]]
    order = (14, 0, 1, 2, 3, 4, 5, 6, 7, 8, 9, 10, 11, 12, 13)
    parts = [seg(i) for i in order]
    used = sum(_IN_WIDTHS)
    parts.append(jnp.zeros((w.shape[0], Z_W - used), w.dtype))
    return jnp.concatenate(parts, axis=1).astype(BF16)


CMP_HALF = CMP_BLOCK // 2
CMP_ROW = CMP_HALF * NSA_GROUPS * NSA_DH


def _n_cmp_pad(seq):
    return seq // CMP_STRIDE


def _cmp_weights(pe, w1, w2):
    g = NSA_GROUPS
    eye = jnp.eye(g, dtype=F32)
    w = w1.reshape(2, CMP_HALF, NSA_DH, CMP_HIDDEN)
    w1ab = jnp.einsum("hldf,gk->hlgdkf", w, eye).reshape(2, CMP_ROW, g * CMP_HIDDEN).astype(BF16)
    peab = jnp.broadcast_to(pe.reshape(2, CMP_HALF, 1, NSA_DH), (2, CMP_HALF, g, NSA_DH)).reshape(2, 1, CMP_ROW)
    w2bd = jnp.einsum("fd,gk->gfkd", w2, eye).reshape(g * CMP_HIDDEN, g * NSA_DH).astype(BF16)
    return peab, w1ab, w2bd


def _cmp_mlp(x, pe_ref, w1_ref, w2_ref):
    x = x.astype(F32)
    a = jnp.dot((x + pe_ref[0]).astype(BF16), w1_ref[0], preferred_element_type=F32)
    b = jnp.dot((x + pe_ref[1]).astype(BF16), w1_ref[1], preferred_element_type=F32)
    hid = a + pltpu.roll(b, b.shape[0] - 1, 0)
    hid = hid * jax.nn.sigmoid(hid)
    return jnp.dot(hid.astype(BF16), w2_ref[...], preferred_element_type=F32)


def _compress_kernel(xk_ref, xv_ref, pek_ref, w1k_ref, w2k_ref, pev_ref, w1v_ref, w2v_ref, cos_ref, sin_ref,
                     k_ref, vt_ref):
    k = _cmp_mlp(xk_ref[...], pek_ref, w1k_ref, w2k_ref)
    k_ref[0] = _nsa_rope(k, cos_ref[...], sin_ref[...]).astype(BF16)
    vt_ref[0] = _cmp_mlp(xv_ref[...], pev_ref, w1v_ref, w2v_ref).T.astype(BF16)


def nsa_compress(xk, xv, wk, wv, cmp_cos, cmp_sin, *, batch):
    pek, w1k, w2k = wk
    pev, w1v, w2v = wv
    npad = xk.shape[0] // batch
    gw = NSA_GROUPS * NSA_DH
    c3 = lambda b: (0, 0, 0)
    c2 = lambda b: (0, 0)
    return pl.pallas_call(
        _compress_kernel,
        out_shape=(jax.ShapeDtypeStruct((batch, npad, gw), BF16), jax.ShapeDtypeStruct((batch, gw, npad), BF16)),
        grid=(batch,),
        in_specs=[pl.BlockSpec((npad, CMP_ROW), lambda b: (b, 0)),
                  pl.BlockSpec((npad, CMP_ROW), lambda b: (b, 0)),
                  pl.BlockSpec(pek.shape, c3), pl.BlockSpec(w1k.shape, c3), pl.BlockSpec(w2k.shape, c2),
                  pl.BlockSpec(pev.shape, c3), pl.BlockSpec(w1v.shape, c3), pl.BlockSpec(w2v.shape, c2),
                  pl.BlockSpec((npad, gw), lambda b: (b, 0)),
                  pl.BlockSpec((npad, gw), lambda b: (b, 0))],
        out_specs=(pl.BlockSpec((1, npad, gw), lambda b: (b, 0, 0)), pl.BlockSpec((1, gw, npad), lambda b: (b, 0, 0))),
        compiler_params=_cparams(("parallel",), 48),
        name="nsa_compress",
    )(xk, xv, pek, w1k, w2k, pev, w1v, w2v, cmp_cos, cmp_sin)


KV_TT = 1024
ATT_TK = 512
ATT_KB = 256
ATT_AHEAD = 8
LOG2E = math.log2(math.e)


def _kv_prep_kernel(z_ref, cos_ref, sin_ref, ksa_ref, kw_ref, vswt_ref):
    tt = z_ref.shape[0]
    cos = cos_ref[...]
    sin = sin_ref[...]
    w = NSA_GROUPS * NSA_DH
    n_sel = ksa_ref.shape[3] - NSA_DH
    ks = _nsa_rope(z_ref[:, 2 * w:3 * w].astype(F32), cos, sin)
    vs = z_ref[:, 3 * w:4 * w].astype(F32)
    kw = _nsa_rope(z_ref[:, 4 * w:5 * w].astype(F32), cos, sin)
    vw = z_ref[:, 5 * w:6 * w].astype(F32)
    blk = (pl.program_id(1) * tt + lax.broadcasted_iota(I32, (tt, n_sel), 0)) // SEL_BLOCK
    onehot = jnp.where(blk == lax.broadcasted_iota(I32, (tt, n_sel), 1), 1.0, 0.0)
    for g in range(NSA_GROUPS):
        cols = slice(g * NSA_DH, (g + 1) * NSA_DH)
        ksa_ref[0, g] = jnp.concatenate([ks[:, cols], onehot], axis=1).astype(BF16)
        kw_ref[0, g] = kw[:, cols].astype(BF16)
        vswt_ref[0, g] = jnp.concatenate([vs[:, cols], vw[:, cols]], axis=1).T.astype(BF16)


def nsa_kv_prep(z, nsa_cos, nsa_sin, *, batch, seq):
    tt = KV_TT
    nt = seq // tt
    g = NSA_GROUPS
    w = g * NSA_DH
    n_sel = seq // SEL_BLOCK
    return pl.pallas_call(
        _kv_prep_kernel,
        out_shape=(jax.ShapeDtypeStruct((batch, g, seq, NSA_DH + n_sel), BF16),
                   jax.ShapeDtypeStruct((batch, g, seq, NSA_DH), BF16),
                   jax.ShapeDtypeStruct((batch, g, w, seq), BF16)),
        grid=(batch, nt),
        in_specs=[pl.BlockSpec((tt, 6 * w), lambda b, i: (b * nt + i, Z_NKV // (6 * w))),
                  pl.BlockSpec((tt, w), lambda b, i: (b * nt + i, 0)),
                  pl.BlockSpec((tt, w), lambda b, i: (b * nt + i, 0))],
        out_specs=(pl.BlockSpec((1, g, tt, NSA_DH + n_sel), lambda b, i: (b, 0, i, 0)),
                   pl.BlockSpec((1, g, tt, NSA_DH), lambda b, i: (b, 0, i, 0)),
                   pl.BlockSpec((1, g, w, tt), lambda b, i: (b, 0, 0, i))),
        compiler_params=_cparams(("parallel", "parallel")),
        name="nsa_kv_prep",
    )(z, nsa_cos, nsa_sin)


ATT_TQ = 256


def _overlap_matrix(seq):
    n_cmp = (seq - CMP_BLOCK) // CMP_STRIDE + 1
    n_sel = seq // SEL_BLOCK
    ii = np.arange(_n_cmp_pad(seq))[None, :]
    jj = np.arange(LANES)[:, None]
    lo = np.maximum(ii * CMP_STRIDE, jj * SEL_BLOCK)
    hi = np.minimum(ii * CMP_STRIDE + CMP_BLOCK, (jj + 1) * SEL_BLOCK)
    ov = np.maximum(hi - lo, 0).astype(np.float32) / CMP_BLOCK
    ov = np.where((ii < n_cmp) & (jj < n_sel), ov, 0.0)
    return jnp.asarray(np.tile(ov, (1, NSA_HPG)), BF16)


def _softmax_step(s, ok, vt, cols, m_ref, l_ref, a_ref):
    m_old = m_ref[:, cols]
    m_new = jnp.maximum(m_old, jnp.max(s, axis=0, keepdims=True))
    alpha = jnp.exp2(m_old - m_new)
    p = jnp.exp2(s - m_new)
    if ok is not None:
        p = jnp.where(ok, p, 0.0)
    l_ref[:, cols] = alpha * l_ref[:, cols] + jnp.sum(p, axis=0, keepdims=True)
    a_ref[:, cols] = alpha * a_ref[:, cols] + jnp.dot(vt, p.astype(BF16), preferred_element_type=F32)
    m_ref[:, cols] = m_new


def _att_kernel(q_ref, cos_ref, sin_ref, gl_ref, kc_ref, vct_ref, ksa_ref, kw_ref, vswt_ref, ovl_ref, o_ref,
                qa_s, m_sel, l_sel, a_sel, m_win, l_win, a_win):
    g = pl.program_id(1)
    qi = pl.program_id(2)
    tq = ATT_TQ
    tk = ATT_TK
    hpg = NSA_HPG
    dh = NSA_DH
    t0 = qi * tq
    seq = ksa_ref.shape[2]
    n_sel = ksa_ref.shape[3] - dh
    heads = [slice(h * tq, (h + 1) * tq) for h in range(hpg)]

    cos = jnp.concatenate([cos_ref[...]] * (hpg // 2), axis=1)
    sin = jnp.concatenate([sin_ref[...]] * (hpg // 2), axis=1)
    qt = (_nsa_rope(q_ref[...].astype(F32), cos, sin) * (dh ** -0.5 * LOG2E)).T
    q_t = jnp.concatenate([qt[h * dh:(h + 1) * dh] for h in range(hpg)], axis=1)
    q_tb = q_t.astype(BF16)
    qa_s[0:dh, :] = q_tb

    zero = jnp.zeros_like(q_tb)
    q2 = jnp.where(g == 0, jnp.concatenate([q_tb, zero], axis=0), jnp.concatenate([zero, q_tb], axis=0))
    s = jnp.dot(kc_ref[0], q2, preferred_element_type=F32)
    n_cmp = (seq - CMP_BLOCK) // CMP_STRIDE + 1
    nrow = lax.broadcasted_iota(I32, s.shape, 0)
    tcol = t0 + lax.broadcasted_iota(I32, s.shape, 1) % tq
    ok = (nrow * CMP_STRIDE + (CMP_BLOCK - 1) <= tcol) & (nrow < n_cmp)
    sm = jnp.where(ok, s, NEG)
    e = jnp.where(ok, jnp.exp2(sm - jnp.max(sm, axis=0, keepdims=True)), 0.0)
    den = jnp.sum(e, axis=0, keepdims=True)
    pb = (e / jnp.where(den > 0.0, den, 1.0)).astype(BF16)
    oc = jnp.dot(vct_ref[0], pb, preferred_element_type=F32)
    o_cmp = jnp.where(g == 0, oc[0:dh], oc[dh:2 * dh])
    pcat = jnp.concatenate([pb[:, hs] for hs in heads], axis=0)
    imp_t = jnp.dot(ovl_ref[...], pcat, preferred_element_type=F32)[0:n_sel]

    jrow = lax.broadcasted_iota(I32, (n_sel, tq), 0)
    cur = (t0 + lax.broadcasted_iota(I32, (n_sel, tq), 1)) // SEL_BLOCK
    forced = (jrow == 0) | (jrow == cur) | (jrow == cur - 1)
    imp_t = jnp.where(forced, jnp.inf, jnp.where(jrow <= cur, imp_t, -jnp.inf))
    sub = 8
    groups = [imp_t[r:r + sub] for r in range(0, n_sel, sub)]
    ranks = [jnp.zeros((sub, tq), F32) for _ in groups]
    srow = lax.broadcasted_iota(I32, (sub, tq), 0)
    for i in range(n_sel):
        gi, si = divmod(i, sub)
        ri = groups[gi][si:si + 1, :]
        for gj, v in enumerate(groups):
            ge = jnp.where(ri >= v, 1.0, 0.0)
            if gj > gi:
                ranks[gj] = ranks[gj] + ge
            else:
                gt = jnp.where(ri > v, 1.0, 0.0)
                ranks[gj] = ranks[gj] + (gt if gj < gi else jnp.where(srow > si, ge, gt))
    rank = jnp.concatenate(ranks, axis=0)
    bias_t = jnp.where(rank < float(SEL_TOPK), 0.0, NEG).astype(BF16)
    qa_s[dh:dh + n_sel, :] = jnp.concatenate([bias_t] * hpg, axis=1)

    m_sel[...] = jnp.full(m_sel.shape, NEG, F32)
    l_sel[...] = jnp.zeros(l_sel.shape, F32)
    a_sel[...] = jnp.zeros(a_sel.shape, F32)

    kb = ATT_KB
    assert tk == 2 * tq and tq % kb == 0
    krow = lax.broadcasted_iota(I32, (kb, tq), 0)
    qtime = t0 + lax.broadcasted_iota(I32, (kb, tq), 1)

    def run_blocks(blocks):
        sc = [blk[0]() for blk in blocks[:ATT_AHEAD]]
        for b, (_, mask, vt, hs, refs) in enumerate(blocks):
            if b + ATT_AHEAD < len(blocks):
                sc.append(blocks[b + ATT_AHEAD][0]())
            ok = mask()
            s_b = sc[b] if ok is None else jnp.where(ok, sc[b], NEG)
            sc[b] = None
            _softmax_step(s_b, ok, vt(), hs, *refs)

    def sel_blocks(k0, n_keys, causal):
        def block(kk, hs):
            return (lambda: jnp.dot(ksa_ref[0, 0, pl.ds(k0 + kk, kb), :], qa_s[:, hs], preferred_element_type=F32),
                    (lambda: (k0 + kk + krow) <= qtime) if causal else (lambda: None),
                    lambda: vswt_ref[0, 0, 0:dh, pl.ds(k0 + kk, kb)], hs, (m_sel, l_sel, a_sel))
        return [block(kk, hs) for kk in range(0, n_keys, kb) for hs in heads]

    def sel_body(j, carry):
        run_blocks(sel_blocks(pl.multiple_of(j * tk, tk), tk, False))
        return carry

    n_full = t0 // tk
    lax.fori_loop(0, n_full, sel_body, 0)

    @pl.when(n_full * tk < t0)
    def _():
        run_blocks(sel_blocks(pl.multiple_of(n_full * tk, tk), tk - tq, False))

    kw0 = pl.multiple_of(jnp.maximum(t0 - WINDOW, 0), tq)

    def win_block(kk, hs):
        def mask():
            dist = qtime - (kw0 + kk + krow)
            return (dist >= 0) & (dist < WINDOW)
        return (lambda: jnp.dot(kw_ref[0, 0, pl.ds(kw0 + kk, kb), :], qa_s[0:dh, hs], preferred_element_type=F32),
                mask, lambda: vswt_ref[0, 0, dh:2 * dh, pl.ds(kw0 + kk, kb)], hs, (m_win, l_win, a_win))

    m_win[...] = jnp.full(m_win.shape, NEG, F32)
    l_win[...] = jnp.zeros(l_win.shape, F32)
    a_win[...] = jnp.zeros(a_win.shape, F32)
    win = [win_block(kk, hs) for kk in range(0, WINDOW + tq, kb) for hs in heads]
    diag = sel_blocks(pl.multiple_of(t0, tq), tq, True)
    per = len(win) // len(diag)
    mixed = []
    for i, blk in enumerate(diag):
        mixed += win[i * per:(i + 1) * per] + [blk]
    run_blocks(mixed + win[len(diag) * per:])

    o_win = a_win[...] / l_win[...]
    o_slc = a_sel[...] / l_sel[...]
    gl_t = jax.nn.sigmoid(gl_ref[...].astype(F32)).T
    nb = NSA_N_BRANCH
    outs = []
    for h, hs in enumerate(heads):
        gate = lambda br: jnp.where(g == 0, gl_t[nb * h + br:nb * h + br + 1],
                                    gl_t[nb * (hpg + h) + br:nb * (hpg + h) + br + 1])
        outs.append(gate(0) * o_cmp[:, hs] + gate(1) * o_slc[:, hs] + gate(2) * o_win[:, hs])
    o_ref[...] = jnp.concatenate(outs, axis=0).T.astype(o_ref.dtype)


def nsa_attention(z, nsa_cos, nsa_sin, kc, vct, ksa, kw, vswt, ovl, *, batch, seq):
    tq = ATT_TQ
    nt = seq // tq
    g = NSA_GROUPS
    gw = g * NSA_DH
    qw = NSA_HPG * NSA_DH
    m_cols = NSA_HPG * tq
    row = lambda b, gg, i: b * nt + i
    per_bg = lambda a: pl.BlockSpec((1, 1) + a.shape[2:], lambda b, gg, i: (b, gg, 0, 0))
    stat = pltpu.VMEM((1, m_cols), F32)
    vals = pltpu.VMEM((NSA_DH, m_cols), F32)
    return pl.pallas_call(
        _att_kernel,
        out_shape=jax.ShapeDtypeStruct((batch * seq, g * qw), BF16),
        grid=(batch, g, nt),
        in_specs=[pl.BlockSpec((tq, qw), lambda b, gg, i: (row(b, gg, i), Z_NQ // qw + gg)),
                  pl.BlockSpec((tq, gw), lambda b, gg, i: (row(b, gg, i), 0)),
                  pl.BlockSpec((tq, gw), lambda b, gg, i: (row(b, gg, i), 0)),
                  pl.BlockSpec((tq, LANES), lambda b, gg, i: (row(b, gg, i), Z_NG // LANES)),
                  pl.BlockSpec((1,) + kc.shape[1:], lambda b, gg, i: (b, 0, 0)),
                  pl.BlockSpec((1,) + vct.shape[1:], lambda b, gg, i: (b, 0, 0)),
                  per_bg(ksa), per_bg(kw), per_bg(vswt),
                  pl.BlockSpec(ovl.shape, lambda b, gg, i: (0, 0))],
        out_specs=pl.BlockSpec((tq, qw), lambda b, gg, i: (row(b, gg, i), gg)),
        scratch_shapes=[pltpu.VMEM((ksa.shape[3], m_cols), BF16), stat, stat, vals, stat, stat, vals],
        compiler_params=_cparams(("parallel", "parallel", "parallel"), 48),
        name="nsa_attention",
    )(z, nsa_cos, nsa_sin, z, kc, vct, ksa, kw, vswt, ovl)


MERGE_TM = 512


def _layer_norm(y, g, b):
    mu = jnp.mean(y, axis=-1, keepdims=True)
    var = jnp.mean(jnp.square(y - mu), axis=-1, keepdims=True)
    return (y - mu) * lax.rsqrt(var + LN_EPS) * g + b


def _merge_kernel(x_ref, yc_ref, or_ref, on_ref, g0_ref, g1_ref, g2_ref, wr_ref, wn_ref, wo_ref, lg_ref, lb_ref, o_ref,
                  op_ref):
    y_ret = jnp.dot(or_ref[...].astype(BF16), wr_ref[...], preferred_element_type=F32)
    y_nsa = jnp.dot(on_ref[...].astype(BF16), wn_ref[...], preferred_element_type=F32)
    gate = lambda ref: jax.nn.sigmoid(ref[...].astype(F32))
    m = gate(g0_ref) * yc_ref[...].astype(F32) + gate(g1_ref) * y_ret + gate(g2_ref) * y_nsa
    h = jnp.dot(m.astype(BF16), wo_ref[...], preferred_element_type=F32)
    y = _layer_norm(DN_ALPHA * x_ref[...] + h, lg_ref[...], lb_ref[...])
    o_ref[...] = y
    _store_chunked(op_ref, _pack_rows(y))


def merge_block(x, y_conv, o_ret, o_nsa, z, ret_w_o, nsa_w_o, w_out, ln_g, ln_b):
    n = x.shape[0]
    tm = MERGE_TM
    d = D_MODEL
    rowd = lambda i: (i, 0)
    const = lambda i: (0, 0)
    return pl.pallas_call(
        _merge_kernel,
        out_shape=(jax.ShapeDtypeStruct((n, d), F32), jax.ShapeDtypeStruct((n * ROW_CHUNKS, LANES), U32)),
        grid=(n // tm,),
        in_specs=[pl.BlockSpec((tm, d), rowd), pl.BlockSpec((tm, d), rowd),
                  pl.BlockSpec((tm, o_ret.shape[1]), rowd), pl.BlockSpec((tm, o_nsa.shape[1]), rowd),
                  pl.BlockSpec((tm, d), lambda i: (i, Z_MG // d)),
                  pl.BlockSpec((tm, d), lambda i: (i, Z_MG // d + 1)),
                  pl.BlockSpec((tm, d), lambda i: (i, Z_MG // d + 2)),
                  pl.BlockSpec(ret_w_o.shape, const), pl.BlockSpec(nsa_w_o.shape, const), pl.BlockSpec(w_out.shape, const),
                  pl.BlockSpec((1, d), const), pl.BlockSpec((1, d), const)],
        out_specs=(pl.BlockSpec((tm, d), rowd), pl.BlockSpec((tm * ROW_CHUNKS, LANES), rowd)),
        compiler_params=_cparams(("parallel",), 48),
        name="merge_ln1",
    )(x, y_conv, o_ret, o_nsa, z, z, z, ret_w_o, nsa_w_o, w_out, ln_g, ln_b)


RT_TM = 1024


def _stable_rank(v):
    n = v.shape[0]
    row = lax.broadcasted_iota(I32, v.shape, 0)
    rank = jnp.zeros(v.shape, F32)
    for i in range(n):
        r = v[i:i + 1, :]
        rank = rank + jnp.where(row > i, jnp.where(r >= v, 1.0, 0.0), jnp.where(r > v, 1.0, 0.0))
    return rank


def _router_kernel(x_ref, wr_ref, b_ref, ltri_ref, utri_ref, eidx_ref, rnk_ref, wts_ref, cnt_ref, carry):
    @pl.when(pl.program_id(0) == 0)
    def _():
        carry[...] = jnp.zeros_like(carry)

    tm = x_ref.shape[0]
    ne = N_EXPERTS
    per = ne // N_EXPERT_GROUPS
    logits = jnp.dot(x_ref[...].astype(BF16), wr_ref[...], preferred_element_type=F32)
    s = jax.nn.sigmoid(logits.T[0:ne])
    sb = s + b_ref[...]
    sub = lax.broadcasted_iota(I32, (per, tm), 0)
    gscore = []
    for gi in range(N_EXPERT_GROUPS):
        v = sb[gi * per:(gi + 1) * per]
        m1 = jnp.max(v, axis=0, keepdims=True)
        first = jnp.min(jnp.where(v == m1, sub, per), axis=0, keepdims=True)
        m2 = jnp.max(jnp.where(sub == first, -jnp.inf, v), axis=0, keepdims=True)
        gscore.append(m1 + m2)
    gscore = jnp.concatenate(gscore, axis=0)
    gkeep = jnp.where(_stable_rank(gscore) < float(TOPK_GROUPS), 1.0, 0.0)
    ekeep = jnp.concatenate([jnp.broadcast_to(gkeep[gi:gi + 1], (per, tm)) for gi in range(N_EXPERT_GROUPS)], axis=0)
    sel = jnp.where(_stable_rank(jnp.where(ekeep > 0.0, sb, -jnp.inf)) < float(TOPK), 1.0, 0.0)
    ssel = s * sel
    gate = ssel / jnp.sum(ssel, axis=0, keepdims=True) * ROUTED_SCALE

    selb = sel.astype(BF16)
    slot = jnp.dot(ltri_ref[...], selb, preferred_element_type=F32)
    incl = jnp.dot(selb, utri_ref[...], preferred_element_type=F32)
    rnk = carry[...] + incl - 1.0
    carry[...] = carry[...] + incl[:, tm - 1:tm]
    erow = lax.broadcasted_iota(I32, (ne, tm), 0).astype(F32)
    es, rs, ws = [], [], []
    for k in range(TOPK):
        pick = jnp.where(slot == float(k), sel, 0.0)
        es.append(jnp.sum(pick * erow, axis=0, keepdims=True))
        rs.append(jnp.sum(pick * rnk, axis=0, keepdims=True))
        ws.append(jnp.sum(pick * gate, axis=0, keepdims=True))
    eidx_ref[...] = jnp.concatenate(es, axis=0).astype(I32)
    rnk_ref[...] = jnp.concatenate(rs, axis=0).astype(I32)
    wts_ref[...] = jnp.concatenate(ws + [jnp.zeros((LANES - TOPK, tm), F32)], axis=0).T
    cnt_ref[...] = jnp.broadcast_to(carry[...], cnt_ref.shape).astype(I32)


def moe_route(x, router_w_pad, router_b_col):
    n = x.shape[0]
    tm = RT_TM
    ne = N_EXPERTS
    ltri = jnp.asarray(np.tril(np.ones((ne, ne), np.float32), -1), BF16)
    utri = jnp.asarray(np.triu(np.ones((tm, tm), np.float32)), BF16)
    const = lambda i: (0, 0)
    return pl.pallas_call(
        _router_kernel,
        out_shape=(jax.ShapeDtypeStruct((TOPK, n), I32), jax.ShapeDtypeStruct((TOPK, n), I32),
                   jax.ShapeDtypeStruct((n, LANES), F32), jax.ShapeDtypeStruct((ne, LANES), I32)),
        grid=(n // tm,),
        in_specs=[pl.BlockSpec((tm, D_MODEL), lambda i: (i, 0)),
                  pl.BlockSpec(router_w_pad.shape, const), pl.BlockSpec((ne, 1), const),
                  pl.BlockSpec((ne, ne), const), pl.BlockSpec((tm, tm), const)],
        out_specs=(pl.BlockSpec((TOPK, tm), lambda i: (0, i)), pl.BlockSpec((TOPK, tm), lambda i: (0, i)),
                   pl.BlockSpec((tm, LANES), lambda i: (i, 0)), pl.BlockSpec((ne, LANES), const)),
        scratch_shapes=[pltpu.VMEM((ne, 1), F32)],
        compiler_params=_cparams(("arbitrary",)),
        name="moe_route",
    )(x, router_w_pad, router_b_col, ltri, utri)


def _moe_rows(n_tokens):
    return n_tokens * TOPK + N_EXPERTS * MOE_TILE


def _plan_kernel(cnt_ref, off_ref, texp_ref, nused_ref):
    shift = MOE_TILE.bit_length() - 1

    def per_expert(e, carry):
        off, ti = carry
        off_ref[e] = off
        ntile = lax.shift_right_logical(cnt_ref[e] + (MOE_TILE - 1), shift)

        def mark(j, c):
            texp_ref[ti + j] = e
            return c

        lax.fori_loop(0, ntile, mark, 0)
        return off + ntile * MOE_TILE, ti + ntile

    _, used = lax.fori_loop(0, N_EXPERTS, per_expert, (jnp.int32(0), jnp.int32(0)))
    nused_ref[0] = used

    def fill(j, c):
        texp_ref[j] = N_EXPERTS - 1
        return c

    lax.fori_loop(used, texp_ref.shape[0], fill, 0)


def moe_plan(counts, n_tokens):
    nt = _moe_rows(n_tokens) // MOE_TILE
    smem = pl.BlockSpec(memory_space=pltpu.SMEM)
    return pl.pallas_call(
        _plan_kernel,
        out_shape=(jax.ShapeDtypeStruct((N_EXPERTS,), I32), jax.ShapeDtypeStruct((nt,), I32),
                   jax.ShapeDtypeStruct((1,), I32)),
        in_specs=[smem],
        out_specs=(smem, smem, smem),
        name="moe_plan",
    )(counts)


DSP_TB = 1024


U32 = jnp.uint32
ROW_WORDS = D_MODEL // 2
ROW_CHUNKS = ROW_WORDS // LANES
assert ROW_CHUNKS == 4


def _pack_rows(y):
    half = y.shape[1] // 2
    bits = lambda v: lax.bitcast_convert_type(v.astype(jnp.bfloat16).astype(F32), U32)
    return bits(y[:, :half]) | lax.shift_right_logical(bits(y[:, half:]), jnp.uint32(16))


def _unpack_rows(words, dtype):
    hi = lax.bitcast_convert_type(words & jnp.uint32(0xFFFF0000), F32)
    lo = lax.bitcast_convert_type(lax.shift_left(words, jnp.uint32(16)), F32)
    return jnp.concatenate([hi, lo], axis=1).astype(dtype)


def _load_chunked(ref, rows):
    return jnp.concatenate([ref[pl.ds(s, rows, stride=ROW_CHUNKS), :] for s in range(ROW_CHUNKS)], axis=1)


def _store_chunked(ref, words):
    rows = words.shape[0]
    for s in range(ROW_CHUNKS):
        ref[pl.ds(s, rows, stride=ROW_CHUNKS), :] = words[:, s * LANES:(s + 1) * LANES]


def _dst_kernel(off_ref, eidx_ref, rnk_ref, dst_ref):
    e = eidx_ref[...]
    base = jnp.zeros(e.shape, I32)
    for x in range(N_EXPERTS):
        base = jnp.where(e == x, off_ref[x], base)
    dst_ref[...] = (base + rnk_ref[...]) * ROW_CHUNKS


def moe_dst(eidx, rnk, off):
    k, n = eidx.shape
    tb = 2048
    blk = pl.BlockSpec((k, tb), lambda i: (0, i))
    return pl.pallas_call(
        _dst_kernel,
        out_shape=jax.ShapeDtypeStruct((k, n), I32),
        grid=(n // tb,),
        in_specs=[pl.BlockSpec(memory_space=pltpu.SMEM), blk, blk],
        out_specs=blk,
        compiler_params=_cparams(("parallel",)),
        name="moe_dst",
    )(off, eidx, rnk)


def _tile_copy(src_ref, src_row, dst_ref, dst_row, sem):
    return pltpu.make_async_copy(src_ref.at[pl.ds(pl.multiple_of(src_row, ROW_CHUNKS), ROW_CHUNKS)],
                                 dst_ref.at[pl.ds(pl.multiple_of(dst_row, ROW_CHUNKS), ROW_CHUNKS)], sem)


def _dispatch_kernel(dst_ref, off_ref, cnt_ref, x_ref, xs_ref, zbuf, sem, zsem):
    tb = x_ref.shape[0] // ROW_CHUNKS
    zrows = MOE_TILE * ROW_CHUNKS

    def pad_copy(e):
        cnt = cnt_ref[e]
        rem = jnp.bitwise_and(cnt, MOE_TILE - 1)
        start = pl.multiple_of((off_ref[e] + cnt - rem) * ROW_CHUNKS, zrows)
        return rem != 0, pltpu.make_async_copy(zbuf, xs_ref.at[pl.ds(start, zrows)], zsem)

    @pl.when(pl.program_id(0) == 0)
    def _():
        zbuf[...] = jnp.zeros_like(zbuf)

        def start(e, c):
            has_pad, cp = pad_copy(e)

            @pl.when(has_pad)
            def _():
                cp.start()
            return c

        def wait(e, c):
            has_pad, cp = pad_copy(e)

            @pl.when(has_pad)
            def _():
                cp.wait()
            return c

        lax.fori_loop(0, N_EXPERTS, start, 0)
        lax.fori_loop(0, N_EXPERTS, wait, 0)

    def issue(t, c):
        for k in range(TOPK):
            _tile_copy(x_ref, t * ROW_CHUNKS, xs_ref, dst_ref[k, t], sem).start(priority=k % 2)
        return c

    def drain(t, c):
        for k in range(TOPK):
            _tile_copy(x_ref, 0, xs_ref, 0, sem).wait()
        return c

    lax.fori_loop(0, tb, issue, 0)
    lax.fori_loop(0, tb, drain, 0)


def moe_dispatch(xc, dst, off, counts):
    n = xc.shape[0] // ROW_CHUNKS
    tb = DSP_TB
    smem_all = pl.BlockSpec(memory_space=pltpu.SMEM)
    smem_blk = pl.BlockSpec((TOPK, tb), lambda i: (0, i), memory_space=pltpu.SMEM)
    return pl.pallas_call(
        _dispatch_kernel,
        out_shape=jax.ShapeDtypeStruct((_moe_rows(n) * ROW_CHUNKS, LANES), xc.dtype),
        grid=(n // tb,),
        in_specs=[smem_blk, smem_all, smem_all, pl.BlockSpec((tb * ROW_CHUNKS, LANES), lambda i: (i, 0))],
        out_specs=pl.BlockSpec(memory_space=pl.ANY),
        scratch_shapes=[pltpu.VMEM((MOE_TILE * ROW_CHUNKS, LANES), xc.dtype), pltpu.SemaphoreType.DMA(()),
                        pltpu.SemaphoreType.DMA(())],
        compiler_params=_cparams(("arbitrary",)),
        name="moe_dispatch",
    )(dst, off, counts, xc)


def _expert_kernel(texp_ref, nused_ref, xs_ref, w1_ref, w3_ref, w2_ref, ys_ref, w1b, w3b, w2b, last):
    i = pl.program_id(0)
    e = texp_ref[i]

    @pl.when(i == 0)
    def _():
        last[0] = -1

    @pl.when(e != last[0])
    def _():
        w1b[...] = w1_ref[0, 0].astype(BF16)
        w3b[...] = w3_ref[0, 0].astype(BF16)
        w2b[...] = w2_ref[0, 0].astype(BF16)
        last[0] = e

    @pl.when(i < nused_ref[0])
    def _():
        xb = _unpack_rows(_load_chunked(xs_ref, MOE_TILE), BF16)
        h1 = jnp.dot(xb, w1b[...], preferred_element_type=F32)
        h3 = jnp.dot(xb, w3b[...], preferred_element_type=F32)
        h = h1 * jax.nn.sigmoid(h1) * h3
        y = jnp.dot(h.astype(BF16), w2b[...], preferred_element_type=F32)
        _store_chunked(ys_ref, _pack_rows(y))

    @pl.when(i >= nused_ref[0])
    def _():
        ys_ref[...] = jnp.zeros_like(ys_ref)


def moe_experts(xs, texp, nused, w1, w3, w2, layer):
    crows = xs.shape[0]
    d, f = w1.shape[2], w1.shape[3]
    blk = MOE_TILE * ROW_CHUNKS
    nt = crows // blk
    grid_spec = pltpu.PrefetchScalarGridSpec(
        num_scalar_prefetch=2,
        grid=(nt,),
        in_specs=[pl.BlockSpec((blk, LANES), lambda i, te, nu: (jnp.where(i < nu[0], i, 0), 0)),
                  pl.BlockSpec((1, 1, d, f), lambda i, te, nu: (layer, te[i], 0, 0)),
                  pl.BlockSpec((1, 1, d, f), lambda i, te, nu: (layer, te[i], 0, 0)),
                  pl.BlockSpec((1, 1, f, d), lambda i, te, nu: (layer, te[i], 0, 0))],
        out_specs=pl.BlockSpec((blk, LANES), lambda i, te, nu: (i, 0)),
        scratch_shapes=[pltpu.VMEM((d, f), BF16), pltpu.VMEM((d, f), BF16), pltpu.VMEM((f, d), BF16),
                        pltpu.SMEM((1,), I32)],
    )
    return pl.pallas_call(
        _expert_kernel,
        out_shape=jax.ShapeDtypeStruct((crows, LANES), U32),
        grid_spec=grid_spec,
        compiler_params=_cparams(("arbitrary",), 48),
        name="moe_experts",
    )(texp, nused, xs, w1, w3, w2)


CMB_TB = 512


def _combine_kernel(dst_ref, x_ref, wts_ref, ws1_ref, ws3_ref, ws2_ref, lg_ref, lb_ref, ys_ref, o_ref, buf, sem):
    tb = x_ref.shape[0]

    def issue(t, c):
        for k in range(TOPK):
            _tile_copy(ys_ref, dst_ref[k, t], buf.at[k], t * ROW_CHUNKS, sem).start(priority=k % 2)
        return c

    def drain(t, c):
        for k in range(TOPK):
            _tile_copy(ys_ref, 0, buf.at[0], 0, sem).wait()
        return c

    lax.fori_loop(0, tb, issue, 0)
    x = x_ref[...]
    xb = x.astype(BF16)
    h1 = jnp.dot(xb, ws1_ref[...], preferred_element_type=F32)
    h3 = jnp.dot(xb, ws3_ref[...], preferred_element_type=F32)
    y = jnp.dot((h1 * jax.nn.sigmoid(h1) * h3).astype(BF16), ws2_ref[...], preferred_element_type=F32)
    lax.fori_loop(0, tb, drain, 0)
    w = wts_ref[...]
    routed = y
    for k in range(TOPK):
        routed = routed + w[:, k:k + 1] * _unpack_rows(_load_chunked(buf.at[k], tb), F32)
    o_ref[...] = _layer_norm(DN_ALPHA * x + routed, lg_ref[...], lb_ref[...])


def moe_combine(x, ys, dst, wts, ws1, ws3, ws2, ln_g, ln_b):
    n, d = x.shape
    tb = CMB_TB
    smem_blk = pl.BlockSpec((TOPK, tb), lambda i: (0, i), memory_space=pltpu.SMEM)
    const = lambda i: (0, 0)
    return pl.pallas_call(
        _combine_kernel,
        out_shape=jax.ShapeDtypeStruct((n, d), F32),
        grid=(n // tb,),
        in_specs=[smem_blk,
                  pl.BlockSpec((tb, d), lambda i: (i, 0)), pl.BlockSpec((tb, LANES), lambda i: (i, 0)),
                  pl.BlockSpec(ws1.shape, const), pl.BlockSpec(ws3.shape, const), pl.BlockSpec(ws2.shape, const),
                  pl.BlockSpec((1, d), const), pl.BlockSpec((1, d), const),
                  pl.BlockSpec(memory_space=pl.ANY)],
        out_specs=pl.BlockSpec((tb, d), lambda i: (i, 0)),
        scratch_shapes=[pltpu.VMEM((TOPK, tb * ROW_CHUNKS, LANES), U32), pltpu.SemaphoreType.DMA(())],
        compiler_params=_cparams(("parallel",), 48),
        name="moe_combine_ln2",
    )(dst, x, wts, ws1, ws3, ws2, ln_g, ln_b, ys)


def moe_block(x, xp, router_w, router_b, w1, w3, w2, layer, ws1, ws3, ws2, ln_g, ln_b):
    n, d = x.shape
    rw = jnp.pad(router_w, ((0, 0), (0, LANES - N_EXPERTS))).astype(BF16)
    eidx, rnk, wts, cnt = moe_route(x, rw, router_b.reshape(N_EXPERTS, 1))
    counts = cnt[:, 0]
    off, texp, nused = moe_plan(counts, n)
    dst = moe_dst(eidx, rnk, off)
    xs = moe_dispatch(xp, dst, off, counts)
    ys = moe_experts(xs, texp, nused, w1, w3, w2, layer)
    return moe_combine(x, ys, dst, wts, ws1.astype(BF16), ws3.astype(BF16), ws2.astype(BF16),
                       ln_g.reshape(1, -1), ln_b.reshape(1, -1))


def nsa_rope_tables(positions):
    batch, seq = positions.shape
    inv, sgn = _nsa_inv_freq()
    posf = positions.astype(F32)
    tok = rope_tables(posf.reshape(batch * seq, 1), inv, sgn, tm=512)
    end = posf[:, CMP_BLOCK - 1::CMP_STRIDE]
    npad = _n_cmp_pad(seq)
    end = jnp.pad(end, ((0, 0), (0, npad - end.shape[1])))
    cmp = rope_tables(end.reshape(batch * npad, 1), inv, sgn, tm=npad)
    return tok, cmp


def nsa_branch(z, tok_tab, cmp_tab, wk, wv, ovl, *, batch, seq):
    n = batch * seq
    w = NSA_GROUPS * NSA_DH
    xk = z[:, Z_NKV:Z_NKV + w].reshape(n // CMP_HALF, CMP_ROW)
    xv = z[:, Z_NKV + w:Z_NKV + 2 * w].reshape(n // CMP_HALF, CMP_ROW)
    kc, vct = nsa_compress(xk, xv, wk, wv, cmp_tab[0], cmp_tab[1], batch=batch)
    ksa, kw, vswt = nsa_kv_prep(z, tok_tab[0], tok_tab[1], batch=batch, seq=seq)
    return nsa_attention(z, tok_tab[0], tok_tab[1], kc, vct, ksa, kw, vswt, ovl, batch=batch, seq=seq)


def kernel(x, positions, w_in, conv_dw, conv_db, conv_ln_g, conv_ln_b, conv_w_pw, ret_w_o, nsa_pe_k, nsa_w1_k, nsa_w2_k,
           nsa_pe_v, nsa_w1_v, nsa_w2_v, nsa_w_o, w_out, ln1_g, ln1_b, router_w, router_b, moe_w1, moe_w3, moe_w2,
           shared_w1, shared_w3, shared_w2, ln2_g, ln2_b):
    batch, seq, d = x.shape
    n = batch * seq
    xf = x.reshape(n, d)
    row = lambda v: v.reshape(1, -1)

    ret_inv, ret_sgn = _ret_inv_freq()
    ret_tab = rope_tables(positions.astype(F32).reshape(n, 1), ret_inv, ret_sgn, tm=512)
    tok_tab, cmp_tab = nsa_rope_tables(positions)
    ret_consts = _ret_tables()
    ovl = _overlap_matrix(seq)

    for l in range(w_in.shape[0]):
        z = matmul(xf, _layout_w_in(w_in[l]), tm=2048, tn=512, out_dtype=BF16, name="in_proj")
        y_conv = conv_branch(z, conv_dw[l], row(conv_db[l]), row(conv_ln_g[l]), row(conv_ln_b[l]),
                             conv_w_pw[l].astype(BF16), batch=batch, seq=seq)
        o_ret = retention_branch(z, ret_tab[0], ret_tab[1], ret_consts, batch=batch, seq=seq)
        o_nsa = nsa_branch(z, tok_tab, cmp_tab, _cmp_weights(nsa_pe_k[l], nsa_w1_k[l], nsa_w2_k[l]),
                           _cmp_weights(nsa_pe_v[l], nsa_w1_v[l], nsa_w2_v[l]), ovl, batch=batch, seq=seq)
        x1, x1p = merge_block(xf, y_conv, o_ret, o_nsa, z, ret_w_o[l].astype(BF16), nsa_w_o[l].astype(BF16),
                              w_out[l].astype(BF16), row(ln1_g[l]), row(ln1_b[l]))
        xf = moe_block(x1, x1p, router_w[l], router_b[l], moe_w1, moe_w3, moe_w2, l,
                       shared_w1[l], shared_w3[l], shared_w2[l], ln2_g[l], ln2_b[l])
    return xf.reshape(batch, seq, d)
```
